```python
import math
import jax, jax.numpy as jnp
from jax import lax
import numpy as np

D_MODEL = 1024
BATCH = 4
SEQ = 4096
DEPTH = 2
DEC_BATCH = 128
DEC_SEQ = 1
PAST_LEN = 8192
PAGE_SIZE = 128

BRANCH_WIDTH = D_MODEL // 2
N_BRANCH = 3
ML_HEADS = 4
ML_DK = BRANCH_WIDTH // ML_HEADS
ML_DV = BRANCH_WIDTH // ML_HEADS
ML_CHUNK = 128
SWA_HEAD_DIM = 64
SWA_HEADS = BRANCH_WIDTH // SWA_HEAD_DIM
SWA_KV_HEADS = 2
SWA_GROUP = SWA_HEADS // SWA_KV_HEADS
WINDOW = 128
ROT_DIM = SWA_HEAD_DIM // 4
ROPE_THETA = 500000.0
MEM_TOKENS = 256
X_HEADS = 4
X_HEAD_DIM = BRANCH_WIDTH // X_HEADS
D_FF = -(-8 * D_MODEL // (3 * 256)) * 256
LN_EPS = 1e-5
HEAD_NORM_EPS = 1e-6
DEEPNORM_ALPHA = (2 * DEPTH) ** 0.25
DEEPNORM_BETA = (8 * DEPTH) ** -0.25
NEG_INF = -1e30
D_IN = (2 * ML_HEADS * ML_DK + 2 * ML_HEADS * ML_DV + 2 * ML_HEADS
        + SWA_HEADS * SWA_HEAD_DIM + 2 * SWA_KV_HEADS * SWA_HEAD_DIM
        + X_HEADS * X_HEAD_DIM + N_BRANCH * D_MODEL)

kernel_name = "hybrid_mlstm_swa_memory_decoder_step"


def _in_split_points():
    sizes = [ML_HEADS * ML_DK, ML_HEADS * ML_DK, ML_HEADS * ML_DV, ML_HEADS * ML_DV,
             ML_HEADS, ML_HEADS,
             SWA_HEADS * SWA_HEAD_DIM, SWA_KV_HEADS * SWA_HEAD_DIM, SWA_KV_HEADS * SWA_HEAD_DIM,
             X_HEADS * X_HEAD_DIM]
    pts = []
    acc = 0
    for s in sizes:
        acc += s
        pts.append(acc)
    return pts


def layer_norm(x, g, b):
    xf = x.astype(jnp.float32)
    mu = xf.mean(-1, keepdims=True)
    var = jnp.square(xf - mu).mean(-1, keepdims=True)
    return ((xf - mu) * lax.rsqrt(var + LN_EPS) * g.astype(jnp.float32) + b.astype(jnp.float32)).astype(x.dtype)


def partial_rope(x, positions):
    half = ROT_DIM // 2
    inv_freq = ROPE_THETA ** (-jnp.arange(half, dtype=jnp.float32) / half)
    ang = positions.astype(jnp.float32)[:, None] * inv_freq[None, :]
    cos = jnp.cos(ang)[None, :, None, :]
    sin = jnp.sin(ang)[None, :, None, :]
    x1 = x[..., :half].astype(jnp.float32)
    x2 = x[..., half:ROT_DIM].astype(jnp.float32)
    rot = jnp.concatenate([x1 * cos - x2 * sin, x2 * cos + x1 * sin], axis=-1).astype(x.dtype)
    return jnp.concatenate([rot, x[..., ROT_DIM:]], axis=-1)


def mlstm_chunk_step(carry, inp):
    c, n, m = carry
    q, k, v, li, lf = inp
    L = q.shape[2]
    b = jnp.cumsum(lf, axis=-1)
    causal = jnp.tril(jnp.ones((L, L), dtype=bool))
    d = jnp.where(causal, b[..., :, None] - b[..., None, :] + li[..., None, :], -jnp.inf)
    inter = b + m[..., None]
    m_t = jnp.maximum(inter, d.max(-1))
    w = jnp.exp(d - m_t[..., None])
    a = jnp.exp(inter - m_t)
    sw = jnp.einsum('bhtd,bhsd->bhts', q, k) * w
    num = jnp.einsum('bhts,bhsv->bhtv', sw, v) + a[..., None] * jnp.einsum('bhvd,bhtd->bhtv', c, q)
    den = sw.sum(-1) + a * jnp.einsum('bhd,bhtd->bht', n, q)
    h = num / jnp.maximum(jnp.abs(den), jnp.exp(-m_t))[..., None]
    b_last = b[..., -1]
    ws = b_last[..., None] - b + li
    m_new = jnp.maximum(b_last + m, ws.max(-1))
    ec = jnp.exp(b_last + m - m_new)
    es = jnp.exp(ws - m_new[..., None])
    c_new = ec[..., None, None] * c + jnp.einsum('bhs,bhsv,bhsd->bhvd', es, v, k)
    n_new = ec[..., None] * n + jnp.einsum('bhs,bhsd->bhd', es, k)
    return (c_new, n_new, m_new), h


def mlstm(q, k, v, li, lf, c0, n0, m0):
    B, T = q.shape[0], q.shape[1]
    L = min(T, ML_CHUNK)
    nc = T // L

    def chunks(t):
        t = jnp.moveaxis(t, 2, 1)
        t = t.reshape(t.shape[:2] + (nc, L) + t.shape[3:])
        return jnp.moveaxis(t, 2, 0)

    carry0 = (c0.astype(jnp.float32), n0.astype(jnp.float32), m0.astype(jnp.float32))
    xs = (chunks(q), chunks(k), chunks(v), chunks(li), chunks(lf))
    (c, n, m), h = lax.scan(mlstm_chunk_step, carry0, xs)
    h = jnp.moveaxis(h, 0, 2).reshape(B, ML_HEADS, T, ML_DV)
    return jnp.moveaxis(h, 1, 2), c, n, m


def sliding_window_attention(q, k, v, past_k, past_v, pos0, sinks):
    B, T = q.shape[0], q.shape[1]
    L = min(T, WINDOW)
    nb = T // L
    full_k = jnp.concatenate([past_k.astype(k.dtype), k], axis=1)
    full_v = jnp.concatenate([past_v.astype(v.dtype), v], axis=1)
    if nb == 1:
        ctx_k = full_k[:, None]
        ctx_v = full_v[:, None]
    else:
        fk = full_k.reshape(B, nb + 1, WINDOW, SWA_KV_HEADS, SWA_HEAD_DIM)
        fv = full_v.reshape(B, nb + 1, WINDOW, SWA_KV_HEADS, SWA_HEAD_DIM)
        ctx_k = jnp.concatenate([fk[:, :-1], fk[:, 1:]], axis=2)
        ctx_v = jnp.concatenate([fv[:, :-1], fv[:, 1:]], axis=2)
    qb = q.reshape(B, nb, L, SWA_KV_HEADS, SWA_GROUP, SWA_HEAD_DIM)
    starts = pos0 + L * jnp.arange(nb)
    qpos = starts[:, None] + jnp.arange(L)[None, :]
    kpos = starts[:, None] - WINDOW + jnp.arange(WINDOW + L)[None, :]
    kp = kpos[:, None, :]
    qp = qpos[:, :, None]
    allowed = (kp <= qp) & (kp >= qp - WINDOW) & (kp >= 0)
    s = jnp.einsum('bnqkgd,bnskd->bnkgqs', qb, ctx_k).astype(jnp.float32) * (SWA_HEAD_DIM ** -0.5)
    s = jnp.where(allowed[None, :, None, None], s, NEG_INF)
    sink = jnp.broadcast_to(sinks.astype(jnp.float32).reshape(1, 1, SWA_KV_HEADS, SWA_GROUP, 1, 1),
                            s.shape[:-1] + (1,))
    p = jax.nn.softmax(jnp.concatenate([s, sink], axis=-1), axis=-1)[..., :-1]
    o = jnp.einsum('bnkgqs,bnskd->bnqkgd', p.astype(ctx_v.dtype), ctx_v)
    return o.reshape(B, T, SWA_HEADS * SWA_HEAD_DIM), full_k[:, -WINDOW:], full_v[:, -WINDOW:]


def memory_cross_attention(q, mem_k, mem_v):
    s = jnp.einsum('bthd,bmhd->bhtm', q, mem_k.astype(q.dtype)).astype(jnp.float32) * (X_HEAD_DIM ** -0.5)
    p = jax.nn.softmax(s, axis=-1)
    o = jnp.einsum('bhtm,bmhd->bthd', p.astype(q.dtype), mem_v.astype(q.dtype))
    return o.reshape(q.shape[0], q.shape[1], X_HEADS * X_HEAD_DIM)


def trunk_layer(x, pos0, past_k, past_v, c0, n0, m0, mem_k, mem_v,
                w_in, b_gates, mlstm_norm_g, swa_sinks, w_branch, w_mix_out,
                ln1_g, ln1_b, w_ffn_in, w_ffn_out, ln2_g, ln2_b):
    B, T, _ = x.shape
    f32 = jnp.float32
    proj = jnp.einsum('btd,de->bte', x, w_in)
    mq, mk, mv, mo, mi, mf, sq, sk, sv, xq, gl = jnp.split(proj, _in_split_points(), axis=-1)

    q_a = mq.reshape(B, T, ML_HEADS, ML_DK).astype(f32)
    k_a = mk.reshape(B, T, ML_HEADS, ML_DK).astype(f32) * (ML_DK ** -0.5)
    v_a = mv.reshape(B, T, ML_HEADS, ML_DV).astype(f32)
    gate_pre = jnp.concatenate([mi, mf], axis=-1).astype(f32) + b_gates.astype(f32)
    li = gate_pre[..., :ML_HEADS]
    lf = jax.nn.log_sigmoid(gate_pre[..., ML_HEADS:])
    h_a, c_new, n_new, m_new = mlstm(q_a, k_a, v_a, li, lf, c0, n0, m0)
    mu = h_a.mean(-1, keepdims=True)
    var = jnp.square(h_a - mu).mean(-1, keepdims=True)
    h_a = ((h_a - mu) * lax.rsqrt(var + HEAD_NORM_EPS)).reshape(B, T, ML_HEADS * ML_DV) * mlstm_norm_g.astype(f32)
    y_a = (jax.nn.sigmoid(mo.astype(f32)) * h_a).astype(x.dtype)

    positions = pos0 + jnp.arange(T)
    q_b = partial_rope(sq.reshape(B, T, SWA_HEADS, SWA_HEAD_DIM), positions)
    k_b = partial_rope(sk.reshape(B, T, SWA_KV_HEADS, SWA_HEAD_DIM), positions)
    v_b = sv.reshape(B, T, SWA_KV_HEADS, SWA_HEAD_DIM)
    y_b, k_buf, v_buf = sliding_window_attention(q_b, k_b, v_b, past_k, past_v, pos0, swa_sinks)

    y_c = memory_cross_attention(xq.reshape(B, T, X_HEADS, X_HEAD_DIM), mem_k, mem_v)

    branches = jnp.stack([y_a, y_b.astype(x.dtype), y_c], axis=2)
    gates = jax.nn.sigmoid(gl.reshape(B, T, N_BRANCH, D_MODEL))
    widened = jnp.einsum('btrc,rcd->btrd', branches, w_branch)
    mixed = jnp.einsum('btd,de->bte', (gates * widened).sum(axis=2), w_mix_out)
    x = layer_norm(DEEPNORM_ALPHA * x + mixed, ln1_g, ln1_b)

    gu = jnp.einsum('btd,df->btf', x, w_ffn_in)
    g, u = jnp.split(gu, 2, axis=-1)
    ffn = jnp.einsum('btf,fd->btd', jax.nn.silu(g) * u, w_ffn_out)
    x = layer_norm(DEEPNORM_ALPHA * x + ffn, ln2_g, ln2_b)
    return x, k_buf, v_buf, c_new, n_new, m_new


def setup_inputs(seed: int = 0) -> dict:
    key = jax.random.key(seed)
    ks = jax.random.split(key, 24)
    f32 = jnp.float32

    def nrm(k, shape, scale=1.0):
        return scale * jax.random.normal(k, shape, f32)

    b_i = nrm(ks[12], (DEPTH, ML_HEADS), 0.1)
    b_f = 3.0 + 3.0 * jax.random.uniform(ks[13], (DEPTH, ML_HEADS), f32)
    return {
        "x_prompt": nrm(ks[0], (BATCH, SEQ, D_MODEL)),
        "x_sample": nrm(ks[1], (DEC_BATCH, DEC_SEQ, D_MODEL)),
        "mem_prompt": nrm(ks[2], (BATCH, MEM_TOKENS, D_MODEL)),
        "cache_swa_k": nrm(ks[3], (DEPTH, DEC_BATCH, WINDOW, SWA_KV_HEADS, SWA_HEAD_DIM)),
        "cache_swa_v": nrm(ks[4], (DEPTH, DEC_BATCH, WINDOW, SWA_KV_HEADS, SWA_HEAD_DIM)),
        "cache_mem_k": nrm(ks[5], (DEPTH, DEC_BATCH, MEM_TOKENS, X_HEADS, X_HEAD_DIM)),
        "cache_mem_v": nrm(ks[6], (DEPTH, DEC_BATCH, MEM_TOKENS, X_HEADS, X_HEAD_DIM)),
        "state_mlstm_c": nrm(ks[7], (DEPTH, DEC_BATCH, ML_HEADS, ML_DV, ML_DK), 0.1),
        "state_mlstm_n": nrm(ks[8], (DEPTH, DEC_BATCH, ML_HEADS, ML_DK), 0.5),
        "state_mlstm_m": nrm(ks[9], (DEPTH, DEC_BATCH, ML_HEADS)),
        "w_in": nrm(ks[10], (DEPTH, D_MODEL, D_IN), D_MODEL ** -0.5),
        "b_gates": jnp.concatenate([b_i, b_f], axis=-1),
        "mlstm_norm_g": 1.0 + nrm(ks[14], (DEPTH, ML_HEADS * ML_DV), 0.02),
        "swa_sinks": nrm(ks[15], (DEPTH, SWA_HEADS), 0.5),
        "w_mem_kv": nrm(ks[16], (DEPTH, D_MODEL, 2 * X_HEADS * X_HEAD_DIM), D_MODEL ** -0.5),
        "w_branch": nrm(ks[17], (DEPTH, N_BRANCH, BRANCH_WIDTH, D_MODEL), DEEPNORM_BETA * BRANCH_WIDTH ** -0.5),
        "w_mix_out": nrm(ks[18], (DEPTH, D_MODEL, D_MODEL), DEEPNORM_BETA * D_MODEL ** -0.5),
        "ln1_g": 1.0 + nrm(ks[19], (DEPTH, D_MODEL), 0.02),
        "ln1_b": nrm(ks[20], (DEPTH, D_MODEL), 0.02),
        "w_ffn_in": nrm(ks[21], (DEPTH, D_MODEL, 2 * D_FF), D_MODEL ** -0.5),
        "w_ffn_out": nrm(ks[22], (DEPTH, D_FF, D_MODEL), DEEPNORM_BETA * D_FF ** -0.5),
        "ln2_g": 1.0 + nrm(ks[23], (DEPTH, D_MODEL), 0.02),
        "ln2_b": nrm(ks[11], (DEPTH, D_MODEL), 0.02),
    }


def reference(x_prompt, x_sample, mem_prompt, cache_swa_k, cache_swa_v, cache_mem_k, cache_mem_v,
              state_mlstm_c, state_mlstm_n, state_mlstm_m, w_in, b_gates, mlstm_norm_g, swa_sinks,
              w_mem_kv, w_branch, w_mix_out, ln1_g, ln1_b, w_ffn_in, w_ffn_out, ln2_g, ln2_b):
    f32 = jnp.float32
    B = x_prompt.shape[0]
    yp = x_prompt
    ys = x_sample
    swa_k_p, swa_v_p, swa_k_s, swa_v_s = [], [], [], []
    mem_k_p, mem_v_p = [], []
    c_p, n_p, m_p, c_s, n_s, m_s = [], [], [], [], [], []
    for l in range(DEPTH):
        weights = (w_in[l], b_gates[l], mlstm_norm_g[l], swa_sinks[l], w_branch[l], w_mix_out[l],
                   ln1_g[l], ln1_b[l], w_ffn_in[l], w_ffn_out[l], ln2_g[l], ln2_b[l])
        mem_kv = jnp.einsum('bmd,de->bme', mem_prompt, w_mem_kv[l]).reshape(B, MEM_TOKENS, 2, X_HEADS, X_HEAD_DIM)
        mk = mem_kv[:, :, 0]
        mv = mem_kv[:, :, 1]
        zero_kv = jnp.zeros((B, WINDOW, SWA_KV_HEADS, SWA_HEAD_DIM), x_prompt.dtype)
        zero_c = jnp.zeros((B, ML_HEADS, ML_DV, ML_DK), f32)
        zero_n = jnp.zeros((B, ML_HEADS, ML_DK), f32)
        zero_m = jnp.zeros((B, ML_HEADS), f32)
        yp, kb, vb, cn, nn_, mn = trunk_layer(yp, 0, zero_kv, zero_kv, zero_c, zero_n, zero_m, mk, mv, *weights)
        swa_k_p.append(kb); swa_v_p.append(vb)
        mem_k_p.append(mk); mem_v_p.append(mv)
        c_p.append(cn); n_p.append(nn_); m_p.append(mn)
        ys, kb, vb, cn, nn_, mn = trunk_layer(ys, PAST_LEN, cache_swa_k[l], cache_swa_v[l],
                                             state_mlstm_c[l], state_mlstm_n[l], state_mlstm_m[l],
                                             cache_mem_k[l], cache_mem_v[l], *weights)
        swa_k_s.append(kb); swa_v_s.append(vb)
        c_s.append(cn); n_s.append(nn_); m_s.append(mn)
    return (yp, ys,
            jnp.stack(swa_k_p), jnp.stack(swa_v_p), jnp.stack(swa_k_s), jnp.stack(swa_v_s),
            jnp.stack(mem_k_p), jnp.stack(mem_v_p),
            jnp.stack(c_p), jnp.stack(n_p), jnp.stack(m_p),
            jnp.stack(c_s), jnp.stack(n_s), jnp.stack(m_s))
```

```python
import functools
import math

import jax
import jax.numpy as jnp
from jax import lax
from jax.experimental import pallas as pl
from jax.experimental.pallas import tpu as pltpu

F32 = jnp.float32
BF16 = jnp.bfloat16

D_MODEL = 1024
DEPTH = 2
BRANCH = 512
ML_HEADS = 4
ML_DK = 128
ML_DV = 128
ML_CHUNK = 128
SWA_HD = 64
SWA_HEADS = 8
SWA_KV = 2
SWA_GROUP = 4
WINDOW = 128
ROT_DIM = 16
ROT_HALF = 8
ROPE_THETA = 500000.0
MEM_TOKENS = 256
X_HEADS = 4
X_HD = 128
D_FF = 2816
LN_EPS = 1e-5
HEAD_NORM_EPS = 1e-6
DEEPNORM_ALPHA = (2 * DEPTH) ** 0.25
NEG_INF = -1e30
PAST_LEN = 8192

LANES = 128
VMEM_LIMIT = 56 * 1024 * 1024

_C_MQ, _C_MK, _C_MV, _C_MO = 0, 512, 1024, 1536
_C_MI, _C_MF = 2048, 2052
_C_SQ, _C_SK, _C_SV, _C_XQ, _C_GL = 2056, 2568, 2696, 2824, 3336

TM_PROJ = 512
TM_MERGE = 256
TQ_CROSS = 256
BB = 8


def _mm(a, b):
    return jnp.dot(a, b, preferred_element_type=F32)


def _mm_nt(a, b):
    return lax.dot_general(a, b, (((1,), (1,)), ((), ())), preferred_element_type=F32)


def _sigmoid(x):
    return 1.0 / (1.0 + jnp.exp(-x))


def _log_sigmoid(x):
    return jnp.minimum(x, 0.0) - jnp.log(1.0 + jnp.exp(-jnp.abs(x)))


def _layer_norm(x, g, b):
    mu = jnp.mean(x, axis=-1, keepdims=True)
    xc = x - mu
    var = jnp.mean(xc * xc, axis=-1, keepdims=True)
    return xc * lax.rsqrt(var + LN_EPS) * g + b


def _rope_lanes(x, cos, sin):
    lane = lax.broadcasted_iota(jnp.int32, x.shape, 1) % SWA_HD
    up = pltpu.roll(x, LANES - ROT_HALF, axis=1)
    dn = pltpu.roll(x, ROT_HALF, axis=1)
    first = x * cos - up * sin
    second = x * cos + dn * sin
    return jnp.where(lane < ROT_HALF, first, jnp.where(lane < ROT_DIM, second, x))


def _const_spec(shape):
    nd = len(shape)
    return pl.BlockSpec(shape, lambda *_: (0,) * nd, pipeline_mode=pl.Buffered(1))


def _params(sem):
    return pltpu.CompilerParams(dimension_semantics=sem, vmem_limit_bytes=VMEM_LIMIT)


def _memkv_kernel(mem_ref, wkv_ref, wkt_ref, k32_ref, v32_ref, kt_ref, v16_ref):
    m = mem_ref[...].astype(BF16)
    kv = _mm(m, wkv_ref[...])
    k32_ref[...] = kv[:, :BRANCH]
    v32_ref[...] = kv[:, BRANCH:]
    v16_ref[...] = kv[:, BRANCH:].astype(BF16)
    kt_ref[...] = _mm_nt(wkt_ref[...], m).astype(BF16)


def _memkv(mem2d, wkv, wkt, nb):
    rows = mem2d.shape[0]
    return pl.pallas_call(
        _memkv_kernel,
        grid=(nb,),
        in_specs=[pl.BlockSpec((MEM_TOKENS, D_MODEL), lambda b: (b, 0)),
                  _const_spec((D_MODEL, 2 * BRANCH)),
                  _const_spec((BRANCH, D_MODEL))],
        out_specs=[pl.BlockSpec((MEM_TOKENS, BRANCH), lambda b: (b, 0)),
                   pl.BlockSpec((MEM_TOKENS, BRANCH), lambda b: (b, 0)),
                   pl.BlockSpec((None, BRANCH, MEM_TOKENS), lambda b: (b, 0, 0)),
                   pl.BlockSpec((MEM_TOKENS, BRANCH), lambda b: (b, 0))],
        out_shape=[jax.ShapeDtypeStruct((rows, BRANCH), F32),
                   jax.ShapeDtypeStruct((rows, BRANCH), F32),
                   jax.ShapeDtypeStruct((nb, BRANCH, MEM_TOKENS), BF16),
                   jax.ShapeDtypeStruct((rows, BRANCH), BF16)],
        compiler_params=_params(("arbitrary",)),
        name="memkv_proj",
    )(mem2d, wkv, wkt)


_N_MQ, _N_MV, _N_MO, _N_SQ, _N_XQ, _N_SK, _N_SV, _N_END = 0, 512, 1024, 1536, 2048, 2560, 2688, 2816
_T_MK, _T_SK, _T_G, _T_END = 0, 512, 640, 656


def _proj_kernel(x_ref, wn_ref, wt_ref, bias_ref, cosn_ref, sinn_ref, cost_ref, sint_ref,
                 q_ref, v_ref, mo_ref, sq_ref, xq_ref, sv_ref, kt_ref, skt_ref, gt_ref, k32_ref, v32_ref):
    tm = x_ref.shape[0]
    xb = x_ref[...].astype(BF16)
    q_ref[...] = _mm(xb, wn_ref[:, _N_MQ:_N_MV]).astype(BF16)
    v_ref[...] = _mm(xb, wn_ref[:, _N_MV:_N_MO]).astype(BF16)
    mo_ref[...] = _mm(xb, wn_ref[:, _N_MO:_N_SQ])
    xq_ref[...] = _mm(xb, wn_ref[:, _N_XQ:_N_SK]).astype(BF16)
    cosn = cosn_ref[...]
    sinn = sinn_ref[...]
    sq = _mm(xb, wn_ref[:, _N_SQ:_N_XQ])
    for c in range(BRANCH // LANES):
        blk = _rope_lanes(sq[:, c * LANES:(c + 1) * LANES], cosn, sinn)
        sq_ref[:, c * LANES:(c + 1) * LANES] = blk.astype(BF16)
    sk = _rope_lanes(_mm(xb, wn_ref[:, _N_SK:_N_SV]), cosn, sinn)
    sv = _mm(xb, wn_ref[:, _N_SV:_N_END])
    sv_ref[...] = sv.astype(BF16)
    k32_ref[...] = sk[tm - WINDOW:, :]
    v32_ref[...] = sv[tm - WINDOW:, :]
    kt = _mm_nt(wt_ref[_T_MK:_T_SK, :], xb) * (ML_DK ** -0.5)
    kt_ref[...] = kt.astype(BF16)
    skt = _mm_nt(wt_ref[_T_SK:_T_G, :], xb)
    cost = cost_ref[...]
    sint = sint_ref[...]
    for g in range(SWA_KV):
        base = g * SWA_HD
        x1 = skt[base:base + ROT_HALF, :]
        x2 = skt[base + ROT_HALF:base + ROT_DIM, :]
        skt_ref[base:base + ROT_HALF, :] = (x1 * cost - x2 * sint).astype(BF16)
        skt_ref[base + ROT_HALF:base + ROT_DIM, :] = (x2 * cost + x1 * sint).astype(BF16)
        skt_ref[base + ROT_DIM:base + SWA_HD, :] = skt[base + ROT_DIM:base + SWA_HD, :].astype(BF16)
    gt_ref[...] = _mm_nt(wt_ref[_T_G:_T_END, :], xb) + bias_ref[...]


def _proj(x2d, wn, wt, bias_col, cosn, sinn, cost, sint, nb, seq):
    m = x2d.shape[0]
    tm = TM_PROJ
    nt = seq // tm
    row = lambda i: (i, 0)
    col = lambda i: (0, i)
    return pl.pallas_call(
        _proj_kernel,
        grid=(m // tm,),
        in_specs=[pl.BlockSpec((tm, D_MODEL), row),
                  _const_spec((D_MODEL, _N_END)),
                  _const_spec((_T_END, D_MODEL)),
                  _const_spec((16, 1)),
                  pl.BlockSpec((tm, LANES), lambda i: (i % nt, 0)),
                  pl.BlockSpec((tm, LANES), lambda i: (i % nt, 0)),
                  pl.BlockSpec((ROT_HALF, tm), lambda i: (0, i % nt)),
                  pl.BlockSpec((ROT_HALF, tm), lambda i: (0, i % nt))],
        out_specs=[pl.BlockSpec((tm, BRANCH), row),
                   pl.BlockSpec((tm, BRANCH), row),
                   pl.BlockSpec((tm, BRANCH), row),
                   pl.BlockSpec((tm, BRANCH), row),
                   pl.BlockSpec((tm, BRANCH), row),
                   pl.BlockSpec((tm, LANES), row),
                   pl.BlockSpec((BRANCH, tm), col),
                   pl.BlockSpec((LANES, tm), col),
                   pl.BlockSpec((16, tm), col),
                   pl.BlockSpec((WINDOW, LANES), lambda i: (i // nt, 0)),
                   pl.BlockSpec((WINDOW, LANES), lambda i: (i // nt, 0))],
        out_shape=[jax.ShapeDtypeStruct((m, BRANCH), BF16),
                   jax.ShapeDtypeStruct((m, BRANCH), BF16),
                   jax.ShapeDtypeStruct((m, BRANCH), F32),
                   jax.ShapeDtypeStruct((m, BRANCH), BF16),
                   jax.ShapeDtypeStruct((m, BRANCH), BF16),
                   jax.ShapeDtypeStruct((m, LANES), BF16),
                   jax.ShapeDtypeStruct((BRANCH, m), BF16),
                   jax.ShapeDtypeStruct((LANES, m), BF16),
                   jax.ShapeDtypeStruct((16, m), F32),
                   jax.ShapeDtypeStruct((nb * WINDOW, LANES), F32),
                   jax.ShapeDtypeStruct((nb * WINDOW, LANES), F32)],
        compiler_params=_params(("arbitrary",)),
        name="prompt_proj",
    )(x2d, wn, wt, bias_col, cosn, sinn, cost, sint)


def _mlstm_kernel(q_ref, v_ref, kt_ref, mo_ref, gt_ref, gain_ref, y_ref, s_out_ref, m_out_ref, s_ref, m_ref):
    j = pl.program_id(1)
    L = ML_CHUNK

    @pl.when(j == 0)
    def _():
        s_ref[...] = jnp.zeros_like(s_ref)
        m_ref[...] = jnp.zeros_like(m_ref)

    g = gt_ref[0:8, :]
    row8 = lax.broadcasted_iota(jnp.int32, (8, L), 0)
    lane8 = lax.broadcasted_iota(jnp.int32, (8, L), 1)
    lib = jnp.where(row8 < ML_HEADS, g, _log_sigmoid(g))
    csum = lib
    shift = 1
    while shift < L:
        csum = csum + jnp.where(lane8 >= shift, pltpu.roll(csum, shift, axis=1), 0.0)
        shift *= 2
    lib = jnp.where(row8 < ML_HEADS, lib, csum)
    cols = jnp.concatenate([lib, jnp.zeros((L - 8, L), F32)], axis=0).T

    r_i = lax.broadcasted_iota(jnp.int32, (L, L), 0)
    c_i = lax.broadcasted_iota(jnp.int32, (L, L), 1)
    causal = c_i <= r_i
    lane_l = lax.broadcasted_iota(jnp.int32, (L, L), 1)

    for h in range(ML_HEADS):
        hs = slice(h * ML_DK, (h + 1) * ML_DK)
        li_r = lib[h:h + 1, :]
        b_r = lib[ML_HEADS + h:ML_HEADS + h + 1, :]
        li_c = cols[:, h:h + 1]
        b_c = cols[:, ML_HEADS + h:ML_HEADS + h + 1]
        m_prev = m_ref[h:h + 1, 0:1]
        d = jnp.where(causal, b_c - b_r + li_r, -jnp.inf)
        inter = b_c + m_prev
        m_t = jnp.maximum(inter, jnp.max(d, axis=1, keepdims=True))
        w = jnp.exp(d - m_t)
        a = jnp.exp(inter - m_t)
        qh = q_ref[:, hs]
        vh = v_ref[:, hs]
        kth = kt_ref[hs, :]
        sw = _mm(qh, kth) * w
        state = s_ref[h]
        qs = _mm(qh, state.astype(BF16))
        num = _mm(sw.astype(BF16), vh) + a * qs[:, :ML_DV]
        den = jnp.sum(sw, axis=1, keepdims=True) + a * qs[:, ML_DV:ML_DV + 1]
        hh = num / jnp.maximum(jnp.abs(den), jnp.exp(-m_t))
        b_last = b_r[:, L - 1:L]
        m_new = jnp.maximum(b_last + m_prev, jnp.max(b_last - b_r + li_r, axis=1, keepdims=True))
        ec = jnp.exp(b_last + m_prev - m_new)
        es_c = jnp.exp(b_last - b_c + li_c - m_new)
        upd = jnp.concatenate([es_c * vh.astype(F32), jnp.where(lane_l == 0, es_c, 0.0)], axis=1)
        s_ref[h] = ec * state + _mm(kth, upd.astype(BF16))
        m_ref[h:h + 1, :] = jnp.broadcast_to(m_new, (1, L))
        mu = jnp.mean(hh, axis=1, keepdims=True)
        hc = hh - mu
        var = jnp.mean(hc * hc, axis=1, keepdims=True)
        hn = hc * lax.rsqrt(var + HEAD_NORM_EPS) * gain_ref[:, hs]
        y_ref[:, hs] = (_sigmoid(mo_ref[:, hs]) * hn).astype(BF16)

    @pl.when(j == pl.num_programs(1) - 1)
    def _():
        s_out_ref[...] = s_ref[...]
        m_out_ref[...] = m_ref[...]


def _mlstm(q, v, kt, mo, gt, gain, nb, seq):
    nc = seq // ML_CHUNK
    row = lambda b, j: (b * nc + j, 0)
    return pl.pallas_call(
        _mlstm_kernel,
        grid=(nb, nc),
        in_specs=[pl.BlockSpec((ML_CHUNK, BRANCH), row),
                  pl.BlockSpec((ML_CHUNK, BRANCH), row),
                  pl.BlockSpec((BRANCH, ML_CHUNK), lambda b, j: (0, b * nc + j)),
                  pl.BlockSpec((ML_CHUNK, BRANCH), row),
                  pl.BlockSpec((16, ML_CHUNK), lambda b, j: (0, b * nc + j)),
                  _const_spec((1, BRANCH))],
        out_specs=[pl.BlockSpec((ML_CHUNK, BRANCH), row),
                   pl.BlockSpec((None, ML_HEADS, ML_DK, 2 * ML_DV), lambda b, j: (b, 0, 0, 0)),
                   pl.BlockSpec((None, 8, LANES), lambda b, j: (b, 0, 0))],
        out_shape=[jax.ShapeDtypeStruct(q.shape, BF16),
                   jax.ShapeDtypeStruct((nb, ML_HEADS, ML_DK, 2 * ML_DV), F32),
                   jax.ShapeDtypeStruct((nb, 8, LANES), F32)],
        scratch_shapes=[pltpu.VMEM((ML_HEADS, ML_DK, 2 * ML_DV), F32),
                        pltpu.VMEM((8, LANES), F32)],
        compiler_params=_params(("arbitrary", "arbitrary")),
        name="mlstm_chunks",
    )(q, v, kt, mo, gt, gain)


def _swa_kernel(q_ref, ktp_ref, ktc_ref, vp_ref, vc_ref, sink_ref, y_ref):
    j = pl.program_id(1)
    L = WINDOW
    r_i = lax.broadcasted_iota(jnp.int32, (L, 2 * L), 0)
    c_i = lax.broadcasted_iota(jnp.int32, (L, 2 * L), 1)
    lo = jnp.where(j == 0, L, 0)
    allowed = (c_i >= jnp.maximum(r_i, lo)) & (c_i <= r_i + L)
    lane = lax.broadcasted_iota(jnp.int32, (2 * L, LANES), 1)
    low_half = lane < SWA_HD
    out_low = lax.broadcasted_iota(jnp.int32, (L, LANES), 1) < SWA_HD

    v2 = jnp.concatenate([vp_ref[...], vc_ref[...]], axis=0).astype(F32)
    v2s = pltpu.roll(v2, SWA_HD, axis=1)
    zeros_k = jnp.zeros((SWA_HD, 2 * L), BF16)
    for g in range(SWA_KV):
        ks = slice(g * SWA_HD, (g + 1) * SWA_HD)
        kt2 = jnp.concatenate([ktp_ref[ks, :], ktc_ref[ks, :]], axis=1)
        kblk = jnp.concatenate([jnp.concatenate([kt2, zeros_k], axis=0),
                                jnp.concatenate([zeros_k, kt2], axis=0)], axis=1)
        va = jnp.where(low_half, v2 if g == 0 else v2s, 0.0).astype(BF16)
        vb = jnp.where(low_half, 0.0, v2s if g == 0 else v2).astype(BF16)
        for pp in range(2 * g, 2 * g + 2):
            ps = slice(pp * LANES, (pp + 1) * LANES)
            s = _mm(q_ref[:, ps], kblk)
            outs = []
            for t, vmat in enumerate((va, vb)):
                head = 2 * pp + t
                sc = jnp.where(allowed, s[:, t * 2 * L:(t + 1) * 2 * L] * (SWA_HD ** -0.5), NEG_INF)
                sink = sink_ref[head:head + 1, 0:1]
                mx = jnp.maximum(jnp.max(sc, axis=1, keepdims=True), sink)
                e = jnp.exp(sc - mx)
                den = jnp.sum(e, axis=1, keepdims=True) + jnp.exp(sink - mx)
                outs.append(_mm(e.astype(BF16), vmat) / den)
            y_ref[:, ps] = jnp.where(out_low, outs[0], outs[1]).astype(BF16)


def _swa(sq, skt, sv, sinks, nb, seq):
    nblk = seq // WINDOW
    cur_r = lambda b, j: (b * nblk + j, 0)
    prev_r = lambda b, j: (b * nblk + jnp.maximum(j - 1, 0), 0)
    cur_c = lambda b, j: (0, b * nblk + j)
    prev_c = lambda b, j: (0, b * nblk + jnp.maximum(j - 1, 0))
    return pl.pallas_call(
        _swa_kernel,
        grid=(nb, nblk),
        in_specs=[pl.BlockSpec((WINDOW, BRANCH), cur_r),
                  pl.BlockSpec((LANES, WINDOW), prev_c),
                  pl.BlockSpec((LANES, WINDOW), cur_c),
                  pl.BlockSpec((WINDOW, LANES), prev_r),
                  pl.BlockSpec((WINDOW, LANES), cur_r),
                  _const_spec((8, LANES))],
        out_specs=pl.BlockSpec((WINDOW, BRANCH), cur_r),
        out_shape=jax.ShapeDtypeStruct(sq.shape, BF16),
        compiler_params=_params(("arbitrary", "arbitrary")),
        name="swa_prompt",
    )(sq, skt, skt, sv, sv, sinks)


def _cross_kernel(q_ref, kt_ref, v_ref, y_ref):
    for h in range(X_HEADS):
        hs = slice(h * X_HD, (h + 1) * X_HD)
        s = _mm(q_ref[:, hs], kt_ref[hs, :]) * (X_HD ** -0.5)
        mx = jnp.max(s, axis=1, keepdims=True)
        e = jnp.exp(s - mx)
        den = jnp.sum(e, axis=1, keepdims=True)
        y_ref[:, hs] = (_mm(e.astype(BF16), v_ref[:, hs]) / den).astype(BF16)


def _cross(xq, mkt, mv16, nb, seq):
    nt = seq // TQ_CROSS
    row = lambda b, i: (b * nt + i, 0)
    return pl.pallas_call(
        _cross_kernel,
        grid=(nb, nt),
        in_specs=[pl.BlockSpec((TQ_CROSS, BRANCH), row),
                  pl.BlockSpec((None, BRANCH, MEM_TOKENS), lambda b, i: (b, 0, 0)),
                  pl.BlockSpec((MEM_TOKENS, BRANCH), lambda b, i: (b, 0))],
        out_specs=pl.BlockSpec((TQ_CROSS, BRANCH), row),
        out_shape=jax.ShapeDtypeStruct(xq.shape, BF16),
        compiler_params=_params(("arbitrary", "arbitrary")),
        name="cross_prompt",
    )(xq, mkt, mv16)


def _merge_ffn_kernel(x_ref, ya_ref, yb_ref, yc_ref, wgl_ref, wbr_ref, wmix_ref, wfi_ref, wfo_ref,
                      g1_ref, b1_ref, g2_ref, b2_ref, o_ref):
    x = x_ref[...]
    xb = x.astype(BF16)
    acc = None
    for r, y_ref in enumerate((ya_ref, yb_ref, yc_ref)):
        gate = _sigmoid(_mm(xb, wgl_ref[:, r * D_MODEL:(r + 1) * D_MODEL]))
        term = gate * _mm(y_ref[...], wbr_ref[r])
        acc = term if acc is None else acc + term
    mixed = _mm(acc.astype(BF16), wmix_ref[...])
    x1 = _layer_norm(DEEPNORM_ALPHA * x + mixed, g1_ref[...], b1_ref[...])
    x1b = x1.astype(BF16)
    gpre = _mm(x1b, wfi_ref[:, :D_FF])
    up = _mm(x1b, wfi_ref[:, D_FF:])
    act = (gpre * _sigmoid(gpre) * up).astype(BF16)
    ffn = _mm(act, wfo_ref[...])
    o_ref[...] = _layer_norm(DEEPNORM_ALPHA * x1 + ffn, g2_ref[...], b2_ref[...])


def _merge_ffn(x2d, ya, yb, yc, wgl, wbr, wmix, wfi, wfo, g1, b1, g2, b2, tm):
    m = x2d.shape[0]
    row = lambda i: (i, 0)
    vec = _const_spec((1, D_MODEL))
    return pl.pallas_call(
        _merge_ffn_kernel,
        grid=(m // tm,),
        in_specs=[pl.BlockSpec((tm, D_MODEL), row),
                  pl.BlockSpec((tm, BRANCH), row),
                  pl.BlockSpec((tm, BRANCH), row),
                  pl.BlockSpec((tm, BRANCH), row),
                  _const_spec((D_MODEL, 3 * D_MODEL)),
                  _const_spec((3, BRANCH, D_MODEL)),
                  _const_spec((D_MODEL, D_MODEL)),
                  _const_spec((D_MODEL, 2 * D_FF)),
                  _const_spec((D_FF, D_MODEL)),
                  vec, vec, vec, vec],
        out_specs=pl.BlockSpec((tm, D_MODEL), row),
        out_shape=jax.ShapeDtypeStruct((m, D_MODEL), F32),
        compiler_params=_params(("arbitrary",)),
        name="merge_ffn",
    )(x2d, ya, yb, yc, wgl, wbr, wmix, wfi, wfo, g1, b1, g2, b2)


_S_MQ, _S_MK, _S_MV, _S_MO, _S_SQ, _S_XQ, _S_SK, _S_SV, _S_G, _S_END = (
    0, 512, 1024, 1536, 2048, 2560, 3072, 3200, 3328, 3456)


def _dproj_kernel(x_ref, w_ref, bias_ref, cos_ref, sin_ref, o_ref):
    xb = x_ref[...].astype(BF16)
    cos = cos_ref[...]
    sin = sin_ref[...]
    for c in range(_S_END // LANES):
        cs = slice(c * LANES, (c + 1) * LANES)
        blk = _mm(xb, w_ref[:, cs])
        if _S_MK <= c * LANES < _S_MV:
            blk = blk * (ML_DK ** -0.5)
        elif _S_SQ <= c * LANES < _S_XQ or _S_SK <= c * LANES < _S_SV:
            blk = _rope_lanes(blk, cos, sin)
        elif c * LANES == _S_G:
            blk = blk + bias_ref[...]
        o_ref[:, cs] = blk


def _dproj(xs, w, bias_row, cos, sin):
    n = xs.shape[0]
    return pl.pallas_call(
        _dproj_kernel,
        grid=(1,),
        in_specs=[_const_spec((n, D_MODEL)), _const_spec((D_MODEL, _S_END)), _const_spec((1, LANES)),
                  _const_spec((n, LANES)), _const_spec((n, LANES))],
        out_specs=_const_spec((n, _S_END)),
        out_shape=jax.ShapeDtypeStruct((n, _S_END), F32),
        compiler_params=_params(("arbitrary",)),
        name="decode_proj",
    )(xs, w, bias_row, cos, sin)


def _dmlstm_kernel(q_ref, k_ref, v_ref, mo_ref, g_ref, c_ref, n_ref, m_ref, gain_ref,
                   y_ref, c_out_ref, n_out_ref, m_out_ref):
    bb = q_ref.shape[0]
    gts = g_ref[...]
    li = gts[:, 0:ML_HEADS]
    lf = _log_sigmoid(gts[:, ML_HEADS:2 * ML_HEADS])
    m_prev = m_ref[...]
    m_t = jnp.maximum(lf + m_prev, li)
    w_all = jnp.exp(li - m_t)
    a_all = jnp.exp(lf + m_prev - m_t)
    emt_all = jnp.exp(-m_t)
    m_out_ref[...] = m_t
    lane = lax.broadcasted_iota(jnp.int32, (ML_DV, LANES), 1)
    for h in range(ML_HEADS):
        hs = slice(h * ML_DK, (h + 1) * ML_DK)
        qh = q_ref[:, hs]
        kh = k_ref[:, hs]
        vh = v_ref[:, hs]
        nh = n_ref[:, hs]
        w = w_all[:, h:h + 1]
        a = a_all[:, h:h + 1]
        sw = jnp.sum(qh * kh, axis=1, keepdims=True) * w
        den = sw + a * jnp.sum(nh * qh, axis=1, keepdims=True)
        dn = jnp.maximum(jnp.abs(den), emt_all[:, h:h + 1])
        n_out_ref[:, hs] = a * nh + w * kh
        vt = jnp.concatenate([vh, jnp.zeros((LANES - bb, ML_DV), F32)], axis=0).T
        ht = jnp.zeros((ML_DV, LANES), F32)
        for j in range(bb):
            c = c_ref[j, h]
            cq = jnp.sum(c * qh[j:j + 1, :], axis=1, keepdims=True)
            vcol = vt[:, j:j + 1]
            hcol = (sw[j:j + 1, :] * vcol + a[j:j + 1, :] * cq) / dn[j:j + 1, :]
            c_out_ref[j, h] = a[j:j + 1, :] * c + (w[j:j + 1, :] * vcol) * kh[j:j + 1, :]
            ht = jnp.where(lane == j, hcol, ht)
        hh = ht.T[0:bb, :]
        mu = jnp.mean(hh, axis=1, keepdims=True)
        hc = hh - mu
        var = jnp.mean(hc * hc, axis=1, keepdims=True)
        hn = hc * lax.rsqrt(var + HEAD_NORM_EPS) * gain_ref[:, hs]
        y_ref[:, hs] = (_sigmoid(mo_ref[:, hs]) * hn).astype(BF16)


def _dmlstm(proj, c_all, n_all, m_all, gain, layer):
    n = proj.shape[0]
    blk = lambda cidx: pl.BlockSpec((BB, BRANCH), lambda i: (i, cidx))
    return pl.pallas_call(
        _dmlstm_kernel,
        grid=(n // BB,),
        in_specs=[blk(_S_MQ // BRANCH), blk(_S_MK // BRANCH), blk(_S_MV // BRANCH), blk(_S_MO // BRANCH),
                  pl.BlockSpec((BB, LANES), lambda i: (i, _S_G // LANES)),
                  pl.BlockSpec((None, BB, ML_HEADS, ML_DV, ML_DK), lambda i: (layer, i, 0, 0, 0)),
                  pl.BlockSpec((None, BB, BRANCH), lambda i: (layer, i, 0)),
                  pl.BlockSpec((None, BB, ML_HEADS), lambda i: (layer, i, 0)),
                  _const_spec((1, BRANCH))],
        out_specs=[pl.BlockSpec((BB, BRANCH), lambda i: (i, 0)),
                   pl.BlockSpec((BB, ML_HEADS, ML_DV, ML_DK), lambda i: (i, 0, 0, 0)),
                   pl.BlockSpec((BB, BRANCH), lambda i: (i, 0)),
                   pl.BlockSpec((BB, ML_HEADS), lambda i: (i, 0))],
        out_shape=[jax.ShapeDtypeStruct((n, BRANCH), BF16),
                   jax.ShapeDtypeStruct((n, ML_HEADS, ML_DV, ML_DK), F32),
                   jax.ShapeDtypeStruct((n, BRANCH), F32),
                   jax.ShapeDtypeStruct((n, ML_HEADS), F32)],
        compiler_params=_params(("arbitrary",)),
        name="decode_mlstm",
    )(proj, proj, proj, proj, proj, c_all, n_all, m_all, gain)


def _dswa_kernel(q_ref, kn_ref, vn_ref, ck_ref, cv_ref, sink_ref, y_ref, ko_ref, vo_ref):
    bb = q_ref.shape[0]
    row = lax.broadcasted_iota(jnp.int32, (WINDOW, LANES), 0)
    lane1 = lax.broadcasted_iota(jnp.int32, (1, LANES), 1)
    low1 = lane1 < SWA_HD
    scale = SWA_HD ** -0.5
    for j in range(bb):
        ck = ck_ref[j]
        cv = cv_ref[j]
        kn = kn_ref[j:j + 1, :]
        vn = vn_ref[j:j + 1, :]
        ko_ref[j] = jnp.where(row == WINDOW - 1, kn, pltpu.roll(ck, WINDOW - 1, axis=0))
        vo_ref[j] = jnp.where(row == WINDOW - 1, vn, pltpu.roll(cv, WINDOW - 1, axis=0))
        for pp in range(SWA_HEADS // 2):
            g = pp // 2
            qpair = q_ref[j:j + 1, pp * LANES:(pp + 1) * LANES]
            qswap = pltpu.roll(qpair, SWA_HD, axis=1)
            in_g = low1 if g == 0 else jnp.logical_not(low1)
            outs = []
            for t in range(2):
                head = 2 * pp + t
                qm = jnp.where(in_g, qpair if t == g else qswap, 0.0)
                s = jnp.sum(ck * qm, axis=1, keepdims=True) * scale
                s_new = jnp.sum(kn * qm, axis=1, keepdims=True) * scale
                sink = sink_ref[head:head + 1, 0:1]
                mx = jnp.maximum(jnp.maximum(jnp.max(s, axis=0, keepdims=True), s_new), sink)
                e = jnp.exp(s - mx)
                e_new = jnp.exp(s_new - mx)
                den = jnp.sum(e, axis=0, keepdims=True) + e_new + jnp.exp(sink - mx)
                o = (jnp.sum(e * cv, axis=0, keepdims=True) + e_new * vn) / den
                outs.append(o if t == g else pltpu.roll(o, SWA_HD, axis=1))
            y_ref[j:j + 1, pp * LANES:(pp + 1) * LANES] = jnp.where(low1, outs[0], outs[1]).astype(BF16)


def _dswa(proj, ck_all, cv_all, sinks, layer):
    n = proj.shape[0]
    cache = pl.BlockSpec((None, BB, WINDOW, LANES), lambda i: (layer, i, 0, 0))
    outc = pl.BlockSpec((BB, WINDOW, LANES), lambda i: (i, 0, 0))
    return pl.pallas_call(
        _dswa_kernel,
        grid=(n // BB,),
        in_specs=[pl.BlockSpec((BB, BRANCH), lambda i: (i, _S_SQ // BRANCH)),
                  pl.BlockSpec((BB, LANES), lambda i: (i, _S_SK // LANES)),
                  pl.BlockSpec((BB, LANES), lambda i: (i, _S_SV // LANES)),
                  cache, cache, _const_spec((8, LANES))],
        out_specs=[pl.BlockSpec((BB, BRANCH), lambda i: (i, 0)), outc, outc],
        out_shape=[jax.ShapeDtypeStruct((n, BRANCH), BF16),
                   jax.ShapeDtypeStruct((n, WINDOW, LANES), F32),
                   jax.ShapeDtypeStruct((n, WINDOW, LANES), F32)],
        compiler_params=_params(("arbitrary",)),
        name="decode_swa",
    )(proj, proj, proj, ck_all, cv_all, sinks)


def _dcross_kernel(q_ref, k_ref, v_ref, y_ref):
    bb = q_ref.shape[0]
    scale = X_HD ** -0.5
    for j in range(bb):
        for h in range(X_HEADS):
            hs = slice(h * X_HD, (h + 1) * X_HD)
            qh = q_ref[j:j + 1, hs]
            s = jnp.sum(k_ref[j, :, hs] * qh, axis=1, keepdims=True) * scale
            mx = jnp.max(s, axis=0, keepdims=True)
            e = jnp.exp(s - mx)
            den = jnp.sum(e, axis=0, keepdims=True)
            o = jnp.sum(e * v_ref[j, :, hs], axis=0, keepdims=True) / den
            y_ref[j:j + 1, hs] = o.astype(BF16)


def _dcross(proj, mk_all, mv_all, layer):
    n = proj.shape[0]
    cache = pl.BlockSpec((None, BB, MEM_TOKENS, BRANCH), lambda i: (layer, i, 0, 0))
    return pl.pallas_call(
        _dcross_kernel,
        grid=(n // BB,),
        in_specs=[pl.BlockSpec((BB, BRANCH), lambda i: (i, _S_XQ // BRANCH)), cache, cache],
        out_specs=pl.BlockSpec((BB, BRANCH), lambda i: (i, 0)),
        out_shape=jax.ShapeDtypeStruct((n, BRANCH), BF16),
        compiler_params=_params(("arbitrary",)),
        name="decode_cross",
    )(proj, mk_all, mv_all)


def _rope_tables(positions):
    inv_freq = ROPE_THETA ** (-jnp.arange(ROT_HALF, dtype=F32) / ROT_HALF)
    ang = positions.astype(F32)[:, None] * inv_freq[None, :]
    cos = jnp.cos(ang)
    sin = jnp.sin(ang)
    reps = LANES // ROT_HALF
    return jnp.tile(cos, (1, reps)), jnp.tile(sin, (1, reps)), cos.T, sin.T


def kernel(x_prompt, x_sample, mem_prompt, cache_swa_k, cache_swa_v, cache_mem_k, cache_mem_v, state_mlstm_c, state_mlstm_n, state_mlstm_m, w_in, b_gates, mlstm_norm_g, swa_sinks, w_mem_kv, w_branch, w_mix_out, ln1_g, ln1_b, w_ffn_in, w_ffn_out, ln2_g, ln2_b):
    nb, seq, _ = x_prompt.shape
    ns = x_sample.shape[0]

    cosn, sinn, cost, sint = _rope_tables(jnp.arange(seq))
    cos_s, sin_s, _, _ = _rope_tables(jnp.full((ns,), PAST_LEN))

    ck_all = cache_swa_k.reshape(DEPTH, ns, WINDOW, SWA_KV * SWA_HD)
    cv_all = cache_swa_v.reshape(DEPTH, ns, WINDOW, SWA_KV * SWA_HD)
    mk_all = cache_mem_k.reshape(DEPTH, ns, MEM_TOKENS, BRANCH)
    mv_all = cache_mem_v.reshape(DEPTH, ns, MEM_TOKENS, BRANCH)
    n_all = state_mlstm_n.reshape(DEPTH, ns, BRANCH)

    yp = x_prompt.reshape(nb * seq, D_MODEL)
    ys = x_sample.reshape(ns, D_MODEL)
    mem2d = mem_prompt.reshape(nb * MEM_TOKENS, D_MODEL)

    outs = {k: [] for k in ("kp", "vp", "ks", "vs", "mk", "mv", "cp", "np", "mp", "cs", "ns", "ms")}
    for l in range(DEPTH):
        w = w_in[l]
        gates_w = jnp.concatenate([w[:, _C_MI:_C_SQ], jnp.zeros((D_MODEL, 8), F32)], axis=1)
        wn = jnp.concatenate([w[:, _C_MQ:_C_MK], w[:, _C_MV:_C_MO], w[:, _C_MO:_C_MI], w[:, _C_SQ:_C_SK],
                              w[:, _C_XQ:_C_GL], w[:, _C_SK:_C_SV], w[:, _C_SV:_C_XQ]], axis=1).astype(BF16)
        wt = jnp.concatenate([w[:, _C_MK:_C_MV], w[:, _C_SK:_C_SV], gates_w], axis=1).T.astype(BF16)
        ws = jnp.concatenate([w[:, _C_MQ:_C_MI], w[:, _C_SQ:_C_SK], w[:, _C_XQ:_C_GL], w[:, _C_SK:_C_XQ],
                              gates_w, jnp.zeros((D_MODEL, LANES - 16), F32)], axis=1).astype(BF16)
        wgl = w[:, _C_GL:].astype(BF16)
        bias = b_gates[l]
        bias_col = jnp.concatenate([bias, jnp.zeros((8,), F32)]).reshape(16, 1)
        bias_row = jnp.concatenate([bias, jnp.zeros((LANES - 8,), F32)]).reshape(1, LANES)
        gain = mlstm_norm_g[l].reshape(1, BRANCH)
        sinks = jnp.broadcast_to(swa_sinks[l][:, None], (SWA_HEADS, LANES))
        wkv = w_mem_kv[l].astype(BF16)
        wkt = w_mem_kv[l][:, :BRANCH].T.astype(BF16)
        wbr = w_branch[l].astype(BF16)
        wmix = w_mix_out[l].astype(BF16)
        wfi = w_ffn_in[l].astype(BF16)
        wfo = w_ffn_out[l].astype(BF16)
        ln = (ln1_g[l].reshape(1, D_MODEL), ln1_b[l].reshape(1, D_MODEL),
              ln2_g[l].reshape(1, D_MODEL), ln2_b[l].reshape(1, D_MODEL))

        mk32, mv32, mkt, mv16 = _memkv(mem2d, wkv, wkt, nb)
        q, v, mo, sq, xq, sv, kt, skt, gt, k32, v32 = _proj(yp, wn, wt, bias_col, cosn, sinn, cost, sint, nb, seq)
        ya, s_fin, m_fin = _mlstm(q, v, kt, mo, gt, gain, nb, seq)
        yb = _swa(sq, skt, sv, sinks, nb, seq)
        yc = _cross(xq, mkt, mv16, nb, seq)
        yp = _merge_ffn(yp, ya, yb, yc, wgl, wbr, wmix, wfi, wfo, *ln, TM_MERGE)
        outs["kp"].append(k32.reshape(nb, WINDOW, SWA_KV, SWA_HD))
        outs["vp"].append(v32.reshape(nb, WINDOW, SWA_KV, SWA_HD))
        outs["mk"].append(mk32.reshape(nb, MEM_TOKENS, X_HEADS, X_HD))
        outs["mv"].append(mv32.reshape(nb, MEM_TOKENS, X_HEADS, X_HD))
        outs["cp"].append(jnp.swapaxes(s_fin[..., :ML_DV], -1, -2))
        outs["np"].append(s_fin[..., ML_DV])
        outs["mp"].append(m_fin[:, :ML_HEADS, 0])

        proj = _dproj(ys, ws, bias_row, cos_s, sin_s)
        ya_s, c_new, n_new, m_new = _dmlstm(proj, state_mlstm_c, n_all, state_mlstm_m, gain, l)
        yb_s, kb, vb = _dswa(proj, ck_all, cv_all, sinks, l)
        yc_s = _dcross(proj, mk_all, mv_all, l)
        ys = _merge_ffn(ys, ya_s, yb_s, yc_s, wgl, wbr, wmix, wfi, wfo, *ln, ns)
        outs["ks"].append(kb.reshape(ns, WINDOW, SWA_KV, SWA_HD))
        outs["vs"].append(vb.reshape(ns, WINDOW, SWA_KV, SWA_HD))
        outs["cs"].append(c_new)
        outs["ns"].append(n_new.reshape(ns, ML_HEADS, ML_DK))
        outs["ms"].append(m_new)

    st = {k: jnp.stack(vals) for k, vals in outs.items()}
    return (yp.reshape(nb, seq, D_MODEL), ys.reshape(ns, 1, D_MODEL),
            st["kp"], st["vp"], st["ks"], st["vs"], st["mk"], st["mv"],
            st["cp"], st["np"], st["mp"], st["cs"], st["ns"], st["ms"])
```

```python
import functools
import math

import jax
import jax.numpy as jnp
from jax import lax
from jax.experimental import pallas as pl
from jax.experimental.pallas import tpu as pltpu

F32 = jnp.float32
BF16 = jnp.bfloat16

D_MODEL = 1024
DEPTH = 2
BRANCH = 512
ML_HEADS = 4
ML_DK = 128
ML_DV = 128
ML_CHUNK = 128
SWA_HD = 64
SWA_HEADS = 8
SWA_KV = 2
SWA_GROUP = 4
WINDOW = 128
ROT_DIM = 16
ROT_HALF = 8
ROPE_THETA = 500000.0
MEM_TOKENS = 256
X_HEADS = 4
X_HD = 128
D_FF = 2816
LN_EPS = 1e-5
HEAD_NORM_EPS = 1e-6
DEEPNORM_ALPHA = (2 * DEPTH) ** 0.25
NEG_INF = -1e30
PAST_LEN = 8192

LANES = 128
VMEM_LIMIT = 56 * 1024 * 1024

_C_MQ, _C_MK, _C_MV, _C_MO = 0, 512, 1024, 1536
_C_MI, _C_MF = 2048, 2052
_C_SQ, _C_SK, _C_SV, _C_XQ, _C_GL = 2056, 2568, 2696, 2824, 3336

TM_PROJ = 512
TM_MIX = 512
TM_MERGE = 256
TQ_CROSS = 256
BB = 8


def _mm(a, b):
    return jnp.dot(a, b, preferred_element_type=F32)


def _mm_nt(a, b):
    return lax.dot_general(a, b, (((1,), (1,)), ((), ())), preferred_element_type=F32)


def _sigmoid(x):
    return 1.0 / (1.0 + jnp.exp(-x))


def _log_sigmoid(x):
    return jnp.minimum(x, 0.0) - jnp.log(1.0 + jnp.exp(-jnp.abs(x)))


def _layer_norm(x, g, b):
    mu = jnp.mean(x, axis=-1, keepdims=True)
    xc = x - mu
    var = jnp.mean(xc * xc, axis=-1, keepdims=True)
    return xc * lax.rsqrt(var + LN_EPS) * g + b


def _rope_lanes(x, cos, sin):
    lane = lax.broadcasted_iota(jnp.int32, x.shape, 1) % SWA_HD
    up = pltpu.roll(x, LANES - ROT_HALF, axis=1)
    dn = pltpu.roll(x, ROT_HALF, axis=1)
    first = x * cos - up * sin
    second = x * cos + dn * sin
    return jnp.where(lane < ROT_HALF, first, jnp.where(lane < ROT_DIM, second, x))


def _const_spec(shape):
    nd = len(shape)
    return pl.BlockSpec(shape, lambda *_: (0,) * nd, pipeline_mode=pl.Buffered(1))


def _params(sem):
    return pltpu.CompilerParams(dimension_semantics=sem, vmem_limit_bytes=VMEM_LIMIT)


def _memkv_kernel(mem_ref, wkv_ref, wkt_ref, k32_ref, v32_ref, kt_ref, v16_ref):
    m = mem_ref[...].astype(BF16)
    kv = _mm(m, wkv_ref[...])
    k32_ref[...] = kv[:, :BRANCH]
    v32_ref[...] = kv[:, BRANCH:]
    v16_ref[...] = kv[:, BRANCH:].astype(BF16)
    kt_ref[...] = _mm_nt(wkt_ref[...], m).astype(BF16)


def _memkv(mem2d, wkv, wkt, nb):
    rows = mem2d.shape[0]
    return pl.pallas_call(
        _memkv_kernel,
        grid=(nb,),
        in_specs=[pl.BlockSpec((MEM_TOKENS, D_MODEL), lambda b: (b, 0)),
                  _const_spec((D_MODEL, 2 * BRANCH)),
                  _const_spec((BRANCH, D_MODEL))],
        out_specs=[pl.BlockSpec((MEM_TOKENS, BRANCH), lambda b: (b, 0)),
                   pl.BlockSpec((MEM_TOKENS, BRANCH), lambda b: (b, 0)),
                   pl.BlockSpec((None, BRANCH, MEM_TOKENS), lambda b: (b, 0, 0)),
                   pl.BlockSpec((MEM_TOKENS, BRANCH), lambda b: (b, 0))],
        out_shape=[jax.ShapeDtypeStruct((rows, BRANCH), F32),
                   jax.ShapeDtypeStruct((rows, BRANCH), F32),
                   jax.ShapeDtypeStruct((nb, BRANCH, MEM_TOKENS), BF16),
                   jax.ShapeDtypeStruct((rows, BRANCH), BF16)],
        compiler_params=_params(("arbitrary",)),
        name="memkv_proj",
    )(mem2d, wkv, wkt)


_N_MQ, _N_MV, _N_MO, _N_SQ, _N_XQ, _N_SV, _N_END = 0, 512, 1024, 1536, 2048, 2560, 2688
_T_MK, _T_SK, _T_G, _T_END = 0, 512, 640, 656


def _proj_kernel(x_ref, wn_ref, wt_ref, bias_ref, cosn_ref, sinn_ref, cost_ref, sint_ref,
                 q_ref, v_ref, mo_ref, sq_ref, xq_ref, sv_ref, kt_ref, skt_ref, gt_ref, gc_ref, k32_ref, v32_ref):
    tm = x_ref.shape[0]
    xb = x_ref[...].astype(BF16)
    q_ref[...] = _mm(xb, wn_ref[:, _N_MQ:_N_MV]).astype(BF16)
    v_ref[...] = _mm(xb, wn_ref[:, _N_MV:_N_MO]).astype(BF16)
    mo_ref[...] = _mm(xb, wn_ref[:, _N_MO:_N_SQ])
    xq_ref[...] = _mm(xb, wn_ref[:, _N_XQ:_N_SV]).astype(BF16)
    cosn = cosn_ref[...]
    sinn = sinn_ref[...]
    sq = _mm(xb, wn_ref[:, _N_SQ:_N_XQ])
    for c in range(BRANCH // LANES):
        blk = _rope_lanes(sq[:, c * LANES:(c + 1) * LANES], cosn, sinn)
        sq_ref[:, c * LANES:(c + 1) * LANES] = blk.astype(BF16)
    sv = _mm(xb, wn_ref[:, _N_SV:_N_END])
    sv_ref[...] = sv.astype(BF16)
    kt = _mm_nt(wt_ref[_T_MK:_T_SK, :], xb) * (ML_DK ** -0.5)
    kt_ref[...] = kt.astype(BF16)
    skt = _mm_nt(wt_ref[_T_SK:_T_G, :], xb)
    cost = cost_ref[...]
    sint = sint_ref[...]
    tail = slice(tm - WINDOW, tm)
    v32_ref[...] = sv[tail, :].T
    for g in range(SWA_KV):
        base = g * SWA_HD
        x1 = skt[base:base + ROT_HALF, :]
        x2 = skt[base + ROT_HALF:base + ROT_DIM, :]
        pieces = ((base, x1 * cost - x2 * sint),
                  (base + ROT_HALF, x2 * cost + x1 * sint),
                  (base + ROT_DIM, skt[base + ROT_DIM:base + SWA_HD, :]))
        for start, val in pieces:
            skt_ref[start:start + val.shape[0], :] = val.astype(BF16)
            k32_ref[start:start + val.shape[0], :] = val[:, tail]
    g = (_mm_nt(wt_ref[_T_G:_T_END, :], xb) + bias_ref[...])[0:8, :]
    row8 = lax.broadcasted_iota(jnp.int32, g.shape, 0)
    lane8 = lax.broadcasted_iota(jnp.int32, g.shape, 1) % ML_CHUNK
    csum = _log_sigmoid(g)
    shift = 1
    while shift < ML_CHUNK:
        csum = csum + jnp.where(lane8 >= shift, pltpu.roll(csum, shift, axis=1), 0.0)
        shift *= 2
    lib = jnp.where(row8 < ML_HEADS, g, csum)
    gt_ref[...] = lib
    pad = jnp.zeros((LANES - 8, ML_CHUNK), F32)
    for c in range(tm // ML_CHUNK):
        blk = lib[:, c * ML_CHUNK:(c + 1) * ML_CHUNK]
        gc_ref[c * ML_CHUNK:(c + 1) * ML_CHUNK, :] = jnp.concatenate([blk, pad], axis=0).T


def _proj(x2d, wn, wt, bias_col, cosn, sinn, cost, sint, nb, seq):
    m = x2d.shape[0]
    tm = TM_PROJ
    nt = seq // tm
    row = lambda i: (i, 0)
    col = lambda i: (0, i)
    return pl.pallas_call(
        _proj_kernel,
        grid=(m // tm,),
        in_specs=[pl.BlockSpec((tm, D_MODEL), row),
                  _const_spec((D_MODEL, _N_END)),
                  _const_spec((_T_END, D_MODEL)),
                  _const_spec((16, 1)),
                  pl.BlockSpec((tm, LANES), lambda i: (i % nt, 0)),
                  pl.BlockSpec((tm, LANES), lambda i: (i % nt, 0)),
                  pl.BlockSpec((ROT_HALF, tm), lambda i: (0, i % nt)),
                  pl.BlockSpec((ROT_HALF, tm), lambda i: (0, i % nt))],
        out_specs=[pl.BlockSpec((tm, BRANCH), row),
                   pl.BlockSpec((tm, BRANCH), row),
                   pl.BlockSpec((tm, BRANCH), row),
                   pl.BlockSpec((tm, BRANCH), row),
                   pl.BlockSpec((tm, BRANCH), row),
                   pl.BlockSpec((tm, LANES), row),
                   pl.BlockSpec((BRANCH, tm), col),
                   pl.BlockSpec((LANES, tm), col),
                   pl.BlockSpec((8, tm), col),
                   pl.BlockSpec((tm, LANES), row),
                   pl.BlockSpec((LANES, WINDOW), lambda i: (i // nt, 0)),
                   pl.BlockSpec((LANES, WINDOW), lambda i: (i // nt, 0))],
        out_shape=[jax.ShapeDtypeStruct((m, BRANCH), BF16),
                   jax.ShapeDtypeStruct((m, BRANCH), BF16),
                   jax.ShapeDtypeStruct((m, BRANCH), F32),
                   jax.ShapeDtypeStruct((m, BRANCH), BF16),
                   jax.ShapeDtypeStruct((m, BRANCH), BF16),
                   jax.ShapeDtypeStruct((m, LANES), BF16),
                   jax.ShapeDtypeStruct((BRANCH, m), BF16),
                   jax.ShapeDtypeStruct((LANES, m), BF16),
                   jax.ShapeDtypeStruct((8, m), F32),
                   jax.ShapeDtypeStruct((m, LANES), F32),
                   jax.ShapeDtypeStruct((nb * LANES, WINDOW), F32),
                   jax.ShapeDtypeStruct((nb * LANES, WINDOW), F32)],
        compiler_params=_params(("arbitrary",)),
        name="prompt_proj",
    )(x2d, wn, wt, bias_col, cosn, sinn, cost, sint)


def _mlstm_kernel(q_ref, v_ref, kt_ref, mo_ref, gt_ref, gc_ref, gain_ref, y_ref, s_out_ref, m_out_ref, s_ref, m_ref):
    j = pl.program_id(1)
    L = ML_CHUNK

    @pl.when(j == 0)
    def _():
        s_ref[...] = jnp.zeros_like(s_ref)
        m_ref[...] = jnp.zeros_like(m_ref)

    r_i = lax.broadcasted_iota(jnp.int32, (L, L), 0)
    c_i = lax.broadcasted_iota(jnp.int32, (L, L), 1)
    causal = c_i <= r_i

    units = [(c, h) for c in range(q_ref.shape[0] // L) for h in range(ML_HEADS)]
    ts = lambda c: slice(c * L, (c + 1) * L)
    hs = lambda h: slice(h * ML_DK, (h + 1) * ML_DK)

    qk = {(c, h): _mm(q_ref[ts(c), hs(h)], kt_ref[hs(h), ts(c)]) for c, h in units}

    m_run = [m_ref[h:h + 1, 0:1] for h in range(ML_HEADS)]
    sw, upd, a_col, ec_s, inv_floor = {}, {}, {}, {}, {}
    for c, h in units:
        lib = gt_ref[:, ts(c)]
        cols = gc_ref[ts(c), :]
        li_r = lib[h:h + 1, :]
        b_r = lib[ML_HEADS + h:ML_HEADS + h + 1, :]
        li_c = cols[:, h:h + 1]
        b_c = cols[:, ML_HEADS + h:ML_HEADS + h + 1]
        m_prev = m_run[h]
        d = jnp.where(causal, b_c - b_r + li_r, -jnp.inf)
        inter = b_c + m_prev
        m_t = jnp.maximum(inter, jnp.max(d, axis=1, keepdims=True))
        sw[c, h] = qk[c, h] * jnp.exp(d - m_t)
        a_col[c, h] = jnp.exp(inter - m_t)
        inv_floor[c, h] = jnp.exp(-m_t)
        b_last = b_r[:, L - 1:L]
        m_new = jnp.maximum(b_last + m_prev, jnp.max(b_last - b_r + li_r, axis=1, keepdims=True))
        ec_s[c, h] = jnp.exp(b_last + m_prev - m_new)
        es_c = jnp.exp(b_last - b_c + li_c - m_new)
        upd[c, h] = jnp.concatenate([es_c * v_ref[ts(c), hs(h)].astype(F32), jnp.where(c_i == 0, es_c, 0.0)],
                                    axis=1).astype(BF16)
        m_run[h] = m_new
    for h in range(ML_HEADS):
        m_ref[h:h + 1, :] = jnp.broadcast_to(m_run[h], (1, L))

    intra = {u: _mm(sw[u].astype(BF16), v_ref[ts(u[0]), hs(u[1])]) for u in units}
    delta = {u: _mm(kt_ref[hs(u[1]), ts(u[0])], upd[u]) for u in units}

    for c, h in units:
        state = s_ref[h]
        qs = _mm(q_ref[ts(c), hs(h)], state.astype(BF16))
        s_ref[h] = ec_s[c, h] * state + delta[c, h]
        a = a_col[c, h]
        num = intra[c, h] + a * qs[:, :ML_DV]
        den = jnp.sum(sw[c, h], axis=1, keepdims=True) + a * qs[:, ML_DV:ML_DV + 1]
        hh = num * (1.0 / jnp.maximum(jnp.abs(den), inv_floor[c, h]))
        mu = jnp.mean(hh, axis=1, keepdims=True)
        hc = hh - mu
        var = jnp.mean(hc * hc, axis=1, keepdims=True)
        hn = hc * lax.rsqrt(var + HEAD_NORM_EPS) * gain_ref[:, hs(h)]
        y_ref[ts(c), hs(h)] = (_sigmoid(mo_ref[ts(c), hs(h)]) * hn).astype(BF16)

    @pl.when(j == pl.num_programs(1) - 1)
    def _():
        s_out_ref[...] = s_ref[...]
        m_out_ref[...] = m_ref[...]


def _mlstm(q, v, kt, mo, gt, gc, gain, nb, seq):
    nc = seq // TM_MIX
    row = lambda b, j: (b * nc + j, 0)
    col = lambda b, j: (0, b * nc + j)
    return pl.pallas_call(
        _mlstm_kernel,
        grid=(nb, nc),
        in_specs=[pl.BlockSpec((TM_MIX, BRANCH), row),
                  pl.BlockSpec((TM_MIX, BRANCH), row),
                  pl.BlockSpec((BRANCH, TM_MIX), col),
                  pl.BlockSpec((TM_MIX, BRANCH), row),
                  pl.BlockSpec((8, TM_MIX), col),
                  pl.BlockSpec((TM_MIX, LANES), row),
                  _const_spec((1, BRANCH))],
        out_specs=[pl.BlockSpec((TM_MIX, BRANCH), row),
                   pl.BlockSpec((None, ML_HEADS, ML_DK, 2 * ML_DV), lambda b, j: (b, 0, 0, 0)),
                   pl.BlockSpec((None, 8, LANES), lambda b, j: (b, 0, 0))],
        out_shape=[jax.ShapeDtypeStruct(q.shape, BF16),
                   jax.ShapeDtypeStruct((nb, ML_HEADS, ML_DK, 2 * ML_DV), F32),
                   jax.ShapeDtypeStruct((nb, 8, LANES), F32)],
        scratch_shapes=[pltpu.VMEM((ML_HEADS, ML_DK, 2 * ML_DV), F32),
                        pltpu.VMEM((8, LANES), F32)],
        compiler_params=_params(("arbitrary", "arbitrary")),
        name="mlstm_chunks",
    )(q, v, kt, mo, gt, gc, gain)


def _swa_kernel(q_ref, ktp_ref, ktc_ref, vp_ref, vc_ref, sink_ref, y_ref):
    j = pl.program_id(1)
    L = WINDOW
    nblk = q_ref.shape[0] // L
    r_i = lax.broadcasted_iota(jnp.int32, (L, 2 * L), 0)
    c_i = lax.broadcasted_iota(jnp.int32, (L, 2 * L), 1)
    band = (c_i >= r_i) & (c_i <= r_i + L)
    first = band & (c_i >= jnp.where(j == 0, L, 0))
    low_half = lax.broadcasted_iota(jnp.int32, (2 * L, LANES), 1) < SWA_HD
    out_low = lax.broadcasted_iota(jnp.int32, (L, LANES), 1) < SWA_HD
    zeros_k = jnp.zeros((SWA_HD, 2 * L), BF16)

    kt_all = jnp.concatenate([ktp_ref[...], ktc_ref[...]], axis=1)
    v_all = jnp.concatenate([vp_ref[...], vc_ref[...]], axis=0).astype(F32)
    v_swap = pltpu.roll(v_all, SWA_HD, axis=1)
    def scores(c):
        win = slice(c * L, (c + 2) * L)
        out = []
        for g in range(SWA_KV):
            kt2 = kt_all[g * SWA_HD:(g + 1) * SWA_HD, win]
            kblk = jnp.concatenate([jnp.concatenate([kt2, zeros_k], axis=0),
                                    jnp.concatenate([zeros_k, kt2], axis=0)], axis=1)
            for pp in range(2 * g, 2 * g + 2):
                out.append(_mm(q_ref[c * L:(c + 1) * L, pp * LANES:(pp + 1) * LANES], kblk))
        return out

    def weights(c, s_list):
        allowed = first if c == 0 else band
        out = []
        for head in range(SWA_HEADS):
            s = s_list[head // 2][:, (head % 2) * 2 * L:(head % 2 + 1) * 2 * L]
            sc = jnp.where(allowed, s * (SWA_HD ** -0.5), NEG_INF)
            sink = sink_ref[head:head + 1, 0:1]
            mx = jnp.maximum(jnp.max(sc, axis=1, keepdims=True), sink)
            e = jnp.exp(sc - mx)
            den = jnp.sum(e, axis=1, keepdims=True) + jnp.exp(sink - mx)
            out.append((e.astype(BF16), 1.0 / den))
        return out

    def outputs(c, e_list):
        win = slice(c * L, (c + 2) * L)
        v2 = v_all[win, :]
        v2s = v_swap[win, :]
        for g in range(SWA_KV):
            va = jnp.where(low_half, v2 if g == 0 else v2s, 0.0).astype(BF16)
            vb = jnp.where(low_half, 0.0, v2s if g == 0 else v2).astype(BF16)
            for pp in range(2 * g, 2 * g + 2):
                (e0, r0), (e1, r1) = e_list[2 * pp], e_list[2 * pp + 1]
                pair = jnp.where(out_low, _mm(e0, va) * r0, _mm(e1, vb) * r1)
                y_ref[c * L:(c + 1) * L, pp * LANES:(pp + 1) * LANES] = pair.astype(BF16)

    s_next = scores(0)
    for c in range(nblk):
        s_cur = s_next
        if c + 1 < nblk:
            s_next = scores(c + 1)
        outputs(c, weights(c, s_cur))


def _swa(sq, skt, sv, sinks, nb, seq):
    nt = seq // TM_MIX
    per = TM_MIX // WINDOW
    cur_r = lambda b, j: (b * nt + j, 0)
    cur_c = lambda b, j: (0, b * nt + j)
    prev_r = lambda b, j: (jnp.maximum((b * nt + j) * per - 1, b * nt * per), 0)
    prev_c = lambda b, j: (0, jnp.maximum((b * nt + j) * per - 1, b * nt * per))
    return pl.pallas_call(
        _swa_kernel,
        grid=(nb, nt),
        in_specs=[pl.BlockSpec((TM_MIX, BRANCH), cur_r),
                  pl.BlockSpec((LANES, WINDOW), prev_c),
                  pl.BlockSpec((LANES, TM_MIX), cur_c),
                  pl.BlockSpec((WINDOW, LANES), prev_r),
                  pl.BlockSpec((TM_MIX, LANES), cur_r),
                  _const_spec((8, LANES))],
        out_specs=pl.BlockSpec((TM_MIX, BRANCH), cur_r),
        out_shape=jax.ShapeDtypeStruct(sq.shape, BF16),
        compiler_params=_params(("arbitrary", "arbitrary")),
        name="swa_prompt",
    )(sq, skt, skt, sv, sv, sinks)


def _cross_kernel(q_ref, kt_ref, v_ref, y_ref):
    hs = lambda h: slice(h * X_HD, (h + 1) * X_HD)
    scores = [_mm(q_ref[:, hs(h)], kt_ref[hs(h), :]) for h in range(X_HEADS)]
    probs = []
    for s in scores:
        s = s * (X_HD ** -0.5)
        e = jnp.exp(s - jnp.max(s, axis=1, keepdims=True))
        probs.append((e.astype(BF16), 1.0 / jnp.sum(e, axis=1, keepdims=True)))
    for h, (e, r) in enumerate(probs):
        y_ref[:, hs(h)] = (_mm(e, v_ref[:, hs(h)]) * r).astype(BF16)


def _cross(xq, mkt, mv16, nb, seq):
    nt = seq // TQ_CROSS
    row = lambda b, i: (b * nt + i, 0)
    return pl.pallas_call(
        _cross_kernel,
        grid=(nb, nt),
        in_specs=[pl.BlockSpec((TQ_CROSS, BRANCH), row),
                  pl.BlockSpec((None, BRANCH, MEM_TOKENS), lambda b, i: (b, 0, 0)),
                  pl.BlockSpec((MEM_TOKENS, BRANCH), lambda b, i: (b, 0))],
        out_specs=pl.BlockSpec((TQ_CROSS, BRANCH), row),
        out_shape=jax.ShapeDtypeStruct(xq.shape, BF16),
        compiler_params=_params(("arbitrary", "arbitrary")),
        name="cross_prompt",
    )(xq, mkt, mv16)


def _merge_ffn_kernel(x_ref, ya_ref, yb_ref, yc_ref, wgl_ref, wbr_ref, wmix_ref, wfi_ref, wfo_ref,
                      g1_ref, b1_ref, g2_ref, b2_ref, o_ref):
    x = x_ref[...]
    xb = x.astype(BF16)
    acc = None
    for r, y_ref in enumerate((ya_ref, yb_ref, yc_ref)):
        gate = _sigmoid(_mm(xb, wgl_ref[:, r * D_MODEL:(r + 1) * D_MODEL]))
        term = gate * _mm(y_ref[...], wbr_ref[r])
        acc = term if acc is None else acc + term
    mixed = _mm(acc.astype(BF16), wmix_ref[...])
    x1 = _layer_norm(DEEPNORM_ALPHA * x + mixed, g1_ref[...], b1_ref[...])
    x1b = x1.astype(BF16)
    gpre = _mm(x1b, wfi_ref[:, :D_FF])
    up = _mm(x1b, wfi_ref[:, D_FF:])
    act = (gpre * _sigmoid(gpre) * up).astype(BF16)
    ffn = _mm(act, wfo_ref[...])
    o_ref[...] = _layer_norm(DEEPNORM_ALPHA * x1 + ffn, g2_ref[...], b2_ref[...])


def _merge_ffn(x2d, ya, yb, yc, wgl, wbr, wmix, wfi, wfo, g1, b1, g2, b2, tm):
    m = x2d.shape[0]
    row = lambda i: (i, 0)
    vec = _const_spec((1, D_MODEL))
    return pl.pallas_call(
        _merge_ffn_kernel,
        grid=(m // tm,),
        in_specs=[pl.BlockSpec((tm, D_MODEL), row),
                  pl.BlockSpec((tm, BRANCH), row),
                  pl.BlockSpec((tm, BRANCH), row),
                  pl.BlockSpec((tm, BRANCH), row),
                  _const_spec((D_MODEL, 3 * D_MODEL)),
                  _const_spec((3, BRANCH, D_MODEL)),
                  _const_spec((D_MODEL, D_MODEL)),
                  _const_spec((D_MODEL, 2 * D_FF)),
                  _const_spec((D_FF, D_MODEL)),
                  vec, vec, vec, vec],
        out_specs=pl.BlockSpec((tm, D_MODEL), row),
        out_shape=jax.ShapeDtypeStruct((m, D_MODEL), F32),
        compiler_params=_params(("arbitrary",)),
        name="merge_ffn",
    )(x2d, ya, yb, yc, wgl, wbr, wmix, wfi, wfo, g1, b1, g2, b2)


_S_MQ, _S_MK, _S_MV, _S_MO, _S_SQ, _S_XQ, _S_SK, _S_SV, _S_G, _S_END = (
    0, 512, 1024, 1536, 2048, 2560, 3072, 3200, 3328, 3456)


def _dproj_kernel(x_ref, w_ref, wt_ref, bias_ref, cos_ref, sin_ref, cost_ref, sint_ref, o_ref, ot_ref):
    xb = x_ref[...].astype(BF16)
    cos = cos_ref[...]
    sin = sin_ref[...]
    kvt = _mm_nt(wt_ref[...], xb)
    cost = cost_ref[...]
    sint = sint_ref[...]
    for g in range(SWA_KV):
        base = g * SWA_HD
        x1 = kvt[base:base + ROT_HALF, :]
        x2 = kvt[base + ROT_HALF:base + ROT_DIM, :]
        ot_ref[base:base + ROT_HALF, :] = x1 * cost - x2 * sint
        ot_ref[base + ROT_HALF:base + ROT_DIM, :] = x2 * cost + x1 * sint
        ot_ref[base + ROT_DIM:base + SWA_HD, :] = kvt[base + ROT_DIM:base + SWA_HD, :]
    ot_ref[LANES:, :] = kvt[LANES:, :]
    for c in range(_S_END // LANES):
        cs = slice(c * LANES, (c + 1) * LANES)
        blk = _mm(xb, w_ref[:, cs])
        if _S_MK <= c * LANES < _S_MV:
            blk = blk * (ML_DK ** -0.5)
        elif _S_SQ <= c * LANES < _S_XQ or _S_SK <= c * LANES < _S_SV:
            blk = _rope_lanes(blk, cos, sin)
        elif c * LANES == _S_G:
            blk = blk + bias_ref[...]
        o_ref[:, cs] = blk


def _dproj(xs, w, wt, bias_row, cos, sin, cost, sint):
    n = xs.shape[0]
    return pl.pallas_call(
        _dproj_kernel,
        grid=(1,),
        in_specs=[_const_spec((n, D_MODEL)), _const_spec((D_MODEL, _S_END)), _const_spec((2 * LANES, D_MODEL)),
                  _const_spec((1, LANES)), _const_spec((n, LANES)), _const_spec((n, LANES)),
                  _const_spec((ROT_HALF, n)), _const_spec((ROT_HALF, n))],
        out_specs=[pl.BlockSpec((n, _S_END), lambda i: (0, 0)), pl.BlockSpec((2 * LANES, n), lambda i: (0, 0))],
        out_shape=[jax.ShapeDtypeStruct((n, _S_END), F32), jax.ShapeDtypeStruct((2 * LANES, n), F32)],
        compiler_params=_params(("arbitrary",)),
        name="decode_proj",
    )(xs, w, wt, bias_row, cos, sin, cost, sint)


def _dmlstm_kernel(q_ref, k_ref, v_ref, mo_ref, g_ref, c_ref, n_ref, m_ref, gain_ref,
                   y_ref, c_out_ref, n_out_ref, m_out_ref):
    bb = q_ref.shape[0]
    gts = g_ref[...]
    li = gts[:, 0:ML_HEADS]
    lf = _log_sigmoid(gts[:, ML_HEADS:2 * ML_HEADS])
    m_prev = m_ref[...]
    m_t = jnp.maximum(lf + m_prev, li)
    w_all = jnp.exp(li - m_t)
    a_all = jnp.exp(lf + m_prev - m_t)
    emt_all = jnp.exp(-m_t)
    m_out_ref[...] = m_t
    lane = lax.broadcasted_iota(jnp.int32, (ML_DV, LANES), 1)
    for h in range(ML_HEADS):
        hs = slice(h * ML_DK, (h + 1) * ML_DK)
        qh = q_ref[:, hs]
        kh = k_ref[:, hs]
        vh = v_ref[:, hs]
        nh = n_ref[:, hs]
        w = w_all[:, h:h + 1]
        a = a_all[:, h:h + 1]
        sw = jnp.sum(qh * kh, axis=1, keepdims=True) * w
        den = sw + a * jnp.sum(nh * qh, axis=1, keepdims=True)
        dn = jnp.maximum(jnp.abs(den), emt_all[:, h:h + 1])
        n_out_ref[:, hs] = a * nh + w * kh
        vt = jnp.concatenate([vh, jnp.zeros((LANES - bb, ML_DV), F32)], axis=0).T
        ht = jnp.zeros((ML_DV, LANES), F32)
        for j in range(bb):
            c = c_ref[j, h]
            cq = jnp.sum(c * qh[j:j + 1, :], axis=1, keepdims=True)
            vcol = vt[:, j:j + 1]
            hcol = (sw[j:j + 1, :] * vcol + a[j:j + 1, :] * cq) / dn[j:j + 1, :]
            c_out_ref[j, h] = a[j:j + 1, :] * c + (w[j:j + 1, :] * vcol) * kh[j:j + 1, :]
            ht = jnp.where(lane == j, hcol, ht)
        hh = ht.T[0:bb, :]
        mu = jnp.mean(hh, axis=1, keepdims=True)
        hc = hh - mu
        var = jnp.mean(hc * hc, axis=1, keepdims=True)
        hn = hc * lax.rsqrt(var + HEAD_NORM_EPS) * gain_ref[:, hs]
        y_ref[:, hs] = (_sigmoid(mo_ref[:, hs]) * hn).astype(BF16)


def _dmlstm(proj, c_all, n_all, m_all, gain, layer):
    n = proj.shape[0]
    blk = lambda cidx: pl.BlockSpec((BB, BRANCH), lambda i: (i, cidx))
    return pl.pallas_call(
        _dmlstm_kernel,
        grid=(n // BB,),
        in_specs=[blk(_S_MQ // BRANCH), blk(_S_MK // BRANCH), blk(_S_MV // BRANCH), blk(_S_MO // BRANCH),
                  pl.BlockSpec((BB, LANES), lambda i: (i, _S_G // LANES)),
                  pl.BlockSpec((None, BB, ML_HEADS, ML_DV, ML_DK), lambda i: (layer, i, 0, 0, 0)),
                  pl.BlockSpec((None, BB, BRANCH), lambda i: (layer, i, 0)),
                  pl.BlockSpec((None, BB, ML_HEADS), lambda i: (layer, i, 0)),
                  _const_spec((1, BRANCH))],
        out_specs=[pl.BlockSpec((BB, BRANCH), lambda i: (i, 0)),
                   pl.BlockSpec((BB, ML_HEADS, ML_DV, ML_DK), lambda i: (i, 0, 0, 0)),
                   pl.BlockSpec((BB, BRANCH), lambda i: (i, 0)),
                   pl.BlockSpec((BB, ML_HEADS), lambda i: (i, 0))],
        out_shape=[jax.ShapeDtypeStruct((n, BRANCH), BF16),
                   jax.ShapeDtypeStruct((n, ML_HEADS, ML_DV, ML_DK), F32),
                   jax.ShapeDtypeStruct((n, BRANCH), F32),
                   jax.ShapeDtypeStruct((n, ML_HEADS), F32)],
        compiler_params=_params(("arbitrary",)),
        name="decode_mlstm",
    )(proj, proj, proj, proj, proj, c_all, n_all, m_all, gain)


def _dswa_kernel(q_ref, kn_ref, vn_ref, kvt_ref, ck_ref, cv_ref, sink_ref, y_ref, ko_ref, vo_ref):
    i = pl.program_id(0)
    bb = q_ref.shape[0]
    lane = lax.broadcasted_iota(jnp.int32, (LANES, LANES), 1)
    row8 = lax.broadcasted_iota(jnp.int32, (SWA_HEADS, LANES), 0)
    low8 = lax.broadcasted_iota(jnp.int32, (SWA_HEADS, LANES), 1) < SWA_HD
    low1 = lax.broadcasted_iota(jnp.int32, (1, LANES), 1) < SWA_HD
    scale = SWA_HD ** -0.5
    sink = sink_ref[:, 0:1]
    knew_t = kvt_ref[0:LANES, :]
    vnew_t = kvt_ref[LANES:, :]
    for j in range(bb):
        ck = ck_ref[j]
        cv = cv_ref[j]
        kn = kn_ref[j:j + 1, :]
        vn = vn_ref[j:j + 1, :]
        qm = jnp.zeros((SWA_HEADS, LANES), F32)
        for pp in range(SWA_HEADS // 2):
            g = pp // 2
            pair = q_ref[j:j + 1, pp * LANES:(pp + 1) * LANES]
            swap = pltpu.roll(pair, SWA_HD, axis=1)
            in_g = low8 if g == 0 else jnp.logical_not(low8)
            for t in range(2):
                qm = jnp.where((row8 == 2 * pp + t) & in_g, pair if t == g else swap, qm)
        s = _mm(qm.astype(BF16), ck.astype(BF16)) * scale
        s_new = jnp.sum(qm * kn, axis=1, keepdims=True) * scale
        mx = jnp.maximum(jnp.maximum(jnp.max(s, axis=1, keepdims=True), s_new), sink)
        e = jnp.exp(s - mx)
        e_new = jnp.exp(s_new - mx)
        den = jnp.sum(e, axis=1, keepdims=True) + e_new + jnp.exp(sink - mx)
        o = (_mm_nt(e.astype(BF16), cv.astype(BF16)) + e_new * vn) / den
        for pp in range(SWA_HEADS // 2):
            g = pp // 2
            halves = []
            for t in range(2):
                oh = o[2 * pp + t:2 * pp + t + 1, :]
                halves.append(oh if t == g else pltpu.roll(oh, SWA_HD, axis=1))
            y_ref[j:j + 1, pp * LANES:(pp + 1) * LANES] = jnp.where(low1, halves[0], halves[1]).astype(BF16)
        bring = LANES - 1 - (i * bb + j)
        ko_ref[j] = jnp.where(lane == LANES - 1, pltpu.roll(knew_t, bring, axis=1), pltpu.roll(ck, LANES - 1, axis=1))
        vo_ref[j] = jnp.where(lane == LANES - 1, pltpu.roll(vnew_t, bring, axis=1), pltpu.roll(cv, LANES - 1, axis=1))


def _dswa(proj, kvt, ck_all, cv_all, sinks, layer):
    n = proj.shape[0]
    cache = pl.BlockSpec((None, BB, LANES, WINDOW), lambda i: (layer, i, 0, 0))
    outc = pl.BlockSpec((BB, LANES, WINDOW), lambda i: (i, 0, 0))
    return pl.pallas_call(
        _dswa_kernel,
        grid=(n // BB,),
        in_specs=[pl.BlockSpec((BB, BRANCH), lambda i: (i, _S_SQ // BRANCH)),
                  pl.BlockSpec((BB, LANES), lambda i: (i, _S_SK // LANES)),
                  pl.BlockSpec((BB, LANES), lambda i: (i, _S_SV // LANES)),
                  _const_spec((2 * LANES, n)),
                  cache, cache, _const_spec((8, LANES))],
        out_specs=[pl.BlockSpec((BB, BRANCH), lambda i: (i, 0)), outc, outc],
        out_shape=[jax.ShapeDtypeStruct((n, BRANCH), BF16),
                   jax.ShapeDtypeStruct((n, LANES, WINDOW), F32),
                   jax.ShapeDtypeStruct((n, LANES, WINDOW), F32)],
        compiler_params=_params(("arbitrary",)),
        name="decode_swa",
    )(proj, proj, proj, kvt, ck_all, cv_all, sinks)


def _dcross_kernel(q_ref, k_ref, v_ref, y_ref):
    bb = q_ref.shape[0]
    scale = X_HD ** -0.5
    row8 = lax.broadcasted_iota(jnp.int32, (8, LANES), 0) % X_HEADS
    for j in range(bb):
        qrep = jnp.zeros((8, LANES), F32)
        for h in range(X_HEADS):
            qrep = jnp.where(row8 == h, q_ref[j:j + 1, h * X_HD:(h + 1) * X_HD], qrep)
        s = jnp.sum(k_ref[j] * qrep[None], axis=2, keepdims=True) * scale
        mx8 = jnp.max(s, axis=0)
        mx4 = jnp.maximum(mx8[0:X_HEADS], mx8[X_HEADS:])
        e = jnp.exp(s - jnp.concatenate([mx4, mx4], axis=0)[None])
        den8 = jnp.sum(e, axis=0)
        o8 = jnp.sum(e * v_ref[j], axis=0)
        o4 = (o8[0:X_HEADS] + o8[X_HEADS:]) / (den8[0:X_HEADS] + den8[X_HEADS:])
        for h in range(X_HEADS):
            y_ref[j:j + 1, h * X_HD:(h + 1) * X_HD] = o4[h:h + 1, :].astype(BF16)


def _dcross(proj, mk_all, mv_all, layer):
    n = proj.shape[0]
    cache = pl.BlockSpec((None, BB, MEM_TOKENS * X_HEADS // 8, 8, X_HD), lambda i: (layer, i, 0, 0, 0))
    return pl.pallas_call(
        _dcross_kernel,
        grid=(n // BB,),
        in_specs=[pl.BlockSpec((BB, BRANCH), lambda i: (i, _S_XQ // BRANCH)), cache, cache],
        out_specs=pl.BlockSpec((BB, BRANCH), lambda i: (i, 0)),
        out_shape=jax.ShapeDtypeStruct((n, BRANCH), BF16),
        compiler_params=_params(("arbitrary",)),
        name="decode_cross",
    )(proj, mk_all, mv_all)


def _rope_tables(positions):
    inv_freq = ROPE_THETA ** (-jnp.arange(ROT_HALF, dtype=F32) / ROT_HALF)
    ang = positions.astype(F32)[:, None] * inv_freq[None, :]
    cos = jnp.cos(ang)
    sin = jnp.sin(ang)
    reps = LANES // ROT_HALF
    return jnp.tile(cos, (1, reps)), jnp.tile(sin, (1, reps)), cos.T, sin.T


def kernel(x_prompt, x_sample, mem_prompt, cache_swa_k, cache_swa_v, cache_mem_k, cache_mem_v, state_mlstm_c, state_mlstm_n, state_mlstm_m, w_in, b_gates, mlstm_norm_g, swa_sinks, w_mem_kv, w_branch, w_mix_out, ln1_g, ln1_b, w_ffn_in, w_ffn_out, ln2_g, ln2_b):
    nb, seq, _ = x_prompt.shape
    ns = x_sample.shape[0]

    cosn, sinn, cost, sint = _rope_tables(jnp.arange(seq))
    cos_s, sin_s, cost_s, sint_s = _rope_tables(jnp.full((ns,), PAST_LEN))
    assert ns == LANES and PAST_LEN >= WINDOW

    ck_all = jnp.transpose(cache_swa_k, (0, 1, 3, 4, 2)).reshape(DEPTH, ns, SWA_KV * SWA_HD, WINDOW)
    cv_all = jnp.transpose(cache_swa_v, (0, 1, 3, 4, 2)).reshape(DEPTH, ns, SWA_KV * SWA_HD, WINDOW)
    mk_all = cache_mem_k.reshape(DEPTH, ns, MEM_TOKENS * X_HEADS // 8, 8, X_HD)
    mv_all = cache_mem_v.reshape(DEPTH, ns, MEM_TOKENS * X_HEADS // 8, 8, X_HD)
    n_all = state_mlstm_n.reshape(DEPTH, ns, BRANCH)

    yp = x_prompt.reshape(nb * seq, D_MODEL)
    ys = x_sample.reshape(ns, D_MODEL)
    mem2d = mem_prompt.reshape(nb * MEM_TOKENS, D_MODEL)

    outs = {k: [] for k in ("kp", "vp", "ks", "vs", "mk", "mv", "cp", "np", "mp", "cs", "ns", "ms")}
    for l in range(DEPTH):
        w = w_in[l]
        gates_w = jnp.concatenate([w[:, _C_MI:_C_SQ], jnp.zeros((D_MODEL, 8), F32)], axis=1)
        wn = jnp.concatenate([w[:, _C_MQ:_C_MK], w[:, _C_MV:_C_MO], w[:, _C_MO:_C_MI], w[:, _C_SQ:_C_SK],
                              w[:, _C_XQ:_C_GL], w[:, _C_SV:_C_XQ]], axis=1).astype(BF16)
        wt = jnp.concatenate([w[:, _C_MK:_C_MV], w[:, _C_SK:_C_SV], gates_w], axis=1).T.astype(BF16)
        ws = jnp.concatenate([w[:, _C_MQ:_C_MI], w[:, _C_SQ:_C_SK], w[:, _C_XQ:_C_GL], w[:, _C_SK:_C_XQ],
                              gates_w, jnp.zeros((D_MODEL, LANES - 16), F32)], axis=1).astype(BF16)
        wgl = w[:, _C_GL:].astype(BF16)
        bias = b_gates[l]
        bias_col = jnp.concatenate([bias, jnp.zeros((8,), F32)]).reshape(16, 1)
        bias_row = jnp.concatenate([bias, jnp.zeros((LANES - 8,), F32)]).reshape(1, LANES)
        gain = mlstm_norm_g[l].reshape(1, BRANCH)
        sinks = jnp.broadcast_to(swa_sinks[l][:, None], (SWA_HEADS, LANES))
        wkv = w_mem_kv[l].astype(BF16)
        wkt = w_mem_kv[l][:, :BRANCH].T.astype(BF16)
        wbr = w_branch[l].astype(BF16)
        wmix = w_mix_out[l].astype(BF16)
        wfi = w_ffn_in[l].astype(BF16)
        wfo = w_ffn_out[l].astype(BF16)
        ln = (ln1_g[l].reshape(1, D_MODEL), ln1_b[l].reshape(1, D_MODEL),
              ln2_g[l].reshape(1, D_MODEL), ln2_b[l].reshape(1, D_MODEL))

        mk32, mv32, mkt, mv16 = _memkv(mem2d, wkv, wkt, nb)
        q, v, mo, sq, xq, sv, kt, skt, gt, gc, k32, v32 = _proj(yp, wn, wt, bias_col, cosn, sinn, cost, sint, nb, seq)
        ya, s_fin, m_fin = _mlstm(q, v, kt, mo, gt, gc, gain, nb, seq)
        yb = _swa(sq, skt, sv, sinks, nb, seq)
        yc = _cross(xq, mkt, mv16, nb, seq)
        yp = _merge_ffn(yp, ya, yb, yc, wgl, wbr, wmix, wfi, wfo, *ln, TM_MERGE)
        outs["kp"].append(jnp.transpose(k32.reshape(nb, SWA_KV, SWA_HD, WINDOW), (0, 3, 1, 2)))
        outs["vp"].append(jnp.transpose(v32.reshape(nb, SWA_KV, SWA_HD, WINDOW), (0, 3, 1, 2)))
        outs["mk"].append(mk32.reshape(nb, MEM_TOKENS, X_HEADS, X_HD))
        outs["mv"].append(mv32.reshape(nb, MEM_TOKENS, X_HEADS, X_HD))
        outs["cp"].append(jnp.swapaxes(s_fin[..., :ML_DV], -1, -2))
        outs["np"].append(s_fin[..., ML_DV])
        outs["mp"].append(m_fin[:, :ML_HEADS, 0])

        wts = w[:, _C_SK:_C_XQ].T.astype(BF16)
        proj, kvt = _dproj(ys, ws, wts, bias_row, cos_s, sin_s, cost_s, sint_s)
        ya_s, c_new, n_new, m_new = _dmlstm(proj, state_mlstm_c, n_all, state_mlstm_m, gain, l)
        yb_s, kb, vb = _dswa(proj, kvt, ck_all, cv_all, sinks, l)
        yc_s = _dcross(proj, mk_all, mv_all, l)
        ys = _merge_ffn(ys, ya_s, yb_s, yc_s, wgl, wbr, wmix, wfi, wfo, *ln, ns)
        outs["ks"].append(jnp.transpose(kb.reshape(ns, SWA_KV, SWA_HD, WINDOW), (0, 3, 1, 2)))
        outs["vs"].append(jnp.transpose(vb.reshape(ns, SWA_KV, SWA_HD, WINDOW), (0, 3, 1, 2)))
        outs["cs"].append(c_new)
        outs["ns"].append(n_new.reshape(ns, ML_HEADS, ML_DK))
        outs["ms"].append(m_new)

    st = {k: jnp.stack(vals) for k, vals in outs.items()}
    return (yp.reshape(nb, seq, D_MODEL), ys.reshape(ns, 1, D_MODEL),
            st["kp"], st["vp"], st["ks"], st["vs"], st["mk"], st["mv"],
            st["cp"], st["np"], st["mp"], st["cs"], st["ns"], st["ms"])
```

```python
import functools
import math

import jax
import jax.numpy as jnp
from jax import lax
from jax.experimental import pallas as pl
from jax.experimental.pallas import tpu as pltpu

F32 = jnp.float32
BF16 = jnp.bfloat16

D_MODEL = 1024
DEPTH = 2
BRANCH = 512
ML_HEADS = 4
ML_DK = 128
ML_DV = 128
ML_CHUNK = 128
SWA_HD = 64
SWA_HEADS = 8
SWA_KV = 2
SWA_GROUP = 4
WINDOW = 128
ROT_DIM = 16
ROT_HALF = 8
ROPE_THETA = 500000.0
MEM_TOKENS = 256
X_HEADS = 4
X_HD = 128
D_FF = 2816
LN_EPS = 1e-5
HEAD_NORM_EPS = 1e-6
DEEPNORM_ALPHA = (2 * DEPTH) ** 0.25
NEG_INF = -1e30
PAST_LEN = 8192

LANES = 128
VMEM_LIMIT = 56 * 1024 * 1024

_C_MQ, _C_MK, _C_MV, _C_MO = 0, 512, 1024, 1536
_C_MI, _C_MF = 2048, 2052
_C_SQ, _C_SK, _C_SV, _C_XQ, _C_GL = 2056, 2568, 2696, 2824, 3336

TM_PROJ = 512
TM_MIX = 512
TM_MERGE = 256
TQ_CROSS = 256
BB = 8


def _mm(a, b):
    return jnp.dot(a, b, preferred_element_type=F32)


def _mm_nt(a, b):
    return lax.dot_general(a, b, (((1,), (1,)), ((), ())), preferred_element_type=F32)


def _sigmoid(x):
    return 1.0 / (1.0 + jnp.exp(-x))


def _log_sigmoid(x):
    return jnp.minimum(x, 0.0) - jnp.log(1.0 + jnp.exp(-jnp.abs(x)))


def _layer_norm(x, g, b):
    mu = jnp.mean(x, axis=-1, keepdims=True)
    xc = x - mu
    var = jnp.mean(xc * xc, axis=-1, keepdims=True)
    return xc * lax.rsqrt(var + LN_EPS) * g + b


def _rope_lanes(x, cos, sin):
    lane = lax.broadcasted_iota(jnp.int32, x.shape, 1) % SWA_HD
    up = pltpu.roll(x, LANES - ROT_HALF, axis=1)
    dn = pltpu.roll(x, ROT_HALF, axis=1)
    first = x * cos - up * sin
    second = x * cos + dn * sin
    return jnp.where(lane < ROT_HALF, first, jnp.where(lane < ROT_DIM, second, x))


def _const_spec(shape):
    nd = len(shape)
    return pl.BlockSpec(shape, lambda *_: (0,) * nd, pipeline_mode=pl.Buffered(1))


def _params(sem):
    return pltpu.CompilerParams(dimension_semantics=sem, vmem_limit_bytes=VMEM_LIMIT)


def _memkv_kernel(mem_ref, wkv_ref, wkt_ref, k32_ref, v32_ref, kt_ref, v16_ref):
    m = mem_ref[...].astype(BF16)
    kv = _mm(m, wkv_ref[...])
    k32_ref[...] = kv[:, :BRANCH]
    v32_ref[...] = kv[:, BRANCH:]
    v16_ref[...] = kv[:, BRANCH:].astype(BF16)
    kt_ref[...] = _mm_nt(wkt_ref[...], m).astype(BF16)


def _memkv(mem2d, wkv, wkt, nb):
    rows = mem2d.shape[0]
    return pl.pallas_call(
        _memkv_kernel,
        grid=(nb,),
        in_specs=[pl.BlockSpec((MEM_TOKENS, D_MODEL), lambda b: (b, 0)),
                  _const_spec((D_MODEL, 2 * BRANCH)),
                  _const_spec((BRANCH, D_MODEL))],
        out_specs=[pl.BlockSpec((MEM_TOKENS, BRANCH), lambda b: (b, 0)),
                   pl.BlockSpec((MEM_TOKENS, BRANCH), lambda b: (b, 0)),
                   pl.BlockSpec((None, BRANCH, MEM_TOKENS), lambda b: (b, 0, 0)),
                   pl.BlockSpec((MEM_TOKENS, BRANCH), lambda b: (b, 0))],
        out_shape=[jax.ShapeDtypeStruct((rows, BRANCH), F32),
                   jax.ShapeDtypeStruct((rows, BRANCH), F32),
                   jax.ShapeDtypeStruct((nb, BRANCH, MEM_TOKENS), BF16),
                   jax.ShapeDtypeStruct((rows, BRANCH), BF16)],
        compiler_params=_params(("arbitrary",)),
        name="memkv_proj",
    )(mem2d, wkv, wkt)


_N_MQ, _N_MV, _N_MO, _N_SQ, _N_XQ, _N_SV, _N_END = 0, 512, 1024, 1536, 2048, 2560, 2688
_T_MK, _T_SK, _T_G, _T_END = 0, 512, 640, 656


def _proj_kernel(x_ref, wn_ref, wt_ref, bias_ref, cosn_ref, sinn_ref, cost_ref, sint_ref,
                 q_ref, v_ref, mo_ref, sq_ref, xq_ref, sv_ref, kt_ref, skt_ref, gt_ref, gc_ref, k32_ref, v32_ref,
                 m_out_ref, m_scr, *, tiles_per_seq):
    tm = x_ref.shape[0]
    xb = x_ref[...].astype(BF16)
    _mlstm_gate_weights(_mm_nt(wt_ref[_T_G:_T_END, :], xb) + bias_ref[...], gt_ref, gc_ref, m_out_ref, m_scr,
                        pl.program_id(0) % tiles_per_seq == 0)
    q_ref[...] = _mm(xb, wn_ref[:, _N_MQ:_N_MV]).astype(BF16)
    v_ref[...] = _mm(xb, wn_ref[:, _N_MV:_N_MO]).astype(BF16)
    mo_ref[...] = _mm(xb, wn_ref[:, _N_MO:_N_SQ])
    xq_ref[...] = _mm(xb, wn_ref[:, _N_XQ:_N_SV]).astype(BF16)
    cosn = cosn_ref[...]
    sinn = sinn_ref[...]
    sq = _mm(xb, wn_ref[:, _N_SQ:_N_XQ])
    for c in range(BRANCH // LANES):
        blk = _rope_lanes(sq[:, c * LANES:(c + 1) * LANES], cosn, sinn)
        sq_ref[:, c * LANES:(c + 1) * LANES] = blk.astype(BF16)
    sv = _mm(xb, wn_ref[:, _N_SV:_N_END])
    sv_ref[...] = sv.astype(BF16)
    kt = _mm_nt(wt_ref[_T_MK:_T_SK, :], xb) * (ML_DK ** -0.5)
    kt_ref[...] = kt.astype(BF16)
    skt = _mm_nt(wt_ref[_T_SK:_T_G, :], xb)
    cost = cost_ref[...]
    sint = sint_ref[...]
    tail = slice(tm - WINDOW, tm)
    v32_ref[...] = sv[tail, :].T
    for g in range(SWA_KV):
        base = g * SWA_HD
        x1 = skt[base:base + ROT_HALF, :]
        x2 = skt[base + ROT_HALF:base + ROT_DIM, :]
        pieces = ((base, x1 * cost - x2 * sint),
                  (base + ROT_HALF, x2 * cost + x1 * sint),
                  (base + ROT_DIM, skt[base + ROT_DIM:base + SWA_HD, :]))
        for start, val in pieces:
            skt_ref[start:start + val.shape[0], :] = val.astype(BF16)
            k32_ref[start:start + val.shape[0], :] = val[:, tail]


def _mlstm_gate_weights(pre, gt_ref, gc_ref, m_out_ref, m_scr, first_tile):
    tm = pre.shape[1]

    @pl.when(first_tile)
    def _():
        m_scr[...] = jnp.zeros_like(m_scr)

    li = pre[0:8, :]
    lane8 = lax.broadcasted_iota(jnp.int32, li.shape, 1) % ML_CHUNK
    b = _log_sigmoid(pre[8:16, :])
    shift = 1
    while shift < ML_CHUNK:
        b = b + jnp.where(lane8 >= shift, pltpu.roll(b, shift, axis=1), 0.0)
        shift *= 2
    g = li - b
    cm = g
    shift = 1
    while shift < ML_CHUNK:
        cm = jnp.maximum(cm, jnp.where(lane8 >= shift, pltpu.roll(cm, shift, axis=1), -jnp.inf))
        shift *= 2
    gt_ref[0:8, :] = g
    pad = jnp.zeros((LANES - 24, ML_CHUNK), F32)
    m_prev = m_scr[...]
    for c in range(tm // ML_CHUNK):
        cs = slice(c * ML_CHUNK, (c + 1) * ML_CHUNK)
        b_c = b[:, cs]
        b_last = jnp.broadcast_to(b_c[:, ML_CHUNK - 1:ML_CHUNK], b_c.shape)
        cm_last = jnp.broadcast_to(cm[:, cs][:, ML_CHUNK - 1:ML_CHUNK], b_c.shape)
        m_t = b_c + jnp.maximum(m_prev, cm[:, cs])
        m_new = b_last + jnp.maximum(m_prev, cm_last)
        gt_ref[8:16, cs] = jnp.exp(b_last + g[:, cs] - m_new)
        gt_ref[16:24, cs] = jnp.exp(b_last + m_prev - m_new)
        rows = jnp.concatenate([b_c - m_t, jnp.exp(b_c + m_prev - m_t), jnp.exp(-m_t), pad], axis=0)
        gc_ref[cs, :] = rows.T
        m_prev = m_new
    m_scr[...] = m_prev
    m_out_ref[...] = m_prev


def _proj(x2d, wn, wt, bias_col, cosn, sinn, cost, sint, nb, seq):
    m = x2d.shape[0]
    tm = TM_PROJ
    nt = seq // tm
    row = lambda i: (i, 0)
    col = lambda i: (0, i)
    return pl.pallas_call(
        functools.partial(_proj_kernel, tiles_per_seq=nt),
        grid=(m // tm,),
        in_specs=[pl.BlockSpec((tm, D_MODEL), row),
                  _const_spec((D_MODEL, _N_END)),
                  _const_spec((_T_END, D_MODEL)),
                  _const_spec((16, 1)),
                  pl.BlockSpec((tm, LANES), lambda i: (i % nt, 0)),
                  pl.BlockSpec((tm, LANES), lambda i: (i % nt, 0)),
                  pl.BlockSpec((ROT_HALF, tm), lambda i: (0, i % nt)),
                  pl.BlockSpec((ROT_HALF, tm), lambda i: (0, i % nt))],
        out_specs=[pl.BlockSpec((tm, BRANCH), row),
                   pl.BlockSpec((tm, BRANCH), row),
                   pl.BlockSpec((tm, BRANCH), row),
                   pl.BlockSpec((tm, BRANCH), row),
                   pl.BlockSpec((tm, BRANCH), row),
                   pl.BlockSpec((tm, LANES), row),
                   pl.BlockSpec((BRANCH, tm), col),
                   pl.BlockSpec((LANES, tm), col),
                   pl.BlockSpec((24, tm), col),
                   pl.BlockSpec((tm, LANES), row),
                   pl.BlockSpec((LANES, WINDOW), lambda i: (i // nt, 0)),
                   pl.BlockSpec((LANES, WINDOW), lambda i: (i // nt, 0)),
                   pl.BlockSpec((8, LANES), lambda i: (i // nt, 0))],
        out_shape=[jax.ShapeDtypeStruct((m, BRANCH), BF16),
                   jax.ShapeDtypeStruct((m, BRANCH), BF16),
                   jax.ShapeDtypeStruct((m, BRANCH), F32),
                   jax.ShapeDtypeStruct((m, BRANCH), BF16),
                   jax.ShapeDtypeStruct((m, BRANCH), BF16),
                   jax.ShapeDtypeStruct((m, LANES), BF16),
                   jax.ShapeDtypeStruct((BRANCH, m), BF16),
                   jax.ShapeDtypeStruct((LANES, m), BF16),
                   jax.ShapeDtypeStruct((24, m), F32),
                   jax.ShapeDtypeStruct((m, LANES), F32),
                   jax.ShapeDtypeStruct((nb * LANES, WINDOW), F32),
                   jax.ShapeDtypeStruct((nb * LANES, WINDOW), F32),
                   jax.ShapeDtypeStruct((nb * 8, LANES), F32)],
        scratch_shapes=[pltpu.VMEM((8, LANES), F32)],
        compiler_params=_params(("arbitrary",)),
        name="prompt_proj",
    )(x2d, wn, wt, bias_col, cosn, sinn, cost, sint)


def _mlstm_kernel(q_ref, v_ref, kt_ref, mo_ref, gt_ref, gc_ref, gain_ref, y_ref, s_out_ref, s_ref):
    j = pl.program_id(1)
    L = ML_CHUNK

    @pl.when(j == 0)
    def _():
        s_ref[...] = jnp.zeros_like(s_ref)

    r_i = lax.broadcasted_iota(jnp.int32, (L, L), 0)
    c_i = lax.broadcasted_iota(jnp.int32, (L, L), 1)
    causal = c_i <= r_i
    ones = jnp.ones((L, ML_DV), BF16)

    units = [(c, h) for c in range(q_ref.shape[0] // L) for h in range(ML_HEADS)]
    ts = lambda c: slice(c * L, (c + 1) * L)
    hs = lambda h: slice(h * ML_DK, (h + 1) * ML_DK)

    qk = {(c, h): _mm(q_ref[ts(c), hs(h)], kt_ref[hs(h), ts(c)]) for c, h in units}

    sw, kts, vext = {}, {}, {}
    for c, h in units:
        g_r = gt_ref[h:h + 1, ts(c)]
        es_r = gt_ref[8 + h:9 + h, ts(c)]
        u_c = gc_ref[ts(c), h:h + 1]
        sw[c, h] = (qk[c, h] * jnp.exp(jnp.where(causal, u_c + g_r, -jnp.inf))).astype(BF16)
        kts[c, h] = (kt_ref[hs(h), ts(c)].astype(F32) * es_r).astype(BF16)
        vext[c, h] = jnp.concatenate([v_ref[ts(c), hs(h)], ones], axis=1)

    intra = {u: _mm(sw[u], vext[u]) for u in units}
    delta = {u: _mm(kts[u], vext[u]) for u in units}

    s_in = {}
    for h in range(ML_HEADS):
        state = s_ref[h]
        for c in range(q_ref.shape[0] // L):
            s_in[c, h] = state.astype(BF16)
            state = gt_ref[16 + h:17 + h, c * L:c * L + 1] * state + delta[c, h]
        s_ref[h] = state
    qs_all = {u: _mm(q_ref[ts(u[0]), hs(u[1])], s_in[u]) for u in units}

    hh, hc = {}, {}
    for c, h in units:
        tot = intra[c, h] + gc_ref[ts(c), 8 + h:9 + h] * qs_all[c, h]
        floor = gc_ref[ts(c), 16 + h:17 + h]
        hh[c, h] = tot[:, :ML_DV] * (1.0 / jnp.maximum(jnp.abs(tot[:, ML_DV:]), floor))
    for u in units:
        hc[u] = hh[u] - jnp.mean(hh[u], axis=1, keepdims=True)
    for c, h in units:
        var = jnp.mean(hc[c, h] * hc[c, h], axis=1, keepdims=True)
        hn = hc[c, h] * lax.rsqrt(var + HEAD_NORM_EPS) * gain_ref[:, hs(h)]
        y_ref[ts(c), hs(h)] = (_sigmoid(mo_ref[ts(c), hs(h)]) * hn).astype(BF16)

    @pl.when(j == pl.num_programs(1) - 1)
    def _():
        s_out_ref[...] = s_ref[...]


def _mlstm(q, v, kt, mo, gt, gc, gain, nb, seq):
    nc = seq // TM_MIX
    row = lambda b, j: (b * nc + j, 0)
    col = lambda b, j: (0, b * nc + j)
    return pl.pallas_call(
        _mlstm_kernel,
        grid=(nb, nc),
        in_specs=[pl.BlockSpec((TM_MIX, BRANCH), row),
                  pl.BlockSpec((TM_MIX, BRANCH), row),
                  pl.BlockSpec((BRANCH, TM_MIX), col),
                  pl.BlockSpec((TM_MIX, BRANCH), row),
                  pl.BlockSpec((24, TM_MIX), col),
                  pl.BlockSpec((TM_MIX, LANES), row),
                  _const_spec((1, BRANCH))],
        out_specs=[pl.BlockSpec((TM_MIX, BRANCH), row),
                   pl.BlockSpec((None, ML_HEADS, ML_DK, 2 * ML_DV), lambda b, j: (b, 0, 0, 0))],
        out_shape=[jax.ShapeDtypeStruct(q.shape, BF16),
                   jax.ShapeDtypeStruct((nb, ML_HEADS, ML_DK, 2 * ML_DV), F32)],
        scratch_shapes=[pltpu.VMEM((ML_HEADS, ML_DK, 2 * ML_DV), F32)],
        compiler_params=_params(("arbitrary", "arbitrary")),
        name="mlstm_chunks",
    )(q, v, kt, mo, gt, gc, gain)


def _swa_kernel(q_ref, ktp_ref, ktc_ref, vp_ref, vc_ref, sink_ref, y_ref):
    j = pl.program_id(1)
    L = WINDOW
    nblk = q_ref.shape[0] // L
    r_i = lax.broadcasted_iota(jnp.int32, (L, 2 * L), 0)
    c_i = lax.broadcasted_iota(jnp.int32, (L, 2 * L), 1)
    band = (c_i >= r_i) & (c_i <= r_i + L)
    first = band & (c_i >= jnp.where(j == 0, L, 0))
    low_half = lax.broadcasted_iota(jnp.int32, (2 * L, LANES), 1) < SWA_HD
    out_low = lax.broadcasted_iota(jnp.int32, (L, LANES), 1) < SWA_HD
    zeros_k = jnp.zeros((SWA_HD, 2 * L), BF16)
    ones_lo = jnp.where(low_half, 1.0, 0.0).astype(BF16)
    ones_hi = jnp.where(low_half, 0.0, 1.0).astype(BF16)

    kt_all =jnp.concatenate([ktp_ref[...], ktc_ref[...]], axis=1)
    v_all = jnp.concatenate([vp_ref[...], vc_ref[...]], axis=0).astype(F32)
    v_swap = pltpu.roll(v_all, SWA_HD, axis=1)
    def scores(c):
        win = slice(c * L, (c + 2) * L)
        out = []
        for g in range(SWA_KV):
            kt2 = kt_all[g * SWA_HD:(g + 1) * SWA_HD, win]
            kblk = jnp.concatenate([jnp.concatenate([kt2, zeros_k], axis=0),
                                    jnp.concatenate([zeros_k, kt2], axis=0)], axis=1)
            for pp in range(2 * g, 2 * g + 2):
                out.append(_mm(q_ref[c * L:(c + 1) * L, pp * LANES:(pp + 1) * LANES], kblk))
        return out

    def weights(c, s_list):
        allowed = first if c == 0 else band
        out = []
        for head in range(SWA_HEADS):
            s = s_list[head // 2][:, (head % 2) * 2 * L:(head % 2 + 1) * 2 * L]
            sc = jnp.where(allowed, s * (SWA_HD ** -0.5), NEG_INF)
            sink = sink_ref[head:head + 1, 0:1]
            mx = jnp.broadcast_to(jnp.maximum(jnp.max(sc, axis=1, keepdims=True), sink), sc.shape)
            out.append((jnp.exp(sc - mx).astype(BF16), jnp.exp(sink - mx[:, :LANES])))
        return out

    def outputs(c, e_list):
        win = slice(c * L, (c + 2) * L)
        v2 = v_all[win, :]
        v2s = v_swap[win, :]
        for g in range(SWA_KV):
            va = jnp.where(low_half, v2 if g == 0 else v2s, 0.0).astype(BF16)
            vb = jnp.where(low_half, 0.0, v2s if g == 0 else v2).astype(BF16)
            vden = jnp.concatenate([jnp.concatenate([va, ones_lo], axis=1),
                                    jnp.concatenate([vb, ones_hi], axis=1)], axis=0)
            for pp in range(2 * g, 2 * g + 2):
                (e0, k0), (e1, k1) = e_list[2 * pp], e_list[2 * pp + 1]
                res = _mm(jnp.concatenate([e0, e1], axis=1), vden)
                den = res[:, LANES:] + jnp.where(out_low, k0, k1)
                y_ref[c * L:(c + 1) * L, pp * LANES:(pp + 1) * LANES] = (res[:, :LANES] * (1.0 / den)).astype(BF16)

    s_next = scores(0)
    for c in range(nblk):
        s_cur = s_next
        if c + 1 < nblk:
            s_next = scores(c + 1)
        outputs(c, weights(c, s_cur))


def _swa(sq, skt, sv, sinks, nb, seq):
    nt = seq // TM_MIX
    per = TM_MIX // WINDOW
    cur_r = lambda b, j: (b * nt + j, 0)
    cur_c = lambda b, j: (0, b * nt + j)
    prev_r = lambda b, j: (jnp.maximum((b * nt + j) * per - 1, b * nt * per), 0)
    prev_c = lambda b, j: (0, jnp.maximum((b * nt + j) * per - 1, b * nt * per))
    return pl.pallas_call(
        _swa_kernel,
        grid=(nb, nt),
        in_specs=[pl.BlockSpec((TM_MIX, BRANCH), cur_r),
                  pl.BlockSpec((LANES, WINDOW), prev_c),
                  pl.BlockSpec((LANES, TM_MIX), cur_c),
                  pl.BlockSpec((WINDOW, LANES), prev_r),
                  pl.BlockSpec((TM_MIX, LANES), cur_r),
                  _const_spec((8, LANES))],
        out_specs=pl.BlockSpec((TM_MIX, BRANCH), cur_r),
        out_shape=jax.ShapeDtypeStruct(sq.shape, BF16),
        compiler_params=_params(("arbitrary", "arbitrary")),
        name="swa_prompt",
    )(sq, skt, skt, sv, sv, sinks)


def _cross_kernel(q_ref, kt_ref, v_ref, y_ref):
    hs = lambda h: slice(h * X_HD, (h + 1) * X_HD)
    scores = [_mm(q_ref[:, hs(h)], kt_ref[hs(h), :]) for h in range(X_HEADS)]
    probs = []
    for s in scores:
        s = s * (X_HD ** -0.5)
        probs.append(jnp.exp(s - jnp.max(s, axis=1, keepdims=True)).astype(BF16))
    ones = jnp.ones((MEM_TOKENS, X_HD), BF16)
    for h, e in enumerate(probs):
        res = _mm(e, jnp.concatenate([v_ref[:, hs(h)], ones], axis=1))
        y_ref[:, hs(h)] = (res[:, :X_HD] * (1.0 / res[:, X_HD:])).astype(BF16)


def _cross(xq, mkt, mv16, nb, seq):
    nt = seq // TQ_CROSS
    row = lambda b, i: (b * nt + i, 0)
    return pl.pallas_call(
        _cross_kernel,
        grid=(nb, nt),
        in_specs=[pl.BlockSpec((TQ_CROSS, BRANCH), row),
                  pl.BlockSpec((None, BRANCH, MEM_TOKENS), lambda b, i: (b, 0, 0)),
                  pl.BlockSpec((MEM_TOKENS, BRANCH), lambda b, i: (b, 0))],
        out_specs=pl.BlockSpec((TQ_CROSS, BRANCH), row),
        out_shape=jax.ShapeDtypeStruct(xq.shape, BF16),
        compiler_params=_params(("arbitrary", "arbitrary")),
        name="cross_prompt",
    )(xq, mkt, mv16)


def _merge_ffn_kernel(x_ref, ya_ref, yb_ref, yc_ref, wgl_ref, wbr_ref, wmix_ref, wfi_ref, wfo_ref,
                      g1_ref, b1_ref, g2_ref, b2_ref, o_ref):
    x = x_ref[...]
    xb = x.astype(BF16)
    acc = None
    for r, y_ref in enumerate((ya_ref, yb_ref, yc_ref)):
        gate = _sigmoid(_mm(xb, wgl_ref[:, r * D_MODEL:(r + 1) * D_MODEL]))
        term = gate * _mm(y_ref[...], wbr_ref[r])
        acc = term if acc is None else acc + term
    mixed = _mm(acc.astype(BF16), wmix_ref[...])
    x1 = _layer_norm(DEEPNORM_ALPHA * x + mixed, g1_ref[...], b1_ref[...])
    x1b = x1.astype(BF16)
    gpre = _mm(x1b, wfi_ref[:, :D_FF])
    up = _mm(x1b, wfi_ref[:, D_FF:])
    act = (gpre * _sigmoid(gpre) * up).astype(BF16)
    ffn = _mm(act, wfo_ref[...])
    o_ref[...] = _layer_norm(DEEPNORM_ALPHA * x1 + ffn, g2_ref[...], b2_ref[...])


def _merge_ffn(x2d, ya, yb, yc, wgl, wbr, wmix, wfi, wfo, g1, b1, g2, b2, tm):
    m = x2d.shape[0]
    row = lambda i: (i, 0)
    vec = _const_spec((1, D_MODEL))
    return pl.pallas_call(
        _merge_ffn_kernel,
        grid=(m // tm,),
        in_specs=[pl.BlockSpec((tm, D_MODEL), row),
                  pl.BlockSpec((tm, BRANCH), row),
                  pl.BlockSpec((tm, BRANCH), row),
                  pl.BlockSpec((tm, BRANCH), row),
                  _const_spec((D_MODEL, 3 * D_MODEL)),
                  _const_spec((3, BRANCH, D_MODEL)),
                  _const_spec((D_MODEL, D_MODEL)),
                  _const_spec((D_MODEL, 2 * D_FF)),
                  _const_spec((D_FF, D_MODEL)),
                  vec, vec, vec, vec],
        out_specs=pl.BlockSpec((tm, D_MODEL), row),
        out_shape=jax.ShapeDtypeStruct((m, D_MODEL), F32),
        compiler_params=_params(("arbitrary",)),
        name="merge_ffn",
    )(x2d, ya, yb, yc, wgl, wbr, wmix, wfi, wfo, g1, b1, g2, b2)


_S_MQ, _S_MK, _S_MV, _S_MO, _S_SQ, _S_XQ, _S_SK, _S_SV, _S_G, _S_END = (
    0, 512, 1024, 1536, 2048, 2560, 3072, 3200, 3328, 3456)


def _dproj_kernel(x_ref, w_ref, wt_ref, bias_ref, cos_ref, sin_ref, cost_ref, sint_ref, o_ref, ot_ref):
    xb = x_ref[...].astype(BF16)
    cos = cos_ref[...]
    sin = sin_ref[...]
    kvt = _mm_nt(wt_ref[...], xb)
    cost = cost_ref[...]
    sint = sint_ref[...]
    for g in range(SWA_KV):
        base = g * SWA_HD
        x1 = kvt[base:base + ROT_HALF, :]
        x2 = kvt[base + ROT_HALF:base + ROT_DIM, :]
        ot_ref[base:base + ROT_HALF, :] = x1 * cost - x2 * sint
        ot_ref[base + ROT_HALF:base + ROT_DIM, :] = x2 * cost + x1 * sint
        ot_ref[base + ROT_DIM:base + SWA_HD, :] = kvt[base + ROT_DIM:base + SWA_HD, :]
    ot_ref[LANES:, :] = kvt[LANES:, :]
    for c in range(_S_END // LANES):
        cs = slice(c * LANES, (c + 1) * LANES)
        blk = _mm(xb, w_ref[:, cs])
        if _S_MK <= c * LANES < _S_MV:
            blk = blk * (ML_DK ** -0.5)
        elif _S_SQ <= c * LANES < _S_XQ or _S_SK <= c * LANES < _S_SV:
            blk = _rope_lanes(blk, cos, sin)
        elif c * LANES == _S_G:
            blk = blk + bias_ref[...]
        o_ref[:, cs] = blk


def _dproj(xs, w, wt, bias_row, cos, sin, cost, sint):
    n = xs.shape[0]
    return pl.pallas_call(
        _dproj_kernel,
        grid=(1,),
        in_specs=[_const_spec((n, D_MODEL)), _const_spec((D_MODEL, _S_END)), _const_spec((2 * LANES, D_MODEL)),
                  _const_spec((1, LANES)), _const_spec((n, LANES)), _const_spec((n, LANES)),
                  _const_spec((ROT_HALF, n)), _const_spec((ROT_HALF, n))],
        out_specs=[pl.BlockSpec((n, _S_END), lambda i: (0, 0)), pl.BlockSpec((2 * LANES, n), lambda i: (0, 0))],
        out_shape=[jax.ShapeDtypeStruct((n, _S_END), F32), jax.ShapeDtypeStruct((2 * LANES, n), F32)],
        compiler_params=_params(("arbitrary",)),
        name="decode_proj",
    )(xs, w, wt, bias_row, cos, sin, cost, sint)


def _dmlstm_kernel(q_ref, k_ref, v_ref, mo_ref, g_ref, c_ref, n_ref, m_ref, gain_ref,
                   y_ref, c_out_ref, n_out_ref, m_out_ref):
    bb = q_ref.shape[0]
    gts = g_ref[...]
    li = gts[:, 0:ML_HEADS]
    lf = _log_sigmoid(gts[:, ML_HEADS:2 * ML_HEADS])
    m_prev = m_ref[...]
    m_t = jnp.maximum(lf + m_prev, li)
    w_all = jnp.exp(li - m_t)
    a_all = jnp.exp(lf + m_prev - m_t)
    emt_all = jnp.exp(-m_t)
    m_out_ref[...] = m_t
    lane = lax.broadcasted_iota(jnp.int32, (ML_DV, LANES), 1)
    for h in range(ML_HEADS):
        hs = slice(h * ML_DK, (h + 1) * ML_DK)
        qh = q_ref[:, hs]
        kh = k_ref[:, hs]
        vh = v_ref[:, hs]
        nh = n_ref[:, hs]
        w = w_all[:, h:h + 1]
        a = a_all[:, h:h + 1]
        sw = jnp.sum(qh * kh, axis=1, keepdims=True) * w
        den = sw + a * jnp.sum(nh * qh, axis=1, keepdims=True)
        dn = jnp.maximum(jnp.abs(den), emt_all[:, h:h + 1])
        n_out_ref[:, hs] = a * nh + w * kh
        vt = jnp.concatenate([vh, jnp.zeros((LANES - bb, ML_DV), F32)], axis=0).T
        ht = jnp.zeros((ML_DV, LANES), F32)
        for j in range(bb):
            c = c_ref[j, h]
            cq = jnp.sum(c * qh[j:j + 1, :], axis=1, keepdims=True)
            vcol = vt[:, j:j + 1]
            hcol = (sw[j:j + 1, :] * vcol + a[j:j + 1, :] * cq) / dn[j:j + 1, :]
            c_out_ref[j, h] = a[j:j + 1, :] * c + (w[j:j + 1, :] * vcol) * kh[j:j + 1, :]
            ht = jnp.where(lane == j, hcol, ht)
        hh = ht.T[0:bb, :]
        mu = jnp.mean(hh, axis=1, keepdims=True)
        hc = hh - mu
        var = jnp.mean(hc * hc, axis=1, keepdims=True)
        hn = hc * lax.rsqrt(var + HEAD_NORM_EPS) * gain_ref[:, hs]
        y_ref[:, hs] = (_sigmoid(mo_ref[:, hs]) * hn).astype(BF16)


def _dmlstm(proj, c_all, n_all, m_all, gain, layer):
    n = proj.shape[0]
    blk = lambda cidx: pl.BlockSpec((BB, BRANCH), lambda i: (i, cidx))
    return pl.pallas_call(
        _dmlstm_kernel,
        grid=(n // BB,),
        in_specs=[blk(_S_MQ // BRANCH), blk(_S_MK // BRANCH), blk(_S_MV // BRANCH), blk(_S_MO // BRANCH),
                  pl.BlockSpec((BB, LANES), lambda i: (i, _S_G // LANES)),
                  pl.BlockSpec((None, BB, ML_HEADS, ML_DV, ML_DK), lambda i: (layer, i, 0, 0, 0)),
                  pl.BlockSpec((None, BB, BRANCH), lambda i: (layer, i, 0)),
                  pl.BlockSpec((None, BB, ML_HEADS), lambda i: (layer, i, 0)),
                  _const_spec((1, BRANCH))],
        out_specs=[pl.BlockSpec((BB, BRANCH), lambda i: (i, 0)),
                   pl.BlockSpec((BB, ML_HEADS, ML_DV, ML_DK), lambda i: (i, 0, 0, 0)),
                   pl.BlockSpec((BB, BRANCH), lambda i: (i, 0)),
                   pl.BlockSpec((BB, ML_HEADS), lambda i: (i, 0))],
        out_shape=[jax.ShapeDtypeStruct((n, BRANCH), BF16),
                   jax.ShapeDtypeStruct((n, ML_HEADS, ML_DV, ML_DK), F32),
                   jax.ShapeDtypeStruct((n, BRANCH), F32),
                   jax.ShapeDtypeStruct((n, ML_HEADS), F32)],
        compiler_params=_params(("arbitrary",)),
        name="decode_mlstm",
    )(proj, proj, proj, proj, proj, c_all, n_all, m_all, gain)


def _dswa_kernel(q_ref, kn_ref, vn_ref, kvt_ref, ck_ref, cv_ref, sink_ref, y_ref, ko_ref, vo_ref):
    i = pl.program_id(0)
    bb = q_ref.shape[0]
    lane = lax.broadcasted_iota(jnp.int32, (LANES, LANES), 1)
    row8 = lax.broadcasted_iota(jnp.int32, (SWA_HEADS, LANES), 0)
    low8 = lax.broadcasted_iota(jnp.int32, (SWA_HEADS, LANES), 1) < SWA_HD
    low1 = lax.broadcasted_iota(jnp.int32, (1, LANES), 1) < SWA_HD
    scale = SWA_HD ** -0.5
    sink = sink_ref[:, 0:1]
    knew_t = kvt_ref[0:LANES, :]
    vnew_t = kvt_ref[LANES:, :]
    for j in range(bb):
        ck = ck_ref[j]
        cv = cv_ref[j]
        kn = kn_ref[j:j + 1, :]
        vn = vn_ref[j:j + 1, :]
        qm = jnp.zeros((SWA_HEADS, LANES), F32)
        for pp in range(SWA_HEADS // 2):
            g = pp // 2
            pair = q_ref[j:j + 1, pp * LANES:(pp + 1) * LANES]
            swap = pltpu.roll(pair, SWA_HD, axis=1)
            in_g = low8 if g == 0 else jnp.logical_not(low8)
            for t in range(2):
                qm = jnp.where((row8 == 2 * pp + t) & in_g, pair if t == g else swap, qm)
        s = _mm(qm.astype(BF16), ck.astype(BF16)) * scale
        s_new = jnp.sum(qm * kn, axis=1, keepdims=True) * scale
        mx = jnp.maximum(jnp.maximum(jnp.max(s, axis=1, keepdims=True), s_new), sink)
        e = jnp.exp(s - mx)
        e_new = jnp.exp(s_new - mx)
        den = jnp.sum(e, axis=1, keepdims=True) + e_new + jnp.exp(sink - mx)
        o = (_mm_nt(e.astype(BF16), cv.astype(BF16)) + e_new * vn) / den
        for pp in range(SWA_HEADS // 2):
            g = pp // 2
            halves = []
            for t in range(2):
                oh = o[2 * pp + t:2 * pp + t + 1, :]
                halves.append(oh if t == g else pltpu.roll(oh, SWA_HD, axis=1))
            y_ref[j:j + 1, pp * LANES:(pp + 1) * LANES] = jnp.where(low1, halves[0], halves[1]).astype(BF16)
        bring = LANES - 1 - (i * bb + j)
        ko_ref[j] = jnp.where(lane == LANES - 1, pltpu.roll(knew_t, bring, axis=1), pltpu.roll(ck, LANES - 1, axis=1))
        vo_ref[j] = jnp.where(lane == LANES - 1, pltpu.roll(vnew_t, bring, axis=1), pltpu.roll(cv, LANES - 1, axis=1))


def _dswa(proj, kvt, ck_all, cv_all, sinks, layer):
    n = proj.shape[0]
    cache = pl.BlockSpec((None, BB, LANES, WINDOW), lambda i: (layer, i, 0, 0))
    outc = pl.BlockSpec((BB, LANES, WINDOW), lambda i: (i, 0, 0))
    return pl.pallas_call(
        _dswa_kernel,
        grid=(n // BB,),
        in_specs=[pl.BlockSpec((BB, BRANCH), lambda i: (i, _S_SQ // BRANCH)),
                  pl.BlockSpec((BB, LANES), lambda i: (i, _S_SK // LANES)),
                  pl.BlockSpec((BB, LANES), lambda i: (i, _S_SV // LANES)),
                  _const_spec((2 * LANES, n)),
                  cache, cache, _const_spec((8, LANES))],
        out_specs=[pl.BlockSpec((BB, BRANCH), lambda i: (i, 0)), outc, outc],
        out_shape=[jax.ShapeDtypeStruct((n, BRANCH), BF16),
                   jax.ShapeDtypeStruct((n, LANES, WINDOW), F32),
                   jax.ShapeDtypeStruct((n, LANES, WINDOW), F32)],
        compiler_params=_params(("arbitrary",)),
        name="decode_swa",
    )(proj, proj, proj, kvt, ck_all, cv_all, sinks)


def _dcross_kernel(q_ref, k_ref, v_ref, y_ref):
    bb = q_ref.shape[0]
    scale = X_HD ** -0.5
    row8 = lax.broadcasted_iota(jnp.int32, (8, LANES), 0) % X_HEADS
    for j in range(bb):
        qrep = jnp.zeros((8, LANES), F32)
        for h in range(X_HEADS):
            qrep = jnp.where(row8 == h, q_ref[j:j + 1, h * X_HD:(h + 1) * X_HD], qrep)
        s = jnp.sum(k_ref[j] * qrep[None], axis=2, keepdims=True) * scale
        mx8 = jnp.max(s, axis=0)
        mx4 = jnp.maximum(mx8[0:X_HEADS], mx8[X_HEADS:])
        e = jnp.exp(s - jnp.concatenate([mx4, mx4], axis=0)[None])
        den8 = jnp.sum(e, axis=0)
        o8 = jnp.sum(e * v_ref[j], axis=0)
        o4 = (o8[0:X_HEADS] + o8[X_HEADS:]) / (den8[0:X_HEADS] + den8[X_HEADS:])
        for h in range(X_HEADS):
            y_ref[j:j + 1, h * X_HD:(h + 1) * X_HD] = o4[h:h + 1, :].astype(BF16)


def _dcross(proj, mk_all, mv_all, layer):
    n = proj.shape[0]
    cache = pl.BlockSpec((None, BB, MEM_TOKENS * X_HEADS // 8, 8, X_HD), lambda i: (layer, i, 0, 0, 0))
    return pl.pallas_call(
        _dcross_kernel,
        grid=(n // BB,),
        in_specs=[pl.BlockSpec((BB, BRANCH), lambda i: (i, _S_XQ // BRANCH)), cache, cache],
        out_specs=pl.BlockSpec((BB, BRANCH), lambda i: (i, 0)),
        out_shape=jax.ShapeDtypeStruct((n, BRANCH), BF16),
        compiler_params=_params(("arbitrary",)),
        name="decode_cross",
    )(proj, mk_all, mv_all)


def _rope_tables(positions):
    inv_freq = ROPE_THETA ** (-jnp.arange(ROT_HALF, dtype=F32) / ROT_HALF)
    ang = positions.astype(F32)[:, None] * inv_freq[None, :]
    cos = jnp.cos(ang)
    sin = jnp.sin(ang)
    reps = LANES // ROT_HALF
    return jnp.tile(cos, (1, reps)), jnp.tile(sin, (1, reps)), cos.T, sin.T


def kernel(x_prompt, x_sample, mem_prompt, cache_swa_k, cache_swa_v, cache_mem_k, cache_mem_v, state_mlstm_c, state_mlstm_n, state_mlstm_m, w_in, b_gates, mlstm_norm_g, swa_sinks, w_mem_kv, w_branch, w_mix_out, ln1_g, ln1_b, w_ffn_in, w_ffn_out, ln2_g, ln2_b):
    nb, seq, _ = x_prompt.shape
    ns = x_sample.shape[0]

    cosn, sinn, cost, sint = _rope_tables(jnp.arange(seq))
    cos_s, sin_s, cost_s, sint_s = _rope_tables(jnp.full((ns,), PAST_LEN))
    assert ns == LANES and PAST_LEN >= WINDOW

    ck_all = jnp.transpose(cache_swa_k, (0, 1, 3, 4, 2)).reshape(DEPTH, ns, SWA_KV * SWA_HD, WINDOW)
    cv_all = jnp.transpose(cache_swa_v, (0, 1, 3, 4, 2)).reshape(DEPTH, ns, SWA_KV * SWA_HD, WINDOW)
    mk_all = cache_mem_k.reshape(DEPTH, ns, MEM_TOKENS * X_HEADS // 8, 8, X_HD)
    mv_all = cache_mem_v.reshape(DEPTH, ns, MEM_TOKENS * X_HEADS // 8, 8, X_HD)
    n_all = state_mlstm_n.reshape(DEPTH, ns, BRANCH)

    yp = x_prompt.reshape(nb * seq, D_MODEL)
    ys = x_sample.reshape(ns, D_MODEL)
    mem2d = mem_prompt.reshape(nb * MEM_TOKENS, D_MODEL)

    outs = {k: [] for k in ("kp", "vp", "ks", "vs", "mk", "mv", "cp", "np", "mp", "cs", "ns", "ms")}
    for l in range(DEPTH):
        w = w_in[l]
        zpad = lambda n: jnp.zeros((D_MODEL, n), F32)
        wn = jnp.concatenate([w[:, _C_MQ:_C_MK], w[:, _C_MV:_C_MO], w[:, _C_MO:_C_MI], w[:, _C_SQ:_C_SK],
                              w[:, _C_XQ:_C_GL], w[:, _C_SV:_C_XQ]], axis=1).astype(BF16)
        wt = jnp.concatenate([w[:, _C_MK:_C_MV], w[:, _C_SK:_C_SV], w[:, _C_MI:_C_MF], zpad(4),
                              w[:, _C_MF:_C_SQ], zpad(4)], axis=1).T.astype(BF16)
        ws = jnp.concatenate([w[:, _C_MQ:_C_MI], w[:, _C_SQ:_C_SK], w[:, _C_XQ:_C_GL], w[:, _C_SK:_C_XQ],
                              w[:, _C_MI:_C_SQ], zpad(LANES - 8)], axis=1).astype(BF16)
        wgl = w[:, _C_GL:].astype(BF16)
        bias = b_gates[l]
        z4 = jnp.zeros((4,), F32)
        bias_col = jnp.concatenate([bias[:ML_HEADS], z4, bias[ML_HEADS:], z4]).reshape(16, 1)
        bias_row = jnp.concatenate([bias, jnp.zeros((LANES - 8,), F32)]).reshape(1, LANES)
        gain = mlstm_norm_g[l].reshape(1, BRANCH)
        sinks = jnp.broadcast_to(swa_sinks[l][:, None], (SWA_HEADS, LANES))
        wkv = w_mem_kv[l].astype(BF16)
        wkt = w_mem_kv[l][:, :BRANCH].T.astype(BF16)
        wbr = w_branch[l].astype(BF16)
        wmix = w_mix_out[l].astype(BF16)
        wfi = w_ffn_in[l].astype(BF16)
        wfo = w_ffn_out[l].astype(BF16)
        ln = (ln1_g[l].reshape(1, D_MODEL), ln1_b[l].reshape(1, D_MODEL),
              ln2_g[l].reshape(1, D_MODEL), ln2_b[l].reshape(1, D_MODEL))

        mk32, mv32, mkt, mv16 = _memkv(mem2d, wkv, wkt, nb)
        q, v, mo, sq, xq, sv, kt, skt, gt, gc, k32, v32, m_fin = _proj(yp, wn, wt, bias_col, cosn, sinn, cost, sint,
                                                                       nb, seq)
        ya, s_fin = _mlstm(q, v, kt, mo, gt, gc, gain, nb, seq)
        yb = _swa(sq, skt, sv, sinks, nb, seq)
        yc = _cross(xq, mkt, mv16, nb, seq)
        yp = _merge_ffn(yp, ya, yb, yc, wgl, wbr, wmix, wfi, wfo, *ln, TM_MERGE)
        outs["kp"].append(jnp.transpose(k32.reshape(nb, SWA_KV, SWA_HD, WINDOW), (0, 3, 1, 2)))
        outs["vp"].append(jnp.transpose(v32.reshape(nb, SWA_KV, SWA_HD, WINDOW), (0, 3, 1, 2)))
        outs["mk"].append(mk32.reshape(nb, MEM_TOKENS, X_HEADS, X_HD))
        outs["mv"].append(mv32.reshape(nb, MEM_TOKENS, X_HEADS, X_HD))
        outs["cp"].append(jnp.swapaxes(s_fin[..., :ML_DV], -1, -2))
        outs["np"].append(s_fin[..., ML_DV])
        outs["mp"].append(m_fin.reshape(nb, 8, LANES)[:, :ML_HEADS, 0])

        wts = w[:, _C_SK:_C_XQ].T.astype(BF16)
        proj, kvt = _dproj(ys, ws, wts, bias_row, cos_s, sin_s, cost_s, sint_s)
        ya_s, c_new, n_new, m_new = _dmlstm(proj, state_mlstm_c, n_all, state_mlstm_m, gain, l)
        yb_s, kb, vb = _dswa(proj, kvt, ck_all, cv_all, sinks, l)
        yc_s = _dcross(proj, mk_all, mv_all, l)
        ys = _merge_ffn(ys, ya_s, yb_s, yc_s, wgl, wbr, wmix, wfi, wfo, *ln, ns)
        outs["ks"].append(jnp.transpose(kb.reshape(ns, SWA_KV, SWA_HD, WINDOW), (0, 3, 1, 2)))
        outs["vs"].append(jnp.transpose(vb.reshape(ns, SWA_KV, SWA_HD, WINDOW), (0, 3, 1, 2)))
        outs["cs"].append(c_new)
        outs["ns"].append(n_new.reshape(ns, ML_HEADS, ML_DK))
        outs["ms"].append(m_new)

    st = {k: jnp.stack(vals) for k, vals in outs.items()}
    return (yp.reshape(nb, seq, D_MODEL), ys.reshape(ns, 1, D_MODEL),
            st["kp"], st["vp"], st["ks"], st["vs"], st["mk"], st["mv"],
            st["cp"], st["np"], st["mp"], st["cs"], st["ns"], st["ms"])
```

```python
import functools

import jax
import jax.numpy as jnp
from jax import lax
from jax.experimental import pallas as pl
from jax.experimental.pallas import tpu as pltpu

F32 = jnp.float32
BF16 = jnp.bfloat16

D_MODEL = 1024
DEPTH = 2
BRANCH = 512
ML_HEADS = 4
ML_DK = 128
ML_DV = 128
ML_CHUNK = 128
SWA_HD = 64
SWA_HEADS = 8
SWA_KV = 2
SWA_GROUP = 4
WINDOW = 128
ROT_DIM = 16
ROT_HALF = 8
ROPE_THETA = 500000.0
MEM_TOKENS = 256
X_HEADS = 4
X_HD = 128
D_FF = 2816
LN_EPS = 1e-5
HEAD_NORM_EPS = 1e-6
DEEPNORM_ALPHA = (2 * DEPTH) ** 0.25
NEG_INF = -1e30
PAST_LEN = 8192

LANES = 128
VMEM_LIMIT = 56 * 1024 * 1024

_C_MQ, _C_MK, _C_MV, _C_MO = 0, 512, 1024, 1536
_C_MI, _C_MF = 2048, 2052
_C_SQ, _C_SK, _C_SV, _C_XQ, _C_GL = 2056, 2568, 2696, 2824, 3336

W_HALF = 3 * D_MODEL
_N_MQ, _N_MV, _N_MO, _N_SQ, _N_XQ, _N_SV, _N_END = 0, 512, 1024, 1536, 2048, 2560, 2688
_T_MV, _T_MK, _T_SK, _T_SV, _T_G, _T_END = 0, 512, 1024, 1152, 1280, 1296

TM_PROJ = 512
TM_MIX = 512
TM_MERGE = 256
TQ_CROSS = 512
BB = 8


def _mm(a, b):
    return jnp.dot(a, b, preferred_element_type=F32)


def _mm_nt(a, b):
    return lax.dot_general(a, b, (((1,), (1,)), ((), ())), preferred_element_type=F32)


def _sigmoid(x):
    return 1.0 / (1.0 + jnp.exp(-x))


def _log_sigmoid(x):
    return jnp.minimum(x, 0.0) - jnp.log(1.0 + jnp.exp(-jnp.abs(x)))


def _layer_norm(x, g, b):
    mu = jnp.mean(x, axis=-1, keepdims=True)
    xc = x - mu
    var = jnp.mean(xc * xc, axis=-1, keepdims=True)
    return xc * lax.rsqrt(var + LN_EPS) * g + b


def _rope_lanes(x, cos, sin):
    lane = lax.broadcasted_iota(jnp.int32, x.shape, 1) % SWA_HD
    up = pltpu.roll(x, LANES - ROT_HALF, axis=1)
    dn = pltpu.roll(x, ROT_HALF, axis=1)
    first = x * cos - up * sin
    second = x * cos + dn * sin
    return jnp.where(lane < ROT_HALF, first, jnp.where(lane < ROT_DIM, second, x))


def _rope_rows(xt, cost, sint):
    x1 = xt[0:ROT_HALF, :]
    x2 = xt[ROT_HALF:ROT_DIM, :]
    return ((0, x1 * cost - x2 * sint), (ROT_HALF, x2 * cost + x1 * sint), (ROT_DIM, xt[ROT_DIM:SWA_HD, :]))


def _const_spec(shape):
    nd = len(shape)
    return pl.BlockSpec(shape, lambda *_: (0,) * nd, pipeline_mode=pl.Buffered(1))


def _layer_spec(shape, layer, *tail):
    idx = (layer,) + (tail if tail else (0,) * len(shape))
    return pl.BlockSpec((None,) + tuple(shape), lambda *_: idx, pipeline_mode=pl.Buffered(1))


def _params(sem):
    return pltpu.CompilerParams(dimension_semantics=sem, vmem_limit_bytes=VMEM_LIMIT)


def _memkv_kernel(mem_ref, wkv_ref, wkt_ref, k32_ref, v32_ref, kt_ref, v16_ref):
    m = mem_ref[...].astype(BF16)
    kv = _mm(m, wkv_ref[...])
    k32_ref[...] = kv[:, :BRANCH]
    v32_ref[...] = kv[:, BRANCH:]
    v16_ref[...] = kv[:, BRANCH:].astype(BF16)
    kt_ref[...] = _mm_nt(wkt_ref[...], m).astype(BF16)


def _memkv(mem2d, wkv, wkt, layer, nb):
    rows = mem2d.shape[0]
    return pl.pallas_call(
        _memkv_kernel,
        grid=(nb,),
        in_specs=[pl.BlockSpec((MEM_TOKENS, D_MODEL), lambda b: (b, 0)),
                  _layer_spec((D_MODEL, 2 * BRANCH), layer),
                  _layer_spec((BRANCH, D_MODEL), layer)],
        out_specs=[pl.BlockSpec((MEM_TOKENS, BRANCH), lambda b: (b, 0)),
                   pl.BlockSpec((MEM_TOKENS, BRANCH), lambda b: (b, 0)),
                   pl.BlockSpec((None, BRANCH, MEM_TOKENS), lambda b: (b, 0, 0)),
                   pl.BlockSpec((MEM_TOKENS, BRANCH), lambda b: (b, 0))],
        out_shape=[jax.ShapeDtypeStruct((rows, BRANCH), F32),
                   jax.ShapeDtypeStruct((rows, BRANCH), F32),
                   jax.ShapeDtypeStruct((nb, BRANCH, MEM_TOKENS), BF16),
                   jax.ShapeDtypeStruct((rows, BRANCH), BF16)],
        compiler_params=_params(("arbitrary",)),
        name="memkv_proj",
    )(mem2d, wkv, wkt)


def _mlstm_gate_weights(pre, gt_ref, gc_ref, m_out_ref, m_scr, first_tile):
    tm = pre.shape[1]

    @pl.when(first_tile)
    def _():
        m_scr[...] = jnp.zeros_like(m_scr)

    li = pre[0:8, :]
    lane8 = lax.broadcasted_iota(jnp.int32, li.shape, 1) % ML_CHUNK
    b = _log_sigmoid(pre[8:16, :])
    shift = 1
    while shift < ML_CHUNK:
        b = b + jnp.where(lane8 >= shift, pltpu.roll(b, shift, axis=1), 0.0)
        shift *= 2
    g = li - b
    cm = g
    shift = 1
    while shift < ML_CHUNK:
        cm = jnp.maximum(cm, jnp.where(lane8 >= shift, pltpu.roll(cm, shift, axis=1), -jnp.inf))
        shift *= 2
    gt_ref[0:8, :] = g
    pad = jnp.zeros((LANES - 24, ML_CHUNK), F32)
    m_prev = m_scr[...]
    for c in range(tm // ML_CHUNK):
        cs = slice(c * ML_CHUNK, (c + 1) * ML_CHUNK)
        b_c = b[:, cs]
        b_last = jnp.broadcast_to(b_c[:, ML_CHUNK - 1:ML_CHUNK], b_c.shape)
        cm_last = jnp.broadcast_to(cm[:, cs][:, ML_CHUNK - 1:ML_CHUNK], b_c.shape)
        m_t = b_c + jnp.maximum(m_prev, cm[:, cs])
        m_new = b_last + jnp.maximum(m_prev, cm_last)
        gt_ref[8:16, cs] = jnp.exp(b_last + g[:, cs] - m_new)
        gt_ref[16:24, cs] = jnp.exp(b_last + m_prev - m_new)
        rows = jnp.concatenate([b_c - m_t, jnp.exp(b_c + m_prev - m_t), jnp.exp(-m_t), pad], axis=0)
        gc_ref[cs, :] = rows.T
        m_prev = m_new
    m_scr[...] = m_prev
    m_out_ref[...] = m_prev


def _proj_kernel(x_ref, wn_ref, wt_ref, bias_ref, cosn_ref, sinn_ref, cost_ref, sint_ref,
                 q_ref, v_ref, mo_ref, sq_ref, xq_ref, sv_ref, kt_ref, skt_ref, gt_ref, gc_ref, k32_ref, v32_ref,
                 m_out_ref, m_scr, *, tiles_per_seq):
    tm = x_ref.shape[0]
    xb = x_ref[...].astype(BF16)
    _mlstm_gate_weights(_mm_nt(wt_ref[_T_G:_T_END, :], xb) + bias_ref[...], gt_ref, gc_ref, m_out_ref, m_scr,
                        pl.program_id(0) % tiles_per_seq == 0)
    q_ref[...] = _mm(xb, wn_ref[:, _N_MQ:_N_MV]).astype(BF16)
    v_ref[...] = _mm(xb, wn_ref[:, _N_MV:_N_MO]).astype(BF16)
    mo_ref[...] = _mm(xb, wn_ref[:, _N_MO:_N_SQ])
    xq_ref[...] = _mm(xb, wn_ref[:, _N_XQ:_N_SV]).astype(BF16)
    cosn = cosn_ref[...]
    sinn = sinn_ref[...]
    sq = _mm(xb, wn_ref[:, _N_SQ:_N_XQ])
    for c in range(BRANCH // LANES):
        blk = _rope_lanes(sq[:, c * LANES:(c + 1) * LANES], cosn, sinn)
        sq_ref[:, c * LANES:(c + 1) * LANES] = blk.astype(BF16)
    sv_ref[...] = _mm(xb, wn_ref[:, _N_SV:_N_END]).astype(BF16)
    kt = _mm_nt(wt_ref[_T_MK:_T_SK, :], xb) * (ML_DK ** -0.5)
    kt_ref[...] = kt.astype(BF16)
    skt = _mm_nt(wt_ref[_T_SK:_T_SV, :], xb)
    svt = _mm_nt(wt_ref[_T_SV:_T_G, :], xb)
    cost = cost_ref[...]
    sint = sint_ref[...]
    tail = slice(tm - WINDOW, tm)
    v32_ref[...] = svt[:, tail]
    for g in range(SWA_KV):
        base = g * SWA_HD
        for off, val in _rope_rows(skt[base:base + SWA_HD, :], cost, sint):
            skt_ref[base + off:base + off + val.shape[0], :] = val.astype(BF16)
            k32_ref[base + off:base + off + val.shape[0], :] = val[:, tail]


def _proj(x2d, w_all, wt_all, bias_all, cosn, sinn, cost, sint, layer, nb, seq):
    m = x2d.shape[0]
    tm = TM_PROJ
    nt = seq // tm
    row = lambda i: (i, 0)
    col = lambda i: (0, i)
    return pl.pallas_call(
        functools.partial(_proj_kernel, tiles_per_seq=nt),
        grid=(m // tm,),
        in_specs=[pl.BlockSpec((tm, D_MODEL), row),
                  _layer_spec((D_MODEL, W_HALF), layer, 0, 1),
                  _layer_spec((_T_END, D_MODEL), layer),
                  _layer_spec((16, 1), layer),
                  pl.BlockSpec((tm, LANES), lambda i: (i % nt, 0)),
                  pl.BlockSpec((tm, LANES), lambda i: (i % nt, 0)),
                  pl.BlockSpec((ROT_HALF, tm), lambda i: (0, i % nt)),
                  pl.BlockSpec((ROT_HALF, tm), lambda i: (0, i % nt))],
        out_specs=[pl.BlockSpec((tm, BRANCH), row),
                   pl.BlockSpec((tm, BRANCH), row),
                   pl.BlockSpec((tm, BRANCH), row),
                   pl.BlockSpec((tm, BRANCH), row),
                   pl.BlockSpec((tm, BRANCH), row),
                   pl.BlockSpec((tm, LANES), row),
                   pl.BlockSpec((BRANCH, tm), col),
                   pl.BlockSpec((LANES, tm), col),
                   pl.BlockSpec((24, tm), col),
                   pl.BlockSpec((tm, LANES), row),
                   pl.BlockSpec((LANES, WINDOW), lambda i: (i // nt, 0)),
                   pl.BlockSpec((LANES, WINDOW), lambda i: (i // nt, 0)),
                   pl.BlockSpec((8, LANES), lambda i: (i // nt, 0))],
        out_shape=[jax.ShapeDtypeStruct((m, BRANCH), BF16),
                   jax.ShapeDtypeStruct((m, BRANCH), BF16),
                   jax.ShapeDtypeStruct((m, BRANCH), F32),
                   jax.ShapeDtypeStruct((m, BRANCH), BF16),
                   jax.ShapeDtypeStruct((m, BRANCH), BF16),
                   jax.ShapeDtypeStruct((m, LANES), BF16),
                   jax.ShapeDtypeStruct((BRANCH, m), BF16),
                   jax.ShapeDtypeStruct((LANES, m), BF16),
                   jax.ShapeDtypeStruct((24, m), F32),
                   jax.ShapeDtypeStruct((m, LANES), F32),
                   jax.ShapeDtypeStruct((nb * LANES, WINDOW), F32),
                   jax.ShapeDtypeStruct((nb * LANES, WINDOW), F32),
                   jax.ShapeDtypeStruct((nb * 8, LANES), F32)],
        scratch_shapes=[pltpu.VMEM((8, LANES), F32)],
        compiler_params=_params(("arbitrary",)),
        name="prompt_proj",
    )(x2d, w_all, wt_all, bias_all, cosn, sinn, cost, sint)


def _mlstm_kernel(q_ref, v_ref, kt_ref, mo_ref, gt_ref, gc_ref, gain_ref, y_ref, s_out_ref, s_ref):
    j = pl.program_id(1)
    L = ML_CHUNK

    @pl.when(j == 0)
    def _():
        s_ref[...] = jnp.zeros_like(s_ref)

    r_i = lax.broadcasted_iota(jnp.int32, (L, L), 0)
    c_i = lax.broadcasted_iota(jnp.int32, (L, L), 1)
    causal = c_i <= r_i
    ones = jnp.ones((L, ML_DV), BF16)

    units = [(c, h) for c in range(q_ref.shape[0] // L) for h in range(ML_HEADS)]
    ts = lambda c: slice(c * L, (c + 1) * L)
    hs = lambda h: slice(h * ML_DK, (h + 1) * ML_DK)

    qk = {(c, h): _mm(q_ref[ts(c), hs(h)], kt_ref[hs(h), ts(c)]) for c, h in units}

    sw, kts, vext = {}, {}, {}
    for c, h in units:
        g_r = gt_ref[h:h + 1, ts(c)]
        es_r = gt_ref[8 + h:9 + h, ts(c)]
        u_c = gc_ref[ts(c), h:h + 1]
        sw[c, h] = (qk[c, h] * jnp.exp(jnp.where(causal, u_c + g_r, -jnp.inf))).astype(BF16)
        kts[c, h] = (kt_ref[hs(h), ts(c)].astype(F32) * es_r).astype(BF16)
        vext[c, h] = jnp.concatenate([v_ref[ts(c), hs(h)], ones], axis=1)

    intra = {u: _mm(sw[u], vext[u]) for u in units}
    delta = {u: _mm(kts[u], vext[u]) for u in units}

    s_in = {}
    for h in range(ML_HEADS):
        state = s_ref[h]
        for c in range(q_ref.shape[0] // L):
            s_in[c, h] = state.astype(BF16)
            state = gt_ref[16 + h:17 + h, c * L:c * L + 1] * state + delta[c, h]
        s_ref[h] = state
    qs_all = {u: _mm(q_ref[ts(u[0]), hs(u[1])], s_in[u]) for u in units}

    hh, hc = {}, {}
    for c, h in units:
        tot = intra[c, h] + gc_ref[ts(c), 8 + h:9 + h] * qs_all[c, h]
        floor = gc_ref[ts(c), 16 + h:17 + h]
        hh[c, h] = tot[:, :ML_DV] * (1.0 / jnp.maximum(jnp.abs(tot[:, ML_DV:]), floor))
    for u in units:
        hc[u] = hh[u] - jnp.mean(hh[u], axis=1, keepdims=True)
    for c, h in units:
        var = jnp.mean(hc[c, h] * hc[c, h], axis=1, keepdims=True)
        hn = hc[c, h] * lax.rsqrt(var + HEAD_NORM_EPS) * gain_ref[:, hs(h)]
        y_ref[ts(c), hs(h)] = (_sigmoid(mo_ref[ts(c), hs(h)]) * hn).astype(BF16)

    @pl.when(j == pl.num_programs(1) - 1)
    def _():
        s_out_ref[...] = s_ref[...]


def _mlstm(q, v, kt, mo, gt, gc, gain_all, layer, nb, seq):
    nc = seq // TM_MIX
    row = lambda b, j: (b * nc + j, 0)
    col = lambda b, j: (0, b * nc + j)
    return pl.pallas_call(
        _mlstm_kernel,
        grid=(nb, nc),
        in_specs=[pl.BlockSpec((TM_MIX, BRANCH), row),
                  pl.BlockSpec((TM_MIX, BRANCH), row),
                  pl.BlockSpec((BRANCH, TM_MIX), col),
                  pl.BlockSpec((TM_MIX, BRANCH), row),
                  pl.BlockSpec((24, TM_MIX), col),
                  pl.BlockSpec((TM_MIX, LANES), row),
                  _layer_spec((1, BRANCH), layer)],
        out_specs=[pl.BlockSpec((TM_MIX, BRANCH), row),
                   pl.BlockSpec((None, ML_HEADS, ML_DK, 2 * ML_DV), lambda b, j: (b, 0, 0, 0))],
        out_shape=[jax.ShapeDtypeStruct(q.shape, BF16),
                   jax.ShapeDtypeStruct((nb, ML_HEADS, ML_DK, 2 * ML_DV), F32)],
        scratch_shapes=[pltpu.VMEM((ML_HEADS, ML_DK, 2 * ML_DV), F32)],
        compiler_params=_params(("arbitrary", "arbitrary")),
        name="mlstm_chunks",
    )(q, v, kt, mo, gt, gc, gain_all)


def _swa_kernel(q_ref, ktp_ref, ktc_ref, vp_ref, vc_ref, sink_ref, y_ref):
    j = pl.program_id(1)
    L = WINDOW
    nblk = q_ref.shape[0] // L
    r_i = lax.broadcasted_iota(jnp.int32, (L, 2 * L), 0)
    c_i = lax.broadcasted_iota(jnp.int32, (L, 2 * L), 1)
    band = (c_i >= r_i) & (c_i <= r_i + L)
    first = band & (c_i >= jnp.where(j == 0, L, 0))
    low_half = lax.broadcasted_iota(jnp.int32, (2 * L, LANES), 1) < SWA_HD
    out_low = lax.broadcasted_iota(jnp.int32, (L, LANES), 1) < SWA_HD
    zeros_k = jnp.zeros((SWA_HD, 2 * L), BF16)
    ones_lo = jnp.where(low_half, 1.0, 0.0).astype(BF16)
    ones_hi = jnp.where(low_half, 0.0, 1.0).astype(BF16)

    kt_all = jnp.concatenate([ktp_ref[...], ktc_ref[...]], axis=1)
    v_all = jnp.concatenate([vp_ref[...], vc_ref[...]], axis=0).astype(F32)
    v_swap = pltpu.roll(v_all, SWA_HD, axis=1)

    def scores(c):
        win = slice(c * L, (c + 2) * L)
        out = []
        for g in range(SWA_KV):
            kt2 = kt_all[g * SWA_HD:(g + 1) * SWA_HD, win]
            kblk = jnp.concatenate([jnp.concatenate([kt2, zeros_k], axis=0),
                                    jnp.concatenate([zeros_k, kt2], axis=0)], axis=1)
            for pp in range(2 * g, 2 * g + 2):
                out.append(_mm(q_ref[c * L:(c + 1) * L, pp * LANES:(pp + 1) * LANES], kblk))
        return out

    def weights(c, s_list):
        allowed = first if c == 0 else band
        out = []
        for head in range(SWA_HEADS):
            s = s_list[head // 2][:, (head % 2) * 2 * L:(head % 2 + 1) * 2 * L]
            sc = jnp.where(allowed, s * (SWA_HD ** -0.5), NEG_INF)
            sink = sink_ref[head:head + 1, 0:1]
            mx = jnp.broadcast_to(jnp.maximum(jnp.max(sc, axis=1, keepdims=True), sink), sc.shape)
            out.append((jnp.exp(sc - mx).astype(BF16), jnp.exp(sink - mx[:, :LANES])))
        return out

    def outputs(c, e_list):
        win = slice(c * L, (c + 2) * L)
        v2 = v_all[win, :]
        v2s = v_swap[win, :]
        for g in range(SWA_KV):
            va = jnp.where(low_half, v2 if g == 0 else v2s, 0.0).astype(BF16)
            vb = jnp.where(low_half, 0.0, v2s if g == 0 else v2).astype(BF16)
            vden = jnp.concatenate([jnp.concatenate([va, ones_lo], axis=1),
                                    jnp.concatenate([vb, ones_hi], axis=1)], axis=0)
            for pp in range(2 * g, 2 * g + 2):
                (e0, k0), (e1, k1) = e_list[2 * pp], e_list[2 * pp + 1]
                res = _mm(jnp.concatenate([e0, e1], axis=1), vden)
                den = res[:, LANES:] + jnp.where(out_low, k0, k1)
                y_ref[c * L:(c + 1) * L, pp * LANES:(pp + 1) * LANES] = (res[:, :LANES] * (1.0 / den)).astype(BF16)

    s_next = scores(0)
    for c in range(nblk):
        s_cur = s_next
        if c + 1 < nblk:
            s_next = scores(c + 1)
        outputs(c, weights(c, s_cur))


def _swa(sq, skt, sv, sinks_all, layer, nb, seq):
    nt = seq // TM_MIX
    per = TM_MIX // WINDOW
    cur_r = lambda b, j: (b * nt + j, 0)
    cur_c = lambda b, j: (0, b * nt + j)
    prev_r = lambda b, j: (jnp.maximum((b * nt + j) * per - 1, b * nt * per), 0)
    prev_c = lambda b, j: (0, jnp.maximum((b * nt + j) * per - 1, b * nt * per))
    return pl.pallas_call(
        _swa_kernel,
        grid=(nb, nt),
        in_specs=[pl.BlockSpec((TM_MIX, BRANCH), cur_r),
                  pl.BlockSpec((LANES, WINDOW), prev_c),
                  pl.BlockSpec((LANES, TM_MIX), cur_c),
                  pl.BlockSpec((WINDOW, LANES), prev_r),
                  pl.BlockSpec((TM_MIX, LANES), cur_r),
                  _layer_spec((8, LANES), layer)],
        out_specs=pl.BlockSpec((TM_MIX, BRANCH), cur_r),
        out_shape=jax.ShapeDtypeStruct(sq.shape, BF16),
        compiler_params=_params(("arbitrary", "arbitrary")),
        name="swa_prompt",
    )(sq, skt, skt, sv, sv, sinks_all)


def _cross_kernel(q_ref, kt_ref, v_ref, y_ref):
    hs = lambda h: slice(h * X_HD, (h + 1) * X_HD)
    scores = [_mm(q_ref[:, hs(h)], kt_ref[hs(h), :]) for h in range(X_HEADS)]
    probs = []
    for s in scores:
        s = s * (X_HD ** -0.5)
        probs.append(jnp.exp(s - jnp.max(s, axis=1, keepdims=True)).astype(BF16))
    ones = jnp.ones((MEM_TOKENS, X_HD), BF16)
    for h, e in enumerate(probs):
        res = _mm(e, jnp.concatenate([v_ref[:, hs(h)], ones], axis=1))
        y_ref[:, hs(h)] = (res[:, :X_HD] * (1.0 / res[:, X_HD:])).astype(BF16)


def _cross(xq, mkt, mv16, nb, seq):
    nt = seq // TQ_CROSS
    row = lambda b, i: (b * nt + i, 0)
    return pl.pallas_call(
        _cross_kernel,
        grid=(nb, nt),
        in_specs=[pl.BlockSpec((TQ_CROSS, BRANCH), row),
                  pl.BlockSpec((None, BRANCH, MEM_TOKENS), lambda b, i: (b, 0, 0)),
                  pl.BlockSpec((MEM_TOKENS, BRANCH), lambda b, i: (b, 0))],
        out_specs=pl.BlockSpec((TQ_CROSS, BRANCH), row),
        out_shape=jax.ShapeDtypeStruct(xq.shape, BF16),
        compiler_params=_params(("arbitrary", "arbitrary")),
        name="cross_prompt",
    )(xq, mkt, mv16)


def _merge_ffn_kernel(x_ref, ya_ref, yb_ref, yc_ref, wgl_ref, wbr_ref, wmix_ref, wfi_ref, wfo_ref,
                      g1_ref, b1_ref, g2_ref, b2_ref, o_ref):
    x = x_ref[...]
    xb = x.astype(BF16)
    acc = None
    for r, y_ref in enumerate((ya_ref, yb_ref, yc_ref)):
        gate = _sigmoid(_mm(xb, wgl_ref[:, r * D_MODEL:(r + 1) * D_MODEL]))
        term = gate * _mm(y_ref[...], wbr_ref[r])
        acc = term if acc is None else acc + term
    mixed = _mm(acc.astype(BF16), wmix_ref[...])
    x1 = _layer_norm(DEEPNORM_ALPHA * x + mixed, g1_ref[...], b1_ref[...])
    x1b = x1.astype(BF16)
    gpre = _mm(x1b, wfi_ref[:, :D_FF])
    up = _mm(x1b, wfi_ref[:, D_FF:])
    act = (gpre * _sigmoid(gpre) * up).astype(BF16)
    ffn = _mm(act, wfo_ref[...])
    o_ref[...] = _layer_norm(DEEPNORM_ALPHA * x1 + ffn, g2_ref[...], b2_ref[...])


def _merge_ffn(x2d, ya, yb, yc, w_all, wbr, wmix, wfi, wfo, g1, b1, g2, b2, layer, tm):
    m = x2d.shape[0]
    row = lambda i: (i, 0)
    vec = _layer_spec((1, D_MODEL), layer)
    return pl.pallas_call(
        _merge_ffn_kernel,
        grid=(m // tm,),
        in_specs=[pl.BlockSpec((tm, D_MODEL), row),
                  pl.BlockSpec((tm, BRANCH), row),
                  pl.BlockSpec((tm, BRANCH), row),
                  pl.BlockSpec((tm, BRANCH), row),
                  _layer_spec((D_MODEL, W_HALF), layer, 0, 0),
                  _layer_spec((3, BRANCH, D_MODEL), layer),
                  _layer_spec((D_MODEL, D_MODEL), layer),
                  _layer_spec((D_MODEL, 2 * D_FF), layer),
                  _layer_spec((D_FF, D_MODEL), layer),
                  vec, vec, vec, vec],
        out_specs=pl.BlockSpec((tm, D_MODEL), row),
        out_shape=jax.ShapeDtypeStruct((m, D_MODEL), F32),
        compiler_params=_params(("arbitrary",)),
        name="merge_ffn",
    )(x2d, ya, yb, yc, w_all, wbr, wmix, wfi, wfo, g1, b1, g2, b2)


_P2_MK, _P2_SK, _P2_END = 0, 512, 640


def _dproj_kernel(x_ref, wn_ref, wt_ref, bias_ref, cos_ref, sin_ref, cost_ref, sint_ref, p_ref, pt_ref, p2_ref):
    xb = x_ref[...].astype(BF16)
    cos = cos_ref[...]
    sin = sin_ref[...]
    for c in range(_N_END // LANES):
        cs = slice(c * LANES, (c + 1) * LANES)
        blk = _mm(xb, wn_ref[:, cs])
        if _N_SQ <= c * LANES < _N_XQ:
            blk = _rope_lanes(blk, cos, sin)
        p_ref[:, cs] = blk
    pt_ref[_T_MV:_T_MK, :] = _mm_nt(wt_ref[_T_MV:_T_MK, :], xb)
    kt = _mm_nt(wt_ref[_T_MK:_T_SK, :], xb) * (ML_DK ** -0.5)
    pt_ref[_T_MK:_T_SK, :] = kt
    skt = _mm_nt(wt_ref[_T_SK:_T_SV, :], xb)
    cost = cost_ref[...]
    sint = sint_ref[...]
    for g in range(SWA_KV):
        base = g * SWA_HD
        for off, val in _rope_rows(skt[base:base + SWA_HD, :], cost, sint):
            pt_ref[_T_SK + base + off:_T_SK + base + off + val.shape[0], :] = val
    pt_ref[_T_SV:_T_G, :] = _mm_nt(wt_ref[_T_SV:_T_G, :], xb)
    pt_ref[_T_G:_T_END, :] = _mm_nt(wt_ref[_T_G:_T_END, :], xb) + bias_ref[...]
    for c in range(BRANCH // LANES):
        p2_ref[:, _P2_MK + c * LANES:_P2_MK + (c + 1) * LANES] = kt[c * LANES:(c + 1) * LANES, :].T
    p2_ref[:, _P2_SK:_P2_END] = pt_ref[_T_SK:_T_SV, :].T


def _dproj(xs, w_all, wt_all, bias_all, cos, sin, cost, sint, layer):
    n = xs.shape[0]
    whole = lambda shape: pl.BlockSpec(shape, lambda i: (0, 0))
    return pl.pallas_call(
        _dproj_kernel,
        grid=(1,),
        in_specs=[_const_spec((n, D_MODEL)),
                  _layer_spec((D_MODEL, W_HALF), layer, 0, 1),
                  _layer_spec((_T_END, D_MODEL), layer),
                  _layer_spec((16, 1), layer),
                  _const_spec((n, LANES)), _const_spec((n, LANES)),
                  _const_spec((ROT_HALF, n)), _const_spec((ROT_HALF, n))],
        out_specs=[whole((n, _N_END)), whole((_T_END, n)), whole((n, _P2_END))],
        out_shape=[jax.ShapeDtypeStruct((n, _N_END), F32),
                   jax.ShapeDtypeStruct((_T_END, n), F32),
                   jax.ShapeDtypeStruct((n, _P2_END), F32)],
        compiler_params=_params(("arbitrary",)),
        name="decode_proj",
    )(xs, w_all, wt_all, bias_all, cos, sin, cost, sint)


def _dmlstm_kernel(q_ref, v_ref, mo_ref, k_ref, vt_ref, gt_ref, c_ref, n_ref, m_ref, gain_ref, *rest):
    y_ref, c_out_ref, n_out_ref, m_out_ref = rest[-4:]
    i = pl.program_id(0)
    bb = q_ref.shape[0]
    nlanes = gt_ref.shape[1]
    li = gt_ref[0:ML_HEADS, :]
    lf = _log_sigmoid(gt_ref[8:8 + ML_HEADS, :])
    m_prev = m_ref[...]
    m_t = jnp.maximum(lf + m_prev, li)
    m_out_ref[...] = m_t
    scal = jnp.concatenate([jnp.exp(li - m_t), jnp.exp(lf + m_prev - m_t), jnp.exp(-m_t), jnp.zeros_like(m_t)], axis=0)
    bring = jnp.where(i == 0, 0, nlanes - i * bb)
    scal = pltpu.roll(scal, bring, axis=1)
    tiles = [(h, j) for h in range(ML_HEADS) for j in range(bb)]
    hs = lambda h: slice(h * ML_DK, (h + 1) * ML_DK)
    nrow_of = lambda h, j: slice(j * ML_HEADS + h, j * ML_HEADS + h + 1)
    q = {(h, j): q_ref[j:j + 1, hs(h)] for h, j in tiles}
    k = {(h, j): k_ref[j:j + 1, hs(h)] for h, j in tiles}
    w = {(h, j): scal[h:h + 1, j:j + 1] for h, j in tiles}
    a = {(h, j): scal[ML_HEADS + h:ML_HEADS + h + 1, j:j + 1] for h, j in tiles}
    cq = {(h, j): _mm_nt(jnp.broadcast_to(q[h, j], (8, ML_DK)).astype(BF16), c_ref[j, h].astype(BF16))[0:1, :]
          for h, j in tiles}
    qk = {t: jnp.sum(q[t] * k[t], axis=1, keepdims=True) for t in tiles}
    nq = {(h, j): jnp.sum(n_ref[nrow_of(h, j), :] * q[h, j], axis=1, keepdims=True) for h, j in tiles}
    hrow = {}
    for h, j in tiles:
        sw = qk[h, j] * w[h, j]
        floor = scal[2 * ML_HEADS + h:2 * ML_HEADS + h + 1, j:j + 1]
        den = jnp.maximum(jnp.abs(sw + a[h, j] * nq[h, j]), floor)
        hrow[h, j] = (sw * v_ref[j:j + 1, hs(h)] + a[h, j] * cq[h, j]) / den
        n_out_ref[nrow_of(h, j), :] = a[h, j] * n_ref[nrow_of(h, j), :] + w[h, j] * k[h, j]
    for h in range(ML_HEADS):
        vt = pltpu.roll(vt_ref[hs(h), :], bring, axis=1)
        for j in range(bb):
            c_out_ref[j, h] = a[h, j] * c_ref[j, h] + (w[h, j] * vt[:, j:j + 1]) * k[h, j]
    hc = {t: hrow[t] - jnp.mean(hrow[t], axis=1, keepdims=True) for t in tiles}
    var = {t: jnp.mean(hc[t] * hc[t], axis=1, keepdims=True) for t in tiles}
    for h, j in tiles:
        hn = hc[h, j] * lax.rsqrt(var[h, j] + HEAD_NORM_EPS) * gain_ref[:, hs(h)]
        y_ref[j:j + 1, hs(h)] = (_sigmoid(mo_ref[j:j + 1, hs(h)]) * hn).astype(BF16)


def _dmlstm(p, p2, pt, c_all, n_all, mt_all, gain_all, layer, c_stack):
    n = p.shape[0]
    blk = lambda cidx: pl.BlockSpec((BB, BRANCH), lambda i: (i, cidx))
    in_specs = [blk(_N_MQ // BRANCH), blk(_N_MV // BRANCH), blk(_N_MO // BRANCH),
                pl.BlockSpec((BB, BRANCH), lambda i: (i, _P2_MK // BRANCH)),
                pl.BlockSpec((BRANCH, n), lambda i: (_T_MV // BRANCH, 0), pipeline_mode=pl.Buffered(1)),
                pl.BlockSpec((16, n), lambda i: (_T_G // 16, 0), pipeline_mode=pl.Buffered(1)),
                pl.BlockSpec((None, BB, ML_HEADS, ML_DV, ML_DK), lambda i: (layer, i, 0, 0, 0)),
                pl.BlockSpec((None, BB * ML_HEADS, ML_DK), lambda i: (layer, i, 0)),
                _layer_spec((ML_HEADS, n), layer),
                _layer_spec((1, BRANCH), layer)]
    args = [p, p, p, p2, pt, pt, c_all, n_all, mt_all, gain_all]
    aliases = {}
    if c_stack is not None:
        in_specs.append(pl.BlockSpec(memory_space=pl.ANY))
        args.append(c_stack)
        aliases = {len(args) - 1: 1}
    return pl.pallas_call(
        _dmlstm_kernel,
        grid=(n // BB,),
        in_specs=in_specs,
        out_specs=[pl.BlockSpec((BB, BRANCH), lambda i: (i, 0)),
                   pl.BlockSpec((None, BB, ML_HEADS, ML_DV, ML_DK), lambda i: (layer, i, 0, 0, 0)),
                   pl.BlockSpec((BB * ML_HEADS, ML_DK), lambda i: (i, 0)),
                   pl.BlockSpec((ML_HEADS, n), lambda i: (0, 0))],
        out_shape=[jax.ShapeDtypeStruct((n, BRANCH), BF16),
                   jax.ShapeDtypeStruct((DEPTH, n, ML_HEADS, ML_DV, ML_DK), F32),
                   jax.ShapeDtypeStruct((n * ML_HEADS, ML_DK), F32),
                   jax.ShapeDtypeStruct((ML_HEADS, n), F32)],
        input_output_aliases=aliases,
        compiler_params=_params(("arbitrary",)),
        name="decode_mlstm",
    )(*args)


def _dswa_kernel(q_ref, kn_ref, vn_ref, kvt_ref, ck_ref, cv_ref, sink_ref, *rest):
    y_ref, ko_ref, vo_ref = rest[-3:]
    i = pl.program_id(0)
    bb = q_ref.shape[0]
    lane = lax.broadcasted_iota(jnp.int32, (LANES, LANES), 1)
    row8 = lax.broadcasted_iota(jnp.int32, (SWA_HEADS, LANES), 0)
    low8 = lax.broadcasted_iota(jnp.int32, (SWA_HEADS, LANES), 1) < SWA_HD
    low1 = lax.broadcasted_iota(jnp.int32, (1, LANES), 1) < SWA_HD
    scale = SWA_HD ** -0.5
    sink = sink_ref[:, 0:1]
    knew_t = kvt_ref[0:LANES, :]
    vnew_t = kvt_ref[LANES:, :]
    for j in range(bb):
        ck = ck_ref[j]
        cv = cv_ref[j]
        kn = kn_ref[j:j + 1, :]
        vn = vn_ref[j:j + 1, :]
        qm = jnp.zeros((SWA_HEADS, LANES), F32)
        for pp in range(SWA_HEADS // 2):
            g = pp // 2
            pair = q_ref[j:j + 1, pp * LANES:(pp + 1) * LANES]
            swap = pltpu.roll(pair, SWA_HD, axis=1)
            in_g = low8 if g == 0 else jnp.logical_not(low8)
            for t in range(2):
                qm = jnp.where((row8 == 2 * pp + t) & in_g, pair if t == g else swap, qm)
        s = _mm(qm.astype(BF16), ck.astype(BF16)) * scale
        s_new = jnp.sum(qm * kn, axis=1, keepdims=True) * scale
        mx = jnp.maximum(jnp.maximum(jnp.max(s, axis=1, keepdims=True), s_new), sink)
        e = jnp.exp(s - mx)
        e_new = jnp.exp(s_new - mx)
        den = jnp.sum(e, axis=1, keepdims=True) + e_new + jnp.exp(sink - mx)
        o = (_mm_nt(e.astype(BF16), cv.astype(BF16)) + e_new * vn) / den
        for pp in range(SWA_HEADS // 2):
            g = pp // 2
            halves = []
            for t in range(2):
                oh = o[2 * pp + t:2 * pp + t + 1, :]
                halves.append(oh if t == g else pltpu.roll(oh, SWA_HD, axis=1))
            y_ref[j:j + 1, pp * LANES:(pp + 1) * LANES] = jnp.where(low1, halves[0], halves[1]).astype(BF16)
        bring = LANES - 1 - (i * bb + j)
        ko_ref[j] = jnp.where(lane == LANES - 1, pltpu.roll(knew_t, bring, axis=1), pltpu.roll(ck, LANES - 1, axis=1))
        vo_ref[j] = jnp.where(lane == LANES - 1, pltpu.roll(vnew_t, bring, axis=1), pltpu.roll(cv, LANES - 1, axis=1))


def _dswa(p, p2, pt, ck_all, cv_all, sinks_all, layer, k_stack, v_stack):
    n = p.shape[0]
    cache = pl.BlockSpec((None, BB, LANES, WINDOW), lambda i: (layer, i, 0, 0))
    in_specs = [pl.BlockSpec((BB, BRANCH), lambda i: (i, _N_SQ // BRANCH)),
                pl.BlockSpec((BB, LANES), lambda i: (i, _P2_SK // LANES)),
                pl.BlockSpec((BB, LANES), lambda i: (i, _N_SV // LANES)),
                pl.BlockSpec((2 * LANES, n), lambda i: (_T_SK // (2 * LANES), 0), pipeline_mode=pl.Buffered(1)),
                cache, cache, _layer_spec((8, LANES), layer)]
    args = [p, p2, p, pt, ck_all, cv_all, sinks_all]
    aliases = {}
    if k_stack is not None:
        in_specs += [pl.BlockSpec(memory_space=pl.ANY), pl.BlockSpec(memory_space=pl.ANY)]
        args += [k_stack, v_stack]
        aliases = {len(args) - 2: 1, len(args) - 1: 2}
    return pl.pallas_call(
        _dswa_kernel,
        grid=(n // BB,),
        in_specs=in_specs,
        out_specs=[pl.BlockSpec((BB, BRANCH), lambda i: (i, 0)), cache, cache],
        out_shape=[jax.ShapeDtypeStruct((n, BRANCH), BF16),
                   jax.ShapeDtypeStruct((DEPTH, n, LANES, WINDOW), F32),
                   jax.ShapeDtypeStruct((DEPTH, n, LANES, WINDOW), F32)],
        input_output_aliases=aliases,
        compiler_params=_params(("arbitrary",)),
        name="decode_swa",
    )(*args)


def _dcross_kernel(q_ref, k_ref, v_ref, y_ref):
    bb = q_ref.shape[0]
    scale = X_HD ** -0.5
    row8 = lax.broadcasted_iota(jnp.int32, (8, LANES), 0) % X_HEADS
    for j in range(bb):
        qrep = jnp.zeros((8, LANES), F32)
        for h in range(X_HEADS):
            qrep = jnp.where(row8 == h, q_ref[j:j + 1, h * X_HD:(h + 1) * X_HD], qrep)
        s = jnp.sum(k_ref[j] * qrep[None], axis=2, keepdims=True) * scale
        mx8 = jnp.max(s, axis=0)
        mx4 = jnp.maximum(mx8[0:X_HEADS], mx8[X_HEADS:])
        e = jnp.exp(s - jnp.concatenate([mx4, mx4], axis=0)[None])
        den8 = jnp.sum(e, axis=0)
        o8 = jnp.sum(e * v_ref[j], axis=0)
        o4 = (o8[0:X_HEADS] + o8[X_HEADS:]) / (den8[0:X_HEADS] + den8[X_HEADS:])
        for h in range(X_HEADS):
            y_ref[j:j + 1, h * X_HD:(h + 1) * X_HD] = o4[h:h + 1, :].astype(BF16)


def _dcross(p, mk_all, mv_all, layer):
    n = p.shape[0]
    cache = pl.BlockSpec((None, BB, MEM_TOKENS * X_HEADS // 8, 8, X_HD), lambda i: (layer, i, 0, 0, 0))
    return pl.pallas_call(
        _dcross_kernel,
        grid=(n // BB,),
        in_specs=[pl.BlockSpec((BB, BRANCH), lambda i: (i, _N_XQ // BRANCH)), cache, cache],
        out_specs=pl.BlockSpec((BB, BRANCH), lambda i: (i, 0)),
        out_shape=jax.ShapeDtypeStruct((n, BRANCH), BF16),
        compiler_params=_params(("arbitrary",)),
        name="decode_cross",
    )(p, mk_all, mv_all)


def _rope_tables(positions):
    inv_freq = ROPE_THETA ** (-jnp.arange(ROT_HALF, dtype=F32) / ROT_HALF)
    ang = positions.astype(F32)[:, None] * inv_freq[None, :]
    cos = jnp.cos(ang)
    sin = jnp.sin(ang)
    reps = LANES // ROT_HALF
    return jnp.tile(cos, (1, reps)), jnp.tile(sin, (1, reps)), cos.T, sin.T


def kernel(x_prompt, x_sample, mem_prompt, cache_swa_k, cache_swa_v, cache_mem_k, cache_mem_v, state_mlstm_c, state_mlstm_n, state_mlstm_m, w_in, b_gates, mlstm_norm_g, swa_sinks, w_mem_kv, w_branch, w_mix_out, ln1_g, ln1_b, w_ffn_in, w_ffn_out, ln2_g, ln2_b):
    nb, seq, _ = x_prompt.shape
    ns = x_sample.shape[0]
    assert ns == LANES and PAST_LEN >= WINDOW

    cosn, sinn, cost, sint = _rope_tables(jnp.arange(seq))
    cos_s, sin_s, cost_s, sint_s = _rope_tables(jnp.full((ns,), PAST_LEN))

    ck_all = jnp.transpose(cache_swa_k, (0, 1, 3, 4, 2)).reshape(DEPTH, ns, SWA_KV * SWA_HD, WINDOW)
    cv_all = jnp.transpose(cache_swa_v, (0, 1, 3, 4, 2)).reshape(DEPTH, ns, SWA_KV * SWA_HD, WINDOW)
    mk_all = cache_mem_k.reshape(DEPTH, ns, MEM_TOKENS * X_HEADS // 8, 8, X_HD)
    mv_all = cache_mem_v.reshape(DEPTH, ns, MEM_TOKENS * X_HEADS // 8, 8, X_HD)
    n_all = state_mlstm_n.reshape(DEPTH, ns * ML_HEADS, ML_DK)
    mt_all = jnp.transpose(state_mlstm_m, (0, 2, 1))

    zcols = lambda n: jnp.zeros((DEPTH, D_MODEL, n), F32)
    w_all = jnp.concatenate([w_in[..., _C_GL:], w_in[..., _C_MQ:_C_MK], w_in[..., _C_MV:_C_MI],
                             w_in[..., _C_SQ:_C_SK], w_in[..., _C_XQ:_C_GL], w_in[..., _C_SV:_C_XQ],
                             zcols(W_HALF - _N_END)], axis=-1).astype(BF16)
    wt_all = jnp.swapaxes(jnp.concatenate([w_in[..., _C_MV:_C_MO], w_in[..., _C_MK:_C_MV], w_in[..., _C_SK:_C_XQ],
                                           w_in[..., _C_MI:_C_MF], zcols(4), w_in[..., _C_MF:_C_SQ], zcols(4)],
                                          axis=-1), 1, 2).astype(BF16)
    z4 = jnp.zeros((DEPTH, 4), F32)
    bias_all = jnp.concatenate([b_gates[:, :ML_HEADS], z4, b_gates[:, ML_HEADS:], z4], axis=1)[..., None]
    gain_all = mlstm_norm_g[:, None, :]
    sinks_all = jnp.broadcast_to(swa_sinks[:, :, None], (DEPTH, SWA_HEADS, LANES))
    wkv_all = w_mem_kv.astype(BF16)
    wkt_all = jnp.swapaxes(w_mem_kv[..., :BRANCH], 1, 2).astype(BF16)
    wbr_all = w_branch.astype(BF16)
    wmix_all = w_mix_out.astype(BF16)
    wfi_all = w_ffn_in.astype(BF16)
    wfo_all = w_ffn_out.astype(BF16)
    ln_all = (ln1_g[:, None, :], ln1_b[:, None, :], ln2_g[:, None, :], ln2_b[:, None, :])

    yp = x_prompt.reshape(nb * seq, D_MODEL)
    ys = x_sample.reshape(ns, D_MODEL)
    mem2d = mem_prompt.reshape(nb * MEM_TOKENS, D_MODEL)

    outs = {k: [] for k in ("kp", "vp", "mk", "mv", "cp", "np", "mp", "ns", "ms")}
    c_stack = k_stack = v_stack = None
    for l in range(DEPTH):
        mk32, mv32, mkt, mv16 = _memkv(mem2d, wkv_all, wkt_all, l, nb)
        q, v, mo, sq, xq, sv, kt, skt, gt, gc, k32, v32, m_fin = _proj(yp, w_all, wt_all, bias_all, cosn, sinn,
                                                                       cost, sint, l, nb, seq)
        ya, s_fin = _mlstm(q, v, kt, mo, gt, gc, gain_all, l, nb, seq)
        yb = _swa(sq, skt, sv, sinks_all, l, nb, seq)
        yc = _cross(xq, mkt, mv16, nb, seq)
        yp = _merge_ffn(yp, ya, yb, yc, w_all, wbr_all, wmix_all, wfi_all, wfo_all, *ln_all, l, TM_MERGE)
        outs["kp"].append(jnp.transpose(k32.reshape(nb, SWA_KV, SWA_HD, WINDOW), (0, 3, 1, 2)))
        outs["vp"].append(jnp.transpose(v32.reshape(nb, SWA_KV, SWA_HD, WINDOW), (0, 3, 1, 2)))
        outs["mk"].append(mk32.reshape(nb, MEM_TOKENS, X_HEADS, X_HD))
        outs["mv"].append(mv32.reshape(nb, MEM_TOKENS, X_HEADS, X_HD))
        outs["cp"].append(jnp.swapaxes(s_fin[..., :ML_DV], -1, -2))
        outs["np"].append(s_fin[..., ML_DV])
        outs["mp"].append(m_fin.reshape(nb, 8, LANES)[:, :ML_HEADS, 0])

        p, pt, p2 = _dproj(ys, w_all, wt_all, bias_all, cos_s, sin_s, cost_s, sint_s, l)
        ya_s, c_stack, n_new, m_new = _dmlstm(p, p2, pt, state_mlstm_c, n_all, mt_all, gain_all, l, c_stack)
        yb_s, k_stack, v_stack = _dswa(p, p2, pt, ck_all, cv_all, sinks_all, l, k_stack, v_stack)
        yc_s = _dcross(p, mk_all, mv_all, l)
        ys = _merge_ffn(ys, ya_s, yb_s, yc_s, w_all, wbr_all, wmix_all, wfi_all, wfo_all, *ln_all, l, ns)
        outs["ns"].append(n_new.reshape(ns, ML_HEADS, ML_DK))
        outs["ms"].append(m_new)

    st = {k: jnp.stack(vals) for k, vals in outs.items()}
    window_out = lambda t: jnp.transpose(t.reshape(DEPTH, ns, SWA_KV, SWA_HD, WINDOW), (0, 1, 4, 2, 3))
    return (yp.reshape(nb, seq, D_MODEL), ys.reshape(ns, 1, D_MODEL),
            st["kp"], st["vp"], window_out(k_stack), window_out(v_stack), st["mk"], st["mv"],
            st["cp"], st["np"], st["mp"], c_stack, st["ns"], jnp.transpose(st["ms"], (0, 2, 1)))
```

```python
import functools

import jax
import jax.numpy as jnp
from jax import lax
from jax.experimental import pallas as pl
from jax.experimental.pallas import tpu as pltpu

F32 = jnp.float32
BF16 = jnp.bfloat16

D_MODEL = 1024
DEPTH = 2
BRANCH = 512
ML_HEADS = 4
ML_DK = 128
ML_DV = 128
ML_CHUNK = 128
SWA_HD = 64
SWA_HEADS = 8
SWA_KV = 2
SWA_GROUP = 4
WINDOW = 128
ROT_DIM = 16
ROT_HALF = 8
ROPE_THETA = 500000.0
MEM_TOKENS = 256
X_HEADS = 4
X_HD = 128
D_FF = 2816
LN_EPS = 1e-5
HEAD_NORM_EPS = 1e-6
DEEPNORM_ALPHA = (2 * DEPTH) ** 0.25
NEG_INF = -1e30
PAST_LEN = 8192

LANES = 128
VMEM_LIMIT = 56 * 1024 * 1024

_C_MQ, _C_MK, _C_MV, _C_MO = 0, 512, 1024, 1536
_C_MI, _C_MF = 2048, 2052
_C_SQ, _C_SK, _C_SV, _C_XQ, _C_GL = 2056, 2568, 2696, 2824, 3336

W_HALF = 3 * D_MODEL
_N_MQ, _N_MV, _N_MO, _N_SQ, _N_XQ, _N_SV, _N_END = 0, 512, 1024, 1536, 2048, 2560, 2688
_T_MV, _T_MK, _T_SK, _T_SV, _T_G, _T_END = 0, 512, 1024, 1152, 1280, 1296

TM_PROJ = 512
TM_MIX = 512
TM_MERGE = 256
TQ_CROSS = 512
BB = 8


def _mm(a, b):
    return jnp.dot(a, b, preferred_element_type=F32)


def _mm_nt(a, b):
    return lax.dot_general(a, b, (((1,), (1,)), ((), ())), preferred_element_type=F32)


def _sigmoid(x):
    return 1.0 / (1.0 + jnp.exp(-x))


def _log_sigmoid(x):
    return jnp.minimum(x, 0.0) - jnp.log(1.0 + jnp.exp(-jnp.abs(x)))


def _layer_norm(x, g, b):
    mu = jnp.mean(x, axis=-1, keepdims=True)
    xc = x - mu
    var = jnp.mean(xc * xc, axis=-1, keepdims=True)
    return xc * lax.rsqrt(var + LN_EPS) * g + b


def _rope_lanes(x, cos, sin):
    lane = lax.broadcasted_iota(jnp.int32, x.shape, 1) % SWA_HD
    up = pltpu.roll(x, LANES - ROT_HALF, axis=1)
    dn = pltpu.roll(x, ROT_HALF, axis=1)
    first = x * cos - up * sin
    second = x * cos + dn * sin
    return jnp.where(lane < ROT_HALF, first, jnp.where(lane < ROT_DIM, second, x))


def _rope_rows(xt, cost, sint):
    x1 = xt[0:ROT_HALF, :]
    x2 = xt[ROT_HALF:ROT_DIM, :]
    return ((0, x1 * cost - x2 * sint), (ROT_HALF, x2 * cost + x1 * sint), (ROT_DIM, xt[ROT_DIM:SWA_HD, :]))


def _const_spec(shape):
    nd = len(shape)
    return pl.BlockSpec(shape, lambda *_: (0,) * nd, pipeline_mode=pl.Buffered(1))


def _layer_spec(shape, layer, *tail):
    idx = (layer,) + (tail if tail else (0,) * len(shape))
    return pl.BlockSpec((None,) + tuple(shape), lambda *_: idx, pipeline_mode=pl.Buffered(1))


def _params(sem):
    return pltpu.CompilerParams(dimension_semantics=sem, vmem_limit_bytes=VMEM_LIMIT)


def _memkv_kernel(mem_ref, wkv_ref, wkt_ref, k32_ref, v32_ref, kt_ref, v16_ref):
    m = mem_ref[...].astype(BF16)
    kv = _mm(m, wkv_ref[...])
    k32_ref[...] = kv[:, :BRANCH]
    v32_ref[...] = kv[:, BRANCH:]
    v16_ref[...] = kv[:, BRANCH:].astype(BF16)
    kt_ref[...] = _mm_nt(wkt_ref[...], m).astype(BF16)


def _memkv(mem2d, wkv, wkt, layer, nb):
    rows = mem2d.shape[0]
    return pl.pallas_call(
        _memkv_kernel,
        grid=(nb,),
        in_specs=[pl.BlockSpec((MEM_TOKENS, D_MODEL), lambda b: (b, 0)),
                  _layer_spec((D_MODEL, 2 * BRANCH), layer),
                  _layer_spec((BRANCH, D_MODEL), layer)],
        out_specs=[pl.BlockSpec((MEM_TOKENS, BRANCH), lambda b: (b, 0)),
                   pl.BlockSpec((MEM_TOKENS, BRANCH), lambda b: (b, 0)),
                   pl.BlockSpec((None, BRANCH, MEM_TOKENS), lambda b: (b, 0, 0)),
                   pl.BlockSpec((MEM_TOKENS, BRANCH), lambda b: (b, 0))],
        out_shape=[jax.ShapeDtypeStruct((rows, BRANCH), F32),
                   jax.ShapeDtypeStruct((rows, BRANCH), F32),
                   jax.ShapeDtypeStruct((nb, BRANCH, MEM_TOKENS), BF16),
                   jax.ShapeDtypeStruct((rows, BRANCH), BF16)],
        compiler_params=_params(("arbitrary",)),
        name="memkv_proj",
    )(mem2d, wkv, wkt)


def _mlstm_gate_weights(pre, gt_ref, gc_ref, m_out_ref, m_scr, first_tile):
    tm = pre.shape[1]

    @pl.when(first_tile)
    def _():
        m_scr[...] = jnp.zeros_like(m_scr)

    li = pre[0:8, :]
    lane8 = lax.broadcasted_iota(jnp.int32, li.shape, 1) % ML_CHUNK
    b = _log_sigmoid(pre[8:16, :])
    shift = 1
    while shift < ML_CHUNK:
        b = b + jnp.where(lane8 >= shift, pltpu.roll(b, shift, axis=1), 0.0)
        shift *= 2
    g = li - b
    cm = g
    shift = 1
    while shift < ML_CHUNK:
        cm = jnp.maximum(cm, jnp.where(lane8 >= shift, pltpu.roll(cm, shift, axis=1), -jnp.inf))
        shift *= 2
    gt_ref[0:8, :] = g
    pad = jnp.zeros((LANES - 24, ML_CHUNK), F32)
    m_prev = m_scr[...]
    for c in range(tm // ML_CHUNK):
        cs = slice(c * ML_CHUNK, (c + 1) * ML_CHUNK)
        b_c = b[:, cs]
        b_last = jnp.broadcast_to(b_c[:, ML_CHUNK - 1:ML_CHUNK], b_c.shape)
        cm_last = jnp.broadcast_to(cm[:, cs][:, ML_CHUNK - 1:ML_CHUNK], b_c.shape)
        m_t = b_c + jnp.maximum(m_prev, cm[:, cs])
        m_new = b_last + jnp.maximum(m_prev, cm_last)
        gt_ref[8:16, cs] = jnp.exp(b_last + g[:, cs] - m_new)
        gt_ref[16:24, cs] = jnp.exp(b_last + m_prev - m_new)
        rows = jnp.concatenate([b_c - m_t, jnp.exp(b_c + m_prev - m_t), jnp.exp(-m_t), pad], axis=0)
        gc_ref[cs, :] = rows.T
        m_prev = m_new
    m_scr[...] = m_prev
    m_out_ref[...] = m_prev


def _proj_kernel(x_ref, wn_ref, wt_ref, bias_ref, cosn_ref, sinn_ref, cost_ref, sint_ref,
                 q_ref, v_ref, mo_ref, sq_ref, xq_ref, sv_ref, kt_ref, skt_ref, gt_ref, gc_ref, k32_ref, v32_ref,
                 m_out_ref, m_scr, *, tiles_per_seq):
    tm = x_ref.shape[0]
    xb = x_ref[...].astype(BF16)
    tr_rows = lambda lo, hi: _mm_nt(wt_ref[lo:hi, :], xb)
    _mlstm_gate_weights(tr_rows(_T_G, _T_END) + bias_ref[...], gt_ref, gc_ref, m_out_ref, m_scr,
                        pl.program_id(0) % tiles_per_seq == 0)
    q_ref[...] = _mm(xb, wn_ref[:, _N_MQ:_N_MV]).astype(BF16)
    v_ref[...] = _mm(xb, wn_ref[:, _N_MV:_N_MO]).astype(BF16)
    mo_ref[...] = _mm(xb, wn_ref[:, _N_MO:_N_SQ])
    xq_ref[...] = _mm(xb, wn_ref[:, _N_XQ:_N_SV]).astype(BF16)
    cosn = cosn_ref[...]
    sinn = sinn_ref[...]
    sq = _mm(xb, wn_ref[:, _N_SQ:_N_XQ])
    for c in range(BRANCH // LANES):
        blk = _rope_lanes(sq[:, c * LANES:(c + 1) * LANES], cosn, sinn)
        sq_ref[:, c * LANES:(c + 1) * LANES] = blk.astype(BF16)
    sv_ref[...] = _mm(xb, wn_ref[:, _N_SV:_N_END]).astype(BF16)
    kt_ref[...] = (tr_rows(_T_MK, _T_SK) * (ML_DK ** -0.5)).astype(BF16)
    skt = tr_rows(_T_SK, _T_SV)
    svt = tr_rows(_T_SV, _T_G)
    cost = cost_ref[...]
    sint = sint_ref[...]
    tail = slice(tm - WINDOW, tm)
    v32_ref[...] = svt[:, tail]
    for g in range(SWA_KV):
        base = g * SWA_HD
        for off, val in _rope_rows(skt[base:base + SWA_HD, :], cost, sint):
            skt_ref[base + off:base + off + val.shape[0], :] = val.astype(BF16)
            k32_ref[base + off:base + off + val.shape[0], :] = val[:, tail]


def _proj(x2d, w_all, wt_all, bias_all, cosn, sinn, cost, sint, layer, nb, seq):
    m = x2d.shape[0]
    tm = TM_PROJ
    nt = seq // tm
    row = lambda i: (i, 0)
    col = lambda i: (0, i)
    return pl.pallas_call(
        functools.partial(_proj_kernel, tiles_per_seq=nt),
        grid=(m // tm,),
        in_specs=[pl.BlockSpec((tm, D_MODEL), row),
                  _layer_spec((D_MODEL, W_HALF), layer, 0, 1),
                  _layer_spec((_T_END, D_MODEL), layer),
                  _layer_spec((16, 1), layer),
                  pl.BlockSpec((tm, LANES), lambda i: (i % nt, 0)),
                  pl.BlockSpec((tm, LANES), lambda i: (i % nt, 0)),
                  pl.BlockSpec((ROT_HALF, tm), lambda i: (0, i % nt)),
                  pl.BlockSpec((ROT_HALF, tm), lambda i: (0, i % nt))],
        out_specs=[pl.BlockSpec((tm, BRANCH), row),
                   pl.BlockSpec((tm, BRANCH), row),
                   pl.BlockSpec((tm, BRANCH), row),
                   pl.BlockSpec((tm, BRANCH), row),
                   pl.BlockSpec((tm, BRANCH), row),
                   pl.BlockSpec((tm, LANES), row),
                   pl.BlockSpec((BRANCH, tm), col),
                   pl.BlockSpec((LANES, tm), col),
                   pl.BlockSpec((24, tm), col),
                   pl.BlockSpec((tm, LANES), row),
                   pl.BlockSpec((LANES, WINDOW), lambda i: (i // nt, 0)),
                   pl.BlockSpec((LANES, WINDOW), lambda i: (i // nt, 0)),
                   pl.BlockSpec((8, LANES), lambda i: (i // nt, 0))],
        out_shape=[jax.ShapeDtypeStruct((m, BRANCH), BF16),
                   jax.ShapeDtypeStruct((m, BRANCH), BF16),
                   jax.ShapeDtypeStruct((m, BRANCH), F32),
                   jax.ShapeDtypeStruct((m, BRANCH), BF16),
                   jax.ShapeDtypeStruct((m, BRANCH), BF16),
                   jax.ShapeDtypeStruct((m, LANES), BF16),
                   jax.ShapeDtypeStruct((BRANCH, m), BF16),
                   jax.ShapeDtypeStruct((LANES, m), BF16),
                   jax.ShapeDtypeStruct((24, m), F32),
                   jax.ShapeDtypeStruct((m, LANES), F32),
                   jax.ShapeDtypeStruct((nb * LANES, WINDOW), F32),
                   jax.ShapeDtypeStruct((nb * LANES, WINDOW), F32),
                   jax.ShapeDtypeStruct((nb * 8, LANES), F32)],
        scratch_shapes=[pltpu.VMEM((8, LANES), F32)],
        compiler_params=_params(("arbitrary",)),
        name="prompt_proj",
    )(x2d, w_all, wt_all, bias_all, cosn, sinn, cost, sint)


def _mlstm_kernel(q_ref, v_ref, kt_ref, mo_ref, gt_ref, gc_ref, gain_ref, y_ref, s_out_ref, s_ref):
    j = pl.program_id(1)
    L = ML_CHUNK

    @pl.when(j == 0)
    def _():
        s_ref[...] = jnp.zeros_like(s_ref)

    r_i = lax.broadcasted_iota(jnp.int32, (L, L), 0)
    c_i = lax.broadcasted_iota(jnp.int32, (L, L), 1)
    causal = c_i <= r_i
    ones = jnp.ones((L, ML_DV), BF16)

    units = [(c, h) for c in range(q_ref.shape[0] // L) for h in range(ML_HEADS)]
    ts = lambda c: slice(c * L, (c + 1) * L)
    hs = lambda h: slice(h * ML_DK, (h + 1) * ML_DK)

    qk = {(c, h): _mm(q_ref[ts(c), hs(h)], kt_ref[hs(h), ts(c)]) for c, h in units}

    sw, kts, vext = {}, {}, {}
    for c, h in units:
        g_r = gt_ref[h:h + 1, ts(c)]
        es_r = gt_ref[8 + h:9 + h, ts(c)]
        u_c = gc_ref[ts(c), h:h + 1]
        sw[c, h] = (qk[c, h] * jnp.exp(jnp.where(causal, u_c + g_r, -jnp.inf))).astype(BF16)
        kts[c, h] = (kt_ref[hs(h), ts(c)].astype(F32) * es_r).astype(BF16)
        vext[c, h] = jnp.concatenate([v_ref[ts(c), hs(h)], ones], axis=1)

    intra = {u: _mm(sw[u], vext[u]) for u in units}
    delta = {u: _mm(kts[u], vext[u]) for u in units}

    s_in = {}
    for h in range(ML_HEADS):
        state = s_ref[h]
        for c in range(q_ref.shape[0] // L):
            s_in[c, h] = state.astype(BF16)
            state = gt_ref[16 + h:17 + h, c * L:c * L + 1] * state + delta[c, h]
        s_ref[h] = state
    qs_all = {u: _mm(q_ref[ts(u[0]), hs(u[1])], s_in[u]) for u in units}

    hh, hc = {}, {}
    for c, h in units:
        tot = intra[c, h] + gc_ref[ts(c), 8 + h:9 + h] * qs_all[c, h]
        floor = gc_ref[ts(c), 16 + h:17 + h]
        hh[c, h] = tot[:, :ML_DV] * (1.0 / jnp.maximum(jnp.abs(tot[:, ML_DV:]), floor))
    for u in units:
        hc[u] = hh[u] - jnp.mean(hh[u], axis=1, keepdims=True)
    for c, h in units:
        var = jnp.mean(hc[c, h] * hc[c, h], axis=1, keepdims=True)
        hn = hc[c, h] * lax.rsqrt(var + HEAD_NORM_EPS) * gain_ref[:, hs(h)]
        y_ref[ts(c), hs(h)] = (_sigmoid(mo_ref[ts(c), hs(h)]) * hn).astype(BF16)

    @pl.when(j == pl.num_programs(1) - 1)
    def _():
        s_out_ref[...] = s_ref[...]


def _mlstm(q, v, kt, mo, gt, gc, gain_all, layer, nb, seq):
    nc = seq // TM_MIX
    row = lambda b, j: (b * nc + j, 0)
    col = lambda b, j: (0, b * nc + j)
    return pl.pallas_call(
        _mlstm_kernel,
        grid=(nb, nc),
        in_specs=[pl.BlockSpec((TM_MIX, BRANCH), row),
                  pl.BlockSpec((TM_MIX, BRANCH), row),
                  pl.BlockSpec((BRANCH, TM_MIX), col),
                  pl.BlockSpec((TM_MIX, BRANCH), row),
                  pl.BlockSpec((24, TM_MIX), col),
                  pl.BlockSpec((TM_MIX, LANES), row),
                  _layer_spec((1, BRANCH), layer)],
        out_specs=[pl.BlockSpec((TM_MIX, BRANCH), row),
                   pl.BlockSpec((None, ML_HEADS, ML_DK, 2 * ML_DV), lambda b, j: (b, 0, 0, 0))],
        out_shape=[jax.ShapeDtypeStruct(q.shape, BF16),
                   jax.ShapeDtypeStruct((nb, ML_HEADS, ML_DK, 2 * ML_DV), F32)],
        scratch_shapes=[pltpu.VMEM((ML_HEADS, ML_DK, 2 * ML_DV), F32)],
        compiler_params=_params(("arbitrary", "arbitrary")),
        name="mlstm_chunks",
    )(q, v, kt, mo, gt, gc, gain_all)


def _swa_kernel(q_ref, ktp_ref, ktc_ref, vp_ref, vc_ref, sink_ref, y_ref):
    j = pl.program_id(1)
    L = WINDOW
    nblk = q_ref.shape[0] // L
    r_i = lax.broadcasted_iota(jnp.int32, (L, 2 * L), 0)
    c_i = lax.broadcasted_iota(jnp.int32, (L, 2 * L), 1)
    band = (c_i >= r_i) & (c_i <= r_i + L)
    first = band & (c_i >= jnp.where(j == 0, L, 0))
    low_half = lax.broadcasted_iota(jnp.int32, (2 * L, LANES), 1) < SWA_HD
    out_low = lax.broadcasted_iota(jnp.int32, (L, LANES), 1) < SWA_HD
    zeros_k = jnp.zeros((SWA_HD, 2 * L), BF16)
    ones_lo = jnp.where(low_half, 1.0, 0.0).astype(BF16)
    ones_hi = jnp.where(low_half, 0.0, 1.0).astype(BF16)

    kt_all = jnp.concatenate([ktp_ref[...], ktc_ref[...]], axis=1)
    v_all = jnp.concatenate([vp_ref[...], vc_ref[...]], axis=0).astype(F32)
    v_swap = pltpu.roll(v_all, SWA_HD, axis=1)

    def scores(c):
        win = slice(c * L, (c + 2) * L)
        out = []
        for g in range(SWA_KV):
            kt2 = kt_all[g * SWA_HD:(g + 1) * SWA_HD, win]
            kblk = jnp.concatenate([jnp.concatenate([kt2, zeros_k], axis=0),
                                    jnp.concatenate([zeros_k, kt2], axis=0)], axis=1)
            for pp in range(2 * g, 2 * g + 2):
                out.append(_mm(q_ref[c * L:(c + 1) * L, pp * LANES:(pp + 1) * LANES], kblk))
        return out

    def weights(c, s_list):
        allowed = first if c == 0 else band
        out = []
        for head in range(SWA_HEADS):
            s = s_list[head // 2][:, (head % 2) * 2 * L:(head % 2 + 1) * 2 * L]
            sc = jnp.where(allowed, s * (SWA_HD ** -0.5), NEG_INF)
            sink = sink_ref[head:head + 1, 0:1]
            mx = jnp.broadcast_to(jnp.maximum(jnp.max(sc, axis=1, keepdims=True), sink), sc.shape)
            out.append((jnp.exp(sc - mx).astype(BF16), jnp.exp(sink - mx[:, :LANES])))
        return out

    def outputs(c, e_list):
        win = slice(c * L, (c + 2) * L)
        v2 = v_all[win, :]
        v2s = v_swap[win, :]
        for g in range(SWA_KV):
            va = jnp.where(low_half, v2 if g == 0 else v2s, 0.0).astype(BF16)
            vb = jnp.where(low_half, 0.0, v2s if g == 0 else v2).astype(BF16)
            vden = jnp.concatenate([jnp.concatenate([va, ones_lo], axis=1),
                                    jnp.concatenate([vb, ones_hi], axis=1)], axis=0)
            for pp in range(2 * g, 2 * g + 2):
                (e0, k0), (e1, k1) = e_list[2 * pp], e_list[2 * pp + 1]
                res = _mm(jnp.concatenate([e0, e1], axis=1), vden)
                den = res[:, LANES:] + jnp.where(out_low, k0, k1)
                y_ref[c * L:(c + 1) * L, pp * LANES:(pp + 1) * LANES] = (res[:, :LANES] * (1.0 / den)).astype(BF16)

    s_next = scores(0)
    for c in range(nblk):
        s_cur = s_next
        if c + 1 < nblk:
            s_next = scores(c + 1)
        outputs(c, weights(c, s_cur))


def _swa(sq, skt, sv, sinks_all, layer, nb, seq):
    nt = seq // TM_MIX
    per = TM_MIX // WINDOW
    cur_r = lambda b, j: (b * nt + j, 0)
    cur_c = lambda b, j: (0, b * nt + j)
    prev_r = lambda b, j: (jnp.maximum((b * nt + j) * per - 1, b * nt * per), 0)
    prev_c = lambda b, j: (0, jnp.maximum((b * nt + j) * per - 1, b * nt * per))
    return pl.pallas_call(
        _swa_kernel,
        grid=(nb, nt),
        in_specs=[pl.BlockSpec((TM_MIX, BRANCH), cur_r),
                  pl.BlockSpec((LANES, WINDOW), prev_c),
                  pl.BlockSpec((LANES, TM_MIX), cur_c),
                  pl.BlockSpec((WINDOW, LANES), prev_r),
                  pl.BlockSpec((TM_MIX, LANES), cur_r),
                  _layer_spec((8, LANES), layer)],
        out_specs=pl.BlockSpec((TM_MIX, BRANCH), cur_r),
        out_shape=jax.ShapeDtypeStruct(sq.shape, BF16),
        compiler_params=_params(("arbitrary", "arbitrary")),
        name="swa_prompt",
    )(sq, skt, skt, sv, sv, sinks_all)


def _cross_scores(q, kt_ref):
    return [_mm(q[:, h * X_HD:(h + 1) * X_HD], kt_ref[h * X_HD:(h + 1) * X_HD, :]) for h in range(X_HEADS)]


def _cross_outputs(scores, v_ref):
    ones = jnp.ones((MEM_TOKENS, X_HD), BF16)
    out = []
    for h, s in enumerate(scores):
        s = s * (X_HD ** -0.5)
        e = jnp.exp(s - jnp.max(s, axis=1, keepdims=True)).astype(BF16)
        res = _mm(e, jnp.concatenate([v_ref[:, h * X_HD:(h + 1) * X_HD], ones], axis=1))
        out.append((res[:, :X_HD] * (1.0 / res[:, X_HD:])).astype(BF16))
    return jnp.concatenate(out, axis=1)


def _merge_ffn_kernel(*refs, cross_attend):
    if cross_attend:
        x_ref, ya_ref, yb_ref, xq_ref, mkt_ref, mv_ref = refs[:6]
        refs = refs[6:]
    else:
        x_ref, ya_ref, yb_ref, yc_ref = refs[:4]
        refs = refs[4:]
    wgl_ref, wbr_ref, wmix_ref, wfi_ref, wfo_ref, g1_ref, b1_ref, g2_ref, b2_ref, o_ref = refs
    tm = x_ref.shape[0]
    halves = [slice(s * (tm // 2), (s + 1) * (tm // 2)) for s in range(2)]
    x = [x_ref[s, :] for s in halves]
    xb = [v.astype(BF16) for v in x]
    if cross_attend:
        scores = [_cross_scores(xq_ref[s, :], mkt_ref) for s in halves]
    gates = [[_sigmoid(_mm(xb[i], wgl_ref[:, r * D_MODEL:(r + 1) * D_MODEL])) for r in range(3)] for i in range(2)]
    if cross_attend:
        yc = [_cross_outputs(sc, mv_ref) for sc in scores]
    else:
        yc = [yc_ref[s, :] for s in halves]
    acc = []
    for i, s in enumerate(halves):
        tot = None
        for r, y in enumerate((ya_ref[s, :], yb_ref[s, :], yc[i])):
            term = gates[i][r] * _mm(y, wbr_ref[r])
            tot = term if tot is None else tot + term
        acc.append(tot.astype(BF16))
    x1 = [_layer_norm(DEEPNORM_ALPHA * x[i] + _mm(acc[i], wmix_ref[...]), g1_ref[...], b1_ref[...])
          for i in range(2)]
    act = []
    for i in range(2):
        x1b = x1[i].astype(BF16)
        gpre = _mm(x1b, wfi_ref[:, :D_FF])
        up = _mm(x1b, wfi_ref[:, D_FF:])
        act.append((gpre * _sigmoid(gpre) * up).astype(BF16))
    for i, s in enumerate(halves):
        o_ref[s, :] = _layer_norm(DEEPNORM_ALPHA * x1[i] + _mm(act[i], wfo_ref[...]), g2_ref[...], b2_ref[...])


def _merge_ffn(x2d, ya, yb, third, w_all, wbr, wmix, wfi, wfo, g1, b1, g2, b2, layer, tm, memory=None):
    m = x2d.shape[0]
    row = lambda i: (i, 0)
    vec = _layer_spec((1, D_MODEL), layer)
    mem_specs, mem_args = [], []
    if memory is not None:
        mkt, mv16, seq = memory
        per = seq // tm
        mem_specs = [pl.BlockSpec((None, BRANCH, MEM_TOKENS), lambda i: (i // per, 0, 0)),
                     pl.BlockSpec((MEM_TOKENS, BRANCH), lambda i: (i // per, 0))]
        mem_args = [mkt, mv16]
    return pl.pallas_call(
        functools.partial(_merge_ffn_kernel, cross_attend=memory is not None),
        grid=(m // tm,),
        in_specs=[pl.BlockSpec((tm, D_MODEL), row),
                  pl.BlockSpec((tm, BRANCH), row),
                  pl.BlockSpec((tm, BRANCH), row),
                  pl.BlockSpec((tm, BRANCH), row),
                  *mem_specs,
                  _layer_spec((D_MODEL, W_HALF), layer, 0, 0),
                  _layer_spec((3, BRANCH, D_MODEL), layer),
                  _layer_spec((D_MODEL, D_MODEL), layer),
                  _layer_spec((D_MODEL, 2 * D_FF), layer),
                  _layer_spec((D_FF, D_MODEL), layer),
                  vec, vec, vec, vec],
        out_specs=pl.BlockSpec((tm, D_MODEL), row),
        out_shape=jax.ShapeDtypeStruct((m, D_MODEL), F32),
        compiler_params=_params(("arbitrary",)),
        name="merge_ffn",
    )(x2d, ya, yb, third, *mem_args, w_all, wbr, wmix, wfi, wfo, g1, b1, g2, b2)


_P2_MK, _P2_SK, _P2_END = 0, 512, 640


def _dproj_kernel(x_ref, wn_ref, wt_ref, bias_ref, cos_ref, sin_ref, cost_ref, sint_ref, p_ref, pt_ref, p2_ref):
    xb = x_ref[...].astype(BF16)
    cos = cos_ref[...]
    sin = sin_ref[...]
    for c in range(_N_END // LANES):
        cs = slice(c * LANES, (c + 1) * LANES)
        blk = _mm(xb, wn_ref[:, cs])
        if _N_SQ <= c * LANES < _N_XQ:
            blk = _rope_lanes(blk, cos, sin)
        p_ref[:, cs] = blk
    pt_ref[_T_MV:_T_MK, :] = _mm_nt(wt_ref[_T_MV:_T_MK, :], xb)
    kt = _mm_nt(wt_ref[_T_MK:_T_SK, :], xb) * (ML_DK ** -0.5)
    pt_ref[_T_MK:_T_SK, :] = kt
    skt = _mm_nt(wt_ref[_T_SK:_T_SV, :], xb)
    cost = cost_ref[...]
    sint = sint_ref[...]
    for g in range(SWA_KV):
        base = g * SWA_HD
        for off, val in _rope_rows(skt[base:base + SWA_HD, :], cost, sint):
            pt_ref[_T_SK + base + off:_T_SK + base + off + val.shape[0], :] = val
    pt_ref[_T_SV:_T_G, :] = _mm_nt(wt_ref[_T_SV:_T_G, :], xb)
    pt_ref[_T_G:_T_END, :] = _mm_nt(wt_ref[_T_G:_T_END, :], xb) + bias_ref[...]
    for c in range(BRANCH // LANES):
        p2_ref[:, _P2_MK + c * LANES:_P2_MK + (c + 1) * LANES] = kt[c * LANES:(c + 1) * LANES, :].T
    p2_ref[:, _P2_SK:_P2_END] = pt_ref[_T_SK:_T_SV, :].T


def _dproj(xs, w_all, wt_all, bias_all, cos, sin, cost, sint, layer):
    n = xs.shape[0]
    whole = lambda shape: pl.BlockSpec(shape, lambda i: (0, 0))
    return pl.pallas_call(
        _dproj_kernel,
        grid=(1,),
        in_specs=[_const_spec((n, D_MODEL)),
                  _layer_spec((D_MODEL, W_HALF), layer, 0, 1),
                  _layer_spec((_T_END, D_MODEL), layer),
                  _layer_spec((16, 1), layer),
                  _const_spec((n, LANES)), _const_spec((n, LANES)),
                  _const_spec((ROT_HALF, n)), _const_spec((ROT_HALF, n))],
        out_specs=[whole((n, _N_END)), whole((_T_END, n)), whole((n, _P2_END))],
        out_shape=[jax.ShapeDtypeStruct((n, _N_END), F32),
                   jax.ShapeDtypeStruct((_T_END, n), F32),
                   jax.ShapeDtypeStruct((n, _P2_END), F32)],
        compiler_params=_params(("arbitrary",)),
        name="decode_proj",
    )(xs, w_all, wt_all, bias_all, cos, sin, cost, sint)


def _dmlstm_kernel(q_ref, v_ref, mo_ref, k_ref, vt_ref, gt_ref, c_ref, n_ref, m_ref, gain_ref, *rest):
    y_ref, c_out_ref, n_out_ref, m_out_ref = rest[-4:]
    i = pl.program_id(0)
    bb = q_ref.shape[0]
    nlanes = gt_ref.shape[1]
    li = gt_ref[0:ML_HEADS, :]
    lf = _log_sigmoid(gt_ref[8:8 + ML_HEADS, :])
    m_prev = m_ref[...]
    m_t = jnp.maximum(lf + m_prev, li)
    m_out_ref[...] = m_t
    scal = jnp.concatenate([jnp.exp(li - m_t), jnp.exp(lf + m_prev - m_t), jnp.exp(-m_t), jnp.zeros_like(m_t)], axis=0)
    bring = jnp.where(i == 0, 0, nlanes - i * bb)
    scal = pltpu.roll(scal, bring, axis=1)
    tiles = [(h, j) for h in range(ML_HEADS) for j in range(bb)]
    hs = lambda h: slice(h * ML_DK, (h + 1) * ML_DK)
    nrow_of = lambda h, j: slice(j * ML_HEADS + h, j * ML_HEADS + h + 1)
    q = {(h, j): q_ref[j:j + 1, hs(h)] for h, j in tiles}
    k = {(h, j): k_ref[j:j + 1, hs(h)] for h, j in tiles}
    w = {(h, j): scal[h:h + 1, j:j + 1] for h, j in tiles}
    a = {(h, j): scal[ML_HEADS + h:ML_HEADS + h + 1, j:j + 1] for h, j in tiles}
    cq = {(h, j): _mm_nt(jnp.broadcast_to(q[h, j], (8, ML_DK)).astype(BF16), c_ref[j, h].astype(BF16))[0:1, :]
          for h, j in tiles}
    qk = {t: jnp.sum(q[t] * k[t], axis=1, keepdims=True) for t in tiles}
    nq = {(h, j): jnp.sum(n_ref[nrow_of(h, j), :] * q[h, j], axis=1, keepdims=True) for h, j in tiles}
    hrow = {}
    for h, j in tiles:
        sw = qk[h, j] * w[h, j]
        floor = scal[2 * ML_HEADS + h:2 * ML_HEADS + h + 1, j:j + 1]
        den = jnp.maximum(jnp.abs(sw + a[h, j] * nq[h, j]), floor)
        hrow[h, j] = (sw * v_ref[j:j + 1, hs(h)] + a[h, j] * cq[h, j]) / den
        n_out_ref[nrow_of(h, j), :] = a[h, j] * n_ref[nrow_of(h, j), :] + w[h, j] * k[h, j]
    for h in range(ML_HEADS):
        vt = pltpu.roll(vt_ref[hs(h), :], bring, axis=1)
        for j in range(bb):
            c_out_ref[j, h] = a[h, j] * c_ref[j, h] + (w[h, j] * vt[:, j:j + 1]) * k[h, j]
    hc = {t: hrow[t] - jnp.mean(hrow[t], axis=1, keepdims=True) for t in tiles}
    var = {t: jnp.mean(hc[t] * hc[t], axis=1, keepdims=True) for t in tiles}
    for h, j in tiles:
        hn = hc[h, j] * lax.rsqrt(var[h, j] + HEAD_NORM_EPS) * gain_ref[:, hs(h)]
        y_ref[j:j + 1, hs(h)] = (_sigmoid(mo_ref[j:j + 1, hs(h)]) * hn).astype(BF16)


def _dmlstm(p, p2, pt, c_all, n_all, mt_all, gain_all, layer, c_stack):
    n = p.shape[0]
    blk = lambda cidx: pl.BlockSpec((BB, BRANCH), lambda i: (i, cidx))
    in_specs = [blk(_N_MQ // BRANCH), blk(_N_MV // BRANCH), blk(_N_MO // BRANCH),
                pl.BlockSpec((BB, BRANCH), lambda i: (i, _P2_MK // BRANCH)),
                pl.BlockSpec((BRANCH, n), lambda i: (_T_MV // BRANCH, 0), pipeline_mode=pl.Buffered(1)),
                pl.BlockSpec((16, n), lambda i: (_T_G // 16, 0), pipeline_mode=pl.Buffered(1)),
                pl.BlockSpec((None, BB, ML_HEADS, ML_DV, ML_DK), lambda i: (layer, i, 0, 0, 0)),
                pl.BlockSpec((None, BB * ML_HEADS, ML_DK), lambda i: (layer, i, 0)),
                _layer_spec((ML_HEADS, n), layer),
                _layer_spec((1, BRANCH), layer)]
    args = [p, p, p, p2, pt, pt, c_all, n_all, mt_all, gain_all]
    aliases = {}
    if c_stack is not None:
        in_specs.append(pl.BlockSpec(memory_space=pl.ANY))
        args.append(c_stack)
        aliases = {len(args) - 1: 1}
    return pl.pallas_call(
        _dmlstm_kernel,
        grid=(n // BB,),
        in_specs=in_specs,
        out_specs=[pl.BlockSpec((BB, BRANCH), lambda i: (i, 0)),
                   pl.BlockSpec((None, BB, ML_HEADS, ML_DV, ML_DK), lambda i: (layer, i, 0, 0, 0)),
                   pl.BlockSpec((BB * ML_HEADS, ML_DK), lambda i: (i, 0)),
                   pl.BlockSpec((ML_HEADS, n), lambda i: (0, 0))],
        out_shape=[jax.ShapeDtypeStruct((n, BRANCH), BF16),
                   jax.ShapeDtypeStruct((DEPTH, n, ML_HEADS, ML_DV, ML_DK), F32),
                   jax.ShapeDtypeStruct((n * ML_HEADS, ML_DK), F32),
                   jax.ShapeDtypeStruct((ML_HEADS, n), F32)],
        input_output_aliases=aliases,
        compiler_params=_params(("arbitrary",)),
        name="decode_mlstm",
    )(*args)


def _dswa_kernel(q_ref, kn_ref, vn_ref, kvt_ref, ck_ref, cv_ref, sink_ref, *rest):
    y_ref, ko_ref, vo_ref = rest[-3:]
    i = pl.program_id(0)
    bb = q_ref.shape[0]
    lane = lax.broadcasted_iota(jnp.int32, (LANES, LANES), 1)
    row8 = lax.broadcasted_iota(jnp.int32, (SWA_HEADS, LANES), 0)
    low8 = lax.broadcasted_iota(jnp.int32, (SWA_HEADS, LANES), 1) < SWA_HD
    low1 = lax.broadcasted_iota(jnp.int32, (1, LANES), 1) < SWA_HD
    scale = SWA_HD ** -0.5
    sink = sink_ref[:, 0:1]
    knew_t = kvt_ref[0:LANES, :]
    vnew_t = kvt_ref[LANES:, :]
    for j in range(bb):
        ck = ck_ref[j]
        cv = cv_ref[j]
        kn = kn_ref[j:j + 1, :]
        vn = vn_ref[j:j + 1, :]
        qm = jnp.zeros((SWA_HEADS, LANES), F32)
        for pp in range(SWA_HEADS // 2):
            g = pp // 2
            pair = q_ref[j:j + 1, pp * LANES:(pp + 1) * LANES]
            swap = pltpu.roll(pair, SWA_HD, axis=1)
            in_g = low8 if g == 0 else jnp.logical_not(low8)
            for t in range(2):
                qm = jnp.where((row8 == 2 * pp + t) & in_g, pair if t == g else swap, qm)
        s = _mm(qm.astype(BF16), ck.astype(BF16)) * scale
        s_new = jnp.sum(qm * kn, axis=1, keepdims=True) * scale
        mx = jnp.maximum(jnp.maximum(jnp.max(s, axis=1, keepdims=True), s_new), sink)
        e = jnp.exp(s - mx)
        e_new = jnp.exp(s_new - mx)
        den = jnp.sum(e, axis=1, keepdims=True) + e_new + jnp.exp(sink - mx)
        o = (_mm_nt(e.astype(BF16), cv.astype(BF16)) + e_new * vn) / den
        for pp in range(SWA_HEADS // 2):
            g = pp // 2
            halves = []
            for t in range(2):
                oh = o[2 * pp + t:2 * pp + t + 1, :]
                halves.append(oh if t == g else pltpu.roll(oh, SWA_HD, axis=1))
            y_ref[j:j + 1, pp * LANES:(pp + 1) * LANES] = jnp.where(low1, halves[0], halves[1]).astype(BF16)
        bring = LANES - 1 - (i * bb + j)
        ko_ref[j] = jnp.where(lane == LANES - 1, pltpu.roll(knew_t, bring, axis=1), pltpu.roll(ck, LANES - 1, axis=1))
        vo_ref[j] = jnp.where(lane == LANES - 1, pltpu.roll(vnew_t, bring, axis=1), pltpu.roll(cv, LANES - 1, axis=1))


def _dswa(p, p2, pt, ck_all, cv_all, sinks_all, layer, k_stack, v_stack):
    n = p.shape[0]
    cache = pl.BlockSpec((None, BB, LANES, WINDOW), lambda i: (layer, i, 0, 0))
    in_specs = [pl.BlockSpec((BB, BRANCH), lambda i: (i, _N_SQ // BRANCH)),
                pl.BlockSpec((BB, LANES), lambda i: (i, _P2_SK // LANES)),
                pl.BlockSpec((BB, LANES), lambda i: (i, _N_SV // LANES)),
                pl.BlockSpec((2 * LANES, n), lambda i: (_T_SK // (2 * LANES), 0), pipeline_mode=pl.Buffered(1)),
                cache, cache, _layer_spec((8, LANES), layer)]
    args = [p, p2, p, pt, ck_all, cv_all, sinks_all]
    aliases = {}
    if k_stack is not None:
        in_specs += [pl.BlockSpec(memory_space=pl.ANY), pl.BlockSpec(memory_space=pl.ANY)]
        args += [k_stack, v_stack]
        aliases = {len(args) - 2: 1, len(args) - 1: 2}
    return pl.pallas_call(
        _dswa_kernel,
        grid=(n // BB,),
        in_specs=in_specs,
        out_specs=[pl.BlockSpec((BB, BRANCH), lambda i: (i, 0)), cache, cache],
        out_shape=[jax.ShapeDtypeStruct((n, BRANCH), BF16),
                   jax.ShapeDtypeStruct((DEPTH, n, LANES, WINDOW), F32),
                   jax.ShapeDtypeStruct((DEPTH, n, LANES, WINDOW), F32)],
        input_output_aliases=aliases,
        compiler_params=_params(("arbitrary",)),
        name="decode_swa",
    )(*args)


def _dcross_kernel(q_ref, k_ref, v_ref, y_ref):
    bb = q_ref.shape[0]
    scale = X_HD ** -0.5
    row8 = lax.broadcasted_iota(jnp.int32, (8, LANES), 0) % X_HEADS
    for j in range(bb):
        qrep = jnp.zeros((8, LANES), F32)
        for h in range(X_HEADS):
            qrep = jnp.where(row8 == h, q_ref[j:j + 1, h * X_HD:(h + 1) * X_HD], qrep)
        s = jnp.sum(k_ref[j] * qrep[None], axis=2, keepdims=True) * scale
        mx8 = jnp.max(s, axis=0)
        mx4 = jnp.maximum(mx8[0:X_HEADS], mx8[X_HEADS:])
        e = jnp.exp(s - jnp.concatenate([mx4, mx4], axis=0)[None])
        den8 = jnp.sum(e, axis=0)
        o8 = jnp.sum(e * v_ref[j], axis=0)
        o4 = (o8[0:X_HEADS] + o8[X_HEADS:]) / (den8[0:X_HEADS] + den8[X_HEADS:])
        for h in range(X_HEADS):
            y_ref[j:j + 1, h * X_HD:(h + 1) * X_HD] = o4[h:h + 1, :].astype(BF16)


def _dcross(p, mk_all, mv_all, layer):
    n = p.shape[0]
    cache = pl.BlockSpec((None, BB, MEM_TOKENS * X_HEADS // 8, 8, X_HD), lambda i: (layer, i, 0, 0, 0))
    return pl.pallas_call(
        _dcross_kernel,
        grid=(n // BB,),
        in_specs=[pl.BlockSpec((BB, BRANCH), lambda i: (i, _N_XQ // BRANCH)), cache, cache],
        out_specs=pl.BlockSpec((BB, BRANCH), lambda i: (i, 0)),
        out_shape=jax.ShapeDtypeStruct((n, BRANCH), BF16),
        compiler_params=_params(("arbitrary",)),
        name="decode_cross",
    )(p, mk_all, mv_all)


def _rope_tables(positions):
    inv_freq = ROPE_THETA ** (-jnp.arange(ROT_HALF, dtype=F32) / ROT_HALF)
    ang = positions.astype(F32)[:, None] * inv_freq[None, :]
    cos = jnp.cos(ang)
    sin = jnp.sin(ang)
    reps = LANES // ROT_HALF
    return jnp.tile(cos, (1, reps)), jnp.tile(sin, (1, reps)), cos.T, sin.T


def kernel(x_prompt, x_sample, mem_prompt, cache_swa_k, cache_swa_v, cache_mem_k, cache_mem_v, state_mlstm_c, state_mlstm_n, state_mlstm_m, w_in, b_gates, mlstm_norm_g, swa_sinks, w_mem_kv, w_branch, w_mix_out, ln1_g, ln1_b, w_ffn_in, w_ffn_out, ln2_g, ln2_b):
    nb, seq, _ = x_prompt.shape
    ns = x_sample.shape[0]
    assert ns == LANES and PAST_LEN >= WINDOW

    cosn, sinn, cost, sint = _rope_tables(jnp.arange(seq))
    cos_s, sin_s, cost_s, sint_s = _rope_tables(jnp.full((ns,), PAST_LEN))

    ck_all = jnp.transpose(cache_swa_k, (0, 1, 3, 4, 2)).reshape(DEPTH, ns, SWA_KV * SWA_HD, WINDOW)
    cv_all = jnp.transpose(cache_swa_v, (0, 1, 3, 4, 2)).reshape(DEPTH, ns, SWA_KV * SWA_HD, WINDOW)
    mk_all = cache_mem_k.reshape(DEPTH, ns, MEM_TOKENS * X_HEADS // 8, 8, X_HD)
    mv_all = cache_mem_v.reshape(DEPTH, ns, MEM_TOKENS * X_HEADS // 8, 8, X_HD)
    n_all = state_mlstm_n.reshape(DEPTH, ns * ML_HEADS, ML_DK)
    mt_all = jnp.transpose(state_mlstm_m, (0, 2, 1))

    zcols = lambda n: jnp.zeros((DEPTH, D_MODEL, n), F32)
    w_all = jnp.concatenate([w_in[..., _C_GL:], w_in[..., _C_MQ:_C_MK], w_in[..., _C_MV:_C_MI],
                             w_in[..., _C_SQ:_C_SK], w_in[..., _C_XQ:_C_GL], w_in[..., _C_SV:_C_XQ],
                             zcols(W_HALF - _N_END)], axis=-1).astype(BF16)
    wt_all = jnp.swapaxes(jnp.concatenate([w_in[..., _C_MV:_C_MO], w_in[..., _C_MK:_C_MV], w_in[..., _C_SK:_C_XQ],
                                           w_in[..., _C_MI:_C_MF], zcols(4), w_in[..., _C_MF:_C_SQ], zcols(4)],
                                          axis=-1), 1, 2).astype(BF16)
    z4 = jnp.zeros((DEPTH, 4), F32)
    bias_all = jnp.concatenate([b_gates[:, :ML_HEADS], z4, b_gates[:, ML_HEADS:], z4], axis=1)[..., None]
    gain_all = mlstm_norm_g[:, None, :]
    sinks_all = jnp.broadcast_to(swa_sinks[:, :, None], (DEPTH, SWA_HEADS, LANES))
    wkv_all = w_mem_kv.astype(BF16)
    wkt_all = jnp.swapaxes(w_mem_kv[..., :BRANCH], 1, 2).astype(BF16)
    wbr_all = w_branch.astype(BF16)
    wmix_all = w_mix_out.astype(BF16)
    wfi_all = w_ffn_in.astype(BF16)
    wfo_all = w_ffn_out.astype(BF16)
    ln_all = (ln1_g[:, None, :], ln1_b[:, None, :], ln2_g[:, None, :], ln2_b[:, None, :])

    yp = x_prompt.reshape(nb * seq, D_MODEL)
    ys = x_sample.reshape(ns, D_MODEL)
    mem2d = mem_prompt.reshape(nb * MEM_TOKENS, D_MODEL)

    outs = {k: [] for k in ("kp", "vp", "mk", "mv", "cp", "np", "mp", "ns", "ms")}
    c_stack = k_stack = v_stack = None
    for l in range(DEPTH):
        mk32, mv32, mkt, mv16 = _memkv(mem2d, wkv_all, wkt_all, l, nb)
        q, v, mo, sq, xq, sv, kt, skt, gt, gc, k32, v32, m_fin = _proj(yp, w_all, wt_all, bias_all, cosn, sinn,
                                                                       cost, sint, l, nb, seq)
        ya, s_fin = _mlstm(q, v, kt, mo, gt, gc, gain_all, l, nb, seq)
        yb = _swa(sq, skt, sv, sinks_all, l, nb, seq)
        yp = _merge_ffn(yp, ya, yb, xq, w_all, wbr_all, wmix_all, wfi_all, wfo_all, *ln_all, l, TM_MERGE,
                        memory=(mkt, mv16, seq))
        outs["kp"].append(jnp.transpose(k32.reshape(nb, SWA_KV, SWA_HD, WINDOW), (0, 3, 1, 2)))
        outs["vp"].append(jnp.transpose(v32.reshape(nb, SWA_KV, SWA_HD, WINDOW), (0, 3, 1, 2)))
        outs["mk"].append(mk32.reshape(nb, MEM_TOKENS, X_HEADS, X_HD))
        outs["mv"].append(mv32.reshape(nb, MEM_TOKENS, X_HEADS, X_HD))
        outs["cp"].append(jnp.swapaxes(s_fin[..., :ML_DV], -1, -2))
        outs["np"].append(s_fin[..., ML_DV])
        outs["mp"].append(m_fin.reshape(nb, 8, LANES)[:, :ML_HEADS, 0])

        p, pt, p2 = _dproj(ys, w_all, wt_all, bias_all, cos_s, sin_s, cost_s, sint_s, l)
        ya_s, c_stack, n_new, m_new = _dmlstm(p, p2, pt, state_mlstm_c, n_all, mt_all, gain_all, l, c_stack)
        yb_s, k_stack, v_stack = _dswa(p, p2, pt, ck_all, cv_all, sinks_all, l, k_stack, v_stack)
        yc_s = _dcross(p, mk_all, mv_all, l)
        ys = _merge_ffn(ys, ya_s, yb_s, yc_s, w_all, wbr_all, wmix_all, wfi_all, wfo_all, *ln_all, l, ns)
        outs["ns"].append(n_new.reshape(ns, ML_HEADS, ML_DK))
        outs["ms"].append(m_new)

    st = {k: jnp.stack(vals) for k, vals in outs.items()}
    window_out = lambda t: jnp.transpose(t.reshape(DEPTH, ns, SWA_KV, SWA_HD, WINDOW), (0, 1, 4, 2, 3))
    return (yp.reshape(nb, seq, D_MODEL), ys.reshape(ns, 1, D_MODEL),
            st["kp"], st["vp"], window_out(k_stack), window_out(v_stack), st["mk"], st["mv"],
            st["cp"], st["np"], st["mp"], c_stack, st["ns"], jnp.transpose(st["ms"], (0, 2, 1)))
```

```python
import functools

import jax
import jax.numpy as jnp
from jax import lax
from jax.experimental import pallas as pl
from jax.experimental.pallas import tpu as pltpu

F32 = jnp.float32
BF16 = jnp.bfloat16

D_MODEL = 1024
DEPTH = 2
BRANCH = 512
ML_HEADS = 4
ML_DK = 128
ML_DV = 128
ML_CHUNK = 128
SWA_HD = 64
SWA_HEADS = 8
SWA_KV = 2
SWA_GROUP = 4
WINDOW = 128
ROT_DIM = 16
ROT_HALF = 8
ROPE_THETA = 500000.0
MEM_TOKENS = 256
X_HEADS = 4
X_HD = 128
D_FF = 2816
LN_EPS = 1e-5
HEAD_NORM_EPS = 1e-6
DEEPNORM_ALPHA = (2 * DEPTH) ** 0.25
NEG_INF = -1e30
PAST_LEN = 8192

LANES = 128
VMEM_LIMIT = 56 * 1024 * 1024

_C_MQ, _C_MK, _C_MV, _C_MO = 0, 512, 1024, 1536
_C_MI, _C_MF = 2048, 2052
_C_SQ, _C_SK, _C_SV, _C_XQ, _C_GL = 2056, 2568, 2696, 2824, 3336

W_HALF = 3 * D_MODEL
_N_MQ, _N_MV, _N_MO, _N_SQ, _N_XQ, _N_SV, _N_END = 0, 512, 1024, 1536, 2048, 2560, 2688
_T_MV, _T_MK, _T_SK, _T_SV, _T_G, _T_END = 0, 512, 1024, 1152, 1280, 1296

TM_PROJ = 512
TM_MIX = 512
TM_MERGE = 256
TQ_CROSS = 512
BB = 8


def _mm(a, b):
    return jnp.dot(a, b, preferred_element_type=F32)


def _mm_nt(a, b):
    return lax.dot_general(a, b, (((1,), (1,)), ((), ())), preferred_element_type=F32)


def _sigmoid(x):
    return 1.0 / (1.0 + jnp.exp(-x))


def _log_sigmoid(x):
    return jnp.minimum(x, 0.0) - jnp.log(1.0 + jnp.exp(-jnp.abs(x)))


def _layer_norm(x, g, b):
    mu = jnp.mean(x, axis=-1, keepdims=True)
    xc = x - mu
    var = jnp.mean(xc * xc, axis=-1, keepdims=True)
    return xc * lax.rsqrt(var + LN_EPS) * g + b


def _rope_lanes(x, cos, sin):
    lane = lax.broadcasted_iota(jnp.int32, x.shape, 1) % SWA_HD
    up = pltpu.roll(x, LANES - ROT_HALF, axis=1)
    dn = pltpu.roll(x, ROT_HALF, axis=1)
    first = x * cos - up * sin
    second = x * cos + dn * sin
    return jnp.where(lane < ROT_HALF, first, jnp.where(lane < ROT_DIM, second, x))


def _rope_rows(xt, cost, sint):
    x1 = xt[0:ROT_HALF, :]
    x2 = xt[ROT_HALF:ROT_DIM, :]
    return ((0, x1 * cost - x2 * sint), (ROT_HALF, x2 * cost + x1 * sint), (ROT_DIM, xt[ROT_DIM:SWA_HD, :]))


def _const_spec(shape):
    nd = len(shape)
    return pl.BlockSpec(shape, lambda *_: (0,) * nd, pipeline_mode=pl.Buffered(1))


def _layer_spec(shape, layer, *tail):
    idx = (layer,) + (tail if tail else (0,) * len(shape))
    return pl.BlockSpec((None,) + tuple(shape), lambda *_: idx, pipeline_mode=pl.Buffered(1))


def _params(sem):
    return pltpu.CompilerParams(dimension_semantics=sem, vmem_limit_bytes=VMEM_LIMIT)


def _memkv_kernel(mem_ref, wkv_ref, wkt_ref, k32_ref, v32_ref, kt_ref, v16_ref):
    m = mem_ref[...].astype(BF16)
    kv = _mm(m, wkv_ref[...])
    k32_ref[...] = kv[:, :BRANCH]
    v32_ref[...] = kv[:, BRANCH:]
    v16_ref[...] = kv[:, BRANCH:].astype(BF16)
    kt_ref[...] = _mm_nt(wkt_ref[...], m).astype(BF16)


def _memkv(mem2d, wkv, wkt, layer, nb):
    rows = mem2d.shape[0]
    return pl.pallas_call(
        _memkv_kernel,
        grid=(nb,),
        in_specs=[pl.BlockSpec((MEM_TOKENS, D_MODEL), lambda b: (b, 0)),
                  _layer_spec((D_MODEL, 2 * BRANCH), layer),
                  _layer_spec((BRANCH, D_MODEL), layer)],
        out_specs=[pl.BlockSpec((MEM_TOKENS, BRANCH), lambda b: (b, 0)),
                   pl.BlockSpec((MEM_TOKENS, BRANCH), lambda b: (b, 0)),
                   pl.BlockSpec((None, BRANCH, MEM_TOKENS), lambda b: (b, 0, 0)),
                   pl.BlockSpec((MEM_TOKENS, BRANCH), lambda b: (b, 0))],
        out_shape=[jax.ShapeDtypeStruct((rows, BRANCH), F32),
                   jax.ShapeDtypeStruct((rows, BRANCH), F32),
                   jax.ShapeDtypeStruct((nb, BRANCH, MEM_TOKENS), BF16),
                   jax.ShapeDtypeStruct((rows, BRANCH), BF16)],
        compiler_params=_params(("arbitrary",)),
        name="memkv_proj",
    )(mem2d, wkv, wkt)


def _mlstm_gate_weights(pre, gt_ref, gc_ref, m_out_ref, m_scr, valid):
    tm = pre.shape[1]
    li = pre[0:8, :]
    lane8 = lax.broadcasted_iota(jnp.int32, li.shape, 1) % ML_CHUNK
    b = _log_sigmoid(pre[8:16, :])
    shift = 1
    while shift < ML_CHUNK:
        b = b + jnp.where(lane8 >= shift, pltpu.roll(b, shift, axis=1), 0.0)
        shift *= 2
    g = li - b
    cm = g
    shift = 1
    while shift < ML_CHUNK:
        cm = jnp.maximum(cm, jnp.where(lane8 >= shift, pltpu.roll(cm, shift, axis=1), -jnp.inf))
        shift *= 2
    gt_ref[0:8, :] = g
    pad = jnp.zeros((LANES - 24, ML_CHUNK), F32)
    m_start = m_scr[...]
    m_prev = m_start
    for c in range(tm // ML_CHUNK):
        cs = slice(c * ML_CHUNK, (c + 1) * ML_CHUNK)
        b_c = b[:, cs]
        b_last = jnp.broadcast_to(b_c[:, ML_CHUNK - 1:ML_CHUNK], b_c.shape)
        cm_last = jnp.broadcast_to(cm[:, cs][:, ML_CHUNK - 1:ML_CHUNK], b_c.shape)
        m_t = b_c + jnp.maximum(m_prev, cm[:, cs])
        m_new = b_last + jnp.maximum(m_prev, cm_last)
        gt_ref[8:16, cs] = jnp.exp(b_last + g[:, cs] - m_new)
        gt_ref[16:24, cs] = jnp.exp(b_last + m_prev - m_new)
        rows = jnp.concatenate([b_c - m_t, jnp.exp(b_c + m_prev - m_t), jnp.exp(-m_t), pad], axis=0)
        gc_ref[cs, :] = rows.T
        m_prev = m_new
    m_prev = jnp.where(valid, m_prev, m_start)
    m_scr[...] = m_prev
    m_out_ref[...] = m_prev


def _proj_stream(x_ref, wn_ref, wt_ref, bias_ref, cosn_ref, sinn_ref, cost_ref, sint_ref, dst,
                 xq_ref, k32_ref, v32_ref, m_out_ref, m_scr, valid):
    tm = x_ref.shape[0]
    xb = x_ref[...].astype(BF16)
    tr_rows = lambda lo, hi: _mm_nt(wt_ref[lo:hi, :], xb)
    _mlstm_gate_weights(tr_rows(_T_G, _T_END) + bias_ref[...], dst["gt"], dst["gc"], m_out_ref, m_scr, valid)
    yield
    dst["q"][...] = _mm(xb, wn_ref[:, _N_MQ:_N_MV]).astype(BF16)
    yield
    dst["v"][...] = _mm(xb, wn_ref[:, _N_MV:_N_MO]).astype(BF16)
    yield
    dst["mo"][...] = _mm(xb, wn_ref[:, _N_MO:_N_SQ])
    yield
    xq_ref[...] = _mm(xb, wn_ref[:, _N_XQ:_N_SV]).astype(BF16)
    yield
    cosn = cosn_ref[...]
    sinn = sinn_ref[...]
    sq = _mm(xb, wn_ref[:, _N_SQ:_N_XQ])
    for c in range(BRANCH // LANES):
        blk = _rope_lanes(sq[:, c * LANES:(c + 1) * LANES], cosn, sinn)
        dst["sq"][:, c * LANES:(c + 1) * LANES] = blk.astype(BF16)
    yield
    dst["sv"][...] = _mm(xb, wn_ref[:, _N_SV:_N_END]).astype(BF16)
    dst["kt"][...] = (tr_rows(_T_MK, _T_SK) * (ML_DK ** -0.5)).astype(BF16)
    yield
    skt = tr_rows(_T_SK, _T_SV)
    svt = tr_rows(_T_SV, _T_G)
    cost = cost_ref[...]
    sint = sint_ref[...]
    tail = slice(tm - WINDOW, tm)
    v32_ref[...] = svt[:, tail]
    for g in range(SWA_KV):
        base = g * SWA_HD
        for off, val in _rope_rows(skt[base:base + SWA_HD, :], cost, sint):
            dst["skt"][base + off:base + off + val.shape[0], :] = val.astype(BF16)
            k32_ref[base + off:base + off + val.shape[0], :] = val[:, tail]


def _mlstm_stream(q_ref, v_ref, kt_ref, mo_ref, gt_ref, gc_ref, gain_ref, y_ref, s_ref):
    L = ML_CHUNK
    r_i = lax.broadcasted_iota(jnp.int32, (L, L), 0)
    c_i = lax.broadcasted_iota(jnp.int32, (L, L), 1)
    causal = c_i <= r_i
    ones = jnp.ones((L, ML_DV), BF16)

    units = [(c, h) for c in range(q_ref.shape[0] // L) for h in range(ML_HEADS)]
    ts = lambda c: slice(c * L, (c + 1) * L)
    hs = lambda h: slice(h * ML_DK, (h + 1) * ML_DK)

    qk = {(c, h): _mm(q_ref[ts(c), hs(h)], kt_ref[hs(h), ts(c)]) for c, h in units}
    yield

    sw, kts, vext = {}, {}, {}
    for c, h in units:
        g_r = gt_ref[h:h + 1, ts(c)]
        es_r = gt_ref[8 + h:9 + h, ts(c)]
        u_c = gc_ref[ts(c), h:h + 1]
        sw[c, h] = (qk[c, h] * jnp.exp(jnp.where(causal, u_c + g_r, -jnp.inf))).astype(BF16)
        kts[c, h] = (kt_ref[hs(h), ts(c)].astype(F32) * es_r).astype(BF16)
        vext[c, h] = jnp.concatenate([v_ref[ts(c), hs(h)], ones], axis=1)
    yield

    intra = {u: _mm(sw[u], vext[u]) for u in units}
    yield
    delta = {u: _mm(kts[u], vext[u]) for u in units}
    yield

    s_in = {}
    for h in range(ML_HEADS):
        state = s_ref[h]
        for c in range(q_ref.shape[0] // L):
            s_in[c, h] = state.astype(BF16)
            state = gt_ref[16 + h:17 + h, c * L:c * L + 1] * state + delta[c, h]
        s_ref[h] = state
    qs_all = {u: _mm(q_ref[ts(u[0]), hs(u[1])], s_in[u]) for u in units}
    yield

    hh, hc = {}, {}
    for c, h in units:
        tot = intra[c, h] + gc_ref[ts(c), 8 + h:9 + h] * qs_all[c, h]
        floor = gc_ref[ts(c), 16 + h:17 + h]
        hh[c, h] = tot[:, :ML_DV] * (1.0 / jnp.maximum(jnp.abs(tot[:, ML_DV:]), floor))
    for u in units:
        hc[u] = hh[u] - jnp.mean(hh[u], axis=1, keepdims=True)
    for c, h in units:
        var = jnp.mean(hc[c, h] * hc[c, h], axis=1, keepdims=True)
        hn = hc[c, h] * lax.rsqrt(var + HEAD_NORM_EPS) * gain_ref[:, hs(h)]
        y_ref[ts(c), hs(h)] = (_sigmoid(mo_ref[ts(c), hs(h)]) * hn).astype(BF16)


def _swa_stream(q_ref, ktp_ref, ktc_ref, vp_ref, vc_ref, sink_ref, y_ref, seq_start):
    L = WINDOW
    nblk = q_ref.shape[0] // L
    r_i = lax.broadcasted_iota(jnp.int32, (L, 2 * L), 0)
    c_i = lax.broadcasted_iota(jnp.int32, (L, 2 * L), 1)
    band = (c_i >= r_i) & (c_i <= r_i + L)
    first = band & (c_i >= jnp.where(seq_start, L, 0))
    low_half = lax.broadcasted_iota(jnp.int32, (2 * L, LANES), 1) < SWA_HD
    out_low = lax.broadcasted_iota(jnp.int32, (L, LANES), 1) < SWA_HD
    zeros_k = jnp.zeros((SWA_HD, 2 * L), BF16)
    ones_lo = jnp.where(low_half, 1.0, 0.0).astype(BF16)
    ones_hi = jnp.where(low_half, 0.0, 1.0).astype(BF16)

    kt_all = jnp.concatenate([ktp_ref[...], ktc_ref[...]], axis=1)
    v_all = jnp.concatenate([vp_ref[...], vc_ref[...]], axis=0).astype(F32)
    v_swap = pltpu.roll(v_all, SWA_HD, axis=1)

    def scores(c):
        win = slice(c * L, (c + 2) * L)
        out = []
        for g in range(SWA_KV):
            kt2 = kt_all[g * SWA_HD:(g + 1) * SWA_HD, win]
            kblk = jnp.concatenate([jnp.concatenate([kt2, zeros_k], axis=0),
                                    jnp.concatenate([zeros_k, kt2], axis=0)], axis=1)
            for pp in range(2 * g, 2 * g + 2):
                out.append(_mm(q_ref[c * L:(c + 1) * L, pp * LANES:(pp + 1) * LANES], kblk))
        return out

    def weights(c, s_list):
        allowed = first if c == 0 else band
        out = []
        for head in range(SWA_HEADS):
            s = s_list[head // 2][:, (head % 2) * 2 * L:(head % 2 + 1) * 2 * L]
            sc = jnp.where(allowed, s * (SWA_HD ** -0.5), NEG_INF)
            sink = sink_ref[head:head + 1, 0:1]
            mx = jnp.broadcast_to(jnp.maximum(jnp.max(sc, axis=1, keepdims=True), sink), sc.shape)
            out.append((jnp.exp(sc - mx).astype(BF16), jnp.exp(sink - mx[:, :LANES])))
        return out

    def outputs(c, e_list):
        win = slice(c * L, (c + 2) * L)
        v2 = v_all[win, :]
        v2s = v_swap[win, :]
        for g in range(SWA_KV):
            va = jnp.where(low_half, v2 if g == 0 else v2s, 0.0).astype(BF16)
            vb = jnp.where(low_half, 0.0, v2s if g == 0 else v2).astype(BF16)
            vden = jnp.concatenate([jnp.concatenate([va, ones_lo], axis=1),
                                    jnp.concatenate([vb, ones_hi], axis=1)], axis=0)
            for pp in range(2 * g, 2 * g + 2):
                (e0, k0), (e1, k1) = e_list[2 * pp], e_list[2 * pp + 1]
                res = _mm(jnp.concatenate([e0, e1], axis=1), vden)
                den = res[:, LANES:] + jnp.where(out_low, k0, k1)
                y_ref[c * L:(c + 1) * L, pp * LANES:(pp + 1) * LANES] = (res[:, :LANES] * (1.0 / den)).astype(BF16)

    s_next = scores(0)
    yield
    for c in range(nblk):
        s_cur = s_next
        if c + 1 < nblk:
            s_next = scores(c + 1)
        outputs(c, weights(c, s_cur))
        yield


_SLOT_BUFFERS = (("q", (TM_PROJ, BRANCH), BF16), ("v", (TM_PROJ, BRANCH), BF16), ("mo", (TM_PROJ, BRANCH), F32),
                 ("sq", (TM_PROJ, BRANCH), BF16), ("sv", (TM_PROJ, LANES), BF16), ("kt", (BRANCH, TM_PROJ), BF16),
                 ("skt", (LANES, TM_PROJ), BF16), ("gt", (24, TM_PROJ), F32), ("gc", (TM_PROJ, LANES), F32))


def _mixer_kernel(x_ref, wn_ref, wt_ref, bias_ref, cosn_ref, sinn_ref, cost_ref, sint_ref, gain_ref, sink_ref,
                  ya_ref, yb_ref, xq_ref, k32_ref, v32_ref, m_out_ref, s_out_ref, *scratch, n_tiles, tiles_per_seq):
    slots = {name: ref for (name, _, _), ref in zip(_SLOT_BUFFERS, scratch)}
    s_ref, m_scr, ktp_ref, vp_ref = scratch[len(_SLOT_BUFFERS):]
    i = pl.program_id(0)
    wr = i % 2
    mix_tile = i - 1
    valid = i < n_tiles

    @pl.when(i == 0)
    def _():
        for ref in scratch:
            ref[...] = jnp.zeros_like(ref)

    @pl.when((i % tiles_per_seq == 0) & valid)
    def _():
        m_scr[...] = jnp.zeros_like(m_scr)

    seq_start = mix_tile % tiles_per_seq == 0

    @pl.when(seq_start)
    def _():
        s_ref[...] = jnp.zeros_like(s_ref)

    dst = {name: ref.at[wr] for name, ref in slots.items()}
    src = {name: ref.at[1 - wr] for name, ref in slots.items()}
    streams = [
        _proj_stream(x_ref, wn_ref, wt_ref, bias_ref, cosn_ref, sinn_ref, cost_ref, sint_ref, dst,
                     xq_ref, k32_ref, v32_ref, m_out_ref, m_scr, valid),
        _mlstm_stream(src["q"], src["v"], src["kt"], src["mo"], src["gt"], src["gc"], gain_ref, ya_ref, s_ref),
        _swa_stream(src["sq"], ktp_ref, src["skt"], vp_ref, src["sv"], sink_ref, yb_ref, seq_start),
    ]
    while streams:
        for g in list(streams):
            if next(g, StopIteration) is StopIteration:
                streams.remove(g)

    tm = x_ref.shape[0]
    ktp_ref[...] = src["skt"][:, tm - WINDOW:]
    vp_ref[...] = src["sv"][tm - WINDOW:, :]

    @pl.when(mix_tile % tiles_per_seq == tiles_per_seq - 1)
    def _():
        s_out_ref[...] = s_ref[...]


def _mixer(x2d, w_all, wt_all, bias_all, cosn, sinn, cost, sint, gain_all, sinks_all, layer, nb, seq):
    m = x2d.shape[0]
    tm = TM_PROJ
    nt = seq // tm
    n = m // tm
    proj = lambda i: jnp.minimum(i, n - 1)
    mix = lambda i: jnp.maximum(i - 1, 0)
    return pl.pallas_call(
        functools.partial(_mixer_kernel, n_tiles=n, tiles_per_seq=nt),
        grid=(n + 1,),
        in_specs=[pl.BlockSpec((tm, D_MODEL), lambda i: (proj(i), 0)),
                  _layer_spec((D_MODEL, W_HALF), layer, 0, 1),
                  _layer_spec((_T_END, D_MODEL), layer),
                  _layer_spec((16, 1), layer),
                  pl.BlockSpec((tm, LANES), lambda i: (proj(i) % nt, 0)),
                  pl.BlockSpec((tm, LANES), lambda i: (proj(i) % nt, 0)),
                  pl.BlockSpec((ROT_HALF, tm), lambda i: (0, proj(i) % nt)),
                  pl.BlockSpec((ROT_HALF, tm), lambda i: (0, proj(i) % nt)),
                  _layer_spec((1, BRANCH), layer),
                  _layer_spec((8, LANES), layer)],
        out_specs=[pl.BlockSpec((tm, BRANCH), lambda i: (mix(i), 0)),
                   pl.BlockSpec((tm, BRANCH), lambda i: (mix(i), 0)),
                   pl.BlockSpec((tm, BRANCH), lambda i: (proj(i), 0)),
                   pl.BlockSpec((LANES, WINDOW), lambda i: (proj(i) // nt, 0)),
                   pl.BlockSpec((LANES, WINDOW), lambda i: (proj(i) // nt, 0)),
                   pl.BlockSpec((8, LANES), lambda i: (proj(i) // nt, 0)),
                   pl.BlockSpec((None, ML_HEADS, ML_DK, 2 * ML_DV), lambda i: (mix(i) // nt, 0, 0, 0))],
        out_shape=[jax.ShapeDtypeStruct((m, BRANCH), BF16),
                   jax.ShapeDtypeStruct((m, BRANCH), BF16),
                   jax.ShapeDtypeStruct((m, BRANCH), BF16),
                   jax.ShapeDtypeStruct((nb * LANES, WINDOW), F32),
                   jax.ShapeDtypeStruct((nb * LANES, WINDOW), F32),
                   jax.ShapeDtypeStruct((nb * 8, LANES), F32),
                   jax.ShapeDtypeStruct((nb, ML_HEADS, ML_DK, 2 * ML_DV), F32)],
        scratch_shapes=[pltpu.VMEM((2,) + shape, dtype) for _, shape, dtype in _SLOT_BUFFERS]
        + [pltpu.VMEM((ML_HEADS, ML_DK, 2 * ML_DV), F32), pltpu.VMEM((8, LANES), F32),
           pltpu.VMEM((LANES, WINDOW), BF16), pltpu.VMEM((WINDOW, LANES), BF16)],
        compiler_params=_params(("arbitrary",)),
        name="prompt_mixer",
    )(x2d, w_all, wt_all, bias_all, cosn, sinn, cost, sint, gain_all, sinks_all)


def _cross_scores(q, kt_ref):
    return [_mm(q[:, h * X_HD:(h + 1) * X_HD], kt_ref[h * X_HD:(h + 1) * X_HD, :]) for h in range(X_HEADS)]


def _cross_outputs(scores, v_ref):
    ones = jnp.ones((MEM_TOKENS, X_HD), BF16)
    out = []
    for h, s in enumerate(scores):
        s = s * (X_HD ** -0.5)
        e = jnp.exp(s - jnp.max(s, axis=1, keepdims=True)).astype(BF16)
        res = _mm(e, jnp.concatenate([v_ref[:, h * X_HD:(h + 1) * X_HD], ones], axis=1))
        out.append((res[:, :X_HD] * (1.0 / res[:, X_HD:])).astype(BF16))
    return jnp.concatenate(out, axis=1)


def _merge_ffn_kernel(*refs, cross_attend):
    if cross_attend:
        x_ref, ya_ref, yb_ref, xq_ref, mkt_ref, mv_ref = refs[:6]
        refs = refs[6:]
    else:
        x_ref, ya_ref, yb_ref, yc_ref = refs[:4]
        refs = refs[4:]
    wgl_ref, wbr_ref, wmix_ref, wfi_ref, wfo_ref, g1_ref, b1_ref, g2_ref, b2_ref, o_ref = refs
    tm = x_ref.shape[0]
    halves = [slice(s * (tm // 2), (s + 1) * (tm // 2)) for s in range(2)]
    x = [x_ref[s, :] for s in halves]
    xb = [v.astype(BF16) for v in x]
    if cross_attend:
        scores = [_cross_scores(xq_ref[s, :], mkt_ref) for s in halves]
    gates = [[_sigmoid(_mm(xb[i], wgl_ref[:, r * D_MODEL:(r + 1) * D_MODEL])) for r in range(3)] for i in range(2)]
    if cross_attend:
        yc = [_cross_outputs(sc, mv_ref) for sc in scores]
    else:
        yc = [yc_ref[s, :] for s in halves]
    acc = []
    for i, s in enumerate(halves):
        tot = None
        for r, y in enumerate((ya_ref[s, :], yb_ref[s, :], yc[i])):
            term = gates[i][r] * _mm(y, wbr_ref[r])
            tot = term if tot is None else tot + term
        acc.append(tot.astype(BF16))
    x1 = [_layer_norm(DEEPNORM_ALPHA * x[i] + _mm(acc[i], wmix_ref[...]), g1_ref[...], b1_ref[...])
          for i in range(2)]
    act = []
    for i in range(2):
        x1b = x1[i].astype(BF16)
        gpre = _mm(x1b, wfi_ref[:, :D_FF])
        up = _mm(x1b, wfi_ref[:, D_FF:])
        act.append((gpre * _sigmoid(gpre) * up).astype(BF16))
    for i, s in enumerate(halves):
        o_ref[s, :] = _layer_norm(DEEPNORM_ALPHA * x1[i] + _mm(act[i], wfo_ref[...]), g2_ref[...], b2_ref[...])


def _merge_ffn(x2d, ya, yb, third, w_all, wbr, wmix, wfi, wfo, g1, b1, g2, b2, layer, tm, memory=None):
    m = x2d.shape[0]
    row = lambda i: (i, 0)
    vec = _layer_spec((1, D_MODEL), layer)
    mem_specs, mem_args = [], []
    if memory is not None:
        mkt, mv16, seq = memory
        per = seq // tm
        mem_specs = [pl.BlockSpec((None, BRANCH, MEM_TOKENS), lambda i: (i // per, 0, 0)),
                     pl.BlockSpec((MEM_TOKENS, BRANCH), lambda i: (i // per, 0))]
        mem_args = [mkt, mv16]
    return pl.pallas_call(
        functools.partial(_merge_ffn_kernel, cross_attend=memory is not None),
        grid=(m // tm,),
        in_specs=[pl.BlockSpec((tm, D_MODEL), row),
                  pl.BlockSpec((tm, BRANCH), row),
                  pl.BlockSpec((tm, BRANCH), row),
                  pl.BlockSpec((tm, BRANCH), row),
                  *mem_specs,
                  _layer_spec((D_MODEL, W_HALF), layer, 0, 0),
                  _layer_spec((3, BRANCH, D_MODEL), layer),
                  _layer_spec((D_MODEL, D_MODEL), layer),
                  _layer_spec((D_MODEL, 2 * D_FF), layer),
                  _layer_spec((D_FF, D_MODEL), layer),
                  vec, vec, vec, vec],
        out_specs=pl.BlockSpec((tm, D_MODEL), row),
        out_shape=jax.ShapeDtypeStruct((m, D_MODEL), F32),
        compiler_params=_params(("arbitrary",)),
        name="merge_ffn",
    )(x2d, ya, yb, third, *mem_args, w_all, wbr, wmix, wfi, wfo, g1, b1, g2, b2)


_P2_MK, _P2_SK, _P2_END = 0, 512, 640


def _dproj_kernel(x_ref, wn_ref, wt_ref, bias_ref, cos_ref, sin_ref, cost_ref, sint_ref, p_ref, pt_ref, p2_ref):
    xb = x_ref[...].astype(BF16)
    cos = cos_ref[...]
    sin = sin_ref[...]
    for c in range(_N_END // LANES):
        cs = slice(c * LANES, (c + 1) * LANES)
        blk = _mm(xb, wn_ref[:, cs])
        if _N_SQ <= c * LANES < _N_XQ:
            blk = _rope_lanes(blk, cos, sin)
        p_ref[:, cs] = blk
    pt_ref[_T_MV:_T_MK, :] = _mm_nt(wt_ref[_T_MV:_T_MK, :], xb)
    kt = _mm_nt(wt_ref[_T_MK:_T_SK, :], xb) * (ML_DK ** -0.5)
    pt_ref[_T_MK:_T_SK, :] = kt
    skt = _mm_nt(wt_ref[_T_SK:_T_SV, :], xb)
    cost = cost_ref[...]
    sint = sint_ref[...]
    for g in range(SWA_KV):
        base = g * SWA_HD
        for off, val in _rope_rows(skt[base:base + SWA_HD, :], cost, sint):
            pt_ref[_T_SK + base + off:_T_SK + base + off + val.shape[0], :] = val
    pt_ref[_T_SV:_T_G, :] = _mm_nt(wt_ref[_T_SV:_T_G, :], xb)
    pt_ref[_T_G:_T_END, :] = _mm_nt(wt_ref[_T_G:_T_END, :], xb) + bias_ref[...]
    for c in range(BRANCH // LANES):
        p2_ref[:, _P2_MK + c * LANES:_P2_MK + (c + 1) * LANES] = kt[c * LANES:(c + 1) * LANES, :].T
    p2_ref[:, _P2_SK:_P2_END] = pt_ref[_T_SK:_T_SV, :].T


def _dproj(xs, w_all, wt_all, bias_all, cos, sin, cost, sint, layer):
    n = xs.shape[0]
    whole = lambda shape: pl.BlockSpec(shape, lambda i: (0, 0))
    return pl.pallas_call(
        _dproj_kernel,
        grid=(1,),
        in_specs=[_const_spec((n, D_MODEL)),
                  _layer_spec((D_MODEL, W_HALF), layer, 0, 1),
                  _layer_spec((_T_END, D_MODEL), layer),
                  _layer_spec((16, 1), layer),
                  _const_spec((n, LANES)), _const_spec((n, LANES)),
                  _const_spec((ROT_HALF, n)), _const_spec((ROT_HALF, n))],
        out_specs=[whole((n, _N_END)), whole((_T_END, n)), whole((n, _P2_END))],
        out_shape=[jax.ShapeDtypeStruct((n, _N_END), F32),
                   jax.ShapeDtypeStruct((_T_END, n), F32),
                   jax.ShapeDtypeStruct((n, _P2_END), F32)],
        compiler_params=_params(("arbitrary",)),
        name="decode_proj",
    )(xs, w_all, wt_all, bias_all, cos, sin, cost, sint)


def _dmlstm_kernel(q_ref, v_ref, mo_ref, k_ref, vt_ref, gt_ref, c_ref, n_ref, m_ref, gain_ref, *rest):
    y_ref, c_out_ref, n_out_ref, m_out_ref = rest[-4:]
    i = pl.program_id(0)
    bb = q_ref.shape[0]
    nlanes = gt_ref.shape[1]
    li = gt_ref[0:ML_HEADS, :]
    lf = _log_sigmoid(gt_ref[8:8 + ML_HEADS, :])
    m_prev = m_ref[...]
    m_t = jnp.maximum(lf + m_prev, li)
    m_out_ref[...] = m_t
    scal = jnp.concatenate([jnp.exp(li - m_t), jnp.exp(lf + m_prev - m_t), jnp.exp(-m_t), jnp.zeros_like(m_t)], axis=0)
    bring = jnp.where(i == 0, 0, nlanes - i * bb)
    scal = pltpu.roll(scal, bring, axis=1)
    tiles = [(h, j) for h in range(ML_HEADS) for j in range(bb)]
    hs = lambda h: slice(h * ML_DK, (h + 1) * ML_DK)
    nrow_of = lambda h, j: slice(j * ML_HEADS + h, j * ML_HEADS + h + 1)
    q = {(h, j): q_ref[j:j + 1, hs(h)] for h, j in tiles}
    k = {(h, j): k_ref[j:j + 1, hs(h)] for h, j in tiles}
    w = {(h, j): scal[h:h + 1, j:j + 1] for h, j in tiles}
    a = {(h, j): scal[ML_HEADS + h:ML_HEADS + h + 1, j:j + 1] for h, j in tiles}
    cq = {(h, j): _mm_nt(jnp.broadcast_to(q[h, j], (8, ML_DK)).astype(BF16), c_ref[j, h].astype(BF16))[0:1, :]
          for h, j in tiles}
    qk = {t: jnp.sum(q[t] * k[t], axis=1, keepdims=True) for t in tiles}
    nq = {(h, j): jnp.sum(n_ref[nrow_of(h, j), :] * q[h, j], axis=1, keepdims=True) for h, j in tiles}
    hrow = {}
    for h, j in tiles:
        sw = qk[h, j] * w[h, j]
        floor = scal[2 * ML_HEADS + h:2 * ML_HEADS + h + 1, j:j + 1]
        den = jnp.maximum(jnp.abs(sw + a[h, j] * nq[h, j]), floor)
        hrow[h, j] = (sw * v_ref[j:j + 1, hs(h)] + a[h, j] * cq[h, j]) / den
        n_out_ref[nrow_of(h, j), :] = a[h, j] * n_ref[nrow_of(h, j), :] + w[h, j] * k[h, j]
    for h in range(ML_HEADS):
        vt = pltpu.roll(vt_ref[hs(h), :], bring, axis=1)
        for j in range(bb):
            c_out_ref[j, h] = a[h, j] * c_ref[j, h] + (w[h, j] * vt[:, j:j + 1]) * k[h, j]
    hc = {t: hrow[t] - jnp.mean(hrow[t], axis=1, keepdims=True) for t in tiles}
    var = {t: jnp.mean(hc[t] * hc[t], axis=1, keepdims=True) for t in tiles}
    for h, j in tiles:
        hn = hc[h, j] * lax.rsqrt(var[h, j] + HEAD_NORM_EPS) * gain_ref[:, hs(h)]
        y_ref[j:j + 1, hs(h)] = (_sigmoid(mo_ref[j:j + 1, hs(h)]) * hn).astype(BF16)


def _dmlstm(p, p2, pt, c_all, n_all, mt_all, gain_all, layer, c_stack):
    n = p.shape[0]
    blk = lambda cidx: pl.BlockSpec((BB, BRANCH), lambda i: (i, cidx))
    in_specs = [blk(_N_MQ // BRANCH), blk(_N_MV // BRANCH), blk(_N_MO // BRANCH),
                pl.BlockSpec((BB, BRANCH), lambda i: (i, _P2_MK // BRANCH)),
                pl.BlockSpec((BRANCH, n), lambda i: (_T_MV // BRANCH, 0), pipeline_mode=pl.Buffered(1)),
                pl.BlockSpec((16, n), lambda i: (_T_G // 16, 0), pipeline_mode=pl.Buffered(1)),
                pl.BlockSpec((None, BB, ML_HEADS, ML_DV, ML_DK), lambda i: (layer, i, 0, 0, 0)),
                pl.BlockSpec((None, BB * ML_HEADS, ML_DK), lambda i: (layer, i, 0)),
                _layer_spec((ML_HEADS, n), layer),
                _layer_spec((1, BRANCH), layer)]
    args = [p, p, p, p2, pt, pt, c_all, n_all, mt_all, gain_all]
    aliases = {}
    if c_stack is not None:
        in_specs.append(pl.BlockSpec(memory_space=pl.ANY))
        args.append(c_stack)
        aliases = {len(args) - 1: 1}
    return pl.pallas_call(
        _dmlstm_kernel,
        grid=(n // BB,),
        in_specs=in_specs,
        out_specs=[pl.BlockSpec((BB, BRANCH), lambda i: (i, 0)),
                   pl.BlockSpec((None, BB, ML_HEADS, ML_DV, ML_DK), lambda i: (layer, i, 0, 0, 0)),
                   pl.BlockSpec((BB * ML_HEADS, ML_DK), lambda i: (i, 0)),
                   pl.BlockSpec((ML_HEADS, n), lambda i: (0, 0))],
        out_shape=[jax.ShapeDtypeStruct((n, BRANCH), BF16),
                   jax.ShapeDtypeStruct((DEPTH, n, ML_HEADS, ML_DV, ML_DK), F32),
                   jax.ShapeDtypeStruct((n * ML_HEADS, ML_DK), F32),
                   jax.ShapeDtypeStruct((ML_HEADS, n), F32)],
        input_output_aliases=aliases,
        compiler_params=_params(("arbitrary",)),
        name="decode_mlstm",
    )(*args)


def _dswa_kernel(q_ref, kn_ref, vn_ref, kvt_ref, ck_ref, cv_ref, sink_ref, *rest):
    y_ref, ko_ref, vo_ref = rest[-3:]
    i = pl.program_id(0)
    bb = q_ref.shape[0]
    lane = lax.broadcasted_iota(jnp.int32, (LANES, LANES), 1)
    row8 = lax.broadcasted_iota(jnp.int32, (SWA_HEADS, LANES), 0)
    low8 = lax.broadcasted_iota(jnp.int32, (SWA_HEADS, LANES), 1) < SWA_HD
    low1 = lax.broadcasted_iota(jnp.int32, (1, LANES), 1) < SWA_HD
    scale = SWA_HD ** -0.5
    sink = sink_ref[:, 0:1]
    knew_t = kvt_ref[0:LANES, :]
    vnew_t = kvt_ref[LANES:, :]
    for j in range(bb):
        ck = ck_ref[j]
        cv = cv_ref[j]
        kn = kn_ref[j:j + 1, :]
        vn = vn_ref[j:j + 1, :]
        qm = jnp.zeros((SWA_HEADS, LANES), F32)
        for pp in range(SWA_HEADS // 2):
            g = pp // 2
            pair = q_ref[j:j + 1, pp * LANES:(pp + 1) * LANES]
            swap = pltpu.roll(pair, SWA_HD, axis=1)
            in_g = low8 if g == 0 else jnp.logical_not(low8)
            for t in range(2):
                qm = jnp.where((row8 == 2 * pp + t) & in_g, pair if t == g else swap, qm)
        s = _mm(qm.astype(BF16), ck.astype(BF16)) * scale
        s_new = jnp.sum(qm * kn, axis=1, keepdims=True) * scale
        mx = jnp.maximum(jnp.maximum(jnp.max(s, axis=1, keepdims=True), s_new), sink)
        e = jnp.exp(s - mx)
        e_new = jnp.exp(s_new - mx)
        den = jnp.sum(e, axis=1, keepdims=True) + e_new + jnp.exp(sink - mx)
        o = (_mm_nt(e.astype(BF16), cv.astype(BF16)) + e_new * vn) / den
        for pp in range(SWA_HEADS // 2):
            g = pp // 2
            halves = []
            for t in range(2):
                oh = o[2 * pp + t:2 * pp + t + 1, :]
                halves.append(oh if t == g else pltpu.roll(oh, SWA_HD, axis=1))
            y_ref[j:j + 1, pp * LANES:(pp + 1) * LANES] = jnp.where(low1, halves[0], halves[1]).astype(BF16)
        bring = LANES - 1 - (i * bb + j)
        ko_ref[j] = jnp.where(lane == LANES - 1, pltpu.roll(knew_t, bring, axis=1), pltpu.roll(ck, LANES - 1, axis=1))
        vo_ref[j] = jnp.where(lane == LANES - 1, pltpu.roll(vnew_t, bring, axis=1), pltpu.roll(cv, LANES - 1, axis=1))


def _dswa(p, p2, pt, ck_all, cv_all, sinks_all, layer, k_stack, v_stack):
    n = p.shape[0]
    cache = pl.BlockSpec((None, BB, LANES, WINDOW), lambda i: (layer, i, 0, 0))
    in_specs = [pl.BlockSpec((BB, BRANCH), lambda i: (i, _N_SQ // BRANCH)),
                pl.BlockSpec((BB, LANES), lambda i: (i, _P2_SK // LANES)),
                pl.BlockSpec((BB, LANES), lambda i: (i, _N_SV // LANES)),
                pl.BlockSpec((2 * LANES, n), lambda i: (_T_SK // (2 * LANES), 0), pipeline_mode=pl.Buffered(1)),
                cache, cache, _layer_spec((8, LANES), layer)]
    args = [p, p2, p, pt, ck_all, cv_all, sinks_all]
    aliases = {}
    if k_stack is not None:
        in_specs += [pl.BlockSpec(memory_space=pl.ANY), pl.BlockSpec(memory_space=pl.ANY)]
        args += [k_stack, v_stack]
        aliases = {len(args) - 2: 1, len(args) - 1: 2}
    return pl.pallas_call(
        _dswa_kernel,
        grid=(n // BB,),
        in_specs=in_specs,
        out_specs=[pl.BlockSpec((BB, BRANCH), lambda i: (i, 0)), cache, cache],
        out_shape=[jax.ShapeDtypeStruct((n, BRANCH), BF16),
                   jax.ShapeDtypeStruct((DEPTH, n, LANES, WINDOW), F32),
                   jax.ShapeDtypeStruct((DEPTH, n, LANES, WINDOW), F32)],
        input_output_aliases=aliases,
        compiler_params=_params(("arbitrary",)),
        name="decode_swa",
    )(*args)


def _dcross_kernel(q_ref, k_ref, v_ref, y_ref):
    bb = q_ref.shape[0]
    scale = X_HD ** -0.5
    row8 = lax.broadcasted_iota(jnp.int32, (8, LANES), 0) % X_HEADS
    for j in range(bb):
        qrep = jnp.zeros((8, LANES), F32)
        for h in range(X_HEADS):
            qrep = jnp.where(row8 == h, q_ref[j:j + 1, h * X_HD:(h + 1) * X_HD], qrep)
        s = jnp.sum(k_ref[j] * qrep[None], axis=2, keepdims=True) * scale
        mx8 = jnp.max(s, axis=0)
        mx4 = jnp.maximum(mx8[0:X_HEADS], mx8[X_HEADS:])
        e = jnp.exp(s - jnp.concatenate([mx4, mx4], axis=0)[None])
        den8 = jnp.sum(e, axis=0)
        o8 = jnp.sum(e * v_ref[j], axis=0)
        o4 = (o8[0:X_HEADS] + o8[X_HEADS:]) / (den8[0:X_HEADS] + den8[X_HEADS:])
        for h in range(X_HEADS):
            y_ref[j:j + 1, h * X_HD:(h + 1) * X_HD] = o4[h:h + 1, :].astype(BF16)


def _dcross(p, mk_all, mv_all, layer):
    n = p.shape[0]
    cache = pl.BlockSpec((None, BB, MEM_TOKENS * X_HEADS // 8, 8, X_HD), lambda i: (layer, i, 0, 0, 0))
    return pl.pallas_call(
        _dcross_kernel,
        grid=(n // BB,),
        in_specs=[pl.BlockSpec((BB, BRANCH), lambda i: (i, _N_XQ // BRANCH)), cache, cache],
        out_specs=pl.BlockSpec((BB, BRANCH), lambda i: (i, 0)),
        out_shape=jax.ShapeDtypeStruct((n, BRANCH), BF16),
        compiler_params=_params(("arbitrary",)),
        name="decode_cross",
    )(p, mk_all, mv_all)


def _rope_tables(positions):
    inv_freq = ROPE_THETA ** (-jnp.arange(ROT_HALF, dtype=F32) / ROT_HALF)
    ang = positions.astype(F32)[:, None] * inv_freq[None, :]
    cos = jnp.cos(ang)
    sin = jnp.sin(ang)
    reps = LANES // ROT_HALF
    return jnp.tile(cos, (1, reps)), jnp.tile(sin, (1, reps)), cos.T, sin.T


def kernel(x_prompt, x_sample, mem_prompt, cache_swa_k, cache_swa_v, cache_mem_k, cache_mem_v, state_mlstm_c, state_mlstm_n, state_mlstm_m, w_in, b_gates, mlstm_norm_g, swa_sinks, w_mem_kv, w_branch, w_mix_out, ln1_g, ln1_b, w_ffn_in, w_ffn_out, ln2_g, ln2_b):
    nb, seq, _ = x_prompt.shape
    ns = x_sample.shape[0]
    assert ns == LANES and PAST_LEN >= WINDOW

    cosn, sinn, cost, sint = _rope_tables(jnp.arange(seq))
    cos_s, sin_s, cost_s, sint_s = _rope_tables(jnp.full((ns,), PAST_LEN))

    ck_all = jnp.transpose(cache_swa_k, (0, 1, 3, 4, 2)).reshape(DEPTH, ns, SWA_KV * SWA_HD, WINDOW)
    cv_all = jnp.transpose(cache_swa_v, (0, 1, 3, 4, 2)).reshape(DEPTH, ns, SWA_KV * SWA_HD, WINDOW)
    mk_all = cache_mem_k.reshape(DEPTH, ns, MEM_TOKENS * X_HEADS // 8, 8, X_HD)
    mv_all = cache_mem_v.reshape(DEPTH, ns, MEM_TOKENS * X_HEADS // 8, 8, X_HD)
    n_all = state_mlstm_n.reshape(DEPTH, ns * ML_HEADS, ML_DK)
    mt_all = jnp.transpose(state_mlstm_m, (0, 2, 1))

    zcols = lambda n: jnp.zeros((DEPTH, D_MODEL, n), F32)
    w_all = jnp.concatenate([w_in[..., _C_GL:], w_in[..., _C_MQ:_C_MK], w_in[..., _C_MV:_C_MI],
                             w_in[..., _C_SQ:_C_SK], w_in[..., _C_XQ:_C_GL], w_in[..., _C_SV:_C_XQ],
                             zcols(W_HALF - _N_END)], axis=-1).astype(BF16)
    wt_all = jnp.swapaxes(jnp.concatenate([w_in[..., _C_MV:_C_MO], w_in[..., _C_MK:_C_MV], w_in[..., _C_SK:_C_XQ],
                                           w_in[..., _C_MI:_C_MF], zcols(4), w_in[..., _C_MF:_C_SQ], zcols(4)],
                                          axis=-1), 1, 2).astype(BF16)
    z4 = jnp.zeros((DEPTH, 4), F32)
    bias_all = jnp.concatenate([b_gates[:, :ML_HEADS], z4, b_gates[:, ML_HEADS:], z4], axis=1)[..., None]
    gain_all = mlstm_norm_g[:, None, :]
    sinks_all = jnp.broadcast_to(swa_sinks[:, :, None], (DEPTH, SWA_HEADS, LANES))
    wkv_all = w_mem_kv.astype(BF16)
    wkt_all = jnp.swapaxes(w_mem_kv[..., :BRANCH], 1, 2).astype(BF16)
    wbr_all = w_branch.astype(BF16)
    wmix_all = w_mix_out.astype(BF16)
    wfi_all = w_ffn_in.astype(BF16)
    wfo_all = w_ffn_out.astype(BF16)
    ln_all = (ln1_g[:, None, :], ln1_b[:, None, :], ln2_g[:, None, :], ln2_b[:, None, :])

    yp = x_prompt.reshape(nb * seq, D_MODEL)
    ys = x_sample.reshape(ns, D_MODEL)
    mem2d = mem_prompt.reshape(nb * MEM_TOKENS, D_MODEL)

    outs = {k: [] for k in ("kp", "vp", "mk", "mv", "cp", "np", "mp", "ns", "ms")}
    c_stack = k_stack = v_stack = None
    for l in range(DEPTH):
        mk32, mv32, mkt, mv16 = _memkv(mem2d, wkv_all, wkt_all, l, nb)
        ya, yb, xq, k32, v32, m_fin, s_fin = _mixer(yp, w_all, wt_all, bias_all, cosn, sinn, cost, sint,
                                                    gain_all, sinks_all, l, nb, seq)
        yp = _merge_ffn(yp, ya, yb, xq, w_all, wbr_all, wmix_all, wfi_all, wfo_all, *ln_all, l, TM_MERGE,
                        memory=(mkt, mv16, seq))
        outs["kp"].append(jnp.transpose(k32.reshape(nb, SWA_KV, SWA_HD, WINDOW), (0, 3, 1, 2)))
        outs["vp"].append(jnp.transpose(v32.reshape(nb, SWA_KV, SWA_HD, WINDOW), (0, 3, 1, 2)))
        outs["mk"].append(mk32.reshape(nb, MEM_TOKENS, X_HEADS, X_HD))
        outs["mv"].append(mv32.reshape(nb, MEM_TOKENS, X_HEADS, X_HD))
        outs["cp"].append(jnp.swapaxes(s_fin[..., :ML_DV], -1, -2))
        outs["np"].append(s_fin[..., ML_DV])
        outs["mp"].append(m_fin.reshape(nb, 8, LANES)[:, :ML_HEADS, 0])

        p, pt, p2 = _dproj(ys, w_all, wt_all, bias_all, cos_s, sin_s, cost_s, sint_s, l)
        ya_s, c_stack, n_new, m_new = _dmlstm(p, p2, pt, state_mlstm_c, n_all, mt_all, gain_all, l, c_stack)
        yb_s, k_stack, v_stack = _dswa(p, p2, pt, ck_all, cv_all, sinks_all, l, k_stack, v_stack)
        yc_s = _dcross(p, mk_all, mv_all, l)
        ys = _merge_ffn(ys, ya_s, yb_s, yc_s, w_all, wbr_all, wmix_all, wfi_all, wfo_all, *ln_all, l, ns)
        outs["ns"].append(n_new.reshape(ns, ML_HEADS, ML_DK))
        outs["ms"].append(m_new)

    st = {k: jnp.stack(vals) for k, vals in outs.items()}
    window_out = lambda t: jnp.transpose(t.reshape(DEPTH, ns, SWA_KV, SWA_HD, WINDOW), (0, 1, 4, 2, 3))
    return (yp.reshape(nb, seq, D_MODEL), ys.reshape(ns, 1, D_MODEL),
            st["kp"], st["vp"], window_out(k_stack), window_out(v_stack), st["mk"], st["mv"],
            st["cp"], st["np"], st["mp"], c_stack, st["ns"], jnp.transpose(st["ms"], (0, 2, 1)))
```

```python
import functools

import jax
import jax.numpy as jnp
from jax import lax
from jax.experimental import pallas as pl
from jax.experimental.pallas import tpu as pltpu

F32 = jnp.float32
BF16 = jnp.bfloat16

D_MODEL = 1024
DEPTH = 2
BRANCH = 512
ML_HEADS = 4
ML_DK = 128
ML_DV = 128
ML_CHUNK = 128
SWA_HD = 64
SWA_HEADS = 8
SWA_KV = 2
SWA_GROUP = 4
WINDOW = 128
ROT_DIM = 16
ROT_HALF = 8
ROPE_THETA = 500000.0
MEM_TOKENS = 256
X_HEADS = 4
X_HD = 128
D_FF = 2816
LN_EPS = 1e-5
HEAD_NORM_EPS = 1e-6
DEEPNORM_ALPHA = (2 * DEPTH) ** 0.25
NEG_INF = -1e30
PAST_LEN = 8192

LANES = 128
VMEM_LIMIT = 56 * 1024 * 1024

_C_MQ, _C_MK, _C_MV, _C_MO = 0, 512, 1024, 1536
_C_MI, _C_MF = 2048, 2052
_C_SQ, _C_SK, _C_SV, _C_XQ, _C_GL = 2056, 2568, 2696, 2824, 3336

W_HALF = 3 * D_MODEL
_N_MQ, _N_MV, _N_MO, _N_SQ, _N_XQ, _N_SV, _N_END = 0, 512, 1024, 1536, 2048, 2560, 2688
_T_MV, _T_MK, _T_SK, _T_SV, _T_G, _T_END = 0, 512, 1024, 1152, 1280, 1296

TM_PROJ = 512
ML_GROUP = 4
TM_MERGE = 256
TQ_CROSS = 512
BB = 8


def _mm(a, b):
    return jnp.dot(a, b, preferred_element_type=F32)


def _mm_nt(a, b):
    return lax.dot_general(a, b, (((1,), (1,)), ((), ())), preferred_element_type=F32)


def _sigmoid(x):
    return 1.0 / (1.0 + jnp.exp(-x))


def _log_sigmoid(x):
    return jnp.minimum(x, 0.0) - jnp.log(1.0 + jnp.exp(-jnp.abs(x)))


def _layer_norm(x, g, b):
    mu = jnp.mean(x, axis=-1, keepdims=True)
    xc = x - mu
    var = jnp.mean(xc * xc, axis=-1, keepdims=True)
    return xc * lax.rsqrt(var + LN_EPS) * g + b


def _rope_lanes(x, cos, sin):
    lane = lax.broadcasted_iota(jnp.int32, x.shape, 1) % SWA_HD
    up = pltpu.roll(x, LANES - ROT_HALF, axis=1)
    dn = pltpu.roll(x, ROT_HALF, axis=1)
    first = x * cos - up * sin
    second = x * cos + dn * sin
    return jnp.where(lane < ROT_HALF, first, jnp.where(lane < ROT_DIM, second, x))


def _rope_rows(xt, cost, sint):
    x1 = xt[0:ROT_HALF, :]
    x2 = xt[ROT_HALF:ROT_DIM, :]
    return ((0, x1 * cost - x2 * sint), (ROT_HALF, x2 * cost + x1 * sint), (ROT_DIM, xt[ROT_DIM:SWA_HD, :]))


def _const_spec(shape):
    nd = len(shape)
    return pl.BlockSpec(shape, lambda *_: (0,) * nd, pipeline_mode=pl.Buffered(1))


def _layer_spec(shape, layer, *tail):
    idx = (layer,) + (tail if tail else (0,) * len(shape))
    return pl.BlockSpec((None,) + tuple(shape), lambda *_: idx, pipeline_mode=pl.Buffered(1))


def _params(sem):
    return pltpu.CompilerParams(dimension_semantics=sem, vmem_limit_bytes=VMEM_LIMIT)


def _memkv_kernel(mem_ref, wkv_ref, wkt_ref, k32_ref, v32_ref, kt_ref, v16_ref):
    m = mem_ref[...].astype(BF16)
    kv = _mm(m, wkv_ref[...])
    k32_ref[...] = kv[:, :BRANCH]
    v32_ref[...] = kv[:, BRANCH:]
    v16_ref[...] = kv[:, BRANCH:].astype(BF16)
    kt_ref[...] = _mm_nt(wkt_ref[...], m).astype(BF16)


def _memkv(mem2d, wkv, wkt, layer, nb):
    rows = mem2d.shape[0]
    return pl.pallas_call(
        _memkv_kernel,
        grid=(nb,),
        in_specs=[pl.BlockSpec((MEM_TOKENS, D_MODEL), lambda b: (b, 0)),
                  _layer_spec((D_MODEL, 2 * BRANCH), layer),
                  _layer_spec((BRANCH, D_MODEL), layer)],
        out_specs=[pl.BlockSpec((MEM_TOKENS, BRANCH), lambda b: (b, 0)),
                   pl.BlockSpec((MEM_TOKENS, BRANCH), lambda b: (b, 0)),
                   pl.BlockSpec((None, BRANCH, MEM_TOKENS), lambda b: (b, 0, 0)),
                   pl.BlockSpec((MEM_TOKENS, BRANCH), lambda b: (b, 0))],
        out_shape=[jax.ShapeDtypeStruct((rows, BRANCH), F32),
                   jax.ShapeDtypeStruct((rows, BRANCH), F32),
                   jax.ShapeDtypeStruct((nb, BRANCH, MEM_TOKENS), BF16),
                   jax.ShapeDtypeStruct((rows, BRANCH), BF16)],
        compiler_params=_params(("arbitrary",)),
        name="memkv_proj",
    )(mem2d, wkv, wkt)


def _mlstm_gate_weights(pre, gt_ref, gc_ref, m_out_ref, m_scr, valid):
    tm = pre.shape[1]
    li = pre[0:8, :]
    lane8 = lax.broadcasted_iota(jnp.int32, li.shape, 1) % ML_CHUNK
    b = _log_sigmoid(pre[8:16, :])
    shift = 1
    while shift < ML_CHUNK:
        b = b + jnp.where(lane8 >= shift, pltpu.roll(b, shift, axis=1), 0.0)
        shift *= 2
    g = li - b
    cm = g
    shift = 1
    while shift < ML_CHUNK:
        cm = jnp.maximum(cm, jnp.where(lane8 >= shift, pltpu.roll(cm, shift, axis=1), -jnp.inf))
        shift *= 2
    gt_ref[0:8, :] = g
    pad = jnp.zeros((LANES - 24, ML_CHUNK), F32)
    m_start = m_scr[...]
    m_prev = m_start
    for c in range(tm // ML_CHUNK):
        cs = slice(c * ML_CHUNK, (c + 1) * ML_CHUNK)
        b_c = b[:, cs]
        b_last = jnp.broadcast_to(b_c[:, ML_CHUNK - 1:ML_CHUNK], b_c.shape)
        cm_last = jnp.broadcast_to(cm[:, cs][:, ML_CHUNK - 1:ML_CHUNK], b_c.shape)
        m_t = b_c + jnp.maximum(m_prev, cm[:, cs])
        m_new = b_last + jnp.maximum(m_prev, cm_last)
        gt_ref[8:16, cs] = jnp.exp(b_last + g[:, cs] - m_new)
        gt_ref[16:24, cs] = jnp.exp(b_last + m_prev - m_new)
        rows = jnp.concatenate([b_c - m_t, jnp.exp(b_c + m_prev - m_t), jnp.exp(-m_t), pad], axis=0)
        gc_ref[cs, :] = rows.T
        m_prev = m_new
    m_prev = jnp.where(valid, m_prev, m_start)
    m_scr[...] = m_prev
    m_out_ref[...] = m_prev


def _proj_stream(x_ref, wn_ref, wt_ref, bias_ref, cosn_ref, sinn_ref, cost_ref, sint_ref, dst,
                 xq_ref, k32_ref, v32_ref, m_out_ref, m_scr, valid):
    tm = x_ref.shape[0]
    xb = x_ref[...].astype(BF16)
    tr_rows = lambda lo, hi: _mm_nt(wt_ref[lo:hi, :], xb)
    _mlstm_gate_weights(tr_rows(_T_G, _T_END) + bias_ref[...], dst["gt"], dst["gc"], m_out_ref, m_scr, valid)
    yield
    dst["q"][...] = _mm(xb, wn_ref[:, _N_MQ:_N_MV]).astype(BF16)
    yield
    dst["v"][...] = _mm(xb, wn_ref[:, _N_MV:_N_MO]).astype(BF16)
    yield
    dst["mo"][...] = _mm(xb, wn_ref[:, _N_MO:_N_SQ])
    yield
    xq_ref[...] = _mm(xb, wn_ref[:, _N_XQ:_N_SV]).astype(BF16)
    yield
    cosn = cosn_ref[...]
    sinn = sinn_ref[...]
    sq = _mm(xb, wn_ref[:, _N_SQ:_N_XQ])
    for c in range(BRANCH // LANES):
        blk = _rope_lanes(sq[:, c * LANES:(c + 1) * LANES], cosn, sinn)
        dst["sq"][:, c * LANES:(c + 1) * LANES] = blk.astype(BF16)
    yield
    dst["sv"][...] = _mm(xb, wn_ref[:, _N_SV:_N_END]).astype(BF16)
    dst["kt"][...] = (tr_rows(_T_MK, _T_SK) * (ML_DK ** -0.5)).astype(BF16)
    yield
    skt = tr_rows(_T_SK, _T_SV)
    svt = tr_rows(_T_SV, _T_G)
    cost = cost_ref[...]
    sint = sint_ref[...]
    tail = slice(tm - WINDOW, tm)
    v32_ref[...] = svt[:, tail]
    for g in range(SWA_KV):
        base = g * SWA_HD
        for off, val in _rope_rows(skt[base:base + SWA_HD, :], cost, sint):
            dst["skt"][base + off:base + off + val.shape[0], :] = val.astype(BF16)
            k32_ref[base + off:base + off + val.shape[0], :] = val[:, tail]


def _mlstm_stream(q_ref, v_ref, kt_ref, mo_ref, gt_ref, gc_ref, gain_ref, y_ref, s_ref):
    L = ML_CHUNK
    r_i = lax.broadcasted_iota(jnp.int32, (L, L), 0)
    c_i = lax.broadcasted_iota(jnp.int32, (L, L), 1)
    causal = c_i <= r_i
    ones = jnp.ones((L, ML_DV), BF16)

    ts = lambda c: slice(c * L, (c + 1) * L)
    hs = lambda h: slice(h * ML_DK, (h + 1) * ML_DK)
    n_chunks = q_ref.shape[0] // L

    for c0 in range(0, n_chunks, ML_GROUP):
        chunks = range(c0, min(c0 + ML_GROUP, n_chunks))
        units = [(c, h) for c in chunks for h in range(ML_HEADS)]

        qk = {(c, h): _mm(q_ref[ts(c), hs(h)], kt_ref[hs(h), ts(c)]) for c, h in units}
        yield

        sw, kts, vext = {}, {}, {}
        for c, h in units:
            g_r = gt_ref[h:h + 1, ts(c)]
            es_r = gt_ref[8 + h:9 + h, ts(c)]
            u_c = gc_ref[ts(c), h:h + 1]
            sw[c, h] = (qk[c, h] * jnp.exp(jnp.where(causal, u_c + g_r, -jnp.inf))).astype(BF16)
            kts[c, h] = (kt_ref[hs(h), ts(c)].astype(F32) * es_r).astype(BF16)
            vext[c, h] = jnp.concatenate([v_ref[ts(c), hs(h)], ones], axis=1)
        yield

        intra = {u: _mm(sw[u], vext[u]) for u in units}
        delta = {u: _mm(kts[u], vext[u]) for u in units}
        yield

        s_in = {}
        for h in range(ML_HEADS):
            state = s_ref[h]
            for c in chunks:
                s_in[c, h] = state.astype(BF16)
                state = gt_ref[16 + h:17 + h, c * L:c * L + 1] * state + delta[c, h]
            s_ref[h] = state
        qs_all = {u: _mm(q_ref[ts(u[0]), hs(u[1])], s_in[u]) for u in units}
        yield

        hh, hc = {}, {}
        for c, h in units:
            tot = intra[c, h] + gc_ref[ts(c), 8 + h:9 + h] * qs_all[c, h]
            floor = gc_ref[ts(c), 16 + h:17 + h]
            hh[c, h] = tot[:, :ML_DV] * (1.0 / jnp.maximum(jnp.abs(tot[:, ML_DV:]), floor))
        for u in units:
            hc[u] = hh[u] - jnp.mean(hh[u], axis=1, keepdims=True)
        for c, h in units:
            var = jnp.mean(hc[c, h] * hc[c, h], axis=1, keepdims=True)
            hn = hc[c, h] * lax.rsqrt(var + HEAD_NORM_EPS) * gain_ref[:, hs(h)]
            y_ref[ts(c), hs(h)] = (_sigmoid(mo_ref[ts(c), hs(h)]) * hn).astype(BF16)
        yield


def _swa_stream(q_ref, ktp_ref, ktc_ref, vp_ref, vc_ref, sink_ref, y_ref, seq_start):
    L = WINDOW
    nblk = q_ref.shape[0] // L
    r_i = lax.broadcasted_iota(jnp.int32, (L, 2 * L), 0)
    c_i = lax.broadcasted_iota(jnp.int32, (L, 2 * L), 1)
    band = (c_i >= r_i) & (c_i <= r_i + L)
    first = band & (c_i >= jnp.where(seq_start, L, 0))
    low_half = lax.broadcasted_iota(jnp.int32, (2 * L, LANES), 1) < SWA_HD
    out_low = lax.broadcasted_iota(jnp.int32, (L, LANES), 1) < SWA_HD
    zeros_k = jnp.zeros((SWA_HD, 2 * L), BF16)
    ones_lo = jnp.where(low_half, 1.0, 0.0).astype(BF16)
    ones_hi = jnp.where(low_half, 0.0, 1.0).astype(BF16)

    kt_all = jnp.concatenate([ktp_ref[...], ktc_ref[...]], axis=1)
    v_all = jnp.concatenate([vp_ref[...], vc_ref[...]], axis=0).astype(F32)
    v_swap = pltpu.roll(v_all, SWA_HD, axis=1)

    def scores(c):
        win = slice(c * L, (c + 2) * L)
        out = []
        for g in range(SWA_KV):
            kt2 = kt_all[g * SWA_HD:(g + 1) * SWA_HD, win]
            kblk = jnp.concatenate([jnp.concatenate([kt2, zeros_k], axis=0),
                                    jnp.concatenate([zeros_k, kt2], axis=0)], axis=1)
            for pp in range(2 * g, 2 * g + 2):
                out.append(_mm(q_ref[c * L:(c + 1) * L, pp * LANES:(pp + 1) * LANES], kblk))
        return out

    def weights(c, s_list):
        allowed = first if c == 0 else band
        out = []
        for head in range(SWA_HEADS):
            s = s_list[head // 2][:, (head % 2) * 2 * L:(head % 2 + 1) * 2 * L]
            sc = jnp.where(allowed, s * (SWA_HD ** -0.5), NEG_INF)
            sink = sink_ref[head:head + 1, 0:1]
            mx = jnp.broadcast_to(jnp.maximum(jnp.max(sc, axis=1, keepdims=True), sink), sc.shape)
            out.append((jnp.exp(sc - mx).astype(BF16), jnp.exp(sink - mx[:, :LANES])))
        return out

    def outputs(c, e_list):
        win = slice(c * L, (c + 2) * L)
        v2 = v_all[win, :]
        v2s = v_swap[win, :]
        for g in range(SWA_KV):
            va = jnp.where(low_half, v2 if g == 0 else v2s, 0.0).astype(BF16)
            vb = jnp.where(low_half, 0.0, v2s if g == 0 else v2).astype(BF16)
            vden = jnp.concatenate([jnp.concatenate([va, ones_lo], axis=1),
                                    jnp.concatenate([vb, ones_hi], axis=1)], axis=0)
            for pp in range(2 * g, 2 * g + 2):
                (e0, k0), (e1, k1) = e_list[2 * pp], e_list[2 * pp + 1]
                res = _mm(jnp.concatenate([e0, e1], axis=1), vden)
                den = res[:, LANES:] + jnp.where(out_low, k0, k1)
                y_ref[c * L:(c + 1) * L, pp * LANES:(pp + 1) * LANES] = (res[:, :LANES] * (1.0 / den)).astype(BF16)

    s_next = scores(0)
    yield
    for c in range(nblk):
        s_cur = s_next
        if c + 1 < nblk:
            s_next = scores(c + 1)
        outputs(c, weights(c, s_cur))
        yield


_SLOT_BUFFERS = (("q", (TM_PROJ, BRANCH), BF16), ("v", (TM_PROJ, BRANCH), BF16), ("mo", (TM_PROJ, BRANCH), F32),
                 ("sq", (TM_PROJ, BRANCH), BF16), ("sv", (TM_PROJ, LANES), BF16), ("kt", (BRANCH, TM_PROJ), BF16),
                 ("skt", (LANES, TM_PROJ), BF16), ("gt", (24, TM_PROJ), F32), ("gc", (TM_PROJ, LANES), F32))


def _mixer_kernel(x_ref, wn_ref, wt_ref, bias_ref, cosn_ref, sinn_ref, cost_ref, sint_ref, gain_ref, sink_ref,
                  ya_ref, yb_ref, xq_ref, k32_ref, v32_ref, m_out_ref, s_out_ref, *scratch, n_tiles, tiles_per_seq):
    slots = {name: ref for (name, _, _), ref in zip(_SLOT_BUFFERS, scratch)}
    s_ref, m_scr, ktp_ref, vp_ref = scratch[len(_SLOT_BUFFERS):]
    i = pl.program_id(0)
    wr = i % 2
    mix_tile = i - 1
    valid = i < n_tiles

    @pl.when(i == 0)
    def _():
        for ref in scratch:
            ref[...] = jnp.zeros_like(ref)

    @pl.when((i % tiles_per_seq == 0) & valid)
    def _():
        m_scr[...] = jnp.zeros_like(m_scr)

    seq_start = mix_tile % tiles_per_seq == 0

    @pl.when(seq_start)
    def _():
        s_ref[...] = jnp.zeros_like(s_ref)

    dst = {name: ref.at[wr] for name, ref in slots.items()}
    src = {name: ref.at[1 - wr] for name, ref in slots.items()}
    streams = [
        _proj_stream(x_ref, wn_ref, wt_ref, bias_ref, cosn_ref, sinn_ref, cost_ref, sint_ref, dst,
                     xq_ref, k32_ref, v32_ref, m_out_ref, m_scr, valid),
        _mlstm_stream(src["q"], src["v"], src["kt"], src["mo"], src["gt"], src["gc"], gain_ref, ya_ref, s_ref),
        _swa_stream(src["sq"], ktp_ref, src["skt"], vp_ref, src["sv"], sink_ref, yb_ref, seq_start),
    ]
    while streams:
        for g in list(streams):
            if next(g, StopIteration) is StopIteration:
                streams.remove(g)

    tm = x_ref.shape[0]
    ktp_ref[...] = src["skt"][:, tm - WINDOW:]
    vp_ref[...] = src["sv"][tm - WINDOW:, :]

    @pl.when(mix_tile % tiles_per_seq == tiles_per_seq - 1)
    def _():
        s_out_ref[...] = s_ref[...]


def _mixer(x2d, w_all, wt_all, bias_all, cosn, sinn, cost, sint, gain_all, sinks_all, layer, nb, seq):
    m = x2d.shape[0]
    tm = TM_PROJ
    nt = seq // tm
    n = m // tm
    proj = lambda i: jnp.minimum(i, n - 1)
    mix = lambda i: jnp.maximum(i - 1, 0)
    return pl.pallas_call(
        functools.partial(_mixer_kernel, n_tiles=n, tiles_per_seq=nt),
        grid=(n + 1,),
        in_specs=[pl.BlockSpec((tm, D_MODEL), lambda i: (proj(i), 0)),
                  _layer_spec((D_MODEL, W_HALF), layer, 0, 1),
                  _layer_spec((_T_END, D_MODEL), layer),
                  _layer_spec((16, 1), layer),
                  pl.BlockSpec((tm, LANES), lambda i: (proj(i) % nt, 0)),
                  pl.BlockSpec((tm, LANES), lambda i: (proj(i) % nt, 0)),
                  pl.BlockSpec((ROT_HALF, tm), lambda i: (0, proj(i) % nt)),
                  pl.BlockSpec((ROT_HALF, tm), lambda i: (0, proj(i) % nt)),
                  _layer_spec((1, BRANCH), layer),
                  _layer_spec((8, LANES), layer)],
        out_specs=[pl.BlockSpec((tm, BRANCH), lambda i: (mix(i), 0)),
                   pl.BlockSpec((tm, BRANCH), lambda i: (mix(i), 0)),
                   pl.BlockSpec((tm, BRANCH), lambda i: (proj(i), 0)),
                   pl.BlockSpec((LANES, WINDOW), lambda i: (proj(i) // nt, 0)),
                   pl.BlockSpec((LANES, WINDOW), lambda i: (proj(i) // nt, 0)),
                   pl.BlockSpec((8, LANES), lambda i: (proj(i) // nt, 0)),
                   pl.BlockSpec((None, ML_HEADS, ML_DK, 2 * ML_DV), lambda i: (mix(i) // nt, 0, 0, 0))],
        out_shape=[jax.ShapeDtypeStruct((m, BRANCH), BF16),
                   jax.ShapeDtypeStruct((m, BRANCH), BF16),
                   jax.ShapeDtypeStruct((m, BRANCH), BF16),
                   jax.ShapeDtypeStruct((nb * LANES, WINDOW), F32),
                   jax.ShapeDtypeStruct((nb * LANES, WINDOW), F32),
                   jax.ShapeDtypeStruct((nb * 8, LANES), F32),
                   jax.ShapeDtypeStruct((nb, ML_HEADS, ML_DK, 2 * ML_DV), F32)],
        scratch_shapes=[pltpu.VMEM((2,) + shape, dtype) for _, shape, dtype in _SLOT_BUFFERS]
        + [pltpu.VMEM((ML_HEADS, ML_DK, 2 * ML_DV), F32), pltpu.VMEM((8, LANES), F32),
           pltpu.VMEM((LANES, WINDOW), BF16), pltpu.VMEM((WINDOW, LANES), BF16)],
        compiler_params=_params(("arbitrary",)),
        name="prompt_mixer",
    )(x2d, w_all, wt_all, bias_all, cosn, sinn, cost, sint, gain_all, sinks_all)


def _cross_scores(q, kt_ref):
    return [_mm(q[:, h * X_HD:(h + 1) * X_HD], kt_ref[h * X_HD:(h + 1) * X_HD, :]) for h in range(X_HEADS)]


def _cross_outputs(scores, v_ref):
    ones = jnp.ones((MEM_TOKENS, X_HD), BF16)
    out = []
    for h, s in enumerate(scores):
        s = s * (X_HD ** -0.5)
        e = jnp.exp(s - jnp.max(s, axis=1, keepdims=True)).astype(BF16)
        res = _mm(e, jnp.concatenate([v_ref[:, h * X_HD:(h + 1) * X_HD], ones], axis=1))
        out.append((res[:, :X_HD] * (1.0 / res[:, X_HD:])).astype(BF16))
    return jnp.concatenate(out, axis=1)


def _merge_ffn_kernel(*refs, cross_attend):
    if cross_attend:
        x_ref, ya_ref, yb_ref, xq_ref, mkt_ref, mv_ref = refs[:6]
        refs = refs[6:]
    else:
        x_ref, ya_ref, yb_ref, yc_ref = refs[:4]
        refs = refs[4:]
    wgl_ref, wbr_ref, wmix_ref, wfi_ref, wfo_ref, g1_ref, b1_ref, g2_ref, b2_ref, o_ref = refs
    tm = x_ref.shape[0]
    n_sub = 2 if tm >= TM_MERGE else 1
    halves = [slice(s * (tm // n_sub), (s + 1) * (tm // n_sub)) for s in range(n_sub)]
    x = [x_ref[s, :] for s in halves]
    xb = [v.astype(BF16) for v in x]
    if cross_attend:
        scores = [_cross_scores(xq_ref[s, :], mkt_ref) for s in halves]
    gates = [[_sigmoid(_mm(xb[i], wgl_ref[:, r * D_MODEL:(r + 1) * D_MODEL])) for r in range(3)]
             for i in range(n_sub)]
    if cross_attend:
        yc = [_cross_outputs(sc, mv_ref) for sc in scores]
    else:
        yc = [yc_ref[s, :] for s in halves]
    acc = []
    for i, s in enumerate(halves):
        tot = None
        for r, y in enumerate((ya_ref[s, :], yb_ref[s, :], yc[i])):
            term = gates[i][r] * _mm(y, wbr_ref[r])
            tot = term if tot is None else tot + term
        acc.append(tot.astype(BF16))
    x1 = [_layer_norm(DEEPNORM_ALPHA * x[i] + _mm(acc[i], wmix_ref[...]), g1_ref[...], b1_ref[...])
          for i in range(n_sub)]
    act = []
    for i in range(n_sub):
        x1b = x1[i].astype(BF16)
        gpre = _mm(x1b, wfi_ref[:, :D_FF])
        up = _mm(x1b, wfi_ref[:, D_FF:])
        act.append((gpre * _sigmoid(gpre) * up).astype(BF16))
    for i, s in enumerate(halves):
        o_ref[s, :] = _layer_norm(DEEPNORM_ALPHA * x1[i] + _mm(act[i], wfo_ref[...]), g2_ref[...], b2_ref[...])


def _merge_ffn(x2d, ya, yb, third, w_all, wbr, wmix, wfi, wfo, g1, b1, g2, b2, layer, tm, memory=None):
    m = x2d.shape[0]
    row = lambda i: (i, 0)
    vec = _layer_spec((1, D_MODEL), layer)
    mem_specs, mem_args = [], []
    if memory is not None:
        mkt, mv16, seq = memory
        per = seq // tm
        mem_specs = [pl.BlockSpec((None, BRANCH, MEM_TOKENS), lambda i: (i // per, 0, 0)),
                     pl.BlockSpec((MEM_TOKENS, BRANCH), lambda i: (i // per, 0))]
        mem_args = [mkt, mv16]
    return pl.pallas_call(
        functools.partial(_merge_ffn_kernel, cross_attend=memory is not None),
        grid=(m // tm,),
        in_specs=[pl.BlockSpec((tm, D_MODEL), row),
                  pl.BlockSpec((tm, BRANCH), row),
                  pl.BlockSpec((tm, BRANCH), row),
                  pl.BlockSpec((tm, BRANCH), row),
                  *mem_specs,
                  _layer_spec((D_MODEL, W_HALF), layer, 0, 0),
                  _layer_spec((3, BRANCH, D_MODEL), layer),
                  _layer_spec((D_MODEL, D_MODEL), layer),
                  _layer_spec((D_MODEL, 2 * D_FF), layer),
                  _layer_spec((D_FF, D_MODEL), layer),
                  vec, vec, vec, vec],
        out_specs=pl.BlockSpec((tm, D_MODEL), row),
        out_shape=jax.ShapeDtypeStruct((m, D_MODEL), F32),
        compiler_params=_params(("arbitrary",)),
        name="merge_ffn",
    )(x2d, ya, yb, third, *mem_args, w_all, wbr, wmix, wfi, wfo, g1, b1, g2, b2)


_P2_MK, _P2_SK, _P2_END = 0, 512, 640


def _dproj_kernel(x_ref, wn_ref, wt_ref, bias_ref, cos_ref, sin_ref, cost_ref, sint_ref, p_ref, pt_ref, p2_ref):
    xb = x_ref[...].astype(BF16)
    cos = cos_ref[...]
    sin = sin_ref[...]
    for c in range(_N_END // LANES):
        cs = slice(c * LANES, (c + 1) * LANES)
        blk = _mm(xb, wn_ref[:, cs])
        if _N_SQ <= c * LANES < _N_XQ:
            blk = _rope_lanes(blk, cos, sin)
        p_ref[:, cs] = blk
    pt_ref[_T_MV:_T_MK, :] = _mm_nt(wt_ref[_T_MV:_T_MK, :], xb)
    kt = _mm_nt(wt_ref[_T_MK:_T_SK, :], xb) * (ML_DK ** -0.5)
    pt_ref[_T_MK:_T_SK, :] = kt
    skt = _mm_nt(wt_ref[_T_SK:_T_SV, :], xb)
    cost = cost_ref[...]
    sint = sint_ref[...]
    for g in range(SWA_KV):
        base = g * SWA_HD
        for off, val in _rope_rows(skt[base:base + SWA_HD, :], cost, sint):
            pt_ref[_T_SK + base + off:_T_SK + base + off + val.shape[0], :] = val
    pt_ref[_T_SV:_T_G, :] = _mm_nt(wt_ref[_T_SV:_T_G, :], xb)
    pt_ref[_T_G:_T_END, :] = _mm_nt(wt_ref[_T_G:_T_END, :], xb) + bias_ref[...]
    for c in range(BRANCH // LANES):
        p2_ref[:, _P2_MK + c * LANES:_P2_MK + (c + 1) * LANES] = kt[c * LANES:(c + 1) * LANES, :].T
    p2_ref[:, _P2_SK:_P2_END] = pt_ref[_T_SK:_T_SV, :].T


def _dproj(xs, w_all, wt_all, bias_all, cos, sin, cost, sint, layer):
    n = xs.shape[0]
    whole = lambda shape: pl.BlockSpec(shape, lambda i: (0, 0))
    return pl.pallas_call(
        _dproj_kernel,
        grid=(1,),
        in_specs=[_const_spec((n, D_MODEL)),
                  _layer_spec((D_MODEL, W_HALF), layer, 0, 1),
                  _layer_spec((_T_END, D_MODEL), layer),
                  _layer_spec((16, 1), layer),
                  _const_spec((n, LANES)), _const_spec((n, LANES)),
                  _const_spec((ROT_HALF, n)), _const_spec((ROT_HALF, n))],
        out_specs=[whole((n, _N_END)), whole((_T_END, n)), whole((n, _P2_END))],
        out_shape=[jax.ShapeDtypeStruct((n, _N_END), F32),
                   jax.ShapeDtypeStruct((_T_END, n), F32),
                   jax.ShapeDtypeStruct((n, _P2_END), F32)],
        compiler_params=_params(("arbitrary",)),
        name="decode_proj",
    )(xs, w_all, wt_all, bias_all, cos, sin, cost, sint)


def _dmlstm_stream(i, q_ref, v_ref, mo_ref, k_ref, vt_ref, gt_ref, c_ref, n_ref, m_ref, gain_ref,
                   y_ref, c_out_ref, n_out_ref, m_out_ref):
    bb = q_ref.shape[0]
    nlanes = gt_ref.shape[1]
    li = gt_ref[0:ML_HEADS, :]
    lf = _log_sigmoid(gt_ref[8:8 + ML_HEADS, :])
    m_prev = m_ref[...]
    m_t = jnp.maximum(lf + m_prev, li)
    m_out_ref[...] = m_t
    scal = jnp.concatenate([jnp.exp(li - m_t), jnp.exp(lf + m_prev - m_t), jnp.exp(-m_t), jnp.zeros_like(m_t)], axis=0)
    bring = jnp.where(i == 0, 0, nlanes - i * bb)
    scal = pltpu.roll(scal, bring, axis=1)
    tiles = [(h, j) for h in range(ML_HEADS) for j in range(bb)]
    hs = lambda h: slice(h * ML_DK, (h + 1) * ML_DK)
    nrow_of = lambda h, j: slice(j * ML_HEADS + h, j * ML_HEADS + h + 1)
    q = {(h, j): q_ref[j:j + 1, hs(h)] for h, j in tiles}
    k = {(h, j): k_ref[j:j + 1, hs(h)] for h, j in tiles}
    w = {(h, j): scal[h:h + 1, j:j + 1] for h, j in tiles}
    a = {(h, j): scal[ML_HEADS + h:ML_HEADS + h + 1, j:j + 1] for h, j in tiles}
    cq = {(h, j): _mm_nt(jnp.broadcast_to(q[h, j], (8, ML_DK)).astype(BF16), c_ref[j, h].astype(BF16))[0:1, :]
          for h, j in tiles}
    yield
    qk = {t: jnp.sum(q[t] * k[t], axis=1, keepdims=True) for t in tiles}
    nq = {(h, j): jnp.sum(n_ref[nrow_of(h, j), :] * q[h, j], axis=1, keepdims=True) for h, j in tiles}
    yield
    hrow = {}
    for h, j in tiles:
        sw = qk[h, j] * w[h, j]
        floor = scal[2 * ML_HEADS + h:2 * ML_HEADS + h + 1, j:j + 1]
        den = jnp.maximum(jnp.abs(sw + a[h, j] * nq[h, j]), floor)
        hrow[h, j] = (sw * v_ref[j:j + 1, hs(h)] + a[h, j] * cq[h, j]) / den
        n_out_ref[nrow_of(h, j), :] = a[h, j] * n_ref[nrow_of(h, j), :] + w[h, j] * k[h, j]
    yield
    for h in range(ML_HEADS):
        vt = pltpu.roll(vt_ref[hs(h), :], bring, axis=1)
        for j in range(bb):
            c_out_ref[j, h] = a[h, j] * c_ref[j, h] + (w[h, j] * vt[:, j:j + 1]) * k[h, j]
        yield
    hc = {t: hrow[t] - jnp.mean(hrow[t], axis=1, keepdims=True) for t in tiles}
    yield
    var = {t: jnp.mean(hc[t] * hc[t], axis=1, keepdims=True) for t in tiles}
    yield
    for h, j in tiles:
        hn = hc[h, j] * lax.rsqrt(var[h, j] + HEAD_NORM_EPS) * gain_ref[:, hs(h)]
        y_ref[j:j + 1, hs(h)] = (_sigmoid(mo_ref[j:j + 1, hs(h)]) * hn).astype(BF16)


def _dswa_stream(i, q_ref, kn_ref, vn_ref, kvt_ref, ck_ref, cv_ref, sink_ref, y_ref, ko_ref, vo_ref):
    bb = q_ref.shape[0]
    lane = lax.broadcasted_iota(jnp.int32, (LANES, LANES), 1)
    row8 = lax.broadcasted_iota(jnp.int32, (SWA_HEADS, LANES), 0)
    low8 = lax.broadcasted_iota(jnp.int32, (SWA_HEADS, LANES), 1) < SWA_HD
    low1 = lax.broadcasted_iota(jnp.int32, (1, LANES), 1) < SWA_HD
    scale = SWA_HD ** -0.5
    sink = sink_ref[:, 0:1]
    qm, s = [], []
    for j in range(bb):
        rows = jnp.zeros((SWA_HEADS, LANES), F32)
        for pp in range(SWA_HEADS // 2):
            g = pp // 2
            pair = q_ref[j:j + 1, pp * LANES:(pp + 1) * LANES]
            swap = pltpu.roll(pair, SWA_HD, axis=1)
            in_g = low8 if g == 0 else jnp.logical_not(low8)
            for t in range(2):
                rows = jnp.where((row8 == 2 * pp + t) & in_g, pair if t == g else swap, rows)
        qm.append(rows)
        s.append(_mm(rows.astype(BF16), ck_ref[j].astype(BF16)) * scale)
    yield
    e, e_new, den = [], [], []
    for j in range(bb):
        s_new = jnp.sum(qm[j] * kn_ref[j:j + 1, :], axis=1, keepdims=True) * scale
        mx = jnp.maximum(jnp.maximum(jnp.max(s[j], axis=1, keepdims=True), s_new), sink)
        e.append(jnp.exp(s[j] - mx))
        e_new.append(jnp.exp(s_new - mx))
        den.append(jnp.sum(e[j], axis=1, keepdims=True) + e_new[j] + jnp.exp(sink - mx))
    yield
    for j in range(bb):
        o = (_mm_nt(e[j].astype(BF16), cv_ref[j].astype(BF16)) + e_new[j] * vn_ref[j:j + 1, :]) / den[j]
        for pp in range(SWA_HEADS // 2):
            g = pp // 2
            halves = []
            for t in range(2):
                oh = o[2 * pp + t:2 * pp + t + 1, :]
                halves.append(oh if t == g else pltpu.roll(oh, SWA_HD, axis=1))
            y_ref[j:j + 1, pp * LANES:(pp + 1) * LANES] = jnp.where(low1, halves[0], halves[1]).astype(BF16)
    yield
    knew_t = kvt_ref[0:LANES, :]
    vnew_t = kvt_ref[LANES:, :]
    for j in range(bb):
        bring = LANES - 1 - (i * bb + j)
        ko_ref[j] = jnp.where(lane == LANES - 1, pltpu.roll(knew_t, bring, axis=1),
                              pltpu.roll(ck_ref[j], LANES - 1, axis=1))
        vo_ref[j] = jnp.where(lane == LANES - 1, pltpu.roll(vnew_t, bring, axis=1),
                              pltpu.roll(cv_ref[j], LANES - 1, axis=1))
        if j % 2 == 1:
            yield


def _dcross_stream(q_ref, k_ref, v_ref, y_ref):
    bb = q_ref.shape[0]
    scale = X_HD ** -0.5
    row8 = lax.broadcasted_iota(jnp.int32, (8, LANES), 0) % X_HEADS
    ones = jnp.ones((X_HD, LANES), BF16)
    tiles, rows = k_ref.shape[1], k_ref.shape[1] * k_ref.shape[2]
    s = []
    for j in range(bb):
        qrep = jnp.zeros((8, LANES), F32)
        for h in range(X_HEADS):
            qrep = jnp.where(row8 == h, q_ref[j:j + 1, h * X_HD:(h + 1) * X_HD], qrep)
        prod = (k_ref[j] * qrep[None]).astype(BF16).reshape(rows, X_HD)
        s.append(_mm(prod, ones).reshape(tiles, 8, LANES) * scale)
        if j % 2 == 1:
            yield
    for j in range(bb):
        mx8 = jnp.max(s[j], axis=0)
        mx4 = jnp.maximum(mx8[0:X_HEADS], mx8[X_HEADS:])
        e = jnp.exp(s[j] - jnp.concatenate([mx4, mx4], axis=0)[None])
        den8 = jnp.sum(e, axis=0)
        o8 = jnp.sum(e * v_ref[j], axis=0)
        o4 = (o8[0:X_HEADS] + o8[X_HEADS:]) / (den8[0:X_HEADS] + den8[X_HEADS:])
        for h in range(X_HEADS):
            y_ref[j:j + 1, h * X_HD:(h + 1) * X_HD] = o4[h:h + 1, :].astype(BF16)
        if j % 2 == 1:
            yield


_N_DEC_IN = 20


def _decode_mixers_kernel(*refs):
    ins, outs = refs[:_N_DEC_IN], refs[-8:]
    (q_ref, v_ref, mo_ref, k_ref, vt_ref, gt_ref, c_ref, n_ref, m_ref, gain_ref,
     sq_ref, kn_ref, vn_ref, kvt_ref, ck_ref, cv_ref, sink_ref, xq_ref, mk_ref, mv_ref) = ins
    ya_ref, c_out_ref, n_out_ref, m_out_ref, yb_ref, ko_ref, vo_ref, yc_ref = outs
    i = pl.program_id(0)
    streams = [
        _dcross_stream(xq_ref, mk_ref, mv_ref, yc_ref),
        _dmlstm_stream(i, q_ref, v_ref, mo_ref, k_ref, vt_ref, gt_ref, c_ref, n_ref, m_ref, gain_ref,
                       ya_ref, c_out_ref, n_out_ref, m_out_ref),
        _dswa_stream(i, sq_ref, kn_ref, vn_ref, kvt_ref, ck_ref, cv_ref, sink_ref, yb_ref, ko_ref, vo_ref),
    ]
    while streams:
        for g in list(streams):
            if next(g, StopIteration) is StopIteration:
                streams.remove(g)


def _decode_mixers(p, p2, pt, c_all, n_all, mt_all, gain_all, ck_all, cv_all, sinks_all, mk_all, mv_all,
                   layer, stacks):
    n = p.shape[0]
    pblk = lambda width, col: pl.BlockSpec((BB, width), lambda i: (i, col // width))
    fixed = lambda rows, row0: pl.BlockSpec((rows, n), lambda i: (row0 // rows, 0), pipeline_mode=pl.Buffered(1))
    c_spec = pl.BlockSpec((None, BB, ML_HEADS, ML_DV, ML_DK), lambda i: (layer, i, 0, 0, 0))
    win = pl.BlockSpec((None, BB, LANES, WINDOW), lambda i: (layer, i, 0, 0))
    mem = pl.BlockSpec((None, BB, MEM_TOKENS * X_HEADS // 8, 8, X_HD), lambda i: (layer, i, 0, 0, 0))
    in_specs = [pblk(BRANCH, _N_MQ), pblk(BRANCH, _N_MV), pblk(BRANCH, _N_MO), pblk(BRANCH, _P2_MK),
                fixed(BRANCH, _T_MV), fixed(16, _T_G), c_spec,
                pl.BlockSpec((None, BB * ML_HEADS, ML_DK), lambda i: (layer, i, 0)),
                _layer_spec((ML_HEADS, n), layer), _layer_spec((1, BRANCH), layer),
                pblk(BRANCH, _N_SQ), pblk(LANES, _P2_SK), pblk(LANES, _N_SV), fixed(2 * LANES, _T_SK),
                win, win, _layer_spec((8, LANES), layer),
                pblk(BRANCH, _N_XQ), mem, mem]
    args = [p, p, p, p2, pt, pt, c_all, n_all, mt_all, gain_all,
            p, p2, p, pt, ck_all, cv_all, sinks_all, p, mk_all, mv_all]
    assert len(args) == _N_DEC_IN
    aliases = {}
    if stacks is not None:
        in_specs += [pl.BlockSpec(memory_space=pl.ANY)] * 3
        args += list(stacks)
        aliases = {_N_DEC_IN: 1, _N_DEC_IN + 1: 5, _N_DEC_IN + 2: 6}
    row = lambda width: pl.BlockSpec((BB, width), lambda i: (i, 0))
    return pl.pallas_call(
        _decode_mixers_kernel,
        grid=(n // BB,),
        in_specs=in_specs,
        out_specs=[row(BRANCH), c_spec,
                   pl.BlockSpec((BB * ML_HEADS, ML_DK), lambda i: (i, 0)),
                   pl.BlockSpec((ML_HEADS, n), lambda i: (0, 0)),
                   row(BRANCH), win, win, row(BRANCH)],
        out_shape=[jax.ShapeDtypeStruct((n, BRANCH), BF16),
                   jax.ShapeDtypeStruct((DEPTH, n, ML_HEADS, ML_DV, ML_DK), F32),
                   jax.ShapeDtypeStruct((n * ML_HEADS, ML_DK), F32),
                   jax.ShapeDtypeStruct((ML_HEADS, n), F32),
                   jax.ShapeDtypeStruct((n, BRANCH), BF16),
                   jax.ShapeDtypeStruct((DEPTH, n, LANES, WINDOW), F32),
                   jax.ShapeDtypeStruct((DEPTH, n, LANES, WINDOW), F32),
                   jax.ShapeDtypeStruct((n, BRANCH), BF16)],
        input_output_aliases=aliases,
        compiler_params=_params(("arbitrary",)),
        name="decode_mixers",
    )(*args)


def _rope_tables(positions):
    inv_freq = ROPE_THETA ** (-jnp.arange(ROT_HALF, dtype=F32) / ROT_HALF)
    ang = positions.astype(F32)[:, None] * inv_freq[None, :]
    cos = jnp.cos(ang)
    sin = jnp.sin(ang)
    reps = LANES // ROT_HALF
    return jnp.tile(cos, (1, reps)), jnp.tile(sin, (1, reps)), cos.T, sin.T


def kernel(x_prompt, x_sample, mem_prompt, cache_swa_k, cache_swa_v, cache_mem_k, cache_mem_v, state_mlstm_c, state_mlstm_n, state_mlstm_m, w_in, b_gates, mlstm_norm_g, swa_sinks, w_mem_kv, w_branch, w_mix_out, ln1_g, ln1_b, w_ffn_in, w_ffn_out, ln2_g, ln2_b):
    nb, seq, _ = x_prompt.shape
    ns = x_sample.shape[0]
    assert ns == LANES and PAST_LEN >= WINDOW

    cosn, sinn, cost, sint = _rope_tables(jnp.arange(seq))
    cos_s, sin_s, cost_s, sint_s = _rope_tables(jnp.full((ns,), PAST_LEN))

    ck_all = jnp.transpose(cache_swa_k, (0, 1, 3, 4, 2)).reshape(DEPTH, ns, SWA_KV * SWA_HD, WINDOW)
    cv_all = jnp.transpose(cache_swa_v, (0, 1, 3, 4, 2)).reshape(DEPTH, ns, SWA_KV * SWA_HD, WINDOW)
    mk_all = cache_mem_k.reshape(DEPTH, ns, MEM_TOKENS * X_HEADS // 8, 8, X_HD)
    mv_all = cache_mem_v.reshape(DEPTH, ns, MEM_TOKENS * X_HEADS // 8, 8, X_HD)
    n_all = state_mlstm_n.reshape(DEPTH, ns * ML_HEADS, ML_DK)
    mt_all = jnp.transpose(state_mlstm_m, (0, 2, 1))

    zcols = lambda n: jnp.zeros((DEPTH, D_MODEL, n), F32)
    w_all = jnp.concatenate([w_in[..., _C_GL:], w_in[..., _C_MQ:_C_MK], w_in[..., _C_MV:_C_MI],
                             w_in[..., _C_SQ:_C_SK], w_in[..., _C_XQ:_C_GL], w_in[..., _C_SV:_C_XQ],
                             zcols(W_HALF - _N_END)], axis=-1).astype(BF16)
    wt_all = jnp.swapaxes(jnp.concatenate([w_in[..., _C_MV:_C_MO], w_in[..., _C_MK:_C_MV], w_in[..., _C_SK:_C_XQ],
                                           w_in[..., _C_MI:_C_MF], zcols(4), w_in[..., _C_MF:_C_SQ], zcols(4)],
                                          axis=-1), 1, 2).astype(BF16)
    z4 = jnp.zeros((DEPTH, 4), F32)
    bias_all = jnp.concatenate([b_gates[:, :ML_HEADS], z4, b_gates[:, ML_HEADS:], z4], axis=1)[..., None]
    gain_all = mlstm_norm_g[:, None, :]
    sinks_all = jnp.broadcast_to(swa_sinks[:, :, None], (DEPTH, SWA_HEADS, LANES))
    wkv_all = w_mem_kv.astype(BF16)
    wkt_all = jnp.swapaxes(w_mem_kv[..., :BRANCH], 1, 2).astype(BF16)
    wbr_all = w_branch.astype(BF16)
    wmix_all = w_mix_out.astype(BF16)
    wfi_all = w_ffn_in.astype(BF16)
    wfo_all = w_ffn_out.astype(BF16)
    ln_all = (ln1_g[:, None, :], ln1_b[:, None, :], ln2_g[:, None, :], ln2_b[:, None, :])

    yp = x_prompt.reshape(nb * seq, D_MODEL)
    ys = x_sample.reshape(ns, D_MODEL)
    mem2d = mem_prompt.reshape(nb * MEM_TOKENS, D_MODEL)

    outs = {k: [] for k in ("kp", "vp", "mk", "mv", "cp", "np", "mp", "ns", "ms")}
    c_stack = k_stack = v_stack = None
    for l in range(DEPTH):
        mk32, mv32, mkt, mv16 = _memkv(mem2d, wkv_all, wkt_all, l, nb)
        ya, yb, xq, k32, v32, m_fin, s_fin = _mixer(yp, w_all, wt_all, bias_all, cosn, sinn, cost, sint,
                                                    gain_all, sinks_all, l, nb, seq)
        yp = _merge_ffn(yp, ya, yb, xq, w_all, wbr_all, wmix_all, wfi_all, wfo_all, *ln_all, l, TM_MERGE,
                        memory=(mkt, mv16, seq))
        outs["kp"].append(jnp.transpose(k32.reshape(nb, SWA_KV, SWA_HD, WINDOW), (0, 3, 1, 2)))
        outs["vp"].append(jnp.transpose(v32.reshape(nb, SWA_KV, SWA_HD, WINDOW), (0, 3, 1, 2)))
        outs["mk"].append(mk32.reshape(nb, MEM_TOKENS, X_HEADS, X_HD))
        outs["mv"].append(mv32.reshape(nb, MEM_TOKENS, X_HEADS, X_HD))
        outs["cp"].append(jnp.swapaxes(s_fin[..., :ML_DV], -1, -2))
        outs["np"].append(s_fin[..., ML_DV])
        outs["mp"].append(m_fin.reshape(nb, 8, LANES)[:, :ML_HEADS, 0])

        p, pt, p2 = _dproj(ys, w_all, wt_all, bias_all, cos_s, sin_s, cost_s, sint_s, l)
        ya_s, c_stack, n_new, m_new, yb_s, k_stack, v_stack, yc_s = _decode_mixers(
            p, p2, pt, state_mlstm_c, n_all, mt_all, gain_all, ck_all, cv_all, sinks_all, mk_all, mv_all, l,
            None if l == 0 else (c_stack, k_stack, v_stack))
        ys = _merge_ffn(ys, ya_s, yb_s, yc_s, w_all, wbr_all, wmix_all, wfi_all, wfo_all, *ln_all, l, ns)
        outs["ns"].append(n_new.reshape(ns, ML_HEADS, ML_DK))
        outs["ms"].append(m_new)

    st = {k: jnp.stack(vals) for k, vals in outs.items()}
    window_out = lambda t: jnp.transpose(t.reshape(DEPTH, ns, SWA_KV, SWA_HD, WINDOW), (0, 1, 4, 2, 3))
    return (yp.reshape(nb, seq, D_MODEL), ys.reshape(ns, 1, D_MODEL),
            st["kp"], st["vp"], window_out(k_stack), window_out(v_stack), st["mk"], st["mv"],
            st["cp"], st["np"], st["mp"], c_stack, st["ns"], jnp.transpose(st["ms"], (0, 2, 1)))
```

```python
import functools

import jax
import jax.numpy as jnp
from jax import lax
from jax.experimental import pallas as pl
from jax.experimental.pallas import tpu as pltpu

F32 = jnp.float32
BF16 = jnp.bfloat16

D_MODEL = 1024
DEPTH = 2
BRANCH = 512
ML_HEADS = 4
ML_DK = 128
ML_DV = 128
ML_CHUNK = 128
SWA_HD = 64
SWA_HEADS = 8
SWA_KV = 2
SWA_GROUP = 4
WINDOW = 128
ROT_DIM = 16
ROT_HALF = 8
ROPE_THETA = 500000.0
MEM_TOKENS = 256
X_HEADS = 4
X_HD = 128
D_FF = 2816
LN_EPS = 1e-5
HEAD_NORM_EPS = 1e-6
DEEPNORM_ALPHA = (2 * DEPTH) ** 0.25
NEG_INF = -1e30
PAST_LEN = 8192

LANES = 128
VMEM_LIMIT = 56 * 1024 * 1024

_C_MQ, _C_MK, _C_MV, _C_MO = 0, 512, 1024, 1536
_C_MI, _C_MF = 2048, 2052
_C_SQ, _C_SK, _C_SV, _C_XQ, _C_GL = 2056, 2568, 2696, 2824, 3336

W_HALF = 3 * D_MODEL
_N_MQ, _N_MV, _N_MO, _N_SQ, _N_XQ, _N_SV, _N_END = 0, 512, 1024, 1536, 2048, 2560, 2688
_T_MV, _T_MK, _T_SK, _T_SV, _T_G, _T_END = 0, 512, 1024, 1152, 1280, 1296

TM_PROJ = 512
ML_GROUP = 4
TM_MERGE = 256
TQ_CROSS = 512
BB = 8


def _mm(a, b):
    return jnp.dot(a, b, preferred_element_type=F32)


def _mm_nt(a, b):
    return lax.dot_general(a, b, (((1,), (1,)), ((), ())), preferred_element_type=F32)


def _sigmoid(x):
    return 1.0 / (1.0 + jnp.exp(-x))


def _log_sigmoid(x):
    return jnp.minimum(x, 0.0) - jnp.log(1.0 + jnp.exp(-jnp.abs(x)))


def _layer_norm(x, g, b):
    mu = jnp.mean(x, axis=-1, keepdims=True)
    xc = x - mu
    var = jnp.mean(xc * xc, axis=-1, keepdims=True)
    return xc * lax.rsqrt(var + LN_EPS) * g + b


def _rope_lanes(x, cos, sin):
    lane = lax.broadcasted_iota(jnp.int32, x.shape, 1) % SWA_HD
    up = pltpu.roll(x, LANES - ROT_HALF, axis=1)
    dn = pltpu.roll(x, ROT_HALF, axis=1)
    first = x * cos - up * sin
    second = x * cos + dn * sin
    return jnp.where(lane < ROT_HALF, first, jnp.where(lane < ROT_DIM, second, x))


def _rope_rows(xt, cost, sint):
    x1 = xt[0:ROT_HALF, :]
    x2 = xt[ROT_HALF:ROT_DIM, :]
    return ((0, x1 * cost - x2 * sint), (ROT_HALF, x2 * cost + x1 * sint), (ROT_DIM, xt[ROT_DIM:SWA_HD, :]))


def _const_spec(shape):
    nd = len(shape)
    return pl.BlockSpec(shape, lambda *_: (0,) * nd, pipeline_mode=pl.Buffered(1))


def _layer_spec(shape, layer, *tail):
    idx = (layer,) + (tail if tail else (0,) * len(shape))
    return pl.BlockSpec((None,) + tuple(shape), lambda *_: idx, pipeline_mode=pl.Buffered(1))


def _params(sem):
    return pltpu.CompilerParams(dimension_semantics=sem, vmem_limit_bytes=VMEM_LIMIT)


TR_RELAYOUT = 256


def _relayout_kernel(w_ref, wn_ref, wt_ref):
    w = w_ref[...]
    rows = w.shape[0]
    col = lambda lo, hi: w[:, lo:hi]
    zeros = lambda n: jnp.zeros((rows, n), F32)
    wn_ref[...] = jnp.concatenate([col(_C_GL, _C_GL + W_HALF), col(_C_MQ, _C_MK), col(_C_MV, _C_MI),
                                   col(_C_SQ, _C_SK), col(_C_XQ, _C_GL), col(_C_SV, _C_XQ),
                                   zeros(W_HALF - _N_END)], axis=1).astype(BF16)
    gates = jnp.concatenate([col(_C_MI, _C_MF), zeros(4), col(_C_MF, _C_SQ), zeros(LANES - 12)], axis=1)
    for lo, hi, dst in ((_C_MV, _C_MO, _T_MV), (_C_MK, _C_MV, _T_MK), (_C_SK, _C_XQ, _T_SK)):
        for c in range((hi - lo) // LANES):
            blk = col(lo + c * LANES, lo + (c + 1) * LANES)
            wt_ref[dst + c * LANES:dst + (c + 1) * LANES, :] = blk.T.astype(BF16)
    wt_ref[_T_G:_T_END, :] = gates.T[0:_T_END - _T_G, :].astype(BF16)


def _relayout_w_in(w_in):
    depth, rows, cols = w_in.shape
    return pl.pallas_call(
        _relayout_kernel,
        grid=(depth, rows // TR_RELAYOUT),
        in_specs=[pl.BlockSpec((None, TR_RELAYOUT, cols), lambda l, i: (l, i, 0))],
        out_specs=[pl.BlockSpec((None, TR_RELAYOUT, 2 * W_HALF), lambda l, i: (l, i, 0)),
                   pl.BlockSpec((None, _T_END, TR_RELAYOUT), lambda l, i: (l, 0, i))],
        out_shape=[jax.ShapeDtypeStruct((depth, rows, 2 * W_HALF), BF16),
                   jax.ShapeDtypeStruct((depth, _T_END, rows), BF16)],
        compiler_params=_params(("arbitrary", "arbitrary")),
        name="relayout_w_in",
    )(w_in)


def _memkv_kernel(mem_ref, wkv_ref, wkt_ref, k32_ref, v32_ref, kt_ref, v16_ref):
    m = mem_ref[...].astype(BF16)
    kv = _mm(m, wkv_ref[...])
    k32_ref[...] = kv[:, :BRANCH]
    v32_ref[...] = kv[:, BRANCH:]
    v16_ref[...] = kv[:, BRANCH:].astype(BF16)
    kt_ref[...] = _mm_nt(wkt_ref[...], m).astype(BF16)


def _memkv(mem2d, wkv, wkt, layer, nb):
    rows = mem2d.shape[0]
    return pl.pallas_call(
        _memkv_kernel,
        grid=(nb,),
        in_specs=[pl.BlockSpec((MEM_TOKENS, D_MODEL), lambda b: (b, 0)),
                  _layer_spec((D_MODEL, 2 * BRANCH), layer),
                  _layer_spec((BRANCH, D_MODEL), layer)],
        out_specs=[pl.BlockSpec((MEM_TOKENS, BRANCH), lambda b: (b, 0)),
                   pl.BlockSpec((MEM_TOKENS, BRANCH), lambda b: (b, 0)),
                   pl.BlockSpec((None, BRANCH, MEM_TOKENS), lambda b: (b, 0, 0)),
                   pl.BlockSpec((MEM_TOKENS, BRANCH), lambda b: (b, 0))],
        out_shape=[jax.ShapeDtypeStruct((rows, BRANCH), F32),
                   jax.ShapeDtypeStruct((rows, BRANCH), F32),
                   jax.ShapeDtypeStruct((nb, BRANCH, MEM_TOKENS), BF16),
                   jax.ShapeDtypeStruct((rows, BRANCH), BF16)],
        compiler_params=_params(("arbitrary",)),
        name="memkv_proj",
    )(mem2d, wkv, wkt)


def _mlstm_gate_weights(pre, gt_ref, gc_ref, m_out_ref, m_scr, valid):
    tm = pre.shape[1]
    li = pre[0:8, :]
    lane8 = lax.broadcasted_iota(jnp.int32, li.shape, 1) % ML_CHUNK
    b = _log_sigmoid(pre[8:16, :])
    shift = 1
    while shift < ML_CHUNK:
        b = b + jnp.where(lane8 >= shift, pltpu.roll(b, shift, axis=1), 0.0)
        shift *= 2
    g = li - b
    cm = g
    shift = 1
    while shift < ML_CHUNK:
        cm = jnp.maximum(cm, jnp.where(lane8 >= shift, pltpu.roll(cm, shift, axis=1), -jnp.inf))
        shift *= 2
    gt_ref[0:8, :] = g
    pad = jnp.zeros((LANES - 24, ML_CHUNK), F32)
    m_start = m_scr[...]
    m_prev = m_start
    for c in range(tm // ML_CHUNK):
        cs = slice(c * ML_CHUNK, (c + 1) * ML_CHUNK)
        b_c = b[:, cs]
        b_last = jnp.broadcast_to(b_c[:, ML_CHUNK - 1:ML_CHUNK], b_c.shape)
        cm_last = jnp.broadcast_to(cm[:, cs][:, ML_CHUNK - 1:ML_CHUNK], b_c.shape)
        m_t = b_c + jnp.maximum(m_prev, cm[:, cs])
        m_new = b_last + jnp.maximum(m_prev, cm_last)
        gt_ref[8:16, cs] = jnp.exp(b_last + g[:, cs] - m_new)
        gt_ref[16:24, cs] = jnp.exp(b_last + m_prev - m_new)
        rows = jnp.concatenate([b_c - m_t, jnp.exp(b_c + m_prev - m_t), jnp.exp(-m_t), pad], axis=0)
        gc_ref[cs, :] = rows.T
        m_prev = m_new
    m_prev = jnp.where(valid, m_prev, m_start)
    m_scr[...] = m_prev
    m_out_ref[...] = m_prev


def _proj_stream(x_ref, wn_ref, wt_ref, bias_ref, cosn_ref, sinn_ref, cost_ref, sint_ref, dst,
                 xq_ref, k32_ref, v32_ref, m_out_ref, m_scr, valid):
    tm = x_ref.shape[0]
    xb = x_ref[...].astype(BF16)
    tr_rows = lambda lo, hi: _mm_nt(wt_ref[lo:hi, :], xb)
    _mlstm_gate_weights(tr_rows(_T_G, _T_END) + bias_ref[...], dst["gt"], dst["gc"], m_out_ref, m_scr, valid)
    yield
    dst["q"][...] = _mm(xb, wn_ref[:, _N_MQ:_N_MV]).astype(BF16)
    yield
    dst["v"][...] = _mm(xb, wn_ref[:, _N_MV:_N_MO]).astype(BF16)
    yield
    dst["mo"][...] = _mm(xb, wn_ref[:, _N_MO:_N_SQ])
    yield
    xq_ref[...] = _mm(xb, wn_ref[:, _N_XQ:_N_SV]).astype(BF16)
    yield
    cosn = cosn_ref[...]
    sinn = sinn_ref[...]
    sq = _mm(xb, wn_ref[:, _N_SQ:_N_XQ])
    for c in range(BRANCH // LANES):
        blk = _rope_lanes(sq[:, c * LANES:(c + 1) * LANES], cosn, sinn)
        dst["sq"][:, c * LANES:(c + 1) * LANES] = blk.astype(BF16)
    yield
    dst["sv"][...] = _mm(xb, wn_ref[:, _N_SV:_N_END]).astype(BF16)
    dst["kt"][...] = (tr_rows(_T_MK, _T_SK) * (ML_DK ** -0.5)).astype(BF16)
    yield
    skt = tr_rows(_T_SK, _T_SV)
    svt = tr_rows(_T_SV, _T_G)
    cost = cost_ref[...]
    sint = sint_ref[...]
    tail = slice(tm - WINDOW, tm)
    v32_ref[...] = svt[:, tail]
    for g in range(SWA_KV):
        base = g * SWA_HD
        for off, val in _rope_rows(skt[base:base + SWA_HD, :], cost, sint):
            dst["skt"][base + off:base + off + val.shape[0], :] = val.astype(BF16)
            k32_ref[base + off:base + off + val.shape[0], :] = val[:, tail]


def _mlstm_stream(q_ref, v_ref, kt_ref, mo_ref, gt_ref, gc_ref, gain_ref, y_ref, s_ref):
    L = ML_CHUNK
    r_i = lax.broadcasted_iota(jnp.int32, (L, L), 0)
    c_i = lax.broadcasted_iota(jnp.int32, (L, L), 1)
    causal = c_i <= r_i
    ones = jnp.ones((L, ML_DV), BF16)

    ts = lambda c: slice(c * L, (c + 1) * L)
    hs = lambda h: slice(h * ML_DK, (h + 1) * ML_DK)
    n_chunks = q_ref.shape[0] // L

    for c0 in range(0, n_chunks, ML_GROUP):
        chunks = range(c0, min(c0 + ML_GROUP, n_chunks))
        units = [(c, h) for c in chunks for h in range(ML_HEADS)]

        qk = {(c, h): _mm(q_ref[ts(c), hs(h)], kt_ref[hs(h), ts(c)]) for c, h in units}
        yield

        sw, kts, vext = {}, {}, {}
        for c, h in units:
            g_r = gt_ref[h:h + 1, ts(c)]
            es_r = gt_ref[8 + h:9 + h, ts(c)]
            u_c = gc_ref[ts(c), h:h + 1]
            sw[c, h] = (qk[c, h] * jnp.exp(jnp.where(causal, u_c + g_r, -jnp.inf))).astype(BF16)
            kts[c, h] = (kt_ref[hs(h), ts(c)].astype(F32) * es_r).astype(BF16)
            vext[c, h] = jnp.concatenate([v_ref[ts(c), hs(h)], ones], axis=1)
        yield

        intra = {u: _mm(sw[u], vext[u]) for u in units}
        delta = {u: _mm(kts[u], vext[u]) for u in units}
        yield

        s_in = {}
        for h in range(ML_HEADS):
            state = s_ref[h]
            for c in chunks:
                s_in[c, h] = state.astype(BF16)
                state = gt_ref[16 + h:17 + h, c * L:c * L + 1] * state + delta[c, h]
            s_ref[h] = state
        qs_all = {u: _mm(q_ref[ts(u[0]), hs(u[1])], s_in[u]) for u in units}
        yield

        hh, hc = {}, {}
        for c, h in units:
            tot = intra[c, h] + gc_ref[ts(c), 8 + h:9 + h] * qs_all[c, h]
            floor = gc_ref[ts(c), 16 + h:17 + h]
            hh[c, h] = tot[:, :ML_DV] * (1.0 / jnp.maximum(jnp.abs(tot[:, ML_DV:]), floor))
        for u in units:
            hc[u] = hh[u] - jnp.mean(hh[u], axis=1, keepdims=True)
        for c, h in units:
            var = jnp.mean(hc[c, h] * hc[c, h], axis=1, keepdims=True)
            hn = hc[c, h] * lax.rsqrt(var + HEAD_NORM_EPS) * gain_ref[:, hs(h)]
            y_ref[ts(c), hs(h)] = (_sigmoid(mo_ref[ts(c), hs(h)]) * hn).astype(BF16)
        yield


def _swa_stream(q_ref, ktp_ref, ktc_ref, vp_ref, vc_ref, sink_ref, y_ref, seq_start):
    L = WINDOW
    nblk = q_ref.shape[0] // L
    r_i = lax.broadcasted_iota(jnp.int32, (L, 2 * L), 0)
    c_i = lax.broadcasted_iota(jnp.int32, (L, 2 * L), 1)
    band = (c_i >= r_i) & (c_i <= r_i + L)
    first = band & (c_i >= jnp.where(seq_start, L, 0))
    low_half = lax.broadcasted_iota(jnp.int32, (2 * L, LANES), 1) < SWA_HD
    out_low = lax.broadcasted_iota(jnp.int32, (L, LANES), 1) < SWA_HD
    zeros_k = jnp.zeros((SWA_HD, 2 * L), BF16)
    ones_lo = jnp.where(low_half, 1.0, 0.0).astype(BF16)
    ones_hi = jnp.where(low_half, 0.0, 1.0).astype(BF16)

    kt_all = jnp.concatenate([ktp_ref[...], ktc_ref[...]], axis=1)
    v_all = jnp.concatenate([vp_ref[...], vc_ref[...]], axis=0).astype(F32)
    v_swap = pltpu.roll(v_all, SWA_HD, axis=1)

    def scores(c):
        win = slice(c * L, (c + 2) * L)
        out = []
        for g in range(SWA_KV):
            kt2 = kt_all[g * SWA_HD:(g + 1) * SWA_HD, win]
            kblk = jnp.concatenate([jnp.concatenate([kt2, zeros_k], axis=0),
                                    jnp.concatenate([zeros_k, kt2], axis=0)], axis=1)
            for pp in range(2 * g, 2 * g + 2):
                out.append(_mm(q_ref[c * L:(c + 1) * L, pp * LANES:(pp + 1) * LANES], kblk))
        return out

    def weights(c, s_list):
        allowed = first if c == 0 else band
        out = []
        for head in range(SWA_HEADS):
            s = s_list[head // 2][:, (head % 2) * 2 * L:(head % 2 + 1) * 2 * L]
            sc = jnp.where(allowed, s * (SWA_HD ** -0.5), NEG_INF)
            sink = sink_ref[head:head + 1, 0:1]
            mx = jnp.broadcast_to(jnp.maximum(jnp.max(sc, axis=1, keepdims=True), sink), sc.shape)
            out.append((jnp.exp(sc - mx).astype(BF16), jnp.exp(sink - mx[:, :LANES])))
        return out

    def outputs(c, e_list):
        win = slice(c * L, (c + 2) * L)
        v2 = v_all[win, :]
        v2s = v_swap[win, :]
        for g in range(SWA_KV):
            va = jnp.where(low_half, v2 if g == 0 else v2s, 0.0).astype(BF16)
            vb = jnp.where(low_half, 0.0, v2s if g == 0 else v2).astype(BF16)
            vden = jnp.concatenate([jnp.concatenate([va, ones_lo], axis=1),
                                    jnp.concatenate([vb, ones_hi], axis=1)], axis=0)
            for pp in range(2 * g, 2 * g + 2):
                (e0, k0), (e1, k1) = e_list[2 * pp], e_list[2 * pp + 1]
                res = _mm(jnp.concatenate([e0, e1], axis=1), vden)
                den = res[:, LANES:] + jnp.where(out_low, k0, k1)
                y_ref[c * L:(c + 1) * L, pp * LANES:(pp + 1) * LANES] = (res[:, :LANES] * (1.0 / den)).astype(BF16)

    s_next = scores(0)
    yield
    for c in range(nblk):
        s_cur = s_next
        if c + 1 < nblk:
            s_next = scores(c + 1)
        outputs(c, weights(c, s_cur))
        yield


_SLOT_BUFFERS = (("q", (TM_PROJ, BRANCH), BF16), ("v", (TM_PROJ, BRANCH), BF16), ("mo", (TM_PROJ, BRANCH), F32),
                 ("sq", (TM_PROJ, BRANCH), BF16), ("sv", (TM_PROJ, LANES), BF16), ("kt", (BRANCH, TM_PROJ), BF16),
                 ("skt", (LANES, TM_PROJ), BF16), ("gt", (24, TM_PROJ), F32), ("gc", (TM_PROJ, LANES), F32))


def _mixer_kernel(x_ref, wn_ref, wt_ref, bias_ref, cosn_ref, sinn_ref, cost_ref, sint_ref, gain_ref, sink_ref,
                  ya_ref, yb_ref, xq_ref, k32_ref, v32_ref, m_out_ref, s_out_ref, *scratch, n_tiles, tiles_per_seq):
    slots = {name: ref for (name, _, _), ref in zip(_SLOT_BUFFERS, scratch)}
    s_ref, m_scr, ktp_ref, vp_ref = scratch[len(_SLOT_BUFFERS):]
    i = pl.program_id(0)
    wr = i % 2
    mix_tile = i - 1
    valid = i < n_tiles

    @pl.when(i == 0)
    def _():
        for ref in scratch:
            ref[...] = jnp.zeros_like(ref)

    @pl.when((i % tiles_per_seq == 0) & valid)
    def _():
        m_scr[...] = jnp.zeros_like(m_scr)

    seq_start = mix_tile % tiles_per_seq == 0

    @pl.when(seq_start)
    def _():
        s_ref[...] = jnp.zeros_like(s_ref)

    dst = {name: ref.at[wr] for name, ref in slots.items()}
    src = {name: ref.at[1 - wr] for name, ref in slots.items()}
    streams = [
        _proj_stream(x_ref, wn_ref, wt_ref, bias_ref, cosn_ref, sinn_ref, cost_ref, sint_ref, dst,
                     xq_ref, k32_ref, v32_ref, m_out_ref, m_scr, valid),
        _mlstm_stream(src["q"], src["v"], src["kt"], src["mo"], src["gt"], src["gc"], gain_ref, ya_ref, s_ref),
        _swa_stream(src["sq"], ktp_ref, src["skt"], vp_ref, src["sv"], sink_ref, yb_ref, seq_start),
    ]
    while streams:
        for g in list(streams):
            if next(g, StopIteration) is StopIteration:
                streams.remove(g)

    tm = x_ref.shape[0]
    ktp_ref[...] = src["skt"][:, tm - WINDOW:]
    vp_ref[...] = src["sv"][tm - WINDOW:, :]

    @pl.when(mix_tile % tiles_per_seq == tiles_per_seq - 1)
    def _():
        s_out_ref[...] = s_ref[...]


def _mixer(x2d, w_all, wt_all, bias_all, cosn, sinn, cost, sint, gain_all, sinks_all, layer, nb, seq):
    m = x2d.shape[0]
    tm = TM_PROJ
    nt = seq // tm
    n = m // tm
    proj = lambda i: jnp.minimum(i, n - 1)
    mix = lambda i: jnp.maximum(i - 1, 0)
    return pl.pallas_call(
        functools.partial(_mixer_kernel, n_tiles=n, tiles_per_seq=nt),
        grid=(n + 1,),
        in_specs=[pl.BlockSpec((tm, D_MODEL), lambda i: (proj(i), 0)),
                  _layer_spec((D_MODEL, W_HALF), layer, 0, 1),
                  _layer_spec((_T_END, D_MODEL), layer),
                  _layer_spec((16, 1), layer),
                  pl.BlockSpec((tm, LANES), lambda i: (proj(i) % nt, 0)),
                  pl.BlockSpec((tm, LANES), lambda i: (proj(i) % nt, 0)),
                  pl.BlockSpec((ROT_HALF, tm), lambda i: (0, proj(i) % nt)),
                  pl.BlockSpec((ROT_HALF, tm), lambda i: (0, proj(i) % nt)),
                  _layer_spec((1, BRANCH), layer),
                  _layer_spec((8, LANES), layer)],
        out_specs=[pl.BlockSpec((tm, BRANCH), lambda i: (mix(i), 0)),
                   pl.BlockSpec((tm, BRANCH), lambda i: (mix(i), 0)),
                   pl.BlockSpec((tm, BRANCH), lambda i: (proj(i), 0)),
                   pl.BlockSpec((LANES, WINDOW), lambda i: (proj(i) // nt, 0)),
                   pl.BlockSpec((LANES, WINDOW), lambda i: (proj(i) // nt, 0)),
                   pl.BlockSpec((8, LANES), lambda i: (proj(i) // nt, 0)),
                   pl.BlockSpec((None, ML_HEADS, ML_DK, 2 * ML_DV), lambda i: (mix(i) // nt, 0, 0, 0))],
        out_shape=[jax.ShapeDtypeStruct((m, BRANCH), BF16),
                   jax.ShapeDtypeStruct((m, BRANCH), BF16),
                   jax.ShapeDtypeStruct((m, BRANCH), BF16),
                   jax.ShapeDtypeStruct((nb * LANES, WINDOW), F32),
                   jax.ShapeDtypeStruct((nb * LANES, WINDOW), F32),
                   jax.ShapeDtypeStruct((nb * 8, LANES), F32),
                   jax.ShapeDtypeStruct((nb, ML_HEADS, ML_DK, 2 * ML_DV), F32)],
        scratch_shapes=[pltpu.VMEM((2,) + shape, dtype) for _, shape, dtype in _SLOT_BUFFERS]
        + [pltpu.VMEM((ML_HEADS, ML_DK, 2 * ML_DV), F32), pltpu.VMEM((8, LANES), F32),
           pltpu.VMEM((LANES, WINDOW), BF16), pltpu.VMEM((WINDOW, LANES), BF16)],
        compiler_params=_params(("arbitrary",)),
        name="prompt_mixer",
    )(x2d, w_all, wt_all, bias_all, cosn, sinn, cost, sint, gain_all, sinks_all)


def _cross_scores(q, kt_ref):
    return [_mm(q[:, h * X_HD:(h + 1) * X_HD], kt_ref[h * X_HD:(h + 1) * X_HD, :]) for h in range(X_HEADS)]


def _cross_outputs(scores, v_ref):
    ones = jnp.ones((MEM_TOKENS, X_HD), BF16)
    out = []
    for h, s in enumerate(scores):
        s = s * (X_HD ** -0.5)
        e = jnp.exp(s - jnp.max(s, axis=1, keepdims=True)).astype(BF16)
        res = _mm(e, jnp.concatenate([v_ref[:, h * X_HD:(h + 1) * X_HD], ones], axis=1))
        out.append((res[:, :X_HD] * (1.0 / res[:, X_HD:])).astype(BF16))
    return jnp.concatenate(out, axis=1)


def _merge_ffn_kernel(*refs, cross_attend):
    if cross_attend:
        x_ref, ya_ref, yb_ref, xq_ref, mkt_ref, mv_ref = refs[:6]
        refs = refs[6:]
    else:
        x_ref, ya_ref, yb_ref, yc_ref = refs[:4]
        refs = refs[4:]
    wgl_ref, wbr_ref, wmix_ref, wfi_ref, wfo_ref, g1_ref, b1_ref, g2_ref, b2_ref, o_ref = refs
    tm = x_ref.shape[0]
    n_sub = 2 if tm >= TM_MERGE else 1
    halves = [slice(s * (tm // n_sub), (s + 1) * (tm // n_sub)) for s in range(n_sub)]
    x = [x_ref[s, :] for s in halves]
    xb = [v.astype(BF16) for v in x]
    if cross_attend:
        scores = [_cross_scores(xq_ref[s, :], mkt_ref) for s in halves]
    gates = [[_sigmoid(_mm(xb[i], wgl_ref[:, r * D_MODEL:(r + 1) * D_MODEL])) for r in range(3)]
             for i in range(n_sub)]
    if cross_attend:
        yc = [_cross_outputs(sc, mv_ref) for sc in scores]
    else:
        yc = [yc_ref[s, :] for s in halves]
    acc = []
    for i, s in enumerate(halves):
        tot = None
        for r, y in enumerate((ya_ref[s, :], yb_ref[s, :], yc[i])):
            term = gates[i][r] * _mm(y, wbr_ref[r])
            tot = term if tot is None else tot + term
        acc.append(tot.astype(BF16))
    x1 = [_layer_norm(DEEPNORM_ALPHA * x[i] + _mm(acc[i], wmix_ref[...]), g1_ref[...], b1_ref[...])
          for i in range(n_sub)]
    act = []
    for i in range(n_sub):
        x1b = x1[i].astype(BF16)
        gpre = _mm(x1b, wfi_ref[:, :D_FF])
        up = _mm(x1b, wfi_ref[:, D_FF:])
        act.append((gpre * _sigmoid(gpre) * up).astype(BF16))
    for i, s in enumerate(halves):
        o_ref[s, :] = _layer_norm(DEEPNORM_ALPHA * x1[i] + _mm(act[i], wfo_ref[...]), g2_ref[...], b2_ref[...])


def _merge_ffn(x2d, ya, yb, third, w_all, wbr, wmix, wfi, wfo, g1, b1, g2, b2, layer, tm, memory=None):
    m = x2d.shape[0]
    row = lambda i: (i, 0)
    vec = _layer_spec((1, D_MODEL), layer)
    mem_specs, mem_args = [], []
    if memory is not None:
        mkt, mv16, seq = memory
        per = seq // tm
        mem_specs = [pl.BlockSpec((None, BRANCH, MEM_TOKENS), lambda i: (i // per, 0, 0)),
                     pl.BlockSpec((MEM_TOKENS, BRANCH), lambda i: (i // per, 0))]
        mem_args = [mkt, mv16]
    return pl.pallas_call(
        functools.partial(_merge_ffn_kernel, cross_attend=memory is not None),
        grid=(m // tm,),
        in_specs=[pl.BlockSpec((tm, D_MODEL), row),
                  pl.BlockSpec((tm, BRANCH), row),
                  pl.BlockSpec((tm, BRANCH), row),
                  pl.BlockSpec((tm, BRANCH), row),
                  *mem_specs,
                  _layer_spec((D_MODEL, W_HALF), layer, 0, 0),
                  _layer_spec((3, BRANCH, D_MODEL), layer),
                  _layer_spec((D_MODEL, D_MODEL), layer),
                  _layer_spec((D_MODEL, 2 * D_FF), layer),
                  _layer_spec((D_FF, D_MODEL), layer),
                  vec, vec, vec, vec],
        out_specs=pl.BlockSpec((tm, D_MODEL), row),
        out_shape=jax.ShapeDtypeStruct((m, D_MODEL), F32),
        compiler_params=_params(("arbitrary",)),
        name="merge_ffn",
    )(x2d, ya, yb, third, *mem_args, w_all, wbr, wmix, wfi, wfo, g1, b1, g2, b2)


_P2_MK, _P2_SK, _P2_END = 0, 512, 640


def _dproj_kernel(x_ref, wn_ref, wt_ref, bias_ref, cos_ref, sin_ref, cost_ref, sint_ref, p_ref, pt_ref, p2_ref):
    xb = x_ref[...].astype(BF16)
    cos = cos_ref[...]
    sin = sin_ref[...]
    for c in range(_N_END // LANES):
        cs = slice(c * LANES, (c + 1) * LANES)
        blk = _mm(xb, wn_ref[:, cs])
        if _N_SQ <= c * LANES < _N_XQ:
            blk = _rope_lanes(blk, cos, sin)
        p_ref[:, cs] = blk
    pt_ref[_T_MV:_T_MK, :] = _mm_nt(wt_ref[_T_MV:_T_MK, :], xb)
    kt = _mm_nt(wt_ref[_T_MK:_T_SK, :], xb) * (ML_DK ** -0.5)
    pt_ref[_T_MK:_T_SK, :] = kt
    skt = _mm_nt(wt_ref[_T_SK:_T_SV, :], xb)
    cost = cost_ref[...]
    sint = sint_ref[...]
    for g in range(SWA_KV):
        base = g * SWA_HD
        for off, val in _rope_rows(skt[base:base + SWA_HD, :], cost, sint):
            pt_ref[_T_SK + base + off:_T_SK + base + off + val.shape[0], :] = val
    pt_ref[_T_SV:_T_G, :] = _mm_nt(wt_ref[_T_SV:_T_G, :], xb)
    pt_ref[_T_G:_T_END, :] = _mm_nt(wt_ref[_T_G:_T_END, :], xb) + bias_ref[...]
    for c in range(BRANCH // LANES):
        p2_ref[:, _P2_MK + c * LANES:_P2_MK + (c + 1) * LANES] = kt[c * LANES:(c + 1) * LANES, :].T
    p2_ref[:, _P2_SK:_P2_END] = pt_ref[_T_SK:_T_SV, :].T


def _dproj(xs, w_all, wt_all, bias_all, cos, sin, cost, sint, layer):
    n = xs.shape[0]
    whole = lambda shape: pl.BlockSpec(shape, lambda i: (0, 0))
    return pl.pallas_call(
        _dproj_kernel,
        grid=(1,),
        in_specs=[_const_spec((n, D_MODEL)),
                  _layer_spec((D_MODEL, W_HALF), layer, 0, 1),
                  _layer_spec((_T_END, D_MODEL), layer),
                  _layer_spec((16, 1), layer),
                  _const_spec((n, LANES)), _const_spec((n, LANES)),
                  _const_spec((ROT_HALF, n)), _const_spec((ROT_HALF, n))],
        out_specs=[whole((n, _N_END)), whole((_T_END, n)), whole((n, _P2_END))],
        out_shape=[jax.ShapeDtypeStruct((n, _N_END), F32),
                   jax.ShapeDtypeStruct((_T_END, n), F32),
                   jax.ShapeDtypeStruct((n, _P2_END), F32)],
        compiler_params=_params(("arbitrary",)),
        name="decode_proj",
    )(xs, w_all, wt_all, bias_all, cos, sin, cost, sint)


def _dmlstm_stream(i, q_ref, v_ref, mo_ref, k_ref, vt_ref, gt_ref, c_ref, n_ref, m_ref, gain_ref,
                   y_ref, c_out_ref, n_out_ref, m_out_ref):
    bb = q_ref.shape[0]
    nlanes = gt_ref.shape[1]
    li = gt_ref[0:ML_HEADS, :]
    lf = _log_sigmoid(gt_ref[8:8 + ML_HEADS, :])
    m_prev = m_ref[...]
    m_t = jnp.maximum(lf + m_prev, li)
    m_out_ref[...] = m_t
    scal = jnp.concatenate([jnp.exp(li - m_t), jnp.exp(lf + m_prev - m_t), jnp.exp(-m_t), jnp.zeros_like(m_t)], axis=0)
    bring = jnp.where(i == 0, 0, nlanes - i * bb)
    scal = pltpu.roll(scal, bring, axis=1)
    tiles = [(h, j) for h in range(ML_HEADS) for j in range(bb)]
    hs = lambda h: slice(h * ML_DK, (h + 1) * ML_DK)
    nrow_of = lambda h, j: slice(j * ML_HEADS + h, j * ML_HEADS + h + 1)
    q = {(h, j): q_ref[j:j + 1, hs(h)] for h, j in tiles}
    k = {(h, j): k_ref[j:j + 1, hs(h)] for h, j in tiles}
    w = {(h, j): scal[h:h + 1, j:j + 1] for h, j in tiles}
    a = {(h, j): scal[ML_HEADS + h:ML_HEADS + h + 1, j:j + 1] for h, j in tiles}
    cq = {(h, j): _mm_nt(jnp.broadcast_to(q[h, j], (8, ML_DK)).astype(BF16), c_ref[j, h].astype(BF16))[0:1, :]
          for h, j in tiles}
    yield
    qk = {t: jnp.sum(q[t] * k[t], axis=1, keepdims=True) for t in tiles}
    nq = {(h, j): jnp.sum(n_ref[nrow_of(h, j), :] * q[h, j], axis=1, keepdims=True) for h, j in tiles}
    yield
    hrow = {}
    for h, j in tiles:
        sw = qk[h, j] * w[h, j]
        floor = scal[2 * ML_HEADS + h:2 * ML_HEADS + h + 1, j:j + 1]
        den = jnp.maximum(jnp.abs(sw + a[h, j] * nq[h, j]), floor)
        hrow[h, j] = (sw * v_ref[j:j + 1, hs(h)] + a[h, j] * cq[h, j]) / den
        n_out_ref[nrow_of(h, j), :] = a[h, j] * n_ref[nrow_of(h, j), :] + w[h, j] * k[h, j]
    yield
    for h in range(ML_HEADS):
        vt = pltpu.roll(vt_ref[hs(h), :], bring, axis=1)
        for j in range(bb):
            c_out_ref[j, h] = a[h, j] * c_ref[j, h] + (w[h, j] * vt[:, j:j + 1]) * k[h, j]
        yield
    hc = {t: hrow[t] - jnp.mean(hrow[t], axis=1, keepdims=True) for t in tiles}
    yield
    var = {t: jnp.mean(hc[t] * hc[t], axis=1, keepdims=True) for t in tiles}
    yield
    for h, j in tiles:
        hn = hc[h, j] * lax.rsqrt(var[h, j] + HEAD_NORM_EPS) * gain_ref[:, hs(h)]
        y_ref[j:j + 1, hs(h)] = (_sigmoid(mo_ref[j:j + 1, hs(h)]) * hn).astype(BF16)


def _dswa_stream(i, q_ref, kn_ref, vn_ref, kvt_ref, ck_ref, cv_ref, sink_ref, y_ref, ko_ref, vo_ref):
    bb = q_ref.shape[0]
    lane = lax.broadcasted_iota(jnp.int32, (LANES, LANES), 1)
    row8 = lax.broadcasted_iota(jnp.int32, (SWA_HEADS, LANES), 0)
    low8 = lax.broadcasted_iota(jnp.int32, (SWA_HEADS, LANES), 1) < SWA_HD
    low1 = lax.broadcasted_iota(jnp.int32, (1, LANES), 1) < SWA_HD
    scale = SWA_HD ** -0.5
    sink = sink_ref[:, 0:1]
    qm, s = [], []
    for j in range(bb):
        rows = jnp.zeros((SWA_HEADS, LANES), F32)
        for pp in range(SWA_HEADS // 2):
            g = pp // 2
            pair = q_ref[j:j + 1, pp * LANES:(pp + 1) * LANES]
            swap = pltpu.roll(pair, SWA_HD, axis=1)
            in_g = low8 if g == 0 else jnp.logical_not(low8)
            for t in range(2):
                rows = jnp.where((row8 == 2 * pp + t) & in_g, pair if t == g else swap, rows)
        qm.append(rows)
        s.append(_mm(rows.astype(BF16), ck_ref[j].astype(BF16)) * scale)
    yield
    e, e_new, den = [], [], []
    for j in range(bb):
        s_new = jnp.sum(qm[j] * kn_ref[j:j + 1, :], axis=1, keepdims=True) * scale
        mx = jnp.maximum(jnp.maximum(jnp.max(s[j], axis=1, keepdims=True), s_new), sink)
        e.append(jnp.exp(s[j] - mx))
        e_new.append(jnp.exp(s_new - mx))
        den.append(jnp.sum(e[j], axis=1, keepdims=True) + e_new[j] + jnp.exp(sink - mx))
    yield
    for j in range(bb):
        o = (_mm_nt(e[j].astype(BF16), cv_ref[j].astype(BF16)) + e_new[j] * vn_ref[j:j + 1, :]) / den[j]
        for pp in range(SWA_HEADS // 2):
            g = pp // 2
            halves = []
            for t in range(2):
                oh = o[2 * pp + t:2 * pp + t + 1, :]
                halves.append(oh if t == g else pltpu.roll(oh, SWA_HD, axis=1))
            y_ref[j:j + 1, pp * LANES:(pp + 1) * LANES] = jnp.where(low1, halves[0], halves[1]).astype(BF16)
    yield
    knew_t = kvt_ref[0:LANES, :]
    vnew_t = kvt_ref[LANES:, :]
    for j in range(bb):
        bring = LANES - 1 - (i * bb + j)
        ko_ref[j] = jnp.where(lane == LANES - 1, pltpu.roll(knew_t, bring, axis=1),
                              pltpu.roll(ck_ref[j], LANES - 1, axis=1))
        vo_ref[j] = jnp.where(lane == LANES - 1, pltpu.roll(vnew_t, bring, axis=1),
                              pltpu.roll(cv_ref[j], LANES - 1, axis=1))
        if j % 2 == 1:
            yield


def _dcross_stream(q_ref, k_ref, v_ref, y_ref):
    bb = q_ref.shape[0]
    scale = X_HD ** -0.5
    row8 = lax.broadcasted_iota(jnp.int32, (8, LANES), 0) % X_HEADS
    ones = jnp.ones((X_HD, LANES), BF16)
    tiles, rows = k_ref.shape[1], k_ref.shape[1] * k_ref.shape[2]
    s = []
    for j in range(bb):
        qrep = jnp.zeros((8, LANES), F32)
        for h in range(X_HEADS):
            qrep = jnp.where(row8 == h, q_ref[j:j + 1, h * X_HD:(h + 1) * X_HD], qrep)
        prod = (k_ref[j] * qrep[None]).astype(BF16).reshape(rows, X_HD)
        s.append(_mm(prod, ones).reshape(tiles, 8, LANES) * scale)
        if j % 2 == 1:
            yield
    for j in range(bb):
        mx8 = jnp.max(s[j], axis=0)
        mx4 = jnp.maximum(mx8[0:X_HEADS], mx8[X_HEADS:])
        e = jnp.exp(s[j] - jnp.concatenate([mx4, mx4], axis=0)[None])
        den8 = jnp.sum(e, axis=0)
        o8 = jnp.sum(e * v_ref[j], axis=0)
        o4 = (o8[0:X_HEADS] + o8[X_HEADS:]) / (den8[0:X_HEADS] + den8[X_HEADS:])
        for h in range(X_HEADS):
            y_ref[j:j + 1, h * X_HD:(h + 1) * X_HD] = o4[h:h + 1, :].astype(BF16)
        if j % 2 == 1:
            yield


_N_DEC_IN = 20


def _decode_mixers_kernel(*refs):
    ins, outs = refs[:_N_DEC_IN], refs[-8:]
    (q_ref, v_ref, mo_ref, k_ref, vt_ref, gt_ref, c_ref, n_ref, m_ref, gain_ref,
     sq_ref, kn_ref, vn_ref, kvt_ref, ck_ref, cv_ref, sink_ref, xq_ref, mk_ref, mv_ref) = ins
    ya_ref, c_out_ref, n_out_ref, m_out_ref, yb_ref, ko_ref, vo_ref, yc_ref = outs
    i = pl.program_id(0)
    streams = [
        _dcross_stream(xq_ref, mk_ref, mv_ref, yc_ref),
        _dmlstm_stream(i, q_ref, v_ref, mo_ref, k_ref, vt_ref, gt_ref, c_ref, n_ref, m_ref, gain_ref,
                       ya_ref, c_out_ref, n_out_ref, m_out_ref),
        _dswa_stream(i, sq_ref, kn_ref, vn_ref, kvt_ref, ck_ref, cv_ref, sink_ref, yb_ref, ko_ref, vo_ref),
    ]
    while streams:
        for g in list(streams):
            if next(g, StopIteration) is StopIteration:
                streams.remove(g)


def _decode_mixers(p, p2, pt, c_all, n_all, mt_all, gain_all, ck_all, cv_all, sinks_all, mk_all, mv_all,
                   layer, stacks):
    n = p.shape[0]
    pblk = lambda width, col: pl.BlockSpec((BB, width), lambda i: (i, col // width))
    fixed = lambda rows, row0: pl.BlockSpec((rows, n), lambda i: (row0 // rows, 0), pipeline_mode=pl.Buffered(1))
    c_spec = pl.BlockSpec((None, BB, ML_HEADS, ML_DV, ML_DK), lambda i: (layer, i, 0, 0, 0))
    win = pl.BlockSpec((None, BB, LANES, WINDOW), lambda i: (layer, i, 0, 0))
    mem = pl.BlockSpec((None, BB, MEM_TOKENS * X_HEADS // 8, 8, X_HD), lambda i: (layer, i, 0, 0, 0))
    in_specs = [pblk(BRANCH, _N_MQ), pblk(BRANCH, _N_MV), pblk(BRANCH, _N_MO), pblk(BRANCH, _P2_MK),
                fixed(BRANCH, _T_MV), fixed(16, _T_G), c_spec,
                pl.BlockSpec((None, BB * ML_HEADS, ML_DK), lambda i: (layer, i, 0)),
                _layer_spec((ML_HEADS, n), layer), _layer_spec((1, BRANCH), layer),
                pblk(BRANCH, _N_SQ), pblk(LANES, _P2_SK), pblk(LANES, _N_SV), fixed(2 * LANES, _T_SK),
                win, win, _layer_spec((8, LANES), layer),
                pblk(BRANCH, _N_XQ), mem, mem]
    args = [p, p, p, p2, pt, pt, c_all, n_all, mt_all, gain_all,
            p, p2, p, pt, ck_all, cv_all, sinks_all, p, mk_all, mv_all]
    assert len(args) == _N_DEC_IN
    aliases = {}
    if stacks is not None:
        in_specs += [pl.BlockSpec(memory_space=pl.ANY)] * 3
        args += list(stacks)
        aliases = {_N_DEC_IN: 1, _N_DEC_IN + 1: 5, _N_DEC_IN + 2: 6}
    row = lambda width: pl.BlockSpec((BB, width), lambda i: (i, 0))
    return pl.pallas_call(
        _decode_mixers_kernel,
        grid=(n // BB,),
        in_specs=in_specs,
        out_specs=[row(BRANCH), c_spec,
                   pl.BlockSpec((BB * ML_HEADS, ML_DK), lambda i: (i, 0)),
                   pl.BlockSpec((ML_HEADS, n), lambda i: (0, 0)),
                   row(BRANCH), win, win, row(BRANCH)],
        out_shape=[jax.ShapeDtypeStruct((n, BRANCH), BF16),
                   jax.ShapeDtypeStruct((DEPTH, n, ML_HEADS, ML_DV, ML_DK), F32),
                   jax.ShapeDtypeStruct((n * ML_HEADS, ML_DK), F32),
                   jax.ShapeDtypeStruct((ML_HEADS, n), F32),
                   jax.ShapeDtypeStruct((n, BRANCH), BF16),
                   jax.ShapeDtypeStruct((DEPTH, n, LANES, WINDOW), F32),
                   jax.ShapeDtypeStruct((DEPTH, n, LANES, WINDOW), F32),
                   jax.ShapeDtypeStruct((n, BRANCH), BF16)],
        input_output_aliases=aliases,
        compiler_params=_params(("arbitrary",)),
        name="decode_mixers",
    )(*args)


def _rope_tables(positions):
    inv_freq = ROPE_THETA ** (-jnp.arange(ROT_HALF, dtype=F32) / ROT_HALF)
    ang = positions.astype(F32)[:, None] * inv_freq[None, :]
    cos = jnp.cos(ang)
    sin = jnp.sin(ang)
    reps = LANES // ROT_HALF
    return jnp.tile(cos, (1, reps)), jnp.tile(sin, (1, reps)), cos.T, sin.T


def kernel(x_prompt, x_sample, mem_prompt, cache_swa_k, cache_swa_v, cache_mem_k, cache_mem_v, state_mlstm_c, state_mlstm_n, state_mlstm_m, w_in, b_gates, mlstm_norm_g, swa_sinks, w_mem_kv, w_branch, w_mix_out, ln1_g, ln1_b, w_ffn_in, w_ffn_out, ln2_g, ln2_b):
    nb, seq, _ = x_prompt.shape
    ns = x_sample.shape[0]
    assert ns == LANES and PAST_LEN >= WINDOW

    cosn, sinn, cost, sint = _rope_tables(jnp.arange(seq))
    cos_s, sin_s, cost_s, sint_s = _rope_tables(jnp.full((ns,), PAST_LEN))

    ck_all = jnp.transpose(cache_swa_k, (0, 1, 3, 4, 2)).reshape(DEPTH, ns, SWA_KV * SWA_HD, WINDOW)
    cv_all = jnp.transpose(cache_swa_v, (0, 1, 3, 4, 2)).reshape(DEPTH, ns, SWA_KV * SWA_HD, WINDOW)
    mk_all = cache_mem_k.reshape(DEPTH, ns, MEM_TOKENS * X_HEADS // 8, 8, X_HD)
    mv_all = cache_mem_v.reshape(DEPTH, ns, MEM_TOKENS * X_HEADS // 8, 8, X_HD)
    n_all = state_mlstm_n.reshape(DEPTH, ns * ML_HEADS, ML_DK)
    mt_all = jnp.transpose(state_mlstm_m, (0, 2, 1))

    zcols = lambda n: jnp.zeros((DEPTH, D_MODEL, n), F32)
    w_all, wt_all = _relayout_w_in(w_in)
    z4 = jnp.zeros((DEPTH, 4), F32)
    bias_all = jnp.concatenate([b_gates[:, :ML_HEADS], z4, b_gates[:, ML_HEADS:], z4], axis=1)[..., None]
    gain_all = mlstm_norm_g[:, None, :]
    sinks_all = jnp.broadcast_to(swa_sinks[:, :, None], (DEPTH, SWA_HEADS, LANES))
    wkv_all = w_mem_kv.astype(BF16)
    wkt_all = jnp.swapaxes(w_mem_kv[..., :BRANCH], 1, 2).astype(BF16)
    wbr_all = w_branch.astype(BF16)
    wmix_all = w_mix_out.astype(BF16)
    wfi_all = w_ffn_in.astype(BF16)
    wfo_all = w_ffn_out.astype(BF16)
    ln_all = (ln1_g[:, None, :], ln1_b[:, None, :], ln2_g[:, None, :], ln2_b[:, None, :])

    yp = x_prompt.reshape(nb * seq, D_MODEL)
    ys = x_sample.reshape(ns, D_MODEL)
    mem2d = mem_prompt.reshape(nb * MEM_TOKENS, D_MODEL)

    outs = {k: [] for k in ("kp", "vp", "mk", "mv", "cp", "np", "mp", "ns", "ms")}
    c_stack = k_stack = v_stack = None
    for l in range(DEPTH):
        mk32, mv32, mkt, mv16 = _memkv(mem2d, wkv_all, wkt_all, l, nb)
        ya, yb, xq, k32, v32, m_fin, s_fin = _mixer(yp, w_all, wt_all, bias_all, cosn, sinn, cost, sint,
                                                    gain_all, sinks_all, l, nb, seq)
        yp = _merge_ffn(yp, ya, yb, xq, w_all, wbr_all, wmix_all, wfi_all, wfo_all, *ln_all, l, TM_MERGE,
                        memory=(mkt, mv16, seq))
        outs["kp"].append(jnp.transpose(k32.reshape(nb, SWA_KV, SWA_HD, WINDOW), (0, 3, 1, 2)))
        outs["vp"].append(jnp.transpose(v32.reshape(nb, SWA_KV, SWA_HD, WINDOW), (0, 3, 1, 2)))
        outs["mk"].append(mk32.reshape(nb, MEM_TOKENS, X_HEADS, X_HD))
        outs["mv"].append(mv32.reshape(nb, MEM_TOKENS, X_HEADS, X_HD))
        outs["cp"].append(jnp.swapaxes(s_fin[..., :ML_DV], -1, -2))
        outs["np"].append(s_fin[..., ML_DV])
        outs["mp"].append(m_fin.reshape(nb, 8, LANES)[:, :ML_HEADS, 0])

        p, pt, p2 = _dproj(ys, w_all, wt_all, bias_all, cos_s, sin_s, cost_s, sint_s, l)
        ya_s, c_stack, n_new, m_new, yb_s, k_stack, v_stack, yc_s = _decode_mixers(
            p, p2, pt, state_mlstm_c, n_all, mt_all, gain_all, ck_all, cv_all, sinks_all, mk_all, mv_all, l,
            None if l == 0 else (c_stack, k_stack, v_stack))
        ys = _merge_ffn(ys, ya_s, yb_s, yc_s, w_all, wbr_all, wmix_all, wfi_all, wfo_all, *ln_all, l, ns)
        outs["ns"].append(n_new.reshape(ns, ML_HEADS, ML_DK))
        outs["ms"].append(m_new)

    st = {k: jnp.stack(vals) for k, vals in outs.items()}
    window_out = lambda t: jnp.transpose(t.reshape(DEPTH, ns, SWA_KV, SWA_HD, WINDOW), (0, 1, 4, 2, 3))
    return (yp.reshape(nb, seq, D_MODEL), ys.reshape(ns, 1, D_MODEL),
            st["kp"], st["vp"], window_out(k_stack), window_out(v_stack), st["mk"], st["mv"],
            st["cp"], st["np"], st["mp"], c_stack, st["ns"], jnp.transpose(st["ms"], (0, 2, 1)))
```

```python
import functools

import jax
import jax.numpy as jnp
from jax import lax
from jax.experimental import pallas as pl
from jax.experimental.pallas import tpu as pltpu

F32 = jnp.float32
BF16 = jnp.bfloat16

D_MODEL = 1024
DEPTH = 2
BRANCH = 512
ML_HEADS = 4
ML_DK = 128
ML_DV = 128
ML_CHUNK = 128
SWA_HD = 64
SWA_HEADS = 8
SWA_KV = 2
SWA_GROUP = 4
WINDOW = 128
ROT_DIM = 16
ROT_HALF = 8
ROPE_THETA = 500000.0
MEM_TOKENS = 256
X_HEADS = 4
X_HD = 128
D_FF = 2816
LN_EPS = 1e-5
HEAD_NORM_EPS = 1e-6
DEEPNORM_ALPHA = (2 * DEPTH) ** 0.25
NEG_INF = -1e30
PAST_LEN = 8192

LANES = 128
VMEM_LIMIT = 56 * 1024 * 1024

_C_MQ, _C_MK, _C_MV, _C_MO = 0, 512, 1024, 1536
_C_MI, _C_MF = 2048, 2052
_C_SQ, _C_SK, _C_SV, _C_XQ, _C_GL = 2056, 2568, 2696, 2824, 3336

W_HALF = 3 * D_MODEL
_N_MQ, _N_MV, _N_MO, _N_SQ, _N_XQ, _N_SV, _N_END = 0, 512, 1024, 1536, 2048, 2560, 2688
_T_MV, _T_MK, _T_SK, _T_SV, _T_G, _T_END = 0, 512, 1024, 1152, 1280, 1296

TM_PROJ = 512
ML_GROUP = 4
TM_MERGE = 256
TQ_CROSS = 512
BB = 8


def _mm(a, b):
    return jnp.dot(a, b, preferred_element_type=F32)


def _mm_nt(a, b):
    return lax.dot_general(a, b, (((1,), (1,)), ((), ())), preferred_element_type=F32)


def _sigmoid(x):
    return 1.0 / (1.0 + jnp.exp(-x))


def _log_sigmoid(x):
    return jnp.minimum(x, 0.0) - jnp.log(1.0 + jnp.exp(-jnp.abs(x)))


def _layer_norm(x, g, b):
    mu = jnp.mean(x, axis=-1, keepdims=True)
    xc = x - mu
    var = jnp.mean(xc * xc, axis=-1, keepdims=True)
    return xc * lax.rsqrt(var + LN_EPS) * g + b


def _rope_lanes(x, cos, sin):
    lane = lax.broadcasted_iota(jnp.int32, x.shape, 1) % SWA_HD
    up = pltpu.roll(x, LANES - ROT_HALF, axis=1)
    dn = pltpu.roll(x, ROT_HALF, axis=1)
    first = x * cos - up * sin
    second = x * cos + dn * sin
    return jnp.where(lane < ROT_HALF, first, jnp.where(lane < ROT_DIM, second, x))


def _rope_rows(xt, cost, sint):
    x1 = xt[0:ROT_HALF, :]
    x2 = xt[ROT_HALF:ROT_DIM, :]
    return ((0, x1 * cost - x2 * sint), (ROT_HALF, x2 * cost + x1 * sint), (ROT_DIM, xt[ROT_DIM:SWA_HD, :]))


def _const_spec(shape):
    nd = len(shape)
    return pl.BlockSpec(shape, lambda *_: (0,) * nd, pipeline_mode=pl.Buffered(1))


def _layer_spec(shape, layer, *tail):
    idx = (layer,) + (tail if tail else (0,) * len(shape))
    return pl.BlockSpec((None,) + tuple(shape), lambda *_: idx, pipeline_mode=pl.Buffered(1))


def _params(sem):
    return pltpu.CompilerParams(dimension_semantics=sem, vmem_limit_bytes=VMEM_LIMIT)


def _relayout_kernel(w_ref, wn_ref, wt_ref):
    half = pl.program_id(1)

    def put(dst, lo, hi):
        for c in range((hi - lo) // LANES):
            tile = w_ref[lo + c * LANES:lo + (c + 1) * LANES, :]
            wn_ref[:, dst + c * LANES:dst + (c + 1) * LANES] = tile.T.astype(BF16)

    @pl.when(half == 0)
    def _():
        put(0, _C_GL, _C_GL + W_HALF)

    @pl.when(half == 1)
    def _():
        for dst, lo, hi in ((_N_MQ, _C_MQ, _C_MK), (_N_MV, _C_MV, _C_MI), (_N_SQ, _C_SQ, _C_SK),
                            (_N_XQ, _C_XQ, _C_GL), (_N_SV, _C_SV, _C_XQ)):
            put(dst, lo, hi)
        wn_ref[:, _N_END:] = jnp.zeros((wn_ref.shape[0], W_HALF - _N_END), BF16)
        for dst, lo, hi in ((_T_MV, _C_MV, _C_MO), (_T_MK, _C_MK, _C_MV), (_T_SK, _C_SK, _C_XQ)):
            wt_ref[dst:dst + hi - lo, :] = w_ref[lo:hi, :].astype(BF16)
        z4 = jnp.zeros((4, w_ref.shape[1]), F32)
        gates = jnp.concatenate([w_ref[_C_MI:_C_MF, :], z4, w_ref[_C_MF:_C_SQ, :], z4], axis=0)
        wt_ref[_T_G:_T_END, :] = gates.astype(BF16)


def _relayout_w_in(w_in):
    depth, rows, cols = w_in.shape
    wt_view = jnp.swapaxes(w_in, 1, 2)
    return pl.pallas_call(
        _relayout_kernel,
        grid=(depth, 2),
        in_specs=[pl.BlockSpec((None, cols, rows), lambda l, j: (l, 0, 0), pipeline_mode=pl.Buffered(1))],
        out_specs=[pl.BlockSpec((None, rows, W_HALF), lambda l, j: (l, 0, j)),
                   pl.BlockSpec((None, _T_END, rows), lambda l, j: (l, 0, 0))],
        out_shape=[jax.ShapeDtypeStruct((depth, rows, 2 * W_HALF), BF16),
                   jax.ShapeDtypeStruct((depth, _T_END, rows), BF16)],
        compiler_params=_params(("arbitrary", "arbitrary")),
        name="relayout_w_in",
    )(wt_view)


def _memkv_kernel(mem_ref, wkv_ref, wkt_ref, k32_ref, v32_ref, kt_ref, v16_ref):
    m = mem_ref[...].astype(BF16)
    kv = _mm(m, wkv_ref[...])
    k32_ref[...] = kv[:, :BRANCH]
    v32_ref[...] = kv[:, BRANCH:]
    v16_ref[...] = kv[:, BRANCH:].astype(BF16)
    kt_ref[...] = _mm_nt(wkt_ref[...], m).astype(BF16)


def _memkv(mem2d, wkv, wkt, layer, nb):
    rows = mem2d.shape[0]
    return pl.pallas_call(
        _memkv_kernel,
        grid=(nb,),
        in_specs=[pl.BlockSpec((MEM_TOKENS, D_MODEL), lambda b: (b, 0)),
                  _layer_spec((D_MODEL, 2 * BRANCH), layer),
                  _layer_spec((BRANCH, D_MODEL), layer)],
        out_specs=[pl.BlockSpec((MEM_TOKENS, BRANCH), lambda b: (b, 0)),
                   pl.BlockSpec((MEM_TOKENS, BRANCH), lambda b: (b, 0)),
                   pl.BlockSpec((None, BRANCH, MEM_TOKENS), lambda b: (b, 0, 0)),
                   pl.BlockSpec((MEM_TOKENS, BRANCH), lambda b: (b, 0))],
        out_shape=[jax.ShapeDtypeStruct((rows, BRANCH), F32),
                   jax.ShapeDtypeStruct((rows, BRANCH), F32),
                   jax.ShapeDtypeStruct((nb, BRANCH, MEM_TOKENS), BF16),
                   jax.ShapeDtypeStruct((rows, BRANCH), BF16)],
        compiler_params=_params(("arbitrary",)),
        name="memkv_proj",
    )(mem2d, wkv, wkt)


def _mlstm_gate_weights(pre, gt_ref, gc_ref, m_out_ref, m_scr, valid):
    tm = pre.shape[1]
    li = pre[0:8, :]
    lane8 = lax.broadcasted_iota(jnp.int32, li.shape, 1) % ML_CHUNK
    b = _log_sigmoid(pre[8:16, :])
    shift = 1
    while shift < ML_CHUNK:
        b = b + jnp.where(lane8 >= shift, pltpu.roll(b, shift, axis=1), 0.0)
        shift *= 2
    g = li - b
    cm = g
    shift = 1
    while shift < ML_CHUNK:
        cm = jnp.maximum(cm, jnp.where(lane8 >= shift, pltpu.roll(cm, shift, axis=1), -jnp.inf))
        shift *= 2
    gt_ref[0:8, :] = g
    pad = jnp.zeros((LANES - 24, ML_CHUNK), F32)
    m_start = m_scr[...]
    m_prev = m_start
    for c in range(tm // ML_CHUNK):
        cs = slice(c * ML_CHUNK, (c + 1) * ML_CHUNK)
        b_c = b[:, cs]
        b_last = jnp.broadcast_to(b_c[:, ML_CHUNK - 1:ML_CHUNK], b_c.shape)
        cm_last = jnp.broadcast_to(cm[:, cs][:, ML_CHUNK - 1:ML_CHUNK], b_c.shape)
        m_t = b_c + jnp.maximum(m_prev, cm[:, cs])
        m_new = b_last + jnp.maximum(m_prev, cm_last)
        gt_ref[8:16, cs] = jnp.exp(b_last + g[:, cs] - m_new)
        gt_ref[16:24, cs] = jnp.exp(b_last + m_prev - m_new)
        rows = jnp.concatenate([b_c - m_t, jnp.exp(b_c + m_prev - m_t), jnp.exp(-m_t), pad], axis=0)
        gc_ref[cs, :] = rows.T
        m_prev = m_new
    m_prev = jnp.where(valid, m_prev, m_start)
    m_scr[...] = m_prev
    m_out_ref[...] = m_prev


def _proj_stream(x_ref, wn_ref, wt_ref, bias_ref, cosn_ref, sinn_ref, cost_ref, sint_ref, dst,
                 xq_ref, k32_ref, v32_ref, m_out_ref, m_scr, valid):
    tm = x_ref.shape[0]
    xb = x_ref[...].astype(BF16)
    tr_rows = lambda lo, hi: _mm_nt(wt_ref[lo:hi, :], xb)
    _mlstm_gate_weights(tr_rows(_T_G, _T_END) + bias_ref[...], dst["gt"], dst["gc"], m_out_ref, m_scr, valid)
    yield
    dst["q"][...] = _mm(xb, wn_ref[:, _N_MQ:_N_MV]).astype(BF16)
    yield
    dst["v"][...] = _mm(xb, wn_ref[:, _N_MV:_N_MO]).astype(BF16)
    yield
    dst["mo"][...] = _mm(xb, wn_ref[:, _N_MO:_N_SQ])
    yield
    xq_ref[...] = _mm(xb, wn_ref[:, _N_XQ:_N_SV]).astype(BF16)
    yield
    cosn = cosn_ref[...]
    sinn = sinn_ref[...]
    sq = _mm(xb, wn_ref[:, _N_SQ:_N_XQ])
    for c in range(BRANCH // LANES):
        blk = _rope_lanes(sq[:, c * LANES:(c + 1) * LANES], cosn, sinn)
        dst["sq"][:, c * LANES:(c + 1) * LANES] = blk.astype(BF16)
    yield
    dst["sv"][...] = _mm(xb, wn_ref[:, _N_SV:_N_END]).astype(BF16)
    dst["kt"][...] = (tr_rows(_T_MK, _T_SK) * (ML_DK ** -0.5)).astype(BF16)
    yield
    skt = tr_rows(_T_SK, _T_SV)
    svt = tr_rows(_T_SV, _T_G)
    cost = cost_ref[...]
    sint = sint_ref[...]
    tail = slice(tm - WINDOW, tm)
    v32_ref[...] = svt[:, tail]
    for g in range(SWA_KV):
        base = g * SWA_HD
        for off, val in _rope_rows(skt[base:base + SWA_HD, :], cost, sint):
            dst["skt"][base + off:base + off + val.shape[0], :] = val.astype(BF16)
            k32_ref[base + off:base + off + val.shape[0], :] = val[:, tail]


def _mlstm_stream(q_ref, v_ref, kt_ref, mo_ref, gt_ref, gc_ref, gain_ref, y_ref, s_ref):
    L = ML_CHUNK
    r_i = lax.broadcasted_iota(jnp.int32, (L, L), 0)
    c_i = lax.broadcasted_iota(jnp.int32, (L, L), 1)
    causal = c_i <= r_i
    ones = jnp.ones((L, ML_DV), BF16)

    ts = lambda c: slice(c * L, (c + 1) * L)
    hs = lambda h: slice(h * ML_DK, (h + 1) * ML_DK)
    n_chunks = q_ref.shape[0] // L

    for c0 in range(0, n_chunks, ML_GROUP):
        chunks = range(c0, min(c0 + ML_GROUP, n_chunks))
        units = [(c, h) for c in chunks for h in range(ML_HEADS)]

        qk = {(c, h): _mm(q_ref[ts(c), hs(h)], kt_ref[hs(h), ts(c)]) for c, h in units}
        yield

        sw, kts, vext = {}, {}, {}
        for c, h in units:
            g_r = gt_ref[h:h + 1, ts(c)]
            es_r = gt_ref[8 + h:9 + h, ts(c)]
            u_c = gc_ref[ts(c), h:h + 1]
            sw[c, h] = (qk[c, h] * jnp.exp(jnp.where(causal, u_c + g_r, -jnp.inf))).astype(BF16)
            kts[c, h] = (kt_ref[hs(h), ts(c)].astype(F32) * es_r).astype(BF16)
            vext[c, h] = jnp.concatenate([v_ref[ts(c), hs(h)], ones], axis=1)
        yield

        intra = {u: _mm(sw[u], vext[u]) for u in units}
        delta = {u: _mm(kts[u], vext[u]) for u in units}
        yield

        s_in = {}
        for h in range(ML_HEADS):
            state = s_ref[h]
            for c in chunks:
                s_in[c, h] = state.astype(BF16)
                state = gt_ref[16 + h:17 + h, c * L:c * L + 1] * state + delta[c, h]
            s_ref[h] = state
        qs_all = {u: _mm(q_ref[ts(u[0]), hs(u[1])], s_in[u]) for u in units}
        yield

        hh, hc = {}, {}
        for c, h in units:
            tot = intra[c, h] + gc_ref[ts(c), 8 + h:9 + h] * qs_all[c, h]
            floor = gc_ref[ts(c), 16 + h:17 + h]
            hh[c, h] = tot[:, :ML_DV] * (1.0 / jnp.maximum(jnp.abs(tot[:, ML_DV:]), floor))
        for u in units:
            hc[u] = hh[u] - jnp.mean(hh[u], axis=1, keepdims=True)
        for c, h in units:
            var = jnp.mean(hc[c, h] * hc[c, h], axis=1, keepdims=True)
            hn = hc[c, h] * lax.rsqrt(var + HEAD_NORM_EPS) * gain_ref[:, hs(h)]
            y_ref[ts(c), hs(h)] = (_sigmoid(mo_ref[ts(c), hs(h)]) * hn).astype(BF16)
        yield


def _swa_stream(q_ref, ktp_ref, ktc_ref, vp_ref, vc_ref, sink_ref, y_ref, seq_start):
    L = WINDOW
    nblk = q_ref.shape[0] // L
    r_i = lax.broadcasted_iota(jnp.int32, (L, 2 * L), 0)
    c_i = lax.broadcasted_iota(jnp.int32, (L, 2 * L), 1)
    band = (c_i >= r_i) & (c_i <= r_i + L)
    first = band & (c_i >= jnp.where(seq_start, L, 0))
    low_half = lax.broadcasted_iota(jnp.int32, (2 * L, LANES), 1) < SWA_HD
    out_low = lax.broadcasted_iota(jnp.int32, (L, LANES), 1) < SWA_HD
    zeros_k = jnp.zeros((SWA_HD, 2 * L), BF16)
    ones_lo = jnp.where(low_half, 1.0, 0.0).astype(BF16)
    ones_hi = jnp.where(low_half, 0.0, 1.0).astype(BF16)

    kt_all = jnp.concatenate([ktp_ref[...], ktc_ref[...]], axis=1)
    v_all = jnp.concatenate([vp_ref[...], vc_ref[...]], axis=0).astype(F32)
    v_swap = pltpu.roll(v_all, SWA_HD, axis=1)

    def scores(c):
        win = slice(c * L, (c + 2) * L)
        out = []
        for g in range(SWA_KV):
            kt2 = kt_all[g * SWA_HD:(g + 1) * SWA_HD, win]
            kblk = jnp.concatenate([jnp.concatenate([kt2, zeros_k], axis=0),
                                    jnp.concatenate([zeros_k, kt2], axis=0)], axis=1)
            for pp in range(2 * g, 2 * g + 2):
                out.append(_mm(q_ref[c * L:(c + 1) * L, pp * LANES:(pp + 1) * LANES], kblk))
        return out

    def weights(c, s_list):
        allowed = first if c == 0 else band
        out = []
        for head in range(SWA_HEADS):
            s = s_list[head // 2][:, (head % 2) * 2 * L:(head % 2 + 1) * 2 * L]
            sc = jnp.where(allowed, s * (SWA_HD ** -0.5), NEG_INF)
            sink = sink_ref[head:head + 1, 0:1]
            mx = jnp.broadcast_to(jnp.maximum(jnp.max(sc, axis=1, keepdims=True), sink), sc.shape)
            out.append((jnp.exp(sc - mx).astype(BF16), jnp.exp(sink - mx[:, :LANES])))
        return out

    def outputs(c, e_list):
        win = slice(c * L, (c + 2) * L)
        v2 = v_all[win, :]
        v2s = v_swap[win, :]
        for g in range(SWA_KV):
            va = jnp.where(low_half, v2 if g == 0 else v2s, 0.0).astype(BF16)
            vb = jnp.where(low_half, 0.0, v2s if g == 0 else v2).astype(BF16)
            vden = jnp.concatenate([jnp.concatenate([va, ones_lo], axis=1),
                                    jnp.concatenate([vb, ones_hi], axis=1)], axis=0)
            for pp in range(2 * g, 2 * g + 2):
                (e0, k0), (e1, k1) = e_list[2 * pp], e_list[2 * pp + 1]
                res = _mm(jnp.concatenate([e0, e1], axis=1), vden)
                den = res[:, LANES:] + jnp.where(out_low, k0, k1)
                y_ref[c * L:(c + 1) * L, pp * LANES:(pp + 1) * LANES] = (res[:, :LANES] * (1.0 / den)).astype(BF16)

    s_next = scores(0)
    yield
    for c in range(nblk):
        s_cur = s_next
        if c + 1 < nblk:
            s_next = scores(c + 1)
        outputs(c, weights(c, s_cur))
        yield


_SLOT_BUFFERS = (("q", (TM_PROJ, BRANCH), BF16), ("v", (TM_PROJ, BRANCH), BF16), ("mo", (TM_PROJ, BRANCH), F32),
                 ("sq", (TM_PROJ, BRANCH), BF16), ("sv", (TM_PROJ, LANES), BF16), ("kt", (BRANCH, TM_PROJ), BF16),
                 ("skt", (LANES, TM_PROJ), BF16), ("gt", (24, TM_PROJ), F32), ("gc", (TM_PROJ, LANES), F32))


def _mixer_kernel(x_ref, wn_ref, wt_ref, bias_ref, cosn_ref, sinn_ref, cost_ref, sint_ref, gain_ref, sink_ref,
                  ya_ref, yb_ref, xq_ref, k32_ref, v32_ref, m_out_ref, s_out_ref, *scratch, n_tiles, tiles_per_seq):
    slots = {name: ref for (name, _, _), ref in zip(_SLOT_BUFFERS, scratch)}
    s_ref, m_scr, ktp_ref, vp_ref = scratch[len(_SLOT_BUFFERS):]
    i = pl.program_id(0)
    wr = i % 2
    mix_tile = i - 1
    valid = i < n_tiles

    @pl.when(i == 0)
    def _():
        for ref in scratch:
            ref[...] = jnp.zeros_like(ref)

    @pl.when((i % tiles_per_seq == 0) & valid)
    def _():
        m_scr[...] = jnp.zeros_like(m_scr)

    seq_start = mix_tile % tiles_per_seq == 0

    @pl.when(seq_start)
    def _():
        s_ref[...] = jnp.zeros_like(s_ref)

    dst = {name: ref.at[wr] for name, ref in slots.items()}
    src = {name: ref.at[1 - wr] for name, ref in slots.items()}
    streams = [
        _proj_stream(x_ref, wn_ref, wt_ref, bias_ref, cosn_ref, sinn_ref, cost_ref, sint_ref, dst,
                     xq_ref, k32_ref, v32_ref, m_out_ref, m_scr, valid),
        _mlstm_stream(src["q"], src["v"], src["kt"], src["mo"], src["gt"], src["gc"], gain_ref, ya_ref, s_ref),
        _swa_stream(src["sq"], ktp_ref, src["skt"], vp_ref, src["sv"], sink_ref, yb_ref, seq_start),
    ]
    while streams:
        for g in list(streams):
            if next(g, StopIteration) is StopIteration:
                streams.remove(g)

    tm = x_ref.shape[0]
    ktp_ref[...] = src["skt"][:, tm - WINDOW:]
    vp_ref[...] = src["sv"][tm - WINDOW:, :]

    @pl.when(mix_tile % tiles_per_seq == tiles_per_seq - 1)
    def _():
        s_out_ref[...] = s_ref[...]


def _mixer(x2d, w_all, wt_all, bias_all, cosn, sinn, cost, sint, gain_all, sinks_all, layer, nb, seq):
    m = x2d.shape[0]
    tm = TM_PROJ
    nt = seq // tm
    n = m // tm
    proj = lambda i: jnp.minimum(i, n - 1)
    mix = lambda i: jnp.maximum(i - 1, 0)
    return pl.pallas_call(
        functools.partial(_mixer_kernel, n_tiles=n, tiles_per_seq=nt),
        grid=(n + 1,),
        in_specs=[pl.BlockSpec((tm, D_MODEL), lambda i: (proj(i), 0)),
                  _layer_spec((D_MODEL, W_HALF), layer, 0, 1),
                  _layer_spec((_T_END, D_MODEL), layer),
                  _layer_spec((16, 1), layer),
                  pl.BlockSpec((tm, LANES), lambda i: (proj(i) % nt, 0)),
                  pl.BlockSpec((tm, LANES), lambda i: (proj(i) % nt, 0)),
                  pl.BlockSpec((ROT_HALF, tm), lambda i: (0, proj(i) % nt)),
                  pl.BlockSpec((ROT_HALF, tm), lambda i: (0, proj(i) % nt)),
                  _layer_spec((1, BRANCH), layer),
                  _layer_spec((8, LANES), layer)],
        out_specs=[pl.BlockSpec((tm, BRANCH), lambda i: (mix(i), 0)),
                   pl.BlockSpec((tm, BRANCH), lambda i: (mix(i), 0)),
                   pl.BlockSpec((tm, BRANCH), lambda i: (proj(i), 0)),
                   pl.BlockSpec((LANES, WINDOW), lambda i: (proj(i) // nt, 0)),
                   pl.BlockSpec((LANES, WINDOW), lambda i: (proj(i) // nt, 0)),
                   pl.BlockSpec((8, LANES), lambda i: (proj(i) // nt, 0)),
                   pl.BlockSpec((None, ML_HEADS, ML_DK, 2 * ML_DV), lambda i: (mix(i) // nt, 0, 0, 0))],
        out_shape=[jax.ShapeDtypeStruct((m, BRANCH), BF16),
                   jax.ShapeDtypeStruct((m, BRANCH), BF16),
                   jax.ShapeDtypeStruct((m, BRANCH), BF16),
                   jax.ShapeDtypeStruct((nb * LANES, WINDOW), F32),
                   jax.ShapeDtypeStruct((nb * LANES, WINDOW), F32),
                   jax.ShapeDtypeStruct((nb * 8, LANES), F32),
                   jax.ShapeDtypeStruct((nb, ML_HEADS, ML_DK, 2 * ML_DV), F32)],
        scratch_shapes=[pltpu.VMEM((2,) + shape, dtype) for _, shape, dtype in _SLOT_BUFFERS]
        + [pltpu.VMEM((ML_HEADS, ML_DK, 2 * ML_DV), F32), pltpu.VMEM((8, LANES), F32),
           pltpu.VMEM((LANES, WINDOW), BF16), pltpu.VMEM((WINDOW, LANES), BF16)],
        compiler_params=_params(("arbitrary",)),
        name="prompt_mixer",
    )(x2d, w_all, wt_all, bias_all, cosn, sinn, cost, sint, gain_all, sinks_all)


def _cross_scores(q, kt_ref):
    return [_mm(q[:, h * X_HD:(h + 1) * X_HD], kt_ref[h * X_HD:(h + 1) * X_HD, :]) for h in range(X_HEADS)]


def _cross_outputs(scores, v_ref):
    ones = jnp.ones((MEM_TOKENS, X_HD), BF16)
    out = []
    for h, s in enumerate(scores):
        s = s * (X_HD ** -0.5)
        e = jnp.exp(s - jnp.max(s, axis=1, keepdims=True)).astype(BF16)
        res = _mm(e, jnp.concatenate([v_ref[:, h * X_HD:(h + 1) * X_HD], ones], axis=1))
        out.append((res[:, :X_HD] * (1.0 / res[:, X_HD:])).astype(BF16))
    return jnp.concatenate(out, axis=1)


def _merge_ffn_kernel(*refs, cross_attend):
    if cross_attend:
        x_ref, ya_ref, yb_ref, xq_ref, mkt_ref, mv_ref = refs[:6]
        refs = refs[6:]
    else:
        x_ref, ya_ref, yb_ref, yc_ref = refs[:4]
        refs = refs[4:]
    wgl_ref, wbr_ref, wmix_ref, wfi_ref, wfo_ref, g1_ref, b1_ref, g2_ref, b2_ref, o_ref = refs
    tm = x_ref.shape[0]
    n_sub = 2 if tm >= TM_MERGE else 1
    halves = [slice(s * (tm // n_sub), (s + 1) * (tm // n_sub)) for s in range(n_sub)]
    x = [x_ref[s, :] for s in halves]
    xb = [v.astype(BF16) for v in x]
    if cross_attend:
        scores = [_cross_scores(xq_ref[s, :], mkt_ref) for s in halves]
    gates = [[_sigmoid(_mm(xb[i], wgl_ref[:, r * D_MODEL:(r + 1) * D_MODEL])) for r in range(3)]
             for i in range(n_sub)]
    if cross_attend:
        yc = [_cross_outputs(sc, mv_ref) for sc in scores]
    else:
        yc = [yc_ref[s, :] for s in halves]
    acc = []
    for i, s in enumerate(halves):
        tot = None
        for r, y in enumerate((ya_ref[s, :], yb_ref[s, :], yc[i])):
            term = gates[i][r] * _mm(y, wbr_ref[r])
            tot = term if tot is None else tot + term
        acc.append(tot.astype(BF16))
    x1 = [_layer_norm(DEEPNORM_ALPHA * x[i] + _mm(acc[i], wmix_ref[...]), g1_ref[...], b1_ref[...])
          for i in range(n_sub)]
    act = []
    for i in range(n_sub):
        x1b = x1[i].astype(BF16)
        gpre = _mm(x1b, wfi_ref[:, :D_FF])
        up = _mm(x1b, wfi_ref[:, D_FF:])
        act.append((gpre * _sigmoid(gpre) * up).astype(BF16))
    for i, s in enumerate(halves):
        o_ref[s, :] = _layer_norm(DEEPNORM_ALPHA * x1[i] + _mm(act[i], wfo_ref[...]), g2_ref[...], b2_ref[...])


def _merge_ffn(x2d, ya, yb, third, w_all, wbr, wmix, wfi, wfo, g1, b1, g2, b2, layer, tm, memory=None):
    m = x2d.shape[0]
    row = lambda i: (i, 0)
    vec = _layer_spec((1, D_MODEL), layer)
    mem_specs, mem_args = [], []
    if memory is not None:
        mkt, mv16, seq = memory
        per = seq // tm
        mem_specs = [pl.BlockSpec((None, BRANCH, MEM_TOKENS), lambda i: (i // per, 0, 0)),
                     pl.BlockSpec((MEM_TOKENS, BRANCH), lambda i: (i // per, 0))]
        mem_args = [mkt, mv16]
    return pl.pallas_call(
        functools.partial(_merge_ffn_kernel, cross_attend=memory is not None),
        grid=(m // tm,),
        in_specs=[pl.BlockSpec((tm, D_MODEL), row),
                  pl.BlockSpec((tm, BRANCH), row),
                  pl.BlockSpec((tm, BRANCH), row),
                  pl.BlockSpec((tm, BRANCH), row),
                  *mem_specs,
                  _layer_spec((D_MODEL, W_HALF), layer, 0, 0),
                  _layer_spec((3, BRANCH, D_MODEL), layer),
                  _layer_spec((D_MODEL, D_MODEL), layer),
                  _layer_spec((D_MODEL, 2 * D_FF), layer),
                  _layer_spec((D_FF, D_MODEL), layer),
                  vec, vec, vec, vec],
        out_specs=pl.BlockSpec((tm, D_MODEL), row),
        out_shape=jax.ShapeDtypeStruct((m, D_MODEL), F32),
        compiler_params=_params(("arbitrary",)),
        name="merge_ffn",
    )(x2d, ya, yb, third, *mem_args, w_all, wbr, wmix, wfi, wfo, g1, b1, g2, b2)


_P2_MK, _P2_SK, _P2_END = 0, 512, 640


def _dproj_kernel(x_ref, wn_ref, wt_ref, bias_ref, cos_ref, sin_ref, cost_ref, sint_ref, p_ref, pt_ref, p2_ref):
    xb = x_ref[...].astype(BF16)
    cos = cos_ref[...]
    sin = sin_ref[...]
    for c in range(_N_END // LANES):
        cs = slice(c * LANES, (c + 1) * LANES)
        blk = _mm(xb, wn_ref[:, cs])
        if _N_SQ <= c * LANES < _N_XQ:
            blk = _rope_lanes(blk, cos, sin)
        p_ref[:, cs] = blk
    pt_ref[_T_MV:_T_MK, :] = _mm_nt(wt_ref[_T_MV:_T_MK, :], xb)
    kt = _mm_nt(wt_ref[_T_MK:_T_SK, :], xb) * (ML_DK ** -0.5)
    pt_ref[_T_MK:_T_SK, :] = kt
    skt = _mm_nt(wt_ref[_T_SK:_T_SV, :], xb)
    cost = cost_ref[...]
    sint = sint_ref[...]
    for g in range(SWA_KV):
        base = g * SWA_HD
        for off, val in _rope_rows(skt[base:base + SWA_HD, :], cost, sint):
            pt_ref[_T_SK + base + off:_T_SK + base + off + val.shape[0], :] = val
    pt_ref[_T_SV:_T_G, :] = _mm_nt(wt_ref[_T_SV:_T_G, :], xb)
    pt_ref[_T_G:_T_END, :] = _mm_nt(wt_ref[_T_G:_T_END, :], xb) + bias_ref[...]
    for c in range(BRANCH // LANES):
        p2_ref[:, _P2_MK + c * LANES:_P2_MK + (c + 1) * LANES] = kt[c * LANES:(c + 1) * LANES, :].T
    p2_ref[:, _P2_SK:_P2_END] = pt_ref[_T_SK:_T_SV, :].T


def _dproj(xs, w_all, wt_all, bias_all, cos, sin, cost, sint, layer):
    n = xs.shape[0]
    whole = lambda shape: pl.BlockSpec(shape, lambda i: (0, 0))
    return pl.pallas_call(
        _dproj_kernel,
        grid=(1,),
        in_specs=[_const_spec((n, D_MODEL)),
                  _layer_spec((D_MODEL, W_HALF), layer, 0, 1),
                  _layer_spec((_T_END, D_MODEL), layer),
                  _layer_spec((16, 1), layer),
                  _const_spec((n, LANES)), _const_spec((n, LANES)),
                  _const_spec((ROT_HALF, n)), _const_spec((ROT_HALF, n))],
        out_specs=[whole((n, _N_END)), whole((_T_END, n)), whole((n, _P2_END))],
        out_shape=[jax.ShapeDtypeStruct((n, _N_END), F32),
                   jax.ShapeDtypeStruct((_T_END, n), F32),
                   jax.ShapeDtypeStruct((n, _P2_END), F32)],
        compiler_params=_params(("arbitrary",)),
        name="decode_proj",
    )(xs, w_all, wt_all, bias_all, cos, sin, cost, sint)


def _dmlstm_stream(i, q_ref, v_ref, mo_ref, k_ref, vt_ref, gt_ref, c_ref, n_ref, m_ref, gain_ref,
                   y_ref, c_out_ref, n_out_ref, m_out_ref):
    bb = q_ref.shape[0]
    nlanes = gt_ref.shape[1]
    li = gt_ref[0:ML_HEADS, :]
    lf = _log_sigmoid(gt_ref[8:8 + ML_HEADS, :])
    m_prev = m_ref[...]
    m_t = jnp.maximum(lf + m_prev, li)
    m_out_ref[...] = m_t
    scal = jnp.concatenate([jnp.exp(li - m_t), jnp.exp(lf + m_prev - m_t), jnp.exp(-m_t), jnp.zeros_like(m_t)], axis=0)
    bring = jnp.where(i == 0, 0, nlanes - i * bb)
    scal = pltpu.roll(scal, bring, axis=1)
    tiles = [(h, j) for h in range(ML_HEADS) for j in range(bb)]
    hs = lambda h: slice(h * ML_DK, (h + 1) * ML_DK)
    nrow_of = lambda h, j: slice(j * ML_HEADS + h, j * ML_HEADS + h + 1)
    q = {(h, j): q_ref[j:j + 1, hs(h)] for h, j in tiles}
    k = {(h, j): k_ref[j:j + 1, hs(h)] for h, j in tiles}
    w = {(h, j): scal[h:h + 1, j:j + 1] for h, j in tiles}
    a = {(h, j): scal[ML_HEADS + h:ML_HEADS + h + 1, j:j + 1] for h, j in tiles}
    cq = {(h, j): _mm_nt(jnp.broadcast_to(q[h, j], (8, ML_DK)).astype(BF16), c_ref[j, h].astype(BF16))[0:1, :]
          for h, j in tiles}
    yield
    qk = {t: jnp.sum(q[t] * k[t], axis=1, keepdims=True) for t in tiles}
    nq = {(h, j): jnp.sum(n_ref[nrow_of(h, j), :] * q[h, j], axis=1, keepdims=True) for h, j in tiles}
    yield
    hrow = {}
    for h, j in tiles:
        sw = qk[h, j] * w[h, j]
        floor = scal[2 * ML_HEADS + h:2 * ML_HEADS + h + 1, j:j + 1]
        den = jnp.maximum(jnp.abs(sw + a[h, j] * nq[h, j]), floor)
        hrow[h, j] = (sw * v_ref[j:j + 1, hs(h)] + a[h, j] * cq[h, j]) / den
        n_out_ref[nrow_of(h, j), :] = a[h, j] * n_ref[nrow_of(h, j), :] + w[h, j] * k[h, j]
    yield
    for h in range(ML_HEADS):
        vt = pltpu.roll(vt_ref[hs(h), :], bring, axis=1)
        for j in range(bb):
            c_out_ref[j, h] = a[h, j] * c_ref[j, h] + (w[h, j] * vt[:, j:j + 1]) * k[h, j]
        yield
    hc = {t: hrow[t] - jnp.mean(hrow[t], axis=1, keepdims=True) for t in tiles}
    yield
    var = {t: jnp.mean(hc[t] * hc[t], axis=1, keepdims=True) for t in tiles}
    yield
    for h, j in tiles:
        hn = hc[h, j] * lax.rsqrt(var[h, j] + HEAD_NORM_EPS) * gain_ref[:, hs(h)]
        y_ref[j:j + 1, hs(h)] = (_sigmoid(mo_ref[j:j + 1, hs(h)]) * hn).astype(BF16)


def _dswa_stream(i, q_ref, kn_ref, vn_ref, kvt_ref, ck_ref, cv_ref, sink_ref, y_ref, ko_ref, vo_ref):
    bb = q_ref.shape[0]
    lane = lax.broadcasted_iota(jnp.int32, (LANES, LANES), 1)
    row8 = lax.broadcasted_iota(jnp.int32, (SWA_HEADS, LANES), 0)
    low8 = lax.broadcasted_iota(jnp.int32, (SWA_HEADS, LANES), 1) < SWA_HD
    low1 = lax.broadcasted_iota(jnp.int32, (1, LANES), 1) < SWA_HD
    scale = SWA_HD ** -0.5
    sink = sink_ref[:, 0:1]
    qm, s = [], []
    for j in range(bb):
        rows = jnp.zeros((SWA_HEADS, LANES), F32)
        for pp in range(SWA_HEADS // 2):
            g = pp // 2
            pair = q_ref[j:j + 1, pp * LANES:(pp + 1) * LANES]
            swap = pltpu.roll(pair, SWA_HD, axis=1)
            in_g = low8 if g == 0 else jnp.logical_not(low8)
            for t in range(2):
                rows = jnp.where((row8 == 2 * pp + t) & in_g, pair if t == g else swap, rows)
        qm.append(rows)
        s.append(_mm(rows.astype(BF16), ck_ref[j].astype(BF16)) * scale)
    yield
    e, e_new, den = [], [], []
    for j in range(bb):
        s_new = jnp.sum(qm[j] * kn_ref[j:j + 1, :], axis=1, keepdims=True) * scale
        mx = jnp.maximum(jnp.maximum(jnp.max(s[j], axis=1, keepdims=True), s_new), sink)
        e.append(jnp.exp(s[j] - mx))
        e_new.append(jnp.exp(s_new - mx))
        den.append(jnp.sum(e[j], axis=1, keepdims=True) + e_new[j] + jnp.exp(sink - mx))
    yield
    for j in range(bb):
        o = (_mm_nt(e[j].astype(BF16), cv_ref[j].astype(BF16)) + e_new[j] * vn_ref[j:j + 1, :]) / den[j]
        for pp in range(SWA_HEADS // 2):
            g = pp // 2
            halves = []
            for t in range(2):
                oh = o[2 * pp + t:2 * pp + t + 1, :]
                halves.append(oh if t == g else pltpu.roll(oh, SWA_HD, axis=1))
            y_ref[j:j + 1, pp * LANES:(pp + 1) * LANES] = jnp.where(low1, halves[0], halves[1]).astype(BF16)
    yield
    knew_t = kvt_ref[0:LANES, :]
    vnew_t = kvt_ref[LANES:, :]
    for j in range(bb):
        bring = LANES - 1 - (i * bb + j)
        ko_ref[j] = jnp.where(lane == LANES - 1, pltpu.roll(knew_t, bring, axis=1),
                              pltpu.roll(ck_ref[j], LANES - 1, axis=1))
        vo_ref[j] = jnp.where(lane == LANES - 1, pltpu.roll(vnew_t, bring, axis=1),
                              pltpu.roll(cv_ref[j], LANES - 1, axis=1))
        if j % 2 == 1:
            yield


def _dcross_stream(q_ref, k_ref, v_ref, y_ref):
    bb = q_ref.shape[0]
    scale = X_HD ** -0.5
    row8 = lax.broadcasted_iota(jnp.int32, (8, LANES), 0) % X_HEADS
    ones = jnp.ones((X_HD, LANES), BF16)
    tiles, rows = k_ref.shape[1], k_ref.shape[1] * k_ref.shape[2]
    s = []
    for j in range(bb):
        qrep = jnp.zeros((8, LANES), F32)
        for h in range(X_HEADS):
            qrep = jnp.where(row8 == h, q_ref[j:j + 1, h * X_HD:(h + 1) * X_HD], qrep)
        prod = (k_ref[j] * qrep[None]).astype(BF16).reshape(rows, X_HD)
        s.append(_mm(prod, ones).reshape(tiles, 8, LANES) * scale)
        if j % 2 == 1:
            yield
    for j in range(bb):
        mx8 = jnp.max(s[j], axis=0)
        mx4 = jnp.maximum(mx8[0:X_HEADS], mx8[X_HEADS:])
        e = jnp.exp(s[j] - jnp.concatenate([mx4, mx4], axis=0)[None])
        den8 = jnp.sum(e, axis=0)
        o8 = jnp.sum(e * v_ref[j], axis=0)
        o4 = (o8[0:X_HEADS] + o8[X_HEADS:]) / (den8[0:X_HEADS] + den8[X_HEADS:])
        for h in range(X_HEADS):
            y_ref[j:j + 1, h * X_HD:(h + 1) * X_HD] = o4[h:h + 1, :].astype(BF16)
        if j % 2 == 1:
            yield


_N_DEC_IN = 20


def _decode_mixers_kernel(*refs):
    ins, outs = refs[:_N_DEC_IN], refs[-8:]
    (q_ref, v_ref, mo_ref, k_ref, vt_ref, gt_ref, c_ref, n_ref, m_ref, gain_ref,
     sq_ref, kn_ref, vn_ref, kvt_ref, ck_ref, cv_ref, sink_ref, xq_ref, mk_ref, mv_ref) = ins
    ya_ref, c_out_ref, n_out_ref, m_out_ref, yb_ref, ko_ref, vo_ref, yc_ref = outs
    i = pl.program_id(0)
    streams = [
        _dcross_stream(xq_ref, mk_ref, mv_ref, yc_ref),
        _dmlstm_stream(i, q_ref, v_ref, mo_ref, k_ref, vt_ref, gt_ref, c_ref, n_ref, m_ref, gain_ref,
                       ya_ref, c_out_ref, n_out_ref, m_out_ref),
        _dswa_stream(i, sq_ref, kn_ref, vn_ref, kvt_ref, ck_ref, cv_ref, sink_ref, yb_ref, ko_ref, vo_ref),
    ]
    while streams:
        for g in list(streams):
            if next(g, StopIteration) is StopIteration:
                streams.remove(g)


def _decode_mixers(p, p2, pt, c_all, n_all, mt_all, gain_all, ck_all, cv_all, sinks_all, mk_all, mv_all,
                   layer, stacks):
    n = p.shape[0]
    pblk = lambda width, col: pl.BlockSpec((BB, width), lambda i: (i, col // width))
    fixed = lambda rows, row0: pl.BlockSpec((rows, n), lambda i: (row0 // rows, 0), pipeline_mode=pl.Buffered(1))
    c_spec = pl.BlockSpec((None, BB, ML_HEADS, ML_DV, ML_DK), lambda i: (layer, i, 0, 0, 0))
    win = pl.BlockSpec((None, BB, LANES, WINDOW), lambda i: (layer, i, 0, 0))
    mem = pl.BlockSpec((None, BB, MEM_TOKENS * X_HEADS // 8, 8, X_HD), lambda i: (layer, i, 0, 0, 0))
    in_specs = [pblk(BRANCH, _N_MQ), pblk(BRANCH, _N_MV), pblk(BRANCH, _N_MO), pblk(BRANCH, _P2_MK),
                fixed(BRANCH, _T_MV), fixed(16, _T_G), c_spec,
                pl.BlockSpec((None, BB * ML_HEADS, ML_DK), lambda i: (layer, i, 0)),
                _layer_spec((ML_HEADS, n), layer), _layer_spec((1, BRANCH), layer),
                pblk(BRANCH, _N_SQ), pblk(LANES, _P2_SK), pblk(LANES, _N_SV), fixed(2 * LANES, _T_SK),
                win, win, _layer_spec((8, LANES), layer),
                pblk(BRANCH, _N_XQ), mem, mem]
    args = [p, p, p, p2, pt, pt, c_all, n_all, mt_all, gain_all,
            p, p2, p, pt, ck_all, cv_all, sinks_all, p, mk_all, mv_all]
    assert len(args) == _N_DEC_IN
    aliases = {}
    if stacks is not None:
        in_specs += [pl.BlockSpec(memory_space=pl.ANY)] * 3
        args += list(stacks)
        aliases = {_N_DEC_IN: 1, _N_DEC_IN + 1: 5, _N_DEC_IN + 2: 6}
    row = lambda width: pl.BlockSpec((BB, width), lambda i: (i, 0))
    return pl.pallas_call(
        _decode_mixers_kernel,
        grid=(n // BB,),
        in_specs=in_specs,
        out_specs=[row(BRANCH), c_spec,
                   pl.BlockSpec((BB * ML_HEADS, ML_DK), lambda i: (i, 0)),
                   pl.BlockSpec((ML_HEADS, n), lambda i: (0, 0)),
                   row(BRANCH), win, win, row(BRANCH)],
        out_shape=[jax.ShapeDtypeStruct((n, BRANCH), BF16),
                   jax.ShapeDtypeStruct((DEPTH, n, ML_HEADS, ML_DV, ML_DK), F32),
                   jax.ShapeDtypeStruct((n * ML_HEADS, ML_DK), F32),
                   jax.ShapeDtypeStruct((ML_HEADS, n), F32),
                   jax.ShapeDtypeStruct((n, BRANCH), BF16),
                   jax.ShapeDtypeStruct((DEPTH, n, LANES, WINDOW), F32),
                   jax.ShapeDtypeStruct((DEPTH, n, LANES, WINDOW), F32),
                   jax.ShapeDtypeStruct((n, BRANCH), BF16)],
        input_output_aliases=aliases,
        compiler_params=_params(("arbitrary",)),
        name="decode_mixers",
    )(*args)


def _rope_tables(positions):
    inv_freq = ROPE_THETA ** (-jnp.arange(ROT_HALF, dtype=F32) / ROT_HALF)
    ang = positions.astype(F32)[:, None] * inv_freq[None, :]
    cos = jnp.cos(ang)
    sin = jnp.sin(ang)
    reps = LANES // ROT_HALF
    return jnp.tile(cos, (1, reps)), jnp.tile(sin, (1, reps)), cos.T, sin.T


def kernel(x_prompt, x_sample, mem_prompt, cache_swa_k, cache_swa_v, cache_mem_k, cache_mem_v, state_mlstm_c, state_mlstm_n, state_mlstm_m, w_in, b_gates, mlstm_norm_g, swa_sinks, w_mem_kv, w_branch, w_mix_out, ln1_g, ln1_b, w_ffn_in, w_ffn_out, ln2_g, ln2_b):
    nb, seq, _ = x_prompt.shape
    ns = x_sample.shape[0]
    assert ns == LANES and PAST_LEN >= WINDOW

    cosn, sinn, cost, sint = _rope_tables(jnp.arange(seq))
    cos_s, sin_s, cost_s, sint_s = _rope_tables(jnp.full((ns,), PAST_LEN))

    ck_all = jnp.transpose(cache_swa_k, (0, 1, 3, 4, 2)).reshape(DEPTH, ns, SWA_KV * SWA_HD, WINDOW)
    cv_all = jnp.transpose(cache_swa_v, (0, 1, 3, 4, 2)).reshape(DEPTH, ns, SWA_KV * SWA_HD, WINDOW)
    mk_all = cache_mem_k.reshape(DEPTH, ns, MEM_TOKENS * X_HEADS // 8, 8, X_HD)
    mv_all = cache_mem_v.reshape(DEPTH, ns, MEM_TOKENS * X_HEADS // 8, 8, X_HD)
    n_all = state_mlstm_n.reshape(DEPTH, ns * ML_HEADS, ML_DK)
    mt_all = jnp.transpose(state_mlstm_m, (0, 2, 1))

    zcols = lambda n: jnp.zeros((DEPTH, D_MODEL, n), F32)
    w_all, wt_all = _relayout_w_in(w_in)
    z4 = jnp.zeros((DEPTH, 4), F32)
    bias_all = jnp.concatenate([b_gates[:, :ML_HEADS], z4, b_gates[:, ML_HEADS:], z4], axis=1)[..., None]
    gain_all = mlstm_norm_g[:, None, :]
    sinks_all = jnp.broadcast_to(swa_sinks[:, :, None], (DEPTH, SWA_HEADS, LANES))
    wkv_all = w_mem_kv.astype(BF16)
    wkt_all = jnp.swapaxes(w_mem_kv[..., :BRANCH], 1, 2).astype(BF16)
    wbr_all = w_branch.astype(BF16)
    wmix_all = w_mix_out.astype(BF16)
    wfi_all = w_ffn_in.astype(BF16)
    wfo_all = w_ffn_out.astype(BF16)
    ln_all = (ln1_g[:, None, :], ln1_b[:, None, :], ln2_g[:, None, :], ln2_b[:, None, :])

    yp = x_prompt.reshape(nb * seq, D_MODEL)
    ys = x_sample.reshape(ns, D_MODEL)
    mem2d = mem_prompt.reshape(nb * MEM_TOKENS, D_MODEL)

    outs = {k: [] for k in ("kp", "vp", "mk", "mv", "cp", "np", "mp", "ns", "ms")}
    c_stack = k_stack = v_stack = None
    for l in range(DEPTH):
        mk32, mv32, mkt, mv16 = _memkv(mem2d, wkv_all, wkt_all, l, nb)
        ya, yb, xq, k32, v32, m_fin, s_fin = _mixer(yp, w_all, wt_all, bias_all, cosn, sinn, cost, sint,
                                                    gain_all, sinks_all, l, nb, seq)
        yp = _merge_ffn(yp, ya, yb, xq, w_all, wbr_all, wmix_all, wfi_all, wfo_all, *ln_all, l, TM_MERGE,
                        memory=(mkt, mv16, seq))
        outs["kp"].append(jnp.transpose(k32.reshape(nb, SWA_KV, SWA_HD, WINDOW), (0, 3, 1, 2)))
        outs["vp"].append(jnp.transpose(v32.reshape(nb, SWA_KV, SWA_HD, WINDOW), (0, 3, 1, 2)))
        outs["mk"].append(mk32.reshape(nb, MEM_TOKENS, X_HEADS, X_HD))
        outs["mv"].append(mv32.reshape(nb, MEM_TOKENS, X_HEADS, X_HD))
        outs["cp"].append(jnp.swapaxes(s_fin[..., :ML_DV], -1, -2))
        outs["np"].append(s_fin[..., ML_DV])
        outs["mp"].append(m_fin.reshape(nb, 8, LANES)[:, :ML_HEADS, 0])

        p, pt, p2 = _dproj(ys, w_all, wt_all, bias_all, cos_s, sin_s, cost_s, sint_s, l)
        ya_s, c_stack, n_new, m_new, yb_s, k_stack, v_stack, yc_s = _decode_mixers(
            p, p2, pt, state_mlstm_c, n_all, mt_all, gain_all, ck_all, cv_all, sinks_all, mk_all, mv_all, l,
            None if l == 0 else (c_stack, k_stack, v_stack))
        ys = _merge_ffn(ys, ya_s, yb_s, yc_s, w_all, wbr_all, wmix_all, wfi_all, wfo_all, *ln_all, l, ns)
        outs["ns"].append(n_new.reshape(ns, ML_HEADS, ML_DK))
        outs["ms"].append(m_new)

    st = {k: jnp.stack(vals) for k, vals in outs.items()}
    window_out = lambda t: jnp.transpose(t.reshape(DEPTH, ns, SWA_KV, SWA_HD, WINDOW), (0, 1, 4, 2, 3))
    return (yp.reshape(nb, seq, D_MODEL), ys.reshape(ns, 1, D_MODEL),
            st["kp"], st["vp"], window_out(k_stack), window_out(v_stack), st["mk"], st["mv"],
            st["cp"], st["np"], st["mp"], c_stack, st["ns"], jnp.transpose(st["ms"], (0, 2, 1)))
```

```python
import functools

import jax
import jax.numpy as jnp
from jax import lax
from jax.experimental import pallas as pl
from jax.experimental.pallas import tpu as pltpu

F32 = jnp.float32
BF16 = jnp.bfloat16

D_MODEL = 1024
DEPTH = 2
BRANCH = 512
ML_HEADS = 4
ML_DK = 128
ML_DV = 128
ML_CHUNK = 128
SWA_HD = 64
SWA_HEADS = 8
SWA_KV = 2
SWA_GROUP = 4
WINDOW = 128
ROT_DIM = 16
ROT_HALF = 8
ROPE_THETA = 500000.0
MEM_TOKENS = 256
X_HEADS = 4
X_HD = 128
D_FF = 2816
LN_EPS = 1e-5
HEAD_NORM_EPS = 1e-6
DEEPNORM_ALPHA = (2 * DEPTH) ** 0.25
NEG_INF = -1e30
PAST_LEN = 8192

LANES = 128
VMEM_LIMIT = 56 * 1024 * 1024

_C_MQ, _C_MK, _C_MV, _C_MO = 0, 512, 1024, 1536
_C_MI, _C_MF = 2048, 2052
_C_SQ, _C_SK, _C_SV, _C_XQ, _C_GL = 2056, 2568, 2696, 2824, 3336

W_HALF = 3 * D_MODEL
_N_MQ, _N_MV, _N_MO, _N_SQ, _N_XQ, _N_SV, _N_END = 0, 512, 1024, 1536, 2048, 2560, 2688
_T_MV, _T_MK, _T_SK, _T_SV, _T_G, _T_END = 0, 512, 1024, 1152, 1280, 1296

TM_PROJ = 512
ML_GROUP = 4
TM_MERGE = 512
TQ_CROSS = 512
BB = 8


def _mm(a, b):
    return jnp.dot(a, b, preferred_element_type=F32)


def _mm_nt(a, b):
    return lax.dot_general(a, b, (((1,), (1,)), ((), ())), preferred_element_type=F32)


def _sigmoid(x):
    return 1.0 / (1.0 + jnp.exp(-x))


def _log_sigmoid(x):
    return jnp.minimum(x, 0.0) - jnp.log(1.0 + jnp.exp(-jnp.abs(x)))


def _layer_norm(x, g, b):
    mu = jnp.mean(x, axis=-1, keepdims=True)
    xc = x - mu
    var = jnp.mean(xc * xc, axis=-1, keepdims=True)
    return xc * lax.rsqrt(var + LN_EPS) * g + b


def _rope_lanes(x, cos, sin):
    lane = lax.broadcasted_iota(jnp.int32, x.shape, 1) % SWA_HD
    up = pltpu.roll(x, LANES - ROT_HALF, axis=1)
    dn = pltpu.roll(x, ROT_HALF, axis=1)
    first = x * cos - up * sin
    second = x * cos + dn * sin
    return jnp.where(lane < ROT_HALF, first, jnp.where(lane < ROT_DIM, second, x))


def _rope_rows(xt, cost, sint):
    x1 = xt[0:ROT_HALF, :]
    x2 = xt[ROT_HALF:ROT_DIM, :]
    return ((0, x1 * cost - x2 * sint), (ROT_HALF, x2 * cost + x1 * sint), (ROT_DIM, xt[ROT_DIM:SWA_HD, :]))


def _const_spec(shape):
    nd = len(shape)
    return pl.BlockSpec(shape, lambda *_: (0,) * nd, pipeline_mode=pl.Buffered(1))


def _layer_spec(shape, layer, *tail):
    idx = (layer,) + (tail if tail else (0,) * len(shape))
    return pl.BlockSpec((None,) + tuple(shape), lambda *_: idx, pipeline_mode=pl.Buffered(1))


def _params(sem):
    return pltpu.CompilerParams(dimension_semantics=sem, vmem_limit_bytes=VMEM_LIMIT)


def _relayout_kernel(w_ref, wn_ref, wt_ref):
    half = pl.program_id(1)

    def put(dst, lo, hi):
        for c in range((hi - lo) // LANES):
            tile = w_ref[lo + c * LANES:lo + (c + 1) * LANES, :]
            wn_ref[:, dst + c * LANES:dst + (c + 1) * LANES] = tile.T.astype(BF16)

    @pl.when(half == 0)
    def _():
        put(0, _C_GL, _C_GL + W_HALF)

    @pl.when(half == 1)
    def _():
        for dst, lo, hi in ((_N_MQ, _C_MQ, _C_MK), (_N_MV, _C_MV, _C_MI), (_N_SQ, _C_SQ, _C_SK),
                            (_N_XQ, _C_XQ, _C_GL), (_N_SV, _C_SV, _C_XQ)):
            put(dst, lo, hi)
        wn_ref[:, _N_END:] = jnp.zeros((wn_ref.shape[0], W_HALF - _N_END), BF16)
        for dst, lo, hi in ((_T_MV, _C_MV, _C_MO), (_T_MK, _C_MK, _C_MV), (_T_SK, _C_SK, _C_XQ)):
            wt_ref[dst:dst + hi - lo, :] = w_ref[lo:hi, :].astype(BF16)
        z4 = jnp.zeros((4, w_ref.shape[1]), F32)
        gates = jnp.concatenate([w_ref[_C_MI:_C_MF, :], z4, w_ref[_C_MF:_C_SQ, :], z4], axis=0)
        wt_ref[_T_G:_T_END, :] = gates.astype(BF16)


def _relayout_w_in(w_in):
    depth, rows, cols = w_in.shape
    wt_view = jnp.swapaxes(w_in, 1, 2)
    return pl.pallas_call(
        _relayout_kernel,
        grid=(depth, 2),
        in_specs=[pl.BlockSpec((None, cols, rows), lambda l, j: (l, 0, 0), pipeline_mode=pl.Buffered(1))],
        out_specs=[pl.BlockSpec((None, rows, W_HALF), lambda l, j: (l, 0, j)),
                   pl.BlockSpec((None, _T_END, rows), lambda l, j: (l, 0, 0))],
        out_shape=[jax.ShapeDtypeStruct((depth, rows, 2 * W_HALF), BF16),
                   jax.ShapeDtypeStruct((depth, _T_END, rows), BF16)],
        compiler_params=_params(("arbitrary", "arbitrary")),
        name="relayout_w_in",
    )(wt_view)


def _memkv_kernel(mem_ref, wkv_ref, wkt_ref, k32_ref, v32_ref, kt_ref, v16_ref):
    m = mem_ref[...].astype(BF16)
    kv = _mm(m, wkv_ref[...])
    k32_ref[...] = kv[:, :BRANCH]
    v32_ref[...] = kv[:, BRANCH:]
    v16_ref[...] = kv[:, BRANCH:].astype(BF16)
    kt_ref[...] = _mm_nt(wkt_ref[...], m).astype(BF16)


def _memkv(mem2d, wkv, wkt, layer, nb):
    rows = mem2d.shape[0]
    return pl.pallas_call(
        _memkv_kernel,
        grid=(nb,),
        in_specs=[pl.BlockSpec((MEM_TOKENS, D_MODEL), lambda b: (b, 0)),
                  _layer_spec((D_MODEL, 2 * BRANCH), layer),
                  _layer_spec((BRANCH, D_MODEL), layer)],
        out_specs=[pl.BlockSpec((MEM_TOKENS, BRANCH), lambda b: (b, 0)),
                   pl.BlockSpec((MEM_TOKENS, BRANCH), lambda b: (b, 0)),
                   pl.BlockSpec((None, BRANCH, MEM_TOKENS), lambda b: (b, 0, 0)),
                   pl.BlockSpec((MEM_TOKENS, BRANCH), lambda b: (b, 0))],
        out_shape=[jax.ShapeDtypeStruct((rows, BRANCH), F32),
                   jax.ShapeDtypeStruct((rows, BRANCH), F32),
                   jax.ShapeDtypeStruct((nb, BRANCH, MEM_TOKENS), BF16),
                   jax.ShapeDtypeStruct((rows, BRANCH), BF16)],
        compiler_params=_params(("arbitrary",)),
        name="memkv_proj",
    )(mem2d, wkv, wkt)


def _mlstm_gate_weights(pre, gt_ref, gc_ref, m_out_ref, m_scr, valid):
    tm = pre.shape[1]
    li = pre[0:8, :]
    lane8 = lax.broadcasted_iota(jnp.int32, li.shape, 1) % ML_CHUNK
    b = _log_sigmoid(pre[8:16, :])
    shift = 1
    while shift < ML_CHUNK:
        b = b + jnp.where(lane8 >= shift, pltpu.roll(b, shift, axis=1), 0.0)
        shift *= 2
    g = li - b
    cm = g
    shift = 1
    while shift < ML_CHUNK:
        cm = jnp.maximum(cm, jnp.where(lane8 >= shift, pltpu.roll(cm, shift, axis=1), -jnp.inf))
        shift *= 2
    gt_ref[0:8, :] = g
    pad = jnp.zeros((LANES - 24, ML_CHUNK), F32)
    m_start = m_scr[...]
    m_prev = m_start
    for c in range(tm // ML_CHUNK):
        cs = slice(c * ML_CHUNK, (c + 1) * ML_CHUNK)
        b_c = b[:, cs]
        b_last = jnp.broadcast_to(b_c[:, ML_CHUNK - 1:ML_CHUNK], b_c.shape)
        cm_last = jnp.broadcast_to(cm[:, cs][:, ML_CHUNK - 1:ML_CHUNK], b_c.shape)
        m_t = b_c + jnp.maximum(m_prev, cm[:, cs])
        m_new = b_last + jnp.maximum(m_prev, cm_last)
        gt_ref[8:16, cs] = jnp.exp(b_last + g[:, cs] - m_new)
        gt_ref[16:24, cs] = jnp.exp(b_last + m_prev - m_new)
        rows = jnp.concatenate([b_c - m_t, jnp.exp(b_c + m_prev - m_t), jnp.exp(-m_t), pad], axis=0)
        gc_ref[cs, :] = rows.T
        m_prev = m_new
    m_prev = jnp.where(valid, m_prev, m_start)
    m_scr[...] = m_prev
    m_out_ref[...] = m_prev


def _proj_stream(x_ref, wn_ref, wt_ref, bias_ref, cosn_ref, sinn_ref, cost_ref, sint_ref, dst,
                 xq_ref, k32_ref, v32_ref, m_out_ref, m_scr, valid):
    tm = x_ref.shape[0]
    xb = x_ref[...].astype(BF16)
    tr_rows = lambda lo, hi: _mm_nt(wt_ref[lo:hi, :], xb)
    _mlstm_gate_weights(tr_rows(_T_G, _T_END) + bias_ref[...], dst["gt"], dst["gc"], m_out_ref, m_scr, valid)
    yield
    dst["q"][...] = _mm(xb, wn_ref[:, _N_MQ:_N_MV]).astype(BF16)
    yield
    dst["v"][...] = _mm(xb, wn_ref[:, _N_MV:_N_MO]).astype(BF16)
    yield
    dst["mo"][...] = _mm(xb, wn_ref[:, _N_MO:_N_SQ])
    yield
    xq_ref[...] = _mm(xb, wn_ref[:, _N_XQ:_N_SV]).astype(BF16)
    yield
    cosn = cosn_ref[...]
    sinn = sinn_ref[...]
    sq = _mm(xb, wn_ref[:, _N_SQ:_N_XQ])
    for c in range(BRANCH // LANES):
        blk = _rope_lanes(sq[:, c * LANES:(c + 1) * LANES], cosn, sinn)
        dst["sq"][:, c * LANES:(c + 1) * LANES] = blk.astype(BF16)
    yield
    dst["sv"][...] = _mm(xb, wn_ref[:, _N_SV:_N_END]).astype(BF16)
    dst["kt"][...] = (tr_rows(_T_MK, _T_SK) * (ML_DK ** -0.5)).astype(BF16)
    yield
    skt = tr_rows(_T_SK, _T_SV)
    svt = tr_rows(_T_SV, _T_G)
    cost = cost_ref[...]
    sint = sint_ref[...]
    tail = slice(tm - WINDOW, tm)
    v32_ref[...] = svt[:, tail]
    for g in range(SWA_KV):
        base = g * SWA_HD
        for off, val in _rope_rows(skt[base:base + SWA_HD, :], cost, sint):
            dst["skt"][base + off:base + off + val.shape[0], :] = val.astype(BF16)
            k32_ref[base + off:base + off + val.shape[0], :] = val[:, tail]


def _mlstm_stream(q_ref, v_ref, kt_ref, mo_ref, gt_ref, gc_ref, gain_ref, y_ref, s_ref):
    L = ML_CHUNK
    r_i = lax.broadcasted_iota(jnp.int32, (L, L), 0)
    c_i = lax.broadcasted_iota(jnp.int32, (L, L), 1)
    causal = c_i <= r_i
    ones = jnp.ones((L, ML_DV), BF16)

    ts = lambda c: slice(c * L, (c + 1) * L)
    hs = lambda h: slice(h * ML_DK, (h + 1) * ML_DK)
    n_chunks = q_ref.shape[0] // L

    for c0 in range(0, n_chunks, ML_GROUP):
        chunks = range(c0, min(c0 + ML_GROUP, n_chunks))
        units = [(c, h) for c in chunks for h in range(ML_HEADS)]

        qk = {(c, h): _mm(q_ref[ts(c), hs(h)], kt_ref[hs(h), ts(c)]) for c, h in units}
        yield

        sw, kts, vext = {}, {}, {}
        for c, h in units:
            g_r = gt_ref[h:h + 1, ts(c)]
            es_r = gt_ref[8 + h:9 + h, ts(c)]
            u_c = gc_ref[ts(c), h:h + 1]
            sw[c, h] = (qk[c, h] * jnp.exp(jnp.where(causal, u_c + g_r, -jnp.inf))).astype(BF16)
            kts[c, h] = (kt_ref[hs(h), ts(c)].astype(F32) * es_r).astype(BF16)
            vext[c, h] = jnp.concatenate([v_ref[ts(c), hs(h)], ones], axis=1)
        yield

        intra = {u: _mm(sw[u], vext[u]) for u in units}
        delta = {u: _mm(kts[u], vext[u]) for u in units}
        yield

        s_in = {}
        for h in range(ML_HEADS):
            state = s_ref[h]
            for c in chunks:
                s_in[c, h] = state.astype(BF16)
                state = gt_ref[16 + h:17 + h, c * L:c * L + 1] * state + delta[c, h]
            s_ref[h] = state
        qs_all = {u: _mm(q_ref[ts(u[0]), hs(u[1])], s_in[u]) for u in units}
        yield

        hh, hc = {}, {}
        for c, h in units:
            tot = intra[c, h] + gc_ref[ts(c), 8 + h:9 + h] * qs_all[c, h]
            floor = gc_ref[ts(c), 16 + h:17 + h]
            hh[c, h] = tot[:, :ML_DV] * (1.0 / jnp.maximum(jnp.abs(tot[:, ML_DV:]), floor))
        for u in units:
            hc[u] = hh[u] - jnp.mean(hh[u], axis=1, keepdims=True)
        for c, h in units:
            var = jnp.mean(hc[c, h] * hc[c, h], axis=1, keepdims=True)
            hn = hc[c, h] * lax.rsqrt(var + HEAD_NORM_EPS) * gain_ref[:, hs(h)]
            y_ref[ts(c), hs(h)] = (_sigmoid(mo_ref[ts(c), hs(h)]) * hn).astype(BF16)
        yield


def _swa_stream(q_ref, ktp_ref, ktc_ref, vp_ref, vc_ref, sink_ref, y_ref, seq_start):
    L = WINDOW
    nblk = q_ref.shape[0] // L
    r_i = lax.broadcasted_iota(jnp.int32, (L, 2 * L), 0)
    c_i = lax.broadcasted_iota(jnp.int32, (L, 2 * L), 1)
    band = (c_i >= r_i) & (c_i <= r_i + L)
    first = band & (c_i >= jnp.where(seq_start, L, 0))
    low_half = lax.broadcasted_iota(jnp.int32, (2 * L, LANES), 1) < SWA_HD
    out_low = lax.broadcasted_iota(jnp.int32, (L, LANES), 1) < SWA_HD
    zeros_k = jnp.zeros((SWA_HD, 2 * L), BF16)
    ones_lo = jnp.where(low_half, 1.0, 0.0).astype(BF16)
    ones_hi = jnp.where(low_half, 0.0, 1.0).astype(BF16)

    kt_all = jnp.concatenate([ktp_ref[...], ktc_ref[...]], axis=1)
    v_all = jnp.concatenate([vp_ref[...], vc_ref[...]], axis=0).astype(F32)
    v_swap = pltpu.roll(v_all, SWA_HD, axis=1)

    def scores(c):
        win = slice(c * L, (c + 2) * L)
        out = []
        for g in range(SWA_KV):
            kt2 = kt_all[g * SWA_HD:(g + 1) * SWA_HD, win]
            kblk = jnp.concatenate([jnp.concatenate([kt2, zeros_k], axis=0),
                                    jnp.concatenate([zeros_k, kt2], axis=0)], axis=1)
            for pp in range(2 * g, 2 * g + 2):
                out.append(_mm(q_ref[c * L:(c + 1) * L, pp * LANES:(pp + 1) * LANES], kblk))
        return out

    def weights(c, s_list):
        allowed = first if c == 0 else band
        out = []
        for head in range(SWA_HEADS):
            s = s_list[head // 2][:, (head % 2) * 2 * L:(head % 2 + 1) * 2 * L]
            sc = jnp.where(allowed, s * (SWA_HD ** -0.5), NEG_INF)
            sink = sink_ref[head:head + 1, 0:1]
            mx = jnp.broadcast_to(jnp.maximum(jnp.max(sc, axis=1, keepdims=True), sink), sc.shape)
            out.append((jnp.exp(sc - mx).astype(BF16), jnp.exp(sink - mx[:, :LANES])))
        return out

    def outputs(c, e_list):
        win = slice(c * L, (c + 2) * L)
        v2 = v_all[win, :]
        v2s = v_swap[win, :]
        for g in range(SWA_KV):
            va = jnp.where(low_half, v2 if g == 0 else v2s, 0.0).astype(BF16)
            vb = jnp.where(low_half, 0.0, v2s if g == 0 else v2).astype(BF16)
            vden = jnp.concatenate([jnp.concatenate([va, ones_lo], axis=1),
                                    jnp.concatenate([vb, ones_hi], axis=1)], axis=0)
            for pp in range(2 * g, 2 * g + 2):
                (e0, k0), (e1, k1) = e_list[2 * pp], e_list[2 * pp + 1]
                res = _mm(jnp.concatenate([e0, e1], axis=1), vden)
                den = res[:, LANES:] + jnp.where(out_low, k0, k1)
                y_ref[c * L:(c + 1) * L, pp * LANES:(pp + 1) * LANES] = (res[:, :LANES] * (1.0 / den)).astype(BF16)

    s_next = scores(0)
    yield
    for c in range(nblk):
        s_cur = s_next
        if c + 1 < nblk:
            s_next = scores(c + 1)
        outputs(c, weights(c, s_cur))
        yield


_SLOT_BUFFERS = (("q", (TM_PROJ, BRANCH), BF16), ("v", (TM_PROJ, BRANCH), BF16), ("mo", (TM_PROJ, BRANCH), F32),
                 ("sq", (TM_PROJ, BRANCH), BF16), ("sv", (TM_PROJ, LANES), BF16), ("kt", (BRANCH, TM_PROJ), BF16),
                 ("skt", (LANES, TM_PROJ), BF16), ("gt", (24, TM_PROJ), F32), ("gc", (TM_PROJ, LANES), F32))


def _mixer_kernel(x_ref, wn_ref, wt_ref, bias_ref, cosn_ref, sinn_ref, cost_ref, sint_ref, gain_ref, sink_ref,
                  ya_ref, yb_ref, xq_ref, k32_ref, v32_ref, m_out_ref, s_out_ref, *scratch, n_tiles, tiles_per_seq):
    slots = {name: ref for (name, _, _), ref in zip(_SLOT_BUFFERS, scratch)}
    s_ref, m_scr, ktp_ref, vp_ref = scratch[len(_SLOT_BUFFERS):]
    i = pl.program_id(0)
    wr = i % 2
    mix_tile = i - 1
    valid = i < n_tiles

    @pl.when(i == 0)
    def _():
        for ref in scratch:
            ref[...] = jnp.zeros_like(ref)

    @pl.when((i % tiles_per_seq == 0) & valid)
    def _():
        m_scr[...] = jnp.zeros_like(m_scr)

    seq_start = mix_tile % tiles_per_seq == 0

    @pl.when(seq_start)
    def _():
        s_ref[...] = jnp.zeros_like(s_ref)

    dst = {name: ref.at[wr] for name, ref in slots.items()}
    src = {name: ref.at[1 - wr] for name, ref in slots.items()}
    streams = [
        _proj_stream(x_ref, wn_ref, wt_ref, bias_ref, cosn_ref, sinn_ref, cost_ref, sint_ref, dst,
                     xq_ref, k32_ref, v32_ref, m_out_ref, m_scr, valid),
        _mlstm_stream(src["q"], src["v"], src["kt"], src["mo"], src["gt"], src["gc"], gain_ref, ya_ref, s_ref),
        _swa_stream(src["sq"], ktp_ref, src["skt"], vp_ref, src["sv"], sink_ref, yb_ref, seq_start),
    ]
    while streams:
        for g in list(streams):
            if next(g, StopIteration) is StopIteration:
                streams.remove(g)

    tm = x_ref.shape[0]
    ktp_ref[...] = src["skt"][:, tm - WINDOW:]
    vp_ref[...] = src["sv"][tm - WINDOW:, :]

    @pl.when(mix_tile % tiles_per_seq == tiles_per_seq - 1)
    def _():
        s_out_ref[...] = s_ref[...]


def _mixer(x2d, w_all, wt_all, bias_all, cosn, sinn, cost, sint, gain_all, sinks_all, layer, nb, seq):
    m = x2d.shape[0]
    tm = TM_PROJ
    nt = seq // tm
    n = m // tm
    proj = lambda i: jnp.minimum(i, n - 1)
    mix = lambda i: jnp.maximum(i - 1, 0)
    return pl.pallas_call(
        functools.partial(_mixer_kernel, n_tiles=n, tiles_per_seq=nt),
        grid=(n + 1,),
        in_specs=[pl.BlockSpec((tm, D_MODEL), lambda i: (proj(i), 0)),
                  _layer_spec((D_MODEL, W_HALF), layer, 0, 1),
                  _layer_spec((_T_END, D_MODEL), layer),
                  _layer_spec((16, 1), layer),
                  pl.BlockSpec((tm, LANES), lambda i: (proj(i) % nt, 0)),
                  pl.BlockSpec((tm, LANES), lambda i: (proj(i) % nt, 0)),
                  pl.BlockSpec((ROT_HALF, tm), lambda i: (0, proj(i) % nt)),
                  pl.BlockSpec((ROT_HALF, tm), lambda i: (0, proj(i) % nt)),
                  _layer_spec((1, BRANCH), layer),
                  _layer_spec((8, LANES), layer)],
        out_specs=[pl.BlockSpec((tm, BRANCH), lambda i: (mix(i), 0)),
                   pl.BlockSpec((tm, BRANCH), lambda i: (mix(i), 0)),
                   pl.BlockSpec((tm, BRANCH), lambda i: (proj(i), 0)),
                   pl.BlockSpec((LANES, WINDOW), lambda i: (proj(i) // nt, 0)),
                   pl.BlockSpec((LANES, WINDOW), lambda i: (proj(i) // nt, 0)),
                   pl.BlockSpec((8, LANES), lambda i: (proj(i) // nt, 0)),
                   pl.BlockSpec((None, ML_HEADS, ML_DK, 2 * ML_DV), lambda i: (mix(i) // nt, 0, 0, 0))],
        out_shape=[jax.ShapeDtypeStruct((m, BRANCH), BF16),
                   jax.ShapeDtypeStruct((m, BRANCH), BF16),
                   jax.ShapeDtypeStruct((m, BRANCH), BF16),
                   jax.ShapeDtypeStruct((nb * LANES, WINDOW), F32),
                   jax.ShapeDtypeStruct((nb * LANES, WINDOW), F32),
                   jax.ShapeDtypeStruct((nb * 8, LANES), F32),
                   jax.ShapeDtypeStruct((nb, ML_HEADS, ML_DK, 2 * ML_DV), F32)],
        scratch_shapes=[pltpu.VMEM((2,) + shape, dtype) for _, shape, dtype in _SLOT_BUFFERS]
        + [pltpu.VMEM((ML_HEADS, ML_DK, 2 * ML_DV), F32), pltpu.VMEM((8, LANES), F32),
           pltpu.VMEM((LANES, WINDOW), BF16), pltpu.VMEM((WINDOW, LANES), BF16)],
        compiler_params=_params(("arbitrary",)),
        name="prompt_mixer",
    )(x2d, w_all, wt_all, bias_all, cosn, sinn, cost, sint, gain_all, sinks_all)


def _cross_scores(q, kt_ref):
    return [_mm(q[:, h * X_HD:(h + 1) * X_HD], kt_ref[h * X_HD:(h + 1) * X_HD, :]) for h in range(X_HEADS)]


def _cross_outputs(scores, v_ref):
    ones = jnp.ones((MEM_TOKENS, X_HD), BF16)
    out = []
    for h, s in enumerate(scores):
        s = s * (X_HD ** -0.5)
        e = jnp.exp(s - jnp.max(s, axis=1, keepdims=True)).astype(BF16)
        res = _mm(e, jnp.concatenate([v_ref[:, h * X_HD:(h + 1) * X_HD], ones], axis=1))
        out.append((res[:, :X_HD] * (1.0 / res[:, X_HD:])).astype(BF16))
    return jnp.concatenate(out, axis=1)


def _merge_ffn_kernel(*refs, cross_attend):
    if cross_attend:
        x_ref, ya_ref, yb_ref, xq_ref, mkt_ref, mv_ref = refs[:6]
        refs = refs[6:]
    else:
        x_ref, ya_ref, yb_ref, yc_ref = refs[:4]
        refs = refs[4:]
    wgl_ref, wbr_ref, wmix_ref, wfi_ref, wfo_ref, g1_ref, b1_ref, g2_ref, b2_ref, o_ref = refs
    tm = x_ref.shape[0]
    n_sub = 2 if tm >= TM_MERGE else 1
    halves = [slice(s * (tm // n_sub), (s + 1) * (tm // n_sub)) for s in range(n_sub)]
    x = [x_ref[s, :] for s in halves]
    xb = [v.astype(BF16) for v in x]
    if cross_attend:
        scores = [_cross_scores(xq_ref[s, :], mkt_ref) for s in halves]
    gates = [[_sigmoid(_mm(xb[i], wgl_ref[:, r * D_MODEL:(r + 1) * D_MODEL])) for r in range(3)]
             for i in range(n_sub)]
    if cross_attend:
        yc = [_cross_outputs(sc, mv_ref) for sc in scores]
    else:
        yc = [yc_ref[s, :] for s in halves]
    acc = []
    for i, s in enumerate(halves):
        tot = None
        for r, y in enumerate((ya_ref[s, :], yb_ref[s, :], yc[i])):
            term = gates[i][r] * _mm(y, wbr_ref[r])
            tot = term if tot is None else tot + term
        acc.append(tot.astype(BF16))
    x1 = [_layer_norm(DEEPNORM_ALPHA * x[i] + _mm(acc[i], wmix_ref[...]), g1_ref[...], b1_ref[...])
          for i in range(n_sub)]
    act = []
    for i in range(n_sub):
        x1b = x1[i].astype(BF16)
        gpre = _mm(x1b, wfi_ref[:, :D_FF])
        up = _mm(x1b, wfi_ref[:, D_FF:])
        act.append((gpre * _sigmoid(gpre) * up).astype(BF16))
    for i, s in enumerate(halves):
        o_ref[s, :] = _layer_norm(DEEPNORM_ALPHA * x1[i] + _mm(act[i], wfo_ref[...]), g2_ref[...], b2_ref[...])


def _merge_ffn(x2d, ya, yb, third, w_all, wbr, wmix, wfi, wfo, g1, b1, g2, b2, layer, tm, memory=None):
    m = x2d.shape[0]
    row = lambda i: (i, 0)
    vec = _layer_spec((1, D_MODEL), layer)
    mem_specs, mem_args = [], []
    if memory is not None:
        mkt, mv16, seq = memory
        per = seq // tm
        mem_specs = [pl.BlockSpec((None, BRANCH, MEM_TOKENS), lambda i: (i // per, 0, 0)),
                     pl.BlockSpec((MEM_TOKENS, BRANCH), lambda i: (i // per, 0))]
        mem_args = [mkt, mv16]
    return pl.pallas_call(
        functools.partial(_merge_ffn_kernel, cross_attend=memory is not None),
        grid=(m // tm,),
        in_specs=[pl.BlockSpec((tm, D_MODEL), row),
                  pl.BlockSpec((tm, BRANCH), row),
                  pl.BlockSpec((tm, BRANCH), row),
                  pl.BlockSpec((tm, BRANCH), row),
                  *mem_specs,
                  _layer_spec((D_MODEL, W_HALF), layer, 0, 0),
                  _layer_spec((3, BRANCH, D_MODEL), layer),
                  _layer_spec((D_MODEL, D_MODEL), layer),
                  _layer_spec((D_MODEL, 2 * D_FF), layer),
                  _layer_spec((D_FF, D_MODEL), layer),
                  vec, vec, vec, vec],
        out_specs=pl.BlockSpec((tm, D_MODEL), row),
        out_shape=jax.ShapeDtypeStruct((m, D_MODEL), F32),
        compiler_params=_params(("arbitrary",)),
        name="merge_ffn",
    )(x2d, ya, yb, third, *mem_args, w_all, wbr, wmix, wfi, wfo, g1, b1, g2, b2)


_P2_MK, _P2_SK, _P2_END = 0, 512, 640


def _dproj_kernel(x_ref, wn_ref, wt_ref, bias_ref, cos_ref, sin_ref, cost_ref, sint_ref, p_ref, pt_ref, p2_ref):
    xb = x_ref[...].astype(BF16)
    cos = cos_ref[...]
    sin = sin_ref[...]
    for c in range(_N_END // LANES):
        cs = slice(c * LANES, (c + 1) * LANES)
        blk = _mm(xb, wn_ref[:, cs])
        if _N_SQ <= c * LANES < _N_XQ:
            blk = _rope_lanes(blk, cos, sin)
        p_ref[:, cs] = blk
    pt_ref[_T_MV:_T_MK, :] = _mm_nt(wt_ref[_T_MV:_T_MK, :], xb)
    kt = _mm_nt(wt_ref[_T_MK:_T_SK, :], xb) * (ML_DK ** -0.5)
    pt_ref[_T_MK:_T_SK, :] = kt
    skt = _mm_nt(wt_ref[_T_SK:_T_SV, :], xb)
    cost = cost_ref[...]
    sint = sint_ref[...]
    for g in range(SWA_KV):
        base = g * SWA_HD
        for off, val in _rope_rows(skt[base:base + SWA_HD, :], cost, sint):
            pt_ref[_T_SK + base + off:_T_SK + base + off + val.shape[0], :] = val
    pt_ref[_T_SV:_T_G, :] = _mm_nt(wt_ref[_T_SV:_T_G, :], xb)
    pt_ref[_T_G:_T_END, :] = _mm_nt(wt_ref[_T_G:_T_END, :], xb) + bias_ref[...]
    for c in range(BRANCH // LANES):
        p2_ref[:, _P2_MK + c * LANES:_P2_MK + (c + 1) * LANES] = kt[c * LANES:(c + 1) * LANES, :].T
    p2_ref[:, _P2_SK:_P2_END] = pt_ref[_T_SK:_T_SV, :].T


def _dproj(xs, w_all, wt_all, bias_all, cos, sin, cost, sint, layer):
    n = xs.shape[0]
    whole = lambda shape: pl.BlockSpec(shape, lambda i: (0, 0))
    return pl.pallas_call(
        _dproj_kernel,
        grid=(1,),
        in_specs=[_const_spec((n, D_MODEL)),
                  _layer_spec((D_MODEL, W_HALF), layer, 0, 1),
                  _layer_spec((_T_END, D_MODEL), layer),
                  _layer_spec((16, 1), layer),
                  _const_spec((n, LANES)), _const_spec((n, LANES)),
                  _const_spec((ROT_HALF, n)), _const_spec((ROT_HALF, n))],
        out_specs=[whole((n, _N_END)), whole((_T_END, n)), whole((n, _P2_END))],
        out_shape=[jax.ShapeDtypeStruct((n, _N_END), F32),
                   jax.ShapeDtypeStruct((_T_END, n), F32),
                   jax.ShapeDtypeStruct((n, _P2_END), F32)],
        compiler_params=_params(("arbitrary",)),
        name="decode_proj",
    )(xs, w_all, wt_all, bias_all, cos, sin, cost, sint)


def _dmlstm_stream(i, q_ref, v_ref, mo_ref, k_ref, vt_ref, gt_ref, c_ref, n_ref, m_ref, gain_ref,
                   y_ref, c_out_ref, n_out_ref, m_out_ref):
    bb = q_ref.shape[0]
    nlanes = gt_ref.shape[1]
    li = gt_ref[0:ML_HEADS, :]
    lf = _log_sigmoid(gt_ref[8:8 + ML_HEADS, :])
    m_prev = m_ref[...]
    m_t = jnp.maximum(lf + m_prev, li)
    m_out_ref[...] = m_t
    scal = jnp.concatenate([jnp.exp(li - m_t), jnp.exp(lf + m_prev - m_t), jnp.exp(-m_t), jnp.zeros_like(m_t)], axis=0)
    bring = jnp.where(i == 0, 0, nlanes - i * bb)
    scal = pltpu.roll(scal, bring, axis=1)
    tiles = [(h, j) for h in range(ML_HEADS) for j in range(bb)]
    hs = lambda h: slice(h * ML_DK, (h + 1) * ML_DK)
    nrow_of = lambda h, j: slice(j * ML_HEADS + h, j * ML_HEADS + h + 1)
    q = {(h, j): q_ref[j:j + 1, hs(h)] for h, j in tiles}
    k = {(h, j): k_ref[j:j + 1, hs(h)] for h, j in tiles}
    w = {(h, j): scal[h:h + 1, j:j + 1] for h, j in tiles}
    a = {(h, j): scal[ML_HEADS + h:ML_HEADS + h + 1, j:j + 1] for h, j in tiles}
    cq = {(h, j): _mm_nt(jnp.broadcast_to(q[h, j], (8, ML_DK)).astype(BF16), c_ref[j, h].astype(BF16))[0:1, :]
          for h, j in tiles}
    yield
    qk = {t: jnp.sum(q[t] * k[t], axis=1, keepdims=True) for t in tiles}
    nq = {(h, j): jnp.sum(n_ref[nrow_of(h, j), :] * q[h, j], axis=1, keepdims=True) for h, j in tiles}
    yield
    hrow = {}
    for h, j in tiles:
        sw = qk[h, j] * w[h, j]
        floor = scal[2 * ML_HEADS + h:2 * ML_HEADS + h + 1, j:j + 1]
        den = jnp.maximum(jnp.abs(sw + a[h, j] * nq[h, j]), floor)
        hrow[h, j] = (sw * v_ref[j:j + 1, hs(h)] + a[h, j] * cq[h, j]) / den
        n_out_ref[nrow_of(h, j), :] = a[h, j] * n_ref[nrow_of(h, j), :] + w[h, j] * k[h, j]
    yield
    for h in range(ML_HEADS):
        vt = pltpu.roll(vt_ref[hs(h), :], bring, axis=1)
        for j in range(bb):
            c_out_ref[j, h] = a[h, j] * c_ref[j, h] + (w[h, j] * vt[:, j:j + 1]) * k[h, j]
        yield
    hc = {t: hrow[t] - jnp.mean(hrow[t], axis=1, keepdims=True) for t in tiles}
    yield
    var = {t: jnp.mean(hc[t] * hc[t], axis=1, keepdims=True) for t in tiles}
    yield
    for h, j in tiles:
        hn = hc[h, j] * lax.rsqrt(var[h, j] + HEAD_NORM_EPS) * gain_ref[:, hs(h)]
        y_ref[j:j + 1, hs(h)] = (_sigmoid(mo_ref[j:j + 1, hs(h)]) * hn).astype(BF16)


def _dswa_stream(i, q_ref, kn_ref, vn_ref, kvt_ref, ck_ref, cv_ref, sink_ref, y_ref, ko_ref, vo_ref):
    bb = q_ref.shape[0]
    lane = lax.broadcasted_iota(jnp.int32, (LANES, LANES), 1)
    row8 = lax.broadcasted_iota(jnp.int32, (SWA_HEADS, LANES), 0)
    low8 = lax.broadcasted_iota(jnp.int32, (SWA_HEADS, LANES), 1) < SWA_HD
    low1 = lax.broadcasted_iota(jnp.int32, (1, LANES), 1) < SWA_HD
    scale = SWA_HD ** -0.5
    sink = sink_ref[:, 0:1]
    qm, s = [], []
    for j in range(bb):
        rows = jnp.zeros((SWA_HEADS, LANES), F32)
        for pp in range(SWA_HEADS // 2):
            g = pp // 2
            pair = q_ref[j:j + 1, pp * LANES:(pp + 1) * LANES]
            swap = pltpu.roll(pair, SWA_HD, axis=1)
            in_g = low8 if g == 0 else jnp.logical_not(low8)
            for t in range(2):
                rows = jnp.where((row8 == 2 * pp + t) & in_g, pair if t == g else swap, rows)
        qm.append(rows)
        s.append(_mm(rows.astype(BF16), ck_ref[j].astype(BF16)) * scale)
    yield
    e, e_new, den = [], [], []
    for j in range(bb):
        s_new = jnp.sum(qm[j] * kn_ref[j:j + 1, :], axis=1, keepdims=True) * scale
        mx = jnp.maximum(jnp.maximum(jnp.max(s[j], axis=1, keepdims=True), s_new), sink)
        e.append(jnp.exp(s[j] - mx))
        e_new.append(jnp.exp(s_new - mx))
        den.append(jnp.sum(e[j], axis=1, keepdims=True) + e_new[j] + jnp.exp(sink - mx))
    yield
    for j in range(bb):
        o = (_mm_nt(e[j].astype(BF16), cv_ref[j].astype(BF16)) + e_new[j] * vn_ref[j:j + 1, :]) / den[j]
        for pp in range(SWA_HEADS // 2):
            g = pp // 2
            halves = []
            for t in range(2):
                oh = o[2 * pp + t:2 * pp + t + 1, :]
                halves.append(oh if t == g else pltpu.roll(oh, SWA_HD, axis=1))
            y_ref[j:j + 1, pp * LANES:(pp + 1) * LANES] = jnp.where(low1, halves[0], halves[1]).astype(BF16)
    yield
    knew_t = kvt_ref[0:LANES, :]
    vnew_t = kvt_ref[LANES:, :]
    for j in range(bb):
        bring = LANES - 1 - (i * bb + j)
        ko_ref[j] = jnp.where(lane == LANES - 1, pltpu.roll(knew_t, bring, axis=1),
                              pltpu.roll(ck_ref[j], LANES - 1, axis=1))
        vo_ref[j] = jnp.where(lane == LANES - 1, pltpu.roll(vnew_t, bring, axis=1),
                              pltpu.roll(cv_ref[j], LANES - 1, axis=1))
        if j % 2 == 1:
            yield


def _dcross_stream(q_ref, k_ref, v_ref, y_ref):
    bb = q_ref.shape[0]
    scale = X_HD ** -0.5
    row8 = lax.broadcasted_iota(jnp.int32, (8, LANES), 0) % X_HEADS
    ones = jnp.ones((X_HD, LANES), BF16)
    tiles, rows = k_ref.shape[1], k_ref.shape[1] * k_ref.shape[2]
    s = []
    for j in range(bb):
        qrep = jnp.zeros((8, LANES), F32)
        for h in range(X_HEADS):
            qrep = jnp.where(row8 == h, q_ref[j:j + 1, h * X_HD:(h + 1) * X_HD], qrep)
        prod = (k_ref[j] * qrep[None]).astype(BF16).reshape(rows, X_HD)
        s.append(_mm(prod, ones).reshape(tiles, 8, LANES) * scale)
        if j % 2 == 1:
            yield
    for j in range(bb):
        mx8 = jnp.max(s[j], axis=0)
        mx4 = jnp.maximum(mx8[0:X_HEADS], mx8[X_HEADS:])
        e = jnp.exp(s[j] - jnp.concatenate([mx4, mx4], axis=0)[None])
        den8 = jnp.sum(e, axis=0)
        o8 = jnp.sum(e * v_ref[j], axis=0)
        o4 = (o8[0:X_HEADS] + o8[X_HEADS:]) / (den8[0:X_HEADS] + den8[X_HEADS:])
        for h in range(X_HEADS):
            y_ref[j:j + 1, h * X_HD:(h + 1) * X_HD] = o4[h:h + 1, :].astype(BF16)
        if j % 2 == 1:
            yield


_N_DEC_IN = 20


def _decode_mixers_kernel(*refs):
    ins, outs = refs[:_N_DEC_IN], refs[-8:]
    (q_ref, v_ref, mo_ref, k_ref, vt_ref, gt_ref, c_ref, n_ref, m_ref, gain_ref,
     sq_ref, kn_ref, vn_ref, kvt_ref, ck_ref, cv_ref, sink_ref, xq_ref, mk_ref, mv_ref) = ins
    ya_ref, c_out_ref, n_out_ref, m_out_ref, yb_ref, ko_ref, vo_ref, yc_ref = outs
    i = pl.program_id(0)
    streams = [
        _dcross_stream(xq_ref, mk_ref, mv_ref, yc_ref),
        _dmlstm_stream(i, q_ref, v_ref, mo_ref, k_ref, vt_ref, gt_ref, c_ref, n_ref, m_ref, gain_ref,
                       ya_ref, c_out_ref, n_out_ref, m_out_ref),
        _dswa_stream(i, sq_ref, kn_ref, vn_ref, kvt_ref, ck_ref, cv_ref, sink_ref, yb_ref, ko_ref, vo_ref),
    ]
    while streams:
        for g in list(streams):
            if next(g, StopIteration) is StopIteration:
                streams.remove(g)


def _decode_mixers(p, p2, pt, c_all, n_all, mt_all, gain_all, ck_all, cv_all, sinks_all, mk_all, mv_all,
                   layer, stacks):
    n = p.shape[0]
    pblk = lambda width, col: pl.BlockSpec((BB, width), lambda i: (i, col // width))
    fixed = lambda rows, row0: pl.BlockSpec((rows, n), lambda i: (row0 // rows, 0), pipeline_mode=pl.Buffered(1))
    c_spec = pl.BlockSpec((None, BB, ML_HEADS, ML_DV, ML_DK), lambda i: (layer, i, 0, 0, 0))
    win = pl.BlockSpec((None, BB, LANES, WINDOW), lambda i: (layer, i, 0, 0))
    mem = pl.BlockSpec((None, BB, MEM_TOKENS * X_HEADS // 8, 8, X_HD), lambda i: (layer, i, 0, 0, 0))
    in_specs = [pblk(BRANCH, _N_MQ), pblk(BRANCH, _N_MV), pblk(BRANCH, _N_MO), pblk(BRANCH, _P2_MK),
                fixed(BRANCH, _T_MV), fixed(16, _T_G), c_spec,
                pl.BlockSpec((None, BB * ML_HEADS, ML_DK), lambda i: (layer, i, 0)),
                _layer_spec((ML_HEADS, n), layer), _layer_spec((1, BRANCH), layer),
                pblk(BRANCH, _N_SQ), pblk(LANES, _P2_SK), pblk(LANES, _N_SV), fixed(2 * LANES, _T_SK),
                win, win, _layer_spec((8, LANES), layer),
                pblk(BRANCH, _N_XQ), mem, mem]
    args = [p, p, p, p2, pt, pt, c_all, n_all, mt_all, gain_all,
            p, p2, p, pt, ck_all, cv_all, sinks_all, p, mk_all, mv_all]
    assert len(args) == _N_DEC_IN
    aliases = {}
    if stacks is not None:
        in_specs += [pl.BlockSpec(memory_space=pl.ANY)] * 3
        args += list(stacks)
        aliases = {_N_DEC_IN: 1, _N_DEC_IN + 1: 5, _N_DEC_IN + 2: 6}
    row = lambda width: pl.BlockSpec((BB, width), lambda i: (i, 0))
    return pl.pallas_call(
        _decode_mixers_kernel,
        grid=(n // BB,),
        in_specs=in_specs,
        out_specs=[row(BRANCH), c_spec,
                   pl.BlockSpec((BB * ML_HEADS, ML_DK), lambda i: (i, 0)),
                   pl.BlockSpec((ML_HEADS, n), lambda i: (0, 0)),
                   row(BRANCH), win, win, row(BRANCH)],
        out_shape=[jax.ShapeDtypeStruct((n, BRANCH), BF16),
                   jax.ShapeDtypeStruct((DEPTH, n, ML_HEADS, ML_DV, ML_DK), F32),
                   jax.ShapeDtypeStruct((n * ML_HEADS, ML_DK), F32),
                   jax.ShapeDtypeStruct((ML_HEADS, n), F32),
                   jax.ShapeDtypeStruct((n, BRANCH), BF16),
                   jax.ShapeDtypeStruct((DEPTH, n, LANES, WINDOW), F32),
                   jax.ShapeDtypeStruct((DEPTH, n, LANES, WINDOW), F32),
                   jax.ShapeDtypeStruct((n, BRANCH), BF16)],
        input_output_aliases=aliases,
        compiler_params=_params(("arbitrary",)),
        name="decode_mixers",
    )(*args)


def _rope_tables(positions):
    inv_freq = ROPE_THETA ** (-jnp.arange(ROT_HALF, dtype=F32) / ROT_HALF)
    ang = positions.astype(F32)[:, None] * inv_freq[None, :]
    cos = jnp.cos(ang)
    sin = jnp.sin(ang)
    reps = LANES // ROT_HALF
    return jnp.tile(cos, (1, reps)), jnp.tile(sin, (1, reps)), cos.T, sin.T


def kernel(x_prompt, x_sample, mem_prompt, cache_swa_k, cache_swa_v, cache_mem_k, cache_mem_v, state_mlstm_c, state_mlstm_n, state_mlstm_m, w_in, b_gates, mlstm_norm_g, swa_sinks, w_mem_kv, w_branch, w_mix_out, ln1_g, ln1_b, w_ffn_in, w_ffn_out, ln2_g, ln2_b):
    nb, seq, _ = x_prompt.shape
    ns = x_sample.shape[0]
    assert ns == LANES and PAST_LEN >= WINDOW

    cosn, sinn, cost, sint = _rope_tables(jnp.arange(seq))
    cos_s, sin_s, cost_s, sint_s = _rope_tables(jnp.full((ns,), PAST_LEN))

    ck_all = jnp.transpose(cache_swa_k, (0, 1, 3, 4, 2)).reshape(DEPTH, ns, SWA_KV * SWA_HD, WINDOW)
    cv_all = jnp.transpose(cache_swa_v, (0, 1, 3, 4, 2)).reshape(DEPTH, ns, SWA_KV * SWA_HD, WINDOW)
    mk_all = cache_mem_k.reshape(DEPTH, ns, MEM_TOKENS * X_HEADS // 8, 8, X_HD)
    mv_all = cache_mem_v.reshape(DEPTH, ns, MEM_TOKENS * X_HEADS // 8, 8, X_HD)
    n_all = state_mlstm_n.reshape(DEPTH, ns * ML_HEADS, ML_DK)
    mt_all = jnp.transpose(state_mlstm_m, (0, 2, 1))

    zcols = lambda n: jnp.zeros((DEPTH, D_MODEL, n), F32)
    w_all, wt_all = _relayout_w_in(w_in)
    z4 = jnp.zeros((DEPTH, 4), F32)
    bias_all = jnp.concatenate([b_gates[:, :ML_HEADS], z4, b_gates[:, ML_HEADS:], z4], axis=1)[..., None]
    gain_all = mlstm_norm_g[:, None, :]
    sinks_all = jnp.broadcast_to(swa_sinks[:, :, None], (DEPTH, SWA_HEADS, LANES))
    wkv_all = w_mem_kv.astype(BF16)
    wkt_all = jnp.swapaxes(w_mem_kv[..., :BRANCH], 1, 2).astype(BF16)
    wbr_all = w_branch.astype(BF16)
    wmix_all = w_mix_out.astype(BF16)
    wfi_all = w_ffn_in.astype(BF16)
    wfo_all = w_ffn_out.astype(BF16)
    ln_all = (ln1_g[:, None, :], ln1_b[:, None, :], ln2_g[:, None, :], ln2_b[:, None, :])

    yp = x_prompt.reshape(nb * seq, D_MODEL)
    ys = x_sample.reshape(ns, D_MODEL)
    mem2d = mem_prompt.reshape(nb * MEM_TOKENS, D_MODEL)

    outs = {k: [] for k in ("kp", "vp", "mk", "mv", "cp", "np", "mp", "ns", "ms")}
    c_stack = k_stack = v_stack = None
    for l in range(DEPTH):
        mk32, mv32, mkt, mv16 = _memkv(mem2d, wkv_all, wkt_all, l, nb)
        ya, yb, xq, k32, v32, m_fin, s_fin = _mixer(yp, w_all, wt_all, bias_all, cosn, sinn, cost, sint,
                                                    gain_all, sinks_all, l, nb, seq)
        yp = _merge_ffn(yp, ya, yb, xq, w_all, wbr_all, wmix_all, wfi_all, wfo_all, *ln_all, l, TM_MERGE,
                        memory=(mkt, mv16, seq))
        outs["kp"].append(jnp.transpose(k32.reshape(nb, SWA_KV, SWA_HD, WINDOW), (0, 3, 1, 2)))
        outs["vp"].append(jnp.transpose(v32.reshape(nb, SWA_KV, SWA_HD, WINDOW), (0, 3, 1, 2)))
        outs["mk"].append(mk32.reshape(nb, MEM_TOKENS, X_HEADS, X_HD))
        outs["mv"].append(mv32.reshape(nb, MEM_TOKENS, X_HEADS, X_HD))
        outs["cp"].append(jnp.swapaxes(s_fin[..., :ML_DV], -1, -2))
        outs["np"].append(s_fin[..., ML_DV])
        outs["mp"].append(m_fin.reshape(nb, 8, LANES)[:, :ML_HEADS, 0])

        p, pt, p2 = _dproj(ys, w_all, wt_all, bias_all, cos_s, sin_s, cost_s, sint_s, l)
        ya_s, c_stack, n_new, m_new, yb_s, k_stack, v_stack, yc_s = _decode_mixers(
            p, p2, pt, state_mlstm_c, n_all, mt_all, gain_all, ck_all, cv_all, sinks_all, mk_all, mv_all, l,
            None if l == 0 else (c_stack, k_stack, v_stack))
        ys = _merge_ffn(ys, ya_s, yb_s, yc_s, w_all, wbr_all, wmix_all, wfi_all, wfo_all, *ln_all, l, ns)
        outs["ns"].append(n_new.reshape(ns, ML_HEADS, ML_DK))
        outs["ms"].append(m_new)

    st = {k: jnp.stack(vals) for k, vals in outs.items()}
    window_out = lambda t: jnp.transpose(t.reshape(DEPTH, ns, SWA_KV, SWA_HD, WINDOW), (0, 1, 4, 2, 3))
    return (yp.reshape(nb, seq, D_MODEL), ys.reshape(ns, 1, D_MODEL),
            st["kp"], st["vp"], window_out(k_stack), window_out(v_stack), st["mk"], st["mv"],
            st["cp"], st["np"], st["mp"], c_stack, st["ns"], jnp.transpose(st["ms"], (0, 2, 1)))
```

```python
import functools

import jax
import jax.numpy as jnp
from jax import lax
from jax.experimental import pallas as pl
from jax.experimental.pallas import tpu as pltpu

F32 = jnp.float32
BF16 = jnp.bfloat16

D_MODEL = 1024
DEPTH = 2
BRANCH = 512
ML_HEADS = 4
ML_DK = 128
ML_DV = 128
ML_CHUNK = 128
SWA_HD = 64
SWA_HEADS = 8
SWA_KV = 2
WINDOW = 128
ROT_DIM = 16
ROT_HALF = 8
ROPE_THETA = 500000.0
MEM_TOKENS = 256
X_HEADS = 4
X_HD = 128
D_FF = 2816
LN_EPS = 1e-5
HEAD_NORM_EPS = 1e-6
DEEPNORM_ALPHA = (2 * DEPTH) ** 0.25
NEG_INF = -1e30
PAST_LEN = 8192

LANES = 128
VMEM_LIMIT = 56 * 1024 * 1024

_C_MQ, _C_MK, _C_MV, _C_MO = 0, 512, 1024, 1536
_C_MI, _C_MF = 2048, 2052
_C_SQ, _C_SK, _C_SV, _C_XQ, _C_GL = 2056, 2568, 2696, 2824, 3336

W_HALF = 3 * D_MODEL
_N_MQ, _N_MV, _N_MO, _N_SQ, _N_XQ, _N_SV, _N_END = 0, 512, 1024, 1536, 2048, 2560, 2688
_T_MV, _T_MK, _T_SK, _T_SV, _T_G, _T_END = 0, 512, 1024, 1152, 1280, 1296

TM_PROJ = 512
TM_MERGE = 512
BB = 8
ML_GROUP = 4


def _mm(a, b):
    return jnp.dot(a, b, preferred_element_type=F32)


def _mm_nt(a, b):
    return lax.dot_general(a, b, (((1,), (1,)), ((), ())), preferred_element_type=F32)


def _sigmoid(x):
    return 1.0 / (1.0 + jnp.exp(-x))


def _log_sigmoid(x):
    return jnp.minimum(x, 0.0) - jnp.log(1.0 + jnp.exp(-jnp.abs(x)))


def _layer_norm(x, g, b):
    mu = jnp.mean(x, axis=-1, keepdims=True)
    xc = x - mu
    var = jnp.mean(xc * xc, axis=-1, keepdims=True)
    return xc * lax.rsqrt(var + LN_EPS) * g + b


def _rope_lanes(x, cos, sin):
    lane = lax.broadcasted_iota(jnp.int32, x.shape, 1) % SWA_HD
    up = pltpu.roll(x, LANES - ROT_HALF, axis=1)
    dn = pltpu.roll(x, ROT_HALF, axis=1)
    first = x * cos - up * sin
    second = x * cos + dn * sin
    return jnp.where(lane < ROT_HALF, first, jnp.where(lane < ROT_DIM, second, x))


def _rope_rows(xt, cost, sint):
    x1 = xt[0:ROT_HALF, :]
    x2 = xt[ROT_HALF:ROT_DIM, :]
    return ((0, x1 * cost - x2 * sint), (ROT_HALF, x2 * cost + x1 * sint), (ROT_DIM, xt[ROT_DIM:SWA_HD, :]))


def _const_spec(shape):
    nd = len(shape)
    return pl.BlockSpec(shape, lambda *_: (0,) * nd, pipeline_mode=pl.Buffered(1))


def _layer_spec(shape, layer, *tail):
    idx = (layer,) + (tail if tail else (0,) * len(shape))
    return pl.BlockSpec((None,) + tuple(shape), lambda *_: idx, pipeline_mode=pl.Buffered(1))


def _params(sem):
    return pltpu.CompilerParams(dimension_semantics=sem, vmem_limit_bytes=VMEM_LIMIT)


def _relayout_kernel(w_ref, wn_ref, wt_ref):
    half = pl.program_id(1)

    def put(dst, lo, hi):
        for c in range((hi - lo) // LANES):
            tile = w_ref[lo + c * LANES:lo + (c + 1) * LANES, :]
            wn_ref[:, dst + c * LANES:dst + (c + 1) * LANES] = tile.T.astype(BF16)

    @pl.when(half == 0)
    def _():
        put(0, _C_GL, _C_GL + W_HALF)

    @pl.when(half == 1)
    def _():
        for dst, lo, hi in ((_N_MQ, _C_MQ, _C_MK), (_N_MV, _C_MV, _C_MI), (_N_SQ, _C_SQ, _C_SK),
                            (_N_XQ, _C_XQ, _C_GL), (_N_SV, _C_SV, _C_XQ)):
            put(dst, lo, hi)
        wn_ref[:, _N_END:] = jnp.zeros((wn_ref.shape[0], W_HALF - _N_END), BF16)
        for dst, lo, hi in ((_T_MV, _C_MV, _C_MO), (_T_MK, _C_MK, _C_MV), (_T_SK, _C_SK, _C_XQ)):
            wt_ref[dst:dst + hi - lo, :] = w_ref[lo:hi, :].astype(BF16)
        z4 = jnp.zeros((4, w_ref.shape[1]), F32)
        gates = jnp.concatenate([w_ref[_C_MI:_C_MF, :], z4, w_ref[_C_MF:_C_SQ, :], z4], axis=0)
        wt_ref[_T_G:_T_END, :] = gates.astype(BF16)


def _relayout_w_in(w_in):
    depth, rows, cols = w_in.shape
    wt_view = jnp.swapaxes(w_in, 1, 2)
    return pl.pallas_call(
        _relayout_kernel,
        grid=(depth, 2),
        in_specs=[pl.BlockSpec((None, cols, rows), lambda l, j: (l, 0, 0), pipeline_mode=pl.Buffered(1))],
        out_specs=[pl.BlockSpec((None, rows, W_HALF), lambda l, j: (l, 0, j)),
                   pl.BlockSpec((None, _T_END, rows), lambda l, j: (l, 0, 0))],
        out_shape=[jax.ShapeDtypeStruct((depth, rows, 2 * W_HALF), BF16),
                   jax.ShapeDtypeStruct((depth, _T_END, rows), BF16)],
        compiler_params=_params(("arbitrary", "arbitrary")),
        name="relayout_w_in",
    )(wt_view)


def _memkv_kernel(mem_ref, wkv_ref, wkt_ref, k32_ref, v32_ref, kt_ref, v16_ref):
    m = mem_ref[...].astype(BF16)
    kv = _mm(m, wkv_ref[...])
    k32_ref[...] = kv[:, :BRANCH]
    v32_ref[...] = kv[:, BRANCH:]
    v16_ref[...] = kv[:, BRANCH:].astype(BF16)
    kt_ref[...] = _mm_nt(wkt_ref[...], m).astype(BF16)


def _memkv(mem2d, wkv, wkt, layer, nb):
    rows = mem2d.shape[0]
    return pl.pallas_call(
        _memkv_kernel,
        grid=(nb,),
        in_specs=[pl.BlockSpec((MEM_TOKENS, D_MODEL), lambda b: (b, 0)),
                  _layer_spec((D_MODEL, 2 * BRANCH), layer),
                  _layer_spec((BRANCH, D_MODEL), layer)],
        out_specs=[pl.BlockSpec((MEM_TOKENS, BRANCH), lambda b: (b, 0)),
                   pl.BlockSpec((MEM_TOKENS, BRANCH), lambda b: (b, 0)),
                   pl.BlockSpec((None, BRANCH, MEM_TOKENS), lambda b: (b, 0, 0)),
                   pl.BlockSpec((MEM_TOKENS, BRANCH), lambda b: (b, 0))],
        out_shape=[jax.ShapeDtypeStruct((rows, BRANCH), F32),
                   jax.ShapeDtypeStruct((rows, BRANCH), F32),
                   jax.ShapeDtypeStruct((nb, BRANCH, MEM_TOKENS), BF16),
                   jax.ShapeDtypeStruct((rows, BRANCH), BF16)],
        compiler_params=_params(("arbitrary",)),
        name="memkv_proj",
    )(mem2d, wkv, wkt)


def _mlstm_gate_weights(pre, gt_ref, gc_ref, m_out_ref, m_scr, valid):
    tm = pre.shape[1]
    li = pre[0:8, :]
    lane8 = lax.broadcasted_iota(jnp.int32, li.shape, 1) % ML_CHUNK
    b = _log_sigmoid(pre[8:16, :])
    shift = 1
    while shift < ML_CHUNK:
        b = b + jnp.where(lane8 >= shift, pltpu.roll(b, shift, axis=1), 0.0)
        shift *= 2
    g = li - b
    cm = g
    shift = 1
    while shift < ML_CHUNK:
        cm = jnp.maximum(cm, jnp.where(lane8 >= shift, pltpu.roll(cm, shift, axis=1), -jnp.inf))
        shift *= 2
    gt_ref[0:8, :] = g
    pad = jnp.zeros((LANES - 24, ML_CHUNK), F32)
    m_start = m_scr[...]
    m_prev = m_start
    for c in range(tm // ML_CHUNK):
        cs = slice(c * ML_CHUNK, (c + 1) * ML_CHUNK)
        b_c = b[:, cs]
        b_last = jnp.broadcast_to(b_c[:, ML_CHUNK - 1:ML_CHUNK], b_c.shape)
        cm_last = jnp.broadcast_to(cm[:, cs][:, ML_CHUNK - 1:ML_CHUNK], b_c.shape)
        m_t = b_c + jnp.maximum(m_prev, cm[:, cs])
        m_new = b_last + jnp.maximum(m_prev, cm_last)
        gt_ref[8:16, cs] = jnp.exp(b_last + g[:, cs] - m_new)
        gt_ref[16:24, cs] = jnp.exp(b_last + m_prev - m_new)
        rows = jnp.concatenate([b_c - m_t, jnp.exp(b_c + m_prev - m_t), jnp.exp(-m_t), pad], axis=0)
        gc_ref[cs, :] = rows.T
        m_prev = m_new
    m_prev = jnp.where(valid, m_prev, m_start)
    m_scr[...] = m_prev
    m_out_ref[...] = m_prev


def _proj_stream(x_ref, wn_ref, wt_ref, bias_ref, cosn_ref, sinn_ref, cost_ref, sint_ref, dst,
                 xq_ref, k32_ref, v32_ref, m_out_ref, m_scr, valid):
    tm = x_ref.shape[0]
    xb = x_ref[...].astype(BF16)
    tr_rows = lambda lo, hi: _mm_nt(wt_ref[lo:hi, :], xb)
    _mlstm_gate_weights(tr_rows(_T_G, _T_END) + bias_ref[...], dst["gt"], dst["gc"], m_out_ref, m_scr, valid)
    yield
    dst["q"][...] = _mm(xb, wn_ref[:, _N_MQ:_N_MV]).astype(BF16)
    yield
    dst["v"][...] = _mm(xb, wn_ref[:, _N_MV:_N_MO]).astype(BF16)
    yield
    dst["mo"][...] = _mm(xb, wn_ref[:, _N_MO:_N_SQ])
    yield
    xq_ref[...] = _mm(xb, wn_ref[:, _N_XQ:_N_SV]).astype(BF16)
    yield
    cosn = cosn_ref[...]
    sinn = sinn_ref[...]
    sq = _mm(xb, wn_ref[:, _N_SQ:_N_XQ])
    for c in range(BRANCH // LANES):
        blk = _rope_lanes(sq[:, c * LANES:(c + 1) * LANES], cosn, sinn)
        dst["sq"][:, c * LANES:(c + 1) * LANES] = blk.astype(BF16)
    yield
    dst["sv"][...] = _mm(xb, wn_ref[:, _N_SV:_N_END]).astype(BF16)
    dst["kt"][...] = (tr_rows(_T_MK, _T_SK) * (ML_DK ** -0.5)).astype(BF16)
    yield
    skt = tr_rows(_T_SK, _T_SV)
    svt = tr_rows(_T_SV, _T_G)
    cost = cost_ref[...]
    sint = sint_ref[...]
    tail = slice(tm - WINDOW, tm)
    v32_ref[...] = svt[:, tail]
    for g in range(SWA_KV):
        base = g * SWA_HD
        for off, val in _rope_rows(skt[base:base + SWA_HD, :], cost, sint):
            dst["skt"][base + off:base + off + val.shape[0], :] = val.astype(BF16)
            k32_ref[base + off:base + off + val.shape[0], :] = val[:, tail]


def _mlstm_stream(q_ref, v_ref, kt_ref, mo_ref, gt_ref, gc_ref, gain_ref, y_ref, s_ref):
    L = ML_CHUNK
    r_i = lax.broadcasted_iota(jnp.int32, (L, L), 0)
    c_i = lax.broadcasted_iota(jnp.int32, (L, L), 1)
    causal = c_i <= r_i
    ones = jnp.ones((L, ML_DV), BF16)

    ts = lambda c: slice(c * L, (c + 1) * L)
    hs = lambda h: slice(h * ML_DK, (h + 1) * ML_DK)
    n_chunks = q_ref.shape[0] // L

    for c0 in range(0, n_chunks, ML_GROUP):
        chunks = range(c0, min(c0 + ML_GROUP, n_chunks))
        units = [(c, h) for c in chunks for h in range(ML_HEADS)]

        qk = {(c, h): _mm(q_ref[ts(c), hs(h)], kt_ref[hs(h), ts(c)]) for c, h in units}
        yield

        sw, kts, vext = {}, {}, {}
        for c, h in units:
            g_r = gt_ref[h:h + 1, ts(c)]
            es_r = gt_ref[8 + h:9 + h, ts(c)]
            u_c = gc_ref[ts(c), h:h + 1]
            sw[c, h] = (qk[c, h] * jnp.exp(jnp.where(causal, u_c + g_r, -jnp.inf))).astype(BF16)
            kts[c, h] = (kt_ref[hs(h), ts(c)].astype(F32) * es_r).astype(BF16)
            vext[c, h] = jnp.concatenate([v_ref[ts(c), hs(h)], ones], axis=1)
        yield

        intra = {u: _mm(sw[u], vext[u]) for u in units}
        delta = {u: _mm(kts[u], vext[u]) for u in units}
        yield

        s_in = {}
        for h in range(ML_HEADS):
            state = s_ref[h]
            for c in chunks:
                s_in[c, h] = state.astype(BF16)
                state = gt_ref[16 + h:17 + h, c * L:c * L + 1] * state + delta[c, h]
            s_ref[h] = state
        qs_all = {u: _mm(q_ref[ts(u[0]), hs(u[1])], s_in[u]) for u in units}
        yield

        hh, hc = {}, {}
        for c, h in units:
            tot = intra[c, h] + gc_ref[ts(c), 8 + h:9 + h] * qs_all[c, h]
            floor = gc_ref[ts(c), 16 + h:17 + h]
            hh[c, h] = tot[:, :ML_DV] * (1.0 / jnp.maximum(jnp.abs(tot[:, ML_DV:]), floor))
        for u in units:
            hc[u] = hh[u] - jnp.mean(hh[u], axis=1, keepdims=True)
        for c, h in units:
            var = jnp.mean(hc[c, h] * hc[c, h], axis=1, keepdims=True)
            hn = hc[c, h] * lax.rsqrt(var + HEAD_NORM_EPS) * gain_ref[:, hs(h)]
            y_ref[ts(c), hs(h)] = (_sigmoid(mo_ref[ts(c), hs(h)]) * hn).astype(BF16)
        yield


def _swa_stream(q_ref, ktp_ref, ktc_ref, vp_ref, vc_ref, sink_ref, y_ref, seq_start):
    L = WINDOW
    nblk = q_ref.shape[0] // L
    r_i = lax.broadcasted_iota(jnp.int32, (L, 2 * L), 0)
    c_i = lax.broadcasted_iota(jnp.int32, (L, 2 * L), 1)
    band = (c_i >= r_i) & (c_i <= r_i + L)
    first = band & (c_i >= jnp.where(seq_start, L, 0))
    low_half = lax.broadcasted_iota(jnp.int32, (2 * L, LANES), 1) < SWA_HD
    out_low = lax.broadcasted_iota(jnp.int32, (L, LANES), 1) < SWA_HD
    zeros_k = jnp.zeros((SWA_HD, 2 * L), BF16)
    ones_lo = jnp.where(low_half, 1.0, 0.0).astype(BF16)
    ones_hi = jnp.where(low_half, 0.0, 1.0).astype(BF16)

    kt_all = jnp.concatenate([ktp_ref[...], ktc_ref[...]], axis=1)
    v_all = jnp.concatenate([vp_ref[...], vc_ref[...]], axis=0).astype(F32)
    v_swap = pltpu.roll(v_all, SWA_HD, axis=1)

    def scores(c):
        win = slice(c * L, (c + 2) * L)
        out = []
        for g in range(SWA_KV):
            kt2 = kt_all[g * SWA_HD:(g + 1) * SWA_HD, win]
            kblk = jnp.concatenate([jnp.concatenate([kt2, zeros_k], axis=0),
                                    jnp.concatenate([zeros_k, kt2], axis=0)], axis=1)
            for pp in range(2 * g, 2 * g + 2):
                out.append(_mm(q_ref[c * L:(c + 1) * L, pp * LANES:(pp + 1) * LANES], kblk))
        return out

    def weights(c, s_list):
        allowed = first if c == 0 else band
        out = []
        for head in range(SWA_HEADS):
            s = s_list[head // 2][:, (head % 2) * 2 * L:(head % 2 + 1) * 2 * L]
            sc = jnp.where(allowed, s * (SWA_HD ** -0.5), NEG_INF)
            sink = sink_ref[head:head + 1, 0:1]
            mx = jnp.broadcast_to(jnp.maximum(jnp.max(sc, axis=1, keepdims=True), sink), sc.shape)
            out.append((jnp.exp(sc - mx).astype(BF16), jnp.exp(sink - mx[:, :LANES])))
        return out

    def outputs(c, e_list):
        win = slice(c * L, (c + 2) * L)
        v2 = v_all[win, :]
        v2s = v_swap[win, :]
        for g in range(SWA_KV):
            va = jnp.where(low_half, v2 if g == 0 else v2s, 0.0).astype(BF16)
            vb = jnp.where(low_half, 0.0, v2s if g == 0 else v2).astype(BF16)
            vden = jnp.concatenate([jnp.concatenate([va, ones_lo], axis=1),
                                    jnp.concatenate([vb, ones_hi], axis=1)], axis=0)
            for pp in range(2 * g, 2 * g + 2):
                (e0, k0), (e1, k1) = e_list[2 * pp], e_list[2 * pp + 1]
                res = _mm(jnp.concatenate([e0, e1], axis=1), vden)
                den = res[:, LANES:] + jnp.where(out_low, k0, k1)
                y_ref[c * L:(c + 1) * L, pp * LANES:(pp + 1) * LANES] = (res[:, :LANES] * (1.0 / den)).astype(BF16)

    s_next = scores(0)
    yield
    for c in range(nblk):
        s_cur = s_next
        if c + 1 < nblk:
            s_next = scores(c + 1)
        outputs(c, weights(c, s_cur))
        yield


_SLOT_BUFFERS = (("q", (TM_PROJ, BRANCH), BF16), ("v", (TM_PROJ, BRANCH), BF16), ("mo", (TM_PROJ, BRANCH), F32),
                 ("sq", (TM_PROJ, BRANCH), BF16), ("sv", (TM_PROJ, LANES), BF16), ("kt", (BRANCH, TM_PROJ), BF16),
                 ("skt", (LANES, TM_PROJ), BF16), ("gt", (24, TM_PROJ), F32), ("gc", (TM_PROJ, LANES), F32))


def _mixer_kernel(x_ref, wn_ref, wt_ref, bias_ref, cosn_ref, sinn_ref, cost_ref, sint_ref, gain_ref, sink_ref,
                  ya_ref, yb_ref, xq_ref, k32_ref, v32_ref, m_out_ref, s_out_ref, *scratch, n_tiles, tiles_per_seq):
    slots = {name: ref for (name, _, _), ref in zip(_SLOT_BUFFERS, scratch)}
    s_ref, m_scr, ktp_ref, vp_ref = scratch[len(_SLOT_BUFFERS):]
    i = pl.program_id(0)
    wr = i % 2
    mix_tile = i - 1
    valid = i < n_tiles

    @pl.when(i == 0)
    def _():
        for ref in scratch:
            ref[...] = jnp.zeros_like(ref)

    @pl.when((i % tiles_per_seq == 0) & valid)
    def _():
        m_scr[...] = jnp.zeros_like(m_scr)

    seq_start = mix_tile % tiles_per_seq == 0

    @pl.when(seq_start)
    def _():
        s_ref[...] = jnp.zeros_like(s_ref)

    dst = {name: ref.at[wr] for name, ref in slots.items()}
    src = {name: ref.at[1 - wr] for name, ref in slots.items()}
    streams = [
        _proj_stream(x_ref, wn_ref, wt_ref, bias_ref, cosn_ref, sinn_ref, cost_ref, sint_ref, dst,
                     xq_ref, k32_ref, v32_ref, m_out_ref, m_scr, valid),
        _mlstm_stream(src["q"], src["v"], src["kt"], src["mo"], src["gt"], src["gc"], gain_ref, ya_ref, s_ref),
        _swa_stream(src["sq"], ktp_ref, src["skt"], vp_ref, src["sv"], sink_ref, yb_ref, seq_start),
    ]
    while streams:
        for g in list(streams):
            if next(g, StopIteration) is StopIteration:
                streams.remove(g)

    tm = x_ref.shape[0]
    ktp_ref[...] = src["skt"][:, tm - WINDOW:]
    vp_ref[...] = src["sv"][tm - WINDOW:, :]

    @pl.when(mix_tile % tiles_per_seq == tiles_per_seq - 1)
    def _():
        s_out_ref[...] = s_ref[...]


def _mixer(x2d, w_all, wt_all, bias_all, cosn, sinn, cost, sint, gain_all, sinks_all, layer, nb, seq):
    m = x2d.shape[0]
    tm = TM_PROJ
    nt = seq // tm
    n = m // tm
    proj = lambda i: jnp.minimum(i, n - 1)
    mix = lambda i: jnp.maximum(i - 1, 0)
    return pl.pallas_call(
        functools.partial(_mixer_kernel, n_tiles=n, tiles_per_seq=nt),
        grid=(n + 1,),
        in_specs=[pl.BlockSpec((tm, D_MODEL), lambda i: (proj(i), 0)),
                  _layer_spec((D_MODEL, W_HALF), layer, 0, 1),
                  _layer_spec((_T_END, D_MODEL), layer),
                  _layer_spec((16, 1), layer),
                  pl.BlockSpec((tm, LANES), lambda i: (proj(i) % nt, 0)),
                  pl.BlockSpec((tm, LANES), lambda i: (proj(i) % nt, 0)),
                  pl.BlockSpec((ROT_HALF, tm), lambda i: (0, proj(i) % nt)),
                  pl.BlockSpec((ROT_HALF, tm), lambda i: (0, proj(i) % nt)),
                  _layer_spec((1, BRANCH), layer),
                  _layer_spec((8, LANES), layer)],
        out_specs=[pl.BlockSpec((tm, BRANCH), lambda i: (mix(i), 0)),
                   pl.BlockSpec((tm, BRANCH), lambda i: (mix(i), 0)),
                   pl.BlockSpec((tm, BRANCH), lambda i: (proj(i), 0)),
                   pl.BlockSpec((LANES, WINDOW), lambda i: (proj(i) // nt, 0)),
                   pl.BlockSpec((LANES, WINDOW), lambda i: (proj(i) // nt, 0)),
                   pl.BlockSpec((8, LANES), lambda i: (proj(i) // nt, 0)),
                   pl.BlockSpec((None, ML_HEADS, ML_DK, 2 * ML_DV), lambda i: (mix(i) // nt, 0, 0, 0))],
        out_shape=[jax.ShapeDtypeStruct((m, BRANCH), BF16),
                   jax.ShapeDtypeStruct((m, BRANCH), BF16),
                   jax.ShapeDtypeStruct((m, BRANCH), BF16),
                   jax.ShapeDtypeStruct((nb * LANES, WINDOW), F32),
                   jax.ShapeDtypeStruct((nb * LANES, WINDOW), F32),
                   jax.ShapeDtypeStruct((nb * 8, LANES), F32),
                   jax.ShapeDtypeStruct((nb, ML_HEADS, ML_DK, 2 * ML_DV), F32)],
        scratch_shapes=[pltpu.VMEM((2,) + shape, dtype) for _, shape, dtype in _SLOT_BUFFERS]
        + [pltpu.VMEM((ML_HEADS, ML_DK, 2 * ML_DV), F32), pltpu.VMEM((8, LANES), F32),
           pltpu.VMEM((LANES, WINDOW), BF16), pltpu.VMEM((WINDOW, LANES), BF16)],
        compiler_params=_params(("arbitrary",)),
        name="prompt_mixer",
    )(x2d, w_all, wt_all, bias_all, cosn, sinn, cost, sint, gain_all, sinks_all)


def _cross_scores(q, kt_ref):
    return [_mm(q[:, h * X_HD:(h + 1) * X_HD], kt_ref[h * X_HD:(h + 1) * X_HD, :]) for h in range(X_HEADS)]


def _cross_outputs(scores, v_ref):
    ones = jnp.ones((MEM_TOKENS, X_HD), BF16)
    out = []
    for h, s in enumerate(scores):
        s = s * (X_HD ** -0.5)
        e = jnp.exp(s - jnp.max(s, axis=1, keepdims=True)).astype(BF16)
        res = _mm(e, jnp.concatenate([v_ref[:, h * X_HD:(h + 1) * X_HD], ones], axis=1))
        out.append((res[:, :X_HD] * (1.0 / res[:, X_HD:])).astype(BF16))
    return jnp.concatenate(out, axis=1)


def _merge_ffn_kernel(*refs, cross_attend):
    if cross_attend:
        x_ref, ya_ref, yb_ref, xq_ref, mkt_ref, mv_ref = refs[:6]
        refs = refs[6:]
    else:
        x_ref, ya_ref, yb_ref, yc_ref = refs[:4]
        refs = refs[4:]
    wgl_ref, wbr_ref, wmix_ref, wfi_ref, wfo_ref, g1_ref, b1_ref, g2_ref, b2_ref, o_ref = refs
    tm = x_ref.shape[0]
    n_sub = 2 if tm >= TM_MERGE else 1
    halves = [slice(s * (tm // n_sub), (s + 1) * (tm // n_sub)) for s in range(n_sub)]
    x = [x_ref[s, :] for s in halves]
    xb = [v.astype(BF16) for v in x]
    if cross_attend:
        scores = [_cross_scores(xq_ref[s, :], mkt_ref) for s in halves]
    gates = [[_sigmoid(_mm(xb[i], wgl_ref[:, r * D_MODEL:(r + 1) * D_MODEL])) for r in range(3)]
             for i in range(n_sub)]
    if cross_attend:
        yc = [_cross_outputs(sc, mv_ref) for sc in scores]
    else:
        yc = [yc_ref[s, :] for s in halves]
    acc = []
    for i, s in enumerate(halves):
        tot = None
        for r, y in enumerate((ya_ref[s, :], yb_ref[s, :], yc[i])):
            term = gates[i][r] * _mm(y, wbr_ref[r])
            tot = term if tot is None else tot + term
        acc.append(tot.astype(BF16))
    x1 = [_layer_norm(DEEPNORM_ALPHA * x[i] + _mm(acc[i], wmix_ref[...]), g1_ref[...], b1_ref[...])
          for i in range(n_sub)]
    act = []
    for i in range(n_sub):
        x1b = x1[i].astype(BF16)
        gpre = _mm(x1b, wfi_ref[:, :D_FF])
        up = _mm(x1b, wfi_ref[:, D_FF:])
        act.append((gpre * _sigmoid(gpre) * up).astype(BF16))
    for i, s in enumerate(halves):
        o_ref[s, :] = _layer_norm(DEEPNORM_ALPHA * x1[i] + _mm(act[i], wfo_ref[...]), g2_ref[...], b2_ref[...])


def _merge_ffn(x2d, ya, yb, third, w_all, wbr, wmix, wfi, wfo, g1, b1, g2, b2, layer, tm, memory=None):
    m = x2d.shape[0]
    row = lambda i: (i, 0)
    vec = _layer_spec((1, D_MODEL), layer)
    mem_specs, mem_args = [], []
    if memory is not None:
        mkt, mv16, seq = memory
        per = seq // tm
        mem_specs = [pl.BlockSpec((None, BRANCH, MEM_TOKENS), lambda i: (i // per, 0, 0)),
                     pl.BlockSpec((MEM_TOKENS, BRANCH), lambda i: (i // per, 0))]
        mem_args = [mkt, mv16]
    return pl.pallas_call(
        functools.partial(_merge_ffn_kernel, cross_attend=memory is not None),
        grid=(m // tm,),
        in_specs=[pl.BlockSpec((tm, D_MODEL), row),
                  pl.BlockSpec((tm, BRANCH), row),
                  pl.BlockSpec((tm, BRANCH), row),
                  pl.BlockSpec((tm, BRANCH), row),
                  *mem_specs,
                  _layer_spec((D_MODEL, W_HALF), layer, 0, 0),
                  _layer_spec((3, BRANCH, D_MODEL), layer),
                  _layer_spec((D_MODEL, D_MODEL), layer),
                  _layer_spec((D_MODEL, 2 * D_FF), layer),
                  _layer_spec((D_FF, D_MODEL), layer),
                  vec, vec, vec, vec],
        out_specs=pl.BlockSpec((tm, D_MODEL), row),
        out_shape=jax.ShapeDtypeStruct((m, D_MODEL), F32),
        compiler_params=_params(("arbitrary",)),
        name="merge_ffn",
    )(x2d, ya, yb, third, *mem_args, w_all, wbr, wmix, wfi, wfo, g1, b1, g2, b2)


_P2_MK, _P2_SK, _P2_END = 0, 512, 640


def _dproj_kernel(x_ref, wn_ref, wt_ref, bias_ref, cos_ref, sin_ref, cost_ref, sint_ref, p_ref, pt_ref, p2_ref):
    xb = x_ref[...].astype(BF16)
    cos = cos_ref[...]
    sin = sin_ref[...]
    for c in range(_N_END // LANES):
        cs = slice(c * LANES, (c + 1) * LANES)
        blk = _mm(xb, wn_ref[:, cs])
        if _N_SQ <= c * LANES < _N_XQ:
            blk = _rope_lanes(blk, cos, sin)
        p_ref[:, cs] = blk
    pt_ref[_T_MV:_T_MK, :] = _mm_nt(wt_ref[_T_MV:_T_MK, :], xb)
    kt = _mm_nt(wt_ref[_T_MK:_T_SK, :], xb) * (ML_DK ** -0.5)
    pt_ref[_T_MK:_T_SK, :] = kt
    skt = _mm_nt(wt_ref[_T_SK:_T_SV, :], xb)
    cost = cost_ref[...]
    sint = sint_ref[...]
    for g in range(SWA_KV):
        base = g * SWA_HD
        for off, val in _rope_rows(skt[base:base + SWA_HD, :], cost, sint):
            pt_ref[_T_SK + base + off:_T_SK + base + off + val.shape[0], :] = val
    pt_ref[_T_SV:_T_G, :] = _mm_nt(wt_ref[_T_SV:_T_G, :], xb)
    pt_ref[_T_G:_T_END, :] = _mm_nt(wt_ref[_T_G:_T_END, :], xb) + bias_ref[...]
    for c in range(BRANCH // LANES):
        p2_ref[:, _P2_MK + c * LANES:_P2_MK + (c + 1) * LANES] = kt[c * LANES:(c + 1) * LANES, :].T
    p2_ref[:, _P2_SK:_P2_END] = pt_ref[_T_SK:_T_SV, :].T


def _dproj(xs, w_all, wt_all, bias_all, cos, sin, cost, sint, layer):
    n = xs.shape[0]
    whole = lambda shape: pl.BlockSpec(shape, lambda i: (0, 0))
    return pl.pallas_call(
        _dproj_kernel,
        grid=(1,),
        in_specs=[_const_spec((n, D_MODEL)),
                  _layer_spec((D_MODEL, W_HALF), layer, 0, 1),
                  _layer_spec((_T_END, D_MODEL), layer),
                  _layer_spec((16, 1), layer),
                  _const_spec((n, LANES)), _const_spec((n, LANES)),
                  _const_spec((ROT_HALF, n)), _const_spec((ROT_HALF, n))],
        out_specs=[whole((n, _N_END)), whole((_T_END, n)), whole((n, _P2_END))],
        out_shape=[jax.ShapeDtypeStruct((n, _N_END), F32),
                   jax.ShapeDtypeStruct((_T_END, n), F32),
                   jax.ShapeDtypeStruct((n, _P2_END), F32)],
        compiler_params=_params(("arbitrary",)),
        name="decode_proj",
    )(xs, w_all, wt_all, bias_all, cos, sin, cost, sint)


def _dmlstm_stream(i, q_ref, v_ref, mo_ref, k_ref, vt_ref, gt_ref, c_ref, n_ref, m_ref, gain_ref,
                   y_ref, c_out_ref, n_out_ref, m_out_ref):
    bb = q_ref.shape[0]
    nlanes = gt_ref.shape[1]
    li = gt_ref[0:ML_HEADS, :]
    lf = _log_sigmoid(gt_ref[8:8 + ML_HEADS, :])
    m_prev = m_ref[...]
    m_t = jnp.maximum(lf + m_prev, li)
    m_out_ref[...] = m_t
    scal = jnp.concatenate([jnp.exp(li - m_t), jnp.exp(lf + m_prev - m_t), jnp.exp(-m_t), jnp.zeros_like(m_t)], axis=0)
    bring = jnp.where(i == 0, 0, nlanes - i * bb)
    scal = pltpu.roll(scal, bring, axis=1)
    tiles = [(h, j) for h in range(ML_HEADS) for j in range(bb)]
    hs = lambda h: slice(h * ML_DK, (h + 1) * ML_DK)
    nrow_of = lambda h, j: slice(j * ML_HEADS + h, j * ML_HEADS + h + 1)
    q = {(h, j): q_ref[j:j + 1, hs(h)] for h, j in tiles}
    k = {(h, j): k_ref[j:j + 1, hs(h)] for h, j in tiles}
    w = {(h, j): scal[h:h + 1, j:j + 1] for h, j in tiles}
    a = {(h, j): scal[ML_HEADS + h:ML_HEADS + h + 1, j:j + 1] for h, j in tiles}
    cq = {(h, j): _mm_nt(jnp.broadcast_to(q[h, j], (8, ML_DK)).astype(BF16), c_ref[j, h].astype(BF16))[0:1, :]
          for h, j in tiles}
    yield
    qk = {t: jnp.sum(q[t] * k[t], axis=1, keepdims=True) for t in tiles}
    nq = {(h, j): jnp.sum(n_ref[nrow_of(h, j), :] * q[h, j], axis=1, keepdims=True) for h, j in tiles}
    yield
    hrow = {}
    for h, j in tiles:
        sw = qk[h, j] * w[h, j]
        floor = scal[2 * ML_HEADS + h:2 * ML_HEADS + h + 1, j:j + 1]
        den = jnp.maximum(jnp.abs(sw + a[h, j] * nq[h, j]), floor)
        hrow[h, j] = (sw * v_ref[j:j + 1, hs(h)] + a[h, j] * cq[h, j]) / den
        n_out_ref[nrow_of(h, j), :] = a[h, j] * n_ref[nrow_of(h, j), :] + w[h, j] * k[h, j]
    yield
    for h in range(ML_HEADS):
        vt = pltpu.roll(vt_ref[hs(h), :], bring, axis=1)
        for j in range(bb):
            c_out_ref[j, h] = a[h, j] * c_ref[j, h] + (w[h, j] * vt[:, j:j + 1]) * k[h, j]
        yield
    hc = {t: hrow[t] - jnp.mean(hrow[t], axis=1, keepdims=True) for t in tiles}
    yield
    var = {t: jnp.mean(hc[t] * hc[t], axis=1, keepdims=True) for t in tiles}
    yield
    for h, j in tiles:
        hn = hc[h, j] * lax.rsqrt(var[h, j] + HEAD_NORM_EPS) * gain_ref[:, hs(h)]
        y_ref[j:j + 1, hs(h)] = (_sigmoid(mo_ref[j:j + 1, hs(h)]) * hn).astype(BF16)


def _dswa_stream(i, q_ref, kn_ref, vn_ref, kvt_ref, ck_ref, cv_ref, sink_ref, y_ref, ko_ref, vo_ref):
    bb = q_ref.shape[0]
    lane = lax.broadcasted_iota(jnp.int32, (LANES, LANES), 1)
    row8 = lax.broadcasted_iota(jnp.int32, (SWA_HEADS, LANES), 0)
    low8 = lax.broadcasted_iota(jnp.int32, (SWA_HEADS, LANES), 1) < SWA_HD
    low1 = lax.broadcasted_iota(jnp.int32, (1, LANES), 1) < SWA_HD
    scale = SWA_HD ** -0.5
    sink = sink_ref[:, 0:1]
    qm, s = [], []
    for j in range(bb):
        rows = jnp.zeros((SWA_HEADS, LANES), F32)
        for pp in range(SWA_HEADS // 2):
            g = pp // 2
            pair = q_ref[j:j + 1, pp * LANES:(pp + 1) * LANES]
            swap = pltpu.roll(pair, SWA_HD, axis=1)
            in_g = low8 if g == 0 else jnp.logical_not(low8)
            for t in range(2):
                rows = jnp.where((row8 == 2 * pp + t) & in_g, pair if t == g else swap, rows)
        qm.append(rows)
        s.append(_mm(rows.astype(BF16), ck_ref[j].astype(BF16)) * scale)
    yield
    e, e_new, den = [], [], []
    for j in range(bb):
        s_new = jnp.sum(qm[j] * kn_ref[j:j + 1, :], axis=1, keepdims=True) * scale
        mx = jnp.maximum(jnp.maximum(jnp.max(s[j], axis=1, keepdims=True), s_new), sink)
        e.append(jnp.exp(s[j] - mx))
        e_new.append(jnp.exp(s_new - mx))
        den.append(jnp.sum(e[j], axis=1, keepdims=True) + e_new[j] + jnp.exp(sink - mx))
    yield
    for j in range(bb):
        o = (_mm_nt(e[j].astype(BF16), cv_ref[j].astype(BF16)) + e_new[j] * vn_ref[j:j + 1, :]) / den[j]
        for pp in range(SWA_HEADS // 2):
            g = pp // 2
            halves = []
            for t in range(2):
                oh = o[2 * pp + t:2 * pp + t + 1, :]
                halves.append(oh if t == g else pltpu.roll(oh, SWA_HD, axis=1))
            y_ref[j:j + 1, pp * LANES:(pp + 1) * LANES] = jnp.where(low1, halves[0], halves[1]).astype(BF16)
    yield
    knew_t = kvt_ref[0:LANES, :]
    vnew_t = kvt_ref[LANES:, :]
    for j in range(bb):
        bring = LANES - 1 - (i * bb + j)
        ko_ref[j] = jnp.where(lane == LANES - 1, pltpu.roll(knew_t, bring, axis=1),
                              pltpu.roll(ck_ref[j], LANES - 1, axis=1))
        vo_ref[j] = jnp.where(lane == LANES - 1, pltpu.roll(vnew_t, bring, axis=1),
                              pltpu.roll(cv_ref[j], LANES - 1, axis=1))
        if j % 2 == 1:
            yield


def _dcross_stream(q_ref, k_ref, v_ref, y_ref):
    bb = q_ref.shape[0]
    scale = X_HD ** -0.5
    row8 = lax.broadcasted_iota(jnp.int32, (8, LANES), 0) % X_HEADS
    ones = jnp.ones((X_HD, LANES), BF16)
    tiles, rows = k_ref.shape[1], k_ref.shape[1] * k_ref.shape[2]
    s = []
    for j in range(bb):
        qrep = jnp.zeros((8, LANES), F32)
        for h in range(X_HEADS):
            qrep = jnp.where(row8 == h, q_ref[j:j + 1, h * X_HD:(h + 1) * X_HD], qrep)
        prod = (k_ref[j] * qrep[None]).astype(BF16).reshape(rows, X_HD)
        s.append(_mm(prod, ones).reshape(tiles, 8, LANES) * scale)
        if j % 2 == 1:
            yield
    for j in range(bb):
        mx8 = jnp.max(s[j], axis=0)
        mx4 = jnp.maximum(mx8[0:X_HEADS], mx8[X_HEADS:])
        e = jnp.exp(s[j] - jnp.concatenate([mx4, mx4], axis=0)[None])
        den8 = jnp.sum(e, axis=0)
        o8 = jnp.sum(e * v_ref[j], axis=0)
        o4 = (o8[0:X_HEADS] + o8[X_HEADS:]) / (den8[0:X_HEADS] + den8[X_HEADS:])
        for h in range(X_HEADS):
            y_ref[j:j + 1, h * X_HD:(h + 1) * X_HD] = o4[h:h + 1, :].astype(BF16)
        if j % 2 == 1:
            yield


_N_DEC_IN = 20


def _decode_mixers_kernel(*refs):
    ins, outs = refs[:_N_DEC_IN], refs[-8:]
    (q_ref, v_ref, mo_ref, k_ref, vt_ref, gt_ref, c_ref, n_ref, m_ref, gain_ref,
     sq_ref, kn_ref, vn_ref, kvt_ref, ck_ref, cv_ref, sink_ref, xq_ref, mk_ref, mv_ref) = ins
    ya_ref, c_out_ref, n_out_ref, m_out_ref, yb_ref, ko_ref, vo_ref, yc_ref = outs
    i = pl.program_id(0)
    streams = [
        _dcross_stream(xq_ref, mk_ref, mv_ref, yc_ref),
        _dmlstm_stream(i, q_ref, v_ref, mo_ref, k_ref, vt_ref, gt_ref, c_ref, n_ref, m_ref, gain_ref,
                       ya_ref, c_out_ref, n_out_ref, m_out_ref),
        _dswa_stream(i, sq_ref, kn_ref, vn_ref, kvt_ref, ck_ref, cv_ref, sink_ref, yb_ref, ko_ref, vo_ref),
    ]
    while streams:
        for g in list(streams):
            if next(g, StopIteration) is StopIteration:
                streams.remove(g)


def _decode_mixers(p, p2, pt, c_all, n_all, mt_all, gain_all, ck_all, cv_all, sinks_all, mk_all, mv_all,
                   layer, stacks):
    n = p.shape[0]
    pblk = lambda width, col: pl.BlockSpec((BB, width), lambda i: (i, col // width))
    fixed = lambda rows, row0: pl.BlockSpec((rows, n), lambda i: (row0 // rows, 0), pipeline_mode=pl.Buffered(1))
    c_spec = pl.BlockSpec((None, BB, ML_HEADS, ML_DV, ML_DK), lambda i: (layer, i, 0, 0, 0))
    win = pl.BlockSpec((None, BB, LANES, WINDOW), lambda i: (layer, i, 0, 0))
    mem = pl.BlockSpec((None, BB, MEM_TOKENS * X_HEADS // 8, 8, X_HD), lambda i: (layer, i, 0, 0, 0))
    in_specs = [pblk(BRANCH, _N_MQ), pblk(BRANCH, _N_MV), pblk(BRANCH, _N_MO), pblk(BRANCH, _P2_MK),
                fixed(BRANCH, _T_MV), fixed(16, _T_G), c_spec,
                pl.BlockSpec((None, BB * ML_HEADS, ML_DK), lambda i: (layer, i, 0)),
                _layer_spec((ML_HEADS, n), layer), _layer_spec((1, BRANCH), layer),
                pblk(BRANCH, _N_SQ), pblk(LANES, _P2_SK), pblk(LANES, _N_SV), fixed(2 * LANES, _T_SK),
                win, win, _layer_spec((8, LANES), layer),
                pblk(BRANCH, _N_XQ), mem, mem]
    args = [p, p, p, p2, pt, pt, c_all, n_all, mt_all, gain_all,
            p, p2, p, pt, ck_all, cv_all, sinks_all, p, mk_all, mv_all]
    assert len(args) == _N_DEC_IN
    aliases = {}
    if stacks is not None:
        in_specs += [pl.BlockSpec(memory_space=pl.ANY)] * 3
        args += list(stacks)
        aliases = {_N_DEC_IN: 1, _N_DEC_IN + 1: 5, _N_DEC_IN + 2: 6}
    row = lambda width: pl.BlockSpec((BB, width), lambda i: (i, 0))
    return pl.pallas_call(
        _decode_mixers_kernel,
        grid=(n // BB,),
        in_specs=in_specs,
        out_specs=[row(BRANCH), c_spec,
                   pl.BlockSpec((BB * ML_HEADS, ML_DK), lambda i: (i, 0)),
                   pl.BlockSpec((ML_HEADS, n), lambda i: (0, 0)),
                   row(BRANCH), win, win, row(BRANCH)],
        out_shape=[jax.ShapeDtypeStruct((n, BRANCH), BF16),
                   jax.ShapeDtypeStruct((DEPTH, n, ML_HEADS, ML_DV, ML_DK), F32),
                   jax.ShapeDtypeStruct((n * ML_HEADS, ML_DK), F32),
                   jax.ShapeDtypeStruct((ML_HEADS, n), F32),
                   jax.ShapeDtypeStruct((n, BRANCH), BF16),
                   jax.ShapeDtypeStruct((DEPTH, n, LANES, WINDOW), F32),
                   jax.ShapeDtypeStruct((DEPTH, n, LANES, WINDOW), F32),
                   jax.ShapeDtypeStruct((n, BRANCH), BF16)],
        input_output_aliases=aliases,
        compiler_params=_params(("arbitrary",)),
        name="decode_mixers",
    )(*args)


def _rope_tables(positions):
    inv_freq = ROPE_THETA ** (-jnp.arange(ROT_HALF, dtype=F32) / ROT_HALF)
    ang = positions.astype(F32)[:, None] * inv_freq[None, :]
    cos = jnp.cos(ang)
    sin = jnp.sin(ang)
    reps = LANES // ROT_HALF
    return jnp.tile(cos, (1, reps)), jnp.tile(sin, (1, reps)), cos.T, sin.T


def kernel(x_prompt, x_sample, mem_prompt, cache_swa_k, cache_swa_v, cache_mem_k, cache_mem_v, state_mlstm_c, state_mlstm_n, state_mlstm_m, w_in, b_gates, mlstm_norm_g, swa_sinks, w_mem_kv, w_branch, w_mix_out, ln1_g, ln1_b, w_ffn_in, w_ffn_out, ln2_g, ln2_b):
    nb, seq, _ = x_prompt.shape
    ns = x_sample.shape[0]
    assert ns == LANES and PAST_LEN >= WINDOW

    cosn, sinn, cost, sint = _rope_tables(jnp.arange(seq))
    cos_s, sin_s, cost_s, sint_s = _rope_tables(jnp.full((ns,), PAST_LEN))

    ck_all = jnp.transpose(cache_swa_k, (0, 1, 3, 4, 2)).reshape(DEPTH, ns, SWA_KV * SWA_HD, WINDOW)
    cv_all = jnp.transpose(cache_swa_v, (0, 1, 3, 4, 2)).reshape(DEPTH, ns, SWA_KV * SWA_HD, WINDOW)
    mk_all = cache_mem_k.reshape(DEPTH, ns, MEM_TOKENS * X_HEADS // 8, 8, X_HD)
    mv_all = cache_mem_v.reshape(DEPTH, ns, MEM_TOKENS * X_HEADS // 8, 8, X_HD)
    n_all = state_mlstm_n.reshape(DEPTH, ns * ML_HEADS, ML_DK)
    mt_all = jnp.transpose(state_mlstm_m, (0, 2, 1))

    w_all, wt_all = _relayout_w_in(w_in)
    z4 = jnp.zeros((DEPTH, 4), F32)
    bias_all = jnp.concatenate([b_gates[:, :ML_HEADS], z4, b_gates[:, ML_HEADS:], z4], axis=1)[..., None]
    gain_all = mlstm_norm_g[:, None, :]
    sinks_all = jnp.broadcast_to(swa_sinks[:, :, None], (DEPTH, SWA_HEADS, LANES))
    wkv_all = w_mem_kv.astype(BF16)
    wkt_all = jnp.swapaxes(w_mem_kv[..., :BRANCH], 1, 2).astype(BF16)
    wbr_all = w_branch.astype(BF16)
    wmix_all = w_mix_out.astype(BF16)
    wfi_all = w_ffn_in.astype(BF16)
    wfo_all = w_ffn_out.astype(BF16)
    ln_all = (ln1_g[:, None, :], ln1_b[:, None, :], ln2_g[:, None, :], ln2_b[:, None, :])

    yp = x_prompt.reshape(nb * seq, D_MODEL)
    ys = x_sample.reshape(ns, D_MODEL)
    mem2d = mem_prompt.reshape(nb * MEM_TOKENS, D_MODEL)

    outs = {k: [] for k in ("kp", "vp", "mk", "mv", "cp", "np", "mp", "ns", "ms")}
    c_stack = k_stack = v_stack = None
    for l in range(DEPTH):
        mk32, mv32, mkt, mv16 = _memkv(mem2d, wkv_all, wkt_all, l, nb)
        ya, yb, xq, k32, v32, m_fin, s_fin = _mixer(yp, w_all, wt_all, bias_all, cosn, sinn, cost, sint,
                                                    gain_all, sinks_all, l, nb, seq)
        yp = _merge_ffn(yp, ya, yb, xq, w_all, wbr_all, wmix_all, wfi_all, wfo_all, *ln_all, l, TM_MERGE,
                        memory=(mkt, mv16, seq))
        outs["kp"].append(jnp.transpose(k32.reshape(nb, SWA_KV, SWA_HD, WINDOW), (0, 3, 1, 2)))
        outs["vp"].append(jnp.transpose(v32.reshape(nb, SWA_KV, SWA_HD, WINDOW), (0, 3, 1, 2)))
        outs["mk"].append(mk32.reshape(nb, MEM_TOKENS, X_HEADS, X_HD))
        outs["mv"].append(mv32.reshape(nb, MEM_TOKENS, X_HEADS, X_HD))
        outs["cp"].append(jnp.swapaxes(s_fin[..., :ML_DV], -1, -2))
        outs["np"].append(s_fin[..., ML_DV])
        outs["mp"].append(m_fin.reshape(nb, 8, LANES)[:, :ML_HEADS, 0])

        p, pt, p2 = _dproj(ys, w_all, wt_all, bias_all, cos_s, sin_s, cost_s, sint_s, l)
        ya_s, c_stack, n_new, m_new, yb_s, k_stack, v_stack, yc_s = _decode_mixers(
            p, p2, pt, state_mlstm_c, n_all, mt_all, gain_all, ck_all, cv_all, sinks_all, mk_all, mv_all, l,
            None if l == 0 else (c_stack, k_stack, v_stack))
        ys = _merge_ffn(ys, ya_s, yb_s, yc_s, w_all, wbr_all, wmix_all, wfi_all, wfo_all, *ln_all, l, ns)
        outs["ns"].append(n_new.reshape(ns, ML_HEADS, ML_DK))
        outs["ms"].append(m_new)

    st = {k: jnp.stack(vals) for k, vals in outs.items()}
    window_out = lambda t: jnp.transpose(t.reshape(DEPTH, ns, SWA_KV, SWA_HD, WINDOW), (0, 1, 4, 2, 3))
    return (yp.reshape(nb, seq, D_MODEL), ys.reshape(ns, 1, D_MODEL),
            st["kp"], st["vp"], window_out(k_stack), window_out(v_stack), st["mk"], st["mv"],
            st["cp"], st["np"], st["mp"], c_stack, st["ns"], jnp.transpose(st["ms"], (0, 2, 1)))
```

```python
import functools

import jax
import jax.numpy as jnp
from jax import lax
from jax.experimental import pallas as pl
from jax.experimental.pallas import tpu as pltpu

F32 = jnp.float32
BF16 = jnp.bfloat16

D_MODEL = 1024
DEPTH = 2
BRANCH = 512
ML_HEADS = 4
ML_DK = 128
ML_DV = 128
ML_CHUNK = 128
SWA_HD = 64
SWA_HEADS = 8
SWA_KV = 2
WINDOW = 128
ROT_DIM = 16
ROT_HALF = 8
ROPE_THETA = 500000.0
MEM_TOKENS = 256
X_HEADS = 4
X_HD = 128
D_FF = 2816
LN_EPS = 1e-5
HEAD_NORM_EPS = 1e-6
DEEPNORM_ALPHA = (2 * DEPTH) ** 0.25
NEG_INF = -1e30
PAST_LEN = 8192

LANES = 128
VMEM_LIMIT = 56 * 1024 * 1024

_C_MQ, _C_MK, _C_MV, _C_MO = 0, 512, 1024, 1536
_C_MI, _C_MF = 2048, 2052
_C_SQ, _C_SK, _C_SV, _C_XQ, _C_GL = 2056, 2568, 2696, 2824, 3336

W_HALF = 3 * D_MODEL
_N_MQ, _N_MV, _N_MO, _N_SQ, _N_XQ, _N_SV, _N_END = 0, 512, 1024, 1536, 2048, 2560, 2688
_T_MV, _T_MK, _T_SK, _T_SV, _T_G, _T_END = 0, 512, 1024, 1152, 1280, 1296

TM_PROJ = 512
TM_MERGE = 512
BB = 8
ML_GROUP = 4


def _mm(a, b):
    return jnp.dot(a, b, preferred_element_type=F32)


def _mm_nt(a, b):
    return lax.dot_general(a, b, (((1,), (1,)), ((), ())), preferred_element_type=F32)


def _sigmoid(x):
    return 1.0 / (1.0 + jnp.exp(-x))


def _log_sigmoid(x):
    return jnp.minimum(x, 0.0) - jnp.log(1.0 + jnp.exp(-jnp.abs(x)))


def _layer_norm(x, g, b):
    mu = jnp.mean(x, axis=-1, keepdims=True)
    xc = x - mu
    var = jnp.mean(xc * xc, axis=-1, keepdims=True)
    return xc * lax.rsqrt(var + LN_EPS) * g + b


def _rope_lanes(x, cos, sin):
    lane = lax.broadcasted_iota(jnp.int32, x.shape, 1) % SWA_HD
    up = pltpu.roll(x, LANES - ROT_HALF, axis=1)
    dn = pltpu.roll(x, ROT_HALF, axis=1)
    first = x * cos - up * sin
    second = x * cos + dn * sin
    return jnp.where(lane < ROT_HALF, first, jnp.where(lane < ROT_DIM, second, x))


def _rope_rows(xt, cost, sint):
    x1 = xt[0:ROT_HALF, :]
    x2 = xt[ROT_HALF:ROT_DIM, :]
    return ((0, x1 * cost - x2 * sint), (ROT_HALF, x2 * cost + x1 * sint), (ROT_DIM, xt[ROT_DIM:SWA_HD, :]))


def _const_spec(shape):
    nd = len(shape)
    return pl.BlockSpec(shape, lambda *_: (0,) * nd, pipeline_mode=pl.Buffered(1))


def _layer_spec(shape, layer, *tail):
    idx = (layer,) + (tail if tail else (0,) * len(shape))
    return pl.BlockSpec((None,) + tuple(shape), lambda *_: idx, pipeline_mode=pl.Buffered(1))


def _params(sem):
    return pltpu.CompilerParams(dimension_semantics=sem, vmem_limit_bytes=VMEM_LIMIT)


def _relayout_kernel(w_ref, wn_ref, wt_ref):
    half = pl.program_id(1)

    def put(dst, lo, hi):
        for c in range((hi - lo) // LANES):
            tile = w_ref[lo + c * LANES:lo + (c + 1) * LANES, :]
            wn_ref[:, dst + c * LANES:dst + (c + 1) * LANES] = tile.T.astype(BF16)

    @pl.when(half == 0)
    def _():
        put(0, _C_GL, _C_GL + W_HALF)

    @pl.when(half == 1)
    def _():
        for dst, lo, hi in ((_N_MQ, _C_MQ, _C_MK), (_N_MV, _C_MV, _C_MI), (_N_SQ, _C_SQ, _C_SK),
                            (_N_XQ, _C_XQ, _C_GL), (_N_SV, _C_SV, _C_XQ)):
            put(dst, lo, hi)
        wn_ref[:, _N_END:] = jnp.zeros((wn_ref.shape[0], W_HALF - _N_END), BF16)
        for dst, lo, hi in ((_T_MV, _C_MV, _C_MO), (_T_MK, _C_MK, _C_MV), (_T_SK, _C_SK, _C_XQ)):
            wt_ref[dst:dst + hi - lo, :] = w_ref[lo:hi, :].astype(BF16)
        z4 = jnp.zeros((4, w_ref.shape[1]), F32)
        gates = jnp.concatenate([w_ref[_C_MI:_C_MF, :], z4, w_ref[_C_MF:_C_SQ, :], z4], axis=0)
        wt_ref[_T_G:_T_END, :] = gates.astype(BF16)


def _relayout_w_in(w_in):
    depth, rows, cols = w_in.shape
    wt_view = jnp.swapaxes(w_in, 1, 2)
    return pl.pallas_call(
        _relayout_kernel,
        grid=(depth, 2),
        in_specs=[pl.BlockSpec((None, cols, rows), lambda l, j: (l, 0, 0), pipeline_mode=pl.Buffered(1))],
        out_specs=[pl.BlockSpec((None, rows, W_HALF), lambda l, j: (l, 0, j)),
                   pl.BlockSpec((None, _T_END, rows), lambda l, j: (l, 0, 0))],
        out_shape=[jax.ShapeDtypeStruct((depth, rows, 2 * W_HALF), BF16),
                   jax.ShapeDtypeStruct((depth, _T_END, rows), BF16)],
        compiler_params=_params(("arbitrary", "arbitrary")),
        name="relayout_w_in",
    )(wt_view)


def _memkv_kernel(mem_ref, wt_ref, k32_ref, v32_ref, kt_ref, v16_ref):
    m = mem_ref[...].astype(BF16)
    kv = _mm_nt(m, wt_ref[...].astype(BF16))
    k32_ref[...] = kv[:, :BRANCH]
    v32_ref[...] = kv[:, BRANCH:]
    v16_ref[...] = kv[:, BRANCH:].astype(BF16)
    kt_ref[...] = kv[:, :BRANCH].T.astype(BF16)


def _memkv(mem2d, w_mem_kv, nb):
    depth = w_mem_kv.shape[0]
    rows = mem2d.shape[0]
    wt_view = jnp.swapaxes(w_mem_kv, 1, 2)
    tok = lambda l, b: (l, b, 0)
    return pl.pallas_call(
        _memkv_kernel,
        grid=(depth, nb),
        in_specs=[pl.BlockSpec((MEM_TOKENS, D_MODEL), lambda l, b: (b, 0)),
                  pl.BlockSpec((None, 2 * BRANCH, D_MODEL), lambda l, b: (l, 0, 0), pipeline_mode=pl.Buffered(1))],
        out_specs=[pl.BlockSpec((None, MEM_TOKENS, BRANCH), tok),
                   pl.BlockSpec((None, MEM_TOKENS, BRANCH), tok),
                   pl.BlockSpec((None, None, BRANCH, MEM_TOKENS), lambda l, b: (l, b, 0, 0)),
                   pl.BlockSpec((None, MEM_TOKENS, BRANCH), tok)],
        out_shape=[jax.ShapeDtypeStruct((depth, rows, BRANCH), F32),
                   jax.ShapeDtypeStruct((depth, rows, BRANCH), F32),
                   jax.ShapeDtypeStruct((depth, nb, BRANCH, MEM_TOKENS), BF16),
                   jax.ShapeDtypeStruct((depth, rows, BRANCH), BF16)],
        compiler_params=_params(("arbitrary", "arbitrary")),
        name="memkv_proj",
    )(mem2d, wt_view)


def _mlstm_gate_weights(pre, gt_ref, gc_ref, m_out_ref, m_scr, valid):
    tm = pre.shape[1]
    li = pre[0:8, :]
    lane8 = lax.broadcasted_iota(jnp.int32, li.shape, 1) % ML_CHUNK
    b = _log_sigmoid(pre[8:16, :])
    shift = 1
    while shift < ML_CHUNK:
        b = b + jnp.where(lane8 >= shift, pltpu.roll(b, shift, axis=1), 0.0)
        shift *= 2
    g = li - b
    cm = g
    shift = 1
    while shift < ML_CHUNK:
        cm = jnp.maximum(cm, jnp.where(lane8 >= shift, pltpu.roll(cm, shift, axis=1), -jnp.inf))
        shift *= 2
    gt_ref[0:8, :] = g
    pad = jnp.zeros((LANES - 24, ML_CHUNK), F32)
    m_start = m_scr[...]
    m_prev = m_start
    for c in range(tm // ML_CHUNK):
        cs = slice(c * ML_CHUNK, (c + 1) * ML_CHUNK)
        b_c = b[:, cs]
        b_last = jnp.broadcast_to(b_c[:, ML_CHUNK - 1:ML_CHUNK], b_c.shape)
        cm_last = jnp.broadcast_to(cm[:, cs][:, ML_CHUNK - 1:ML_CHUNK], b_c.shape)
        m_t = b_c + jnp.maximum(m_prev, cm[:, cs])
        m_new = b_last + jnp.maximum(m_prev, cm_last)
        gt_ref[8:16, cs] = jnp.exp(b_last + g[:, cs] - m_new)
        gt_ref[16:24, cs] = jnp.exp(b_last + m_prev - m_new)
        rows = jnp.concatenate([b_c - m_t, jnp.exp(b_c + m_prev - m_t), jnp.exp(-m_t), pad], axis=0)
        gc_ref[cs, :] = rows.T
        m_prev = m_new
    m_prev = jnp.where(valid, m_prev, m_start)
    m_scr[...] = m_prev
    m_out_ref[...] = m_prev


def _proj_stream(x_ref, wn_ref, wt_ref, bias_ref, cosn_ref, sinn_ref, cost_ref, sint_ref, dst,
                 xq_ref, k32_ref, v32_ref, m_out_ref, m_scr, valid):
    tm = x_ref.shape[0]
    xb = x_ref[...].astype(BF16)
    tr_rows = lambda lo, hi: _mm_nt(wt_ref[lo:hi, :], xb)
    _mlstm_gate_weights(tr_rows(_T_G, _T_END) + bias_ref[...], dst["gt"], dst["gc"], m_out_ref, m_scr, valid)
    yield
    dst["q"][...] = _mm(xb, wn_ref[:, _N_MQ:_N_MV]).astype(BF16)
    yield
    dst["v"][...] = _mm(xb, wn_ref[:, _N_MV:_N_MO]).astype(BF16)
    yield
    dst["mo"][...] = _mm(xb, wn_ref[:, _N_MO:_N_SQ])
    yield
    xq_ref[...] = _mm(xb, wn_ref[:, _N_XQ:_N_SV]).astype(BF16)
    yield
    cosn = cosn_ref[...]
    sinn = sinn_ref[...]
    sq = _mm(xb, wn_ref[:, _N_SQ:_N_XQ])
    for c in range(BRANCH // LANES):
        blk = _rope_lanes(sq[:, c * LANES:(c + 1) * LANES], cosn, sinn)
        dst["sq"][:, c * LANES:(c + 1) * LANES] = blk.astype(BF16)
    yield
    dst["sv"][...] = _mm(xb, wn_ref[:, _N_SV:_N_END]).astype(BF16)
    dst["kt"][...] = (tr_rows(_T_MK, _T_SK) * (ML_DK ** -0.5)).astype(BF16)
    yield
    skt = tr_rows(_T_SK, _T_SV)
    svt = tr_rows(_T_SV, _T_G)
    cost = cost_ref[...]
    sint = sint_ref[...]
    tail = slice(tm - WINDOW, tm)
    v32_ref[...] = svt[:, tail]
    for g in range(SWA_KV):
        base = g * SWA_HD
        for off, val in _rope_rows(skt[base:base + SWA_HD, :], cost, sint):
            dst["skt"][base + off:base + off + val.shape[0], :] = val.astype(BF16)
            k32_ref[base + off:base + off + val.shape[0], :] = val[:, tail]


def _mlstm_stream(q_ref, v_ref, kt_ref, mo_ref, gt_ref, gc_ref, gain_ref, y_ref, s_ref):
    L = ML_CHUNK
    r_i = lax.broadcasted_iota(jnp.int32, (L, L), 0)
    c_i = lax.broadcasted_iota(jnp.int32, (L, L), 1)
    causal = c_i <= r_i
    ones = jnp.ones((L, ML_DV), BF16)

    ts = lambda c: slice(c * L, (c + 1) * L)
    hs = lambda h: slice(h * ML_DK, (h + 1) * ML_DK)
    n_chunks = q_ref.shape[0] // L

    for c0 in range(0, n_chunks, ML_GROUP):
        chunks = range(c0, min(c0 + ML_GROUP, n_chunks))
        units = [(c, h) for c in chunks for h in range(ML_HEADS)]

        qk = {(c, h): _mm(q_ref[ts(c), hs(h)], kt_ref[hs(h), ts(c)]) for c, h in units}
        yield

        sw, kts, vext = {}, {}, {}
        for c, h in units:
            g_r = gt_ref[h:h + 1, ts(c)]
            es_r = gt_ref[8 + h:9 + h, ts(c)]
            u_c = gc_ref[ts(c), h:h + 1]
            sw[c, h] = (qk[c, h] * jnp.exp(jnp.where(causal, u_c + g_r, -jnp.inf))).astype(BF16)
            kts[c, h] = (kt_ref[hs(h), ts(c)].astype(F32) * es_r).astype(BF16)
            vext[c, h] = jnp.concatenate([v_ref[ts(c), hs(h)], ones], axis=1)
        yield

        intra = {u: _mm(sw[u], vext[u]) for u in units}
        delta = {u: _mm(kts[u], vext[u]) for u in units}
        yield

        s_in = {}
        for h in range(ML_HEADS):
            state = s_ref[h]
            for c in chunks:
                s_in[c, h] = state.astype(BF16)
                state = gt_ref[16 + h:17 + h, c * L:c * L + 1] * state + delta[c, h]
            s_ref[h] = state
        qs_all = {u: _mm(q_ref[ts(u[0]), hs(u[1])], s_in[u]) for u in units}
        yield

        hh, hc = {}, {}
        for c, h in units:
            tot = intra[c, h] + gc_ref[ts(c), 8 + h:9 + h] * qs_all[c, h]
            floor = gc_ref[ts(c), 16 + h:17 + h]
            hh[c, h] = tot[:, :ML_DV] * (1.0 / jnp.maximum(jnp.abs(tot[:, ML_DV:]), floor))
        for u in units:
            hc[u] = hh[u] - jnp.mean(hh[u], axis=1, keepdims=True)
        for c, h in units:
            var = jnp.mean(hc[c, h] * hc[c, h], axis=1, keepdims=True)
            hn = hc[c, h] * lax.rsqrt(var + HEAD_NORM_EPS) * gain_ref[:, hs(h)]
            y_ref[ts(c), hs(h)] = (_sigmoid(mo_ref[ts(c), hs(h)]) * hn).astype(BF16)
        yield


def _swa_stream(q_ref, ktp_ref, ktc_ref, vp_ref, vc_ref, sink_ref, y_ref, seq_start):
    L = WINDOW
    nblk = q_ref.shape[0] // L
    r_i = lax.broadcasted_iota(jnp.int32, (L, 2 * L), 0)
    c_i = lax.broadcasted_iota(jnp.int32, (L, 2 * L), 1)
    band = (c_i >= r_i) & (c_i <= r_i + L)
    first = band & (c_i >= jnp.where(seq_start, L, 0))
    low_half = lax.broadcasted_iota(jnp.int32, (2 * L, LANES), 1) < SWA_HD
    out_low = lax.broadcasted_iota(jnp.int32, (L, LANES), 1) < SWA_HD
    zeros_k = jnp.zeros((SWA_HD, 2 * L), BF16)
    ones_lo = jnp.where(low_half, 1.0, 0.0).astype(BF16)
    ones_hi = jnp.where(low_half, 0.0, 1.0).astype(BF16)

    kt_all = jnp.concatenate([ktp_ref[...], ktc_ref[...]], axis=1)
    v_all = jnp.concatenate([vp_ref[...], vc_ref[...]], axis=0).astype(F32)
    v_swap = pltpu.roll(v_all, SWA_HD, axis=1)

    def scores(c):
        win = slice(c * L, (c + 2) * L)
        out = []
        for g in range(SWA_KV):
            kt2 = kt_all[g * SWA_HD:(g + 1) * SWA_HD, win]
            kblk = jnp.concatenate([jnp.concatenate([kt2, zeros_k], axis=0),
                                    jnp.concatenate([zeros_k, kt2], axis=0)], axis=1)
            for pp in range(2 * g, 2 * g + 2):
                out.append(_mm(q_ref[c * L:(c + 1) * L, pp * LANES:(pp + 1) * LANES], kblk))
        return out

    def weights(c, s_list):
        allowed = first if c == 0 else band
        out = []
        for head in range(SWA_HEADS):
            s = s_list[head // 2][:, (head % 2) * 2 * L:(head % 2 + 1) * 2 * L]
            sc = jnp.where(allowed, s * (SWA_HD ** -0.5), NEG_INF)
            sink = sink_ref[head:head + 1, 0:1]
            mx = jnp.broadcast_to(jnp.maximum(jnp.max(sc, axis=1, keepdims=True), sink), sc.shape)
            out.append((jnp.exp(sc - mx).astype(BF16), jnp.exp(sink - mx[:, :LANES])))
        return out

    def outputs(c, e_list):
        win = slice(c * L, (c + 2) * L)
        v2 = v_all[win, :]
        v2s = v_swap[win, :]
        for g in range(SWA_KV):
            va = jnp.where(low_half, v2 if g == 0 else v2s, 0.0).astype(BF16)
            vb = jnp.where(low_half, 0.0, v2s if g == 0 else v2).astype(BF16)
            vden = jnp.concatenate([jnp.concatenate([va, ones_lo], axis=1),
                                    jnp.concatenate([vb, ones_hi], axis=1)], axis=0)
            for pp in range(2 * g, 2 * g + 2):
                (e0, k0), (e1, k1) = e_list[2 * pp], e_list[2 * pp + 1]
                res = _mm(jnp.concatenate([e0, e1], axis=1), vden)
                den = res[:, LANES:] + jnp.where(out_low, k0, k1)
                y_ref[c * L:(c + 1) * L, pp * LANES:(pp + 1) * LANES] = (res[:, :LANES] * (1.0 / den)).astype(BF16)

    s_next = scores(0)
    yield
    for c in range(nblk):
        s_cur = s_next
        if c + 1 < nblk:
            s_next = scores(c + 1)
        outputs(c, weights(c, s_cur))
        yield


_SLOT_BUFFERS = (("q", (TM_PROJ, BRANCH), BF16), ("v", (TM_PROJ, BRANCH), BF16), ("mo", (TM_PROJ, BRANCH), F32),
                 ("sq", (TM_PROJ, BRANCH), BF16), ("sv", (TM_PROJ, LANES), BF16), ("kt", (BRANCH, TM_PROJ), BF16),
                 ("skt", (LANES, TM_PROJ), BF16), ("gt", (24, TM_PROJ), F32), ("gc", (TM_PROJ, LANES), F32))


def _mixer_kernel(x_ref, wn_ref, wt_ref, bias_ref, cosn_ref, sinn_ref, cost_ref, sint_ref, gain_ref, sink_ref,
                  ya_ref, yb_ref, xq_ref, k32_ref, v32_ref, m_out_ref, s_out_ref, *scratch, n_tiles, tiles_per_seq):
    slots = {name: ref for (name, _, _), ref in zip(_SLOT_BUFFERS, scratch)}
    s_ref, m_scr, ktp_ref, vp_ref = scratch[len(_SLOT_BUFFERS):]
    i = pl.program_id(0)
    wr = i % 2
    mix_tile = i - 1
    valid = i < n_tiles

    @pl.when(i == 0)
    def _():
        for ref in scratch:
            ref[...] = jnp.zeros_like(ref)

    @pl.when((i % tiles_per_seq == 0) & valid)
    def _():
        m_scr[...] = jnp.zeros_like(m_scr)

    seq_start = mix_tile % tiles_per_seq == 0

    @pl.when(seq_start)
    def _():
        s_ref[...] = jnp.zeros_like(s_ref)

    dst = {name: ref.at[wr] for name, ref in slots.items()}
    src = {name: ref.at[1 - wr] for name, ref in slots.items()}
    streams = [
        _proj_stream(x_ref, wn_ref, wt_ref, bias_ref, cosn_ref, sinn_ref, cost_ref, sint_ref, dst,
                     xq_ref, k32_ref, v32_ref, m_out_ref, m_scr, valid),
        _mlstm_stream(src["q"], src["v"], src["kt"], src["mo"], src["gt"], src["gc"], gain_ref, ya_ref, s_ref),
        _swa_stream(src["sq"], ktp_ref, src["skt"], vp_ref, src["sv"], sink_ref, yb_ref, seq_start),
    ]
    while streams:
        for g in list(streams):
            if next(g, StopIteration) is StopIteration:
                streams.remove(g)

    tm = x_ref.shape[0]
    ktp_ref[...] = src["skt"][:, tm - WINDOW:]
    vp_ref[...] = src["sv"][tm - WINDOW:, :]

    @pl.when(mix_tile % tiles_per_seq == tiles_per_seq - 1)
    def _():
        s_out_ref[...] = s_ref[...]


def _mixer(x2d, w_all, wt_all, bias_all, cosn, sinn, cost, sint, gain_all, sinks_all, layer, nb, seq):
    m = x2d.shape[0]
    tm = TM_PROJ
    nt = seq // tm
    n = m // tm
    proj = lambda i: jnp.minimum(i, n - 1)
    mix = lambda i: jnp.maximum(i - 1, 0)
    return pl.pallas_call(
        functools.partial(_mixer_kernel, n_tiles=n, tiles_per_seq=nt),
        grid=(n + 1,),
        in_specs=[pl.BlockSpec((tm, D_MODEL), lambda i: (proj(i), 0)),
                  _layer_spec((D_MODEL, W_HALF), layer, 0, 1),
                  _layer_spec((_T_END, D_MODEL), layer),
                  _layer_spec((16, 1), layer),
                  pl.BlockSpec((tm, LANES), lambda i: (proj(i) % nt, 0)),
                  pl.BlockSpec((tm, LANES), lambda i: (proj(i) % nt, 0)),
                  pl.BlockSpec((ROT_HALF, tm), lambda i: (0, proj(i) % nt)),
                  pl.BlockSpec((ROT_HALF, tm), lambda i: (0, proj(i) % nt)),
                  _layer_spec((1, BRANCH), layer),
                  _layer_spec((8, LANES), layer)],
        out_specs=[pl.BlockSpec((tm, BRANCH), lambda i: (mix(i), 0)),
                   pl.BlockSpec((tm, BRANCH), lambda i: (mix(i), 0)),
                   pl.BlockSpec((tm, BRANCH), lambda i: (proj(i), 0)),
                   pl.BlockSpec((LANES, WINDOW), lambda i: (proj(i) // nt, 0)),
                   pl.BlockSpec((LANES, WINDOW), lambda i: (proj(i) // nt, 0)),
                   pl.BlockSpec((8, LANES), lambda i: (proj(i) // nt, 0)),
                   pl.BlockSpec((None, ML_HEADS, ML_DK, 2 * ML_DV), lambda i: (mix(i) // nt, 0, 0, 0))],
        out_shape=[jax.ShapeDtypeStruct((m, BRANCH), BF16),
                   jax.ShapeDtypeStruct((m, BRANCH), BF16),
                   jax.ShapeDtypeStruct((m, BRANCH), BF16),
                   jax.ShapeDtypeStruct((nb * LANES, WINDOW), F32),
                   jax.ShapeDtypeStruct((nb * LANES, WINDOW), F32),
                   jax.ShapeDtypeStruct((nb * 8, LANES), F32),
                   jax.ShapeDtypeStruct((nb, ML_HEADS, ML_DK, 2 * ML_DV), F32)],
        scratch_shapes=[pltpu.VMEM((2,) + shape, dtype) for _, shape, dtype in _SLOT_BUFFERS]
        + [pltpu.VMEM((ML_HEADS, ML_DK, 2 * ML_DV), F32), pltpu.VMEM((8, LANES), F32),
           pltpu.VMEM((LANES, WINDOW), BF16), pltpu.VMEM((WINDOW, LANES), BF16)],
        compiler_params=_params(("arbitrary",)),
        name="prompt_mixer",
    )(x2d, w_all, wt_all, bias_all, cosn, sinn, cost, sint, gain_all, sinks_all)


def _cross_scores(q, kt_ref):
    return [_mm(q[:, h * X_HD:(h + 1) * X_HD], kt_ref[h * X_HD:(h + 1) * X_HD, :]) for h in range(X_HEADS)]


def _cross_outputs(scores, v_ref):
    ones = jnp.ones((MEM_TOKENS, X_HD), BF16)
    out = []
    for h, s in enumerate(scores):
        s = s * (X_HD ** -0.5)
        e = jnp.exp(s - jnp.max(s, axis=1, keepdims=True)).astype(BF16)
        res = _mm(e, jnp.concatenate([v_ref[:, h * X_HD:(h + 1) * X_HD], ones], axis=1))
        out.append((res[:, :X_HD] * (1.0 / res[:, X_HD:])).astype(BF16))
    return jnp.concatenate(out, axis=1)


def _merge_ffn_kernel(*refs, cross_attend):
    if cross_attend:
        x_ref, ya_ref, yb_ref, xq_ref, mkt_ref, mv_ref = refs[:6]
        refs = refs[6:]
    else:
        x_ref, ya_ref, yb_ref, yc_ref = refs[:4]
        refs = refs[4:]
    wgl_ref, wbr_ref, wmix_ref, wfi_ref, wfo_ref, g1_ref, b1_ref, g2_ref, b2_ref, o_ref = refs
    tm = x_ref.shape[0]
    n_sub = 2 if tm >= TM_MERGE else 1
    halves = [slice(s * (tm // n_sub), (s + 1) * (tm // n_sub)) for s in range(n_sub)]
    x = [x_ref[s, :] for s in halves]
    xb = [v.astype(BF16) for v in x]
    if cross_attend:
        scores = [_cross_scores(xq_ref[s, :], mkt_ref) for s in halves]
    gates = [[_sigmoid(_mm(xb[i], wgl_ref[:, r * D_MODEL:(r + 1) * D_MODEL])) for r in range(3)]
             for i in range(n_sub)]
    if cross_attend:
        yc = [_cross_outputs(sc, mv_ref) for sc in scores]
    else:
        yc = [yc_ref[s, :] for s in halves]
    acc = []
    for i, s in enumerate(halves):
        tot = None
        for r, y in enumerate((ya_ref[s, :], yb_ref[s, :], yc[i])):
            term = gates[i][r] * _mm(y, wbr_ref[r])
            tot = term if tot is None else tot + term
        acc.append(tot.astype(BF16))
    x1 = [_layer_norm(DEEPNORM_ALPHA * x[i] + _mm(acc[i], wmix_ref[...]), g1_ref[...], b1_ref[...])
          for i in range(n_sub)]
    act = []
    for i in range(n_sub):
        x1b = x1[i].astype(BF16)
        gpre = _mm(x1b, wfi_ref[:, :D_FF])
        up = _mm(x1b, wfi_ref[:, D_FF:])
        act.append((gpre * _sigmoid(gpre) * up).astype(BF16))
    for i, s in enumerate(halves):
        o_ref[s, :] = _layer_norm(DEEPNORM_ALPHA * x1[i] + _mm(act[i], wfo_ref[...]), g2_ref[...], b2_ref[...])


def _merge_ffn(x2d, ya, yb, third, w_all, wbr, wmix, wfi, wfo, g1, b1, g2, b2, layer, tm, memory=None):
    m = x2d.shape[0]
    row = lambda i: (i, 0)
    vec = _layer_spec((1, D_MODEL), layer)
    mem_specs, mem_args = [], []
    if memory is not None:
        mkt, mv16, seq = memory
        per = seq // tm
        mem_specs = [pl.BlockSpec((None, None, BRANCH, MEM_TOKENS), lambda i: (layer, i // per, 0, 0)),
                     pl.BlockSpec((None, MEM_TOKENS, BRANCH), lambda i: (layer, i // per, 0))]
        mem_args = [mkt, mv16]
    return pl.pallas_call(
        functools.partial(_merge_ffn_kernel, cross_attend=memory is not None),
        grid=(m // tm,),
        in_specs=[pl.BlockSpec((tm, D_MODEL), row),
                  pl.BlockSpec((tm, BRANCH), row),
                  pl.BlockSpec((tm, BRANCH), row),
                  pl.BlockSpec((tm, BRANCH), row),
                  *mem_specs,
                  _layer_spec((D_MODEL, W_HALF), layer, 0, 0),
                  _layer_spec((3, BRANCH, D_MODEL), layer),
                  _layer_spec((D_MODEL, D_MODEL), layer),
                  _layer_spec((D_MODEL, 2 * D_FF), layer),
                  _layer_spec((D_FF, D_MODEL), layer),
                  vec, vec, vec, vec],
        out_specs=pl.BlockSpec((tm, D_MODEL), row),
        out_shape=jax.ShapeDtypeStruct((m, D_MODEL), F32),
        compiler_params=_params(("arbitrary",)),
        name="merge_ffn",
    )(x2d, ya, yb, third, *mem_args, w_all, wbr, wmix, wfi, wfo, g1, b1, g2, b2)


_P2_MK, _P2_SK, _P2_END = 0, 512, 640


def _dproj_kernel(x_ref, wn_ref, wt_ref, bias_ref, cos_ref, sin_ref, cost_ref, sint_ref, p_ref, pt_ref, p2_ref):
    xb = x_ref[...].astype(BF16)
    cos = cos_ref[...]
    sin = sin_ref[...]
    for c in range(_N_END // LANES):
        cs = slice(c * LANES, (c + 1) * LANES)
        blk = _mm(xb, wn_ref[:, cs])
        if _N_SQ <= c * LANES < _N_XQ:
            blk = _rope_lanes(blk, cos, sin)
        p_ref[:, cs] = blk
    pt_ref[_T_MV:_T_MK, :] = _mm_nt(wt_ref[_T_MV:_T_MK, :], xb)
    kt = _mm_nt(wt_ref[_T_MK:_T_SK, :], xb) * (ML_DK ** -0.5)
    pt_ref[_T_MK:_T_SK, :] = kt
    skt = _mm_nt(wt_ref[_T_SK:_T_SV, :], xb)
    cost = cost_ref[...]
    sint = sint_ref[...]
    for g in range(SWA_KV):
        base = g * SWA_HD
        for off, val in _rope_rows(skt[base:base + SWA_HD, :], cost, sint):
            pt_ref[_T_SK + base + off:_T_SK + base + off + val.shape[0], :] = val
    pt_ref[_T_SV:_T_G, :] = _mm_nt(wt_ref[_T_SV:_T_G, :], xb)
    pt_ref[_T_G:_T_END, :] = _mm_nt(wt_ref[_T_G:_T_END, :], xb) + bias_ref[...]
    for c in range(BRANCH // LANES):
        p2_ref[:, _P2_MK + c * LANES:_P2_MK + (c + 1) * LANES] = kt[c * LANES:(c + 1) * LANES, :].T
    p2_ref[:, _P2_SK:_P2_END] = pt_ref[_T_SK:_T_SV, :].T


def _dproj(xs, w_all, wt_all, bias_all, cos, sin, cost, sint, layer):
    n = xs.shape[0]
    whole = lambda shape: pl.BlockSpec(shape, lambda i: (0, 0))
    return pl.pallas_call(
        _dproj_kernel,
        grid=(1,),
        in_specs=[_const_spec((n, D_MODEL)),
                  _layer_spec((D_MODEL, W_HALF), layer, 0, 1),
                  _layer_spec((_T_END, D_MODEL), layer),
                  _layer_spec((16, 1), layer),
                  _const_spec((n, LANES)), _const_spec((n, LANES)),
                  _const_spec((ROT_HALF, n)), _const_spec((ROT_HALF, n))],
        out_specs=[whole((n, _N_END)), whole((_T_END, n)), whole((n, _P2_END))],
        out_shape=[jax.ShapeDtypeStruct((n, _N_END), F32),
                   jax.ShapeDtypeStruct((_T_END, n), F32),
                   jax.ShapeDtypeStruct((n, _P2_END), F32)],
        compiler_params=_params(("arbitrary",)),
        name="decode_proj",
    )(xs, w_all, wt_all, bias_all, cos, sin, cost, sint)


def _dmlstm_stream(i, q_ref, v_ref, mo_ref, k_ref, vt_ref, gt_ref, c_ref, n_ref, m_ref, gain_ref,
                   y_ref, c_out_ref, n_out_ref, m_out_ref):
    bb = q_ref.shape[0]
    nlanes = gt_ref.shape[1]
    li = gt_ref[0:ML_HEADS, :]
    lf = _log_sigmoid(gt_ref[8:8 + ML_HEADS, :])
    m_prev = m_ref[...]
    m_t = jnp.maximum(lf + m_prev, li)
    m_out_ref[...] = m_t
    scal = jnp.concatenate([jnp.exp(li - m_t), jnp.exp(lf + m_prev - m_t), jnp.exp(-m_t), jnp.zeros_like(m_t)], axis=0)
    bring = jnp.where(i == 0, 0, nlanes - i * bb)
    scal = pltpu.roll(scal, bring, axis=1)
    tiles = [(h, j) for h in range(ML_HEADS) for j in range(bb)]
    hs = lambda h: slice(h * ML_DK, (h + 1) * ML_DK)
    nrow_of = lambda h, j: slice(j * ML_HEADS + h, j * ML_HEADS + h + 1)
    q = {(h, j): q_ref[j:j + 1, hs(h)] for h, j in tiles}
    k = {(h, j): k_ref[j:j + 1, hs(h)] for h, j in tiles}
    w = {(h, j): scal[h:h + 1, j:j + 1] for h, j in tiles}
    a = {(h, j): scal[ML_HEADS + h:ML_HEADS + h + 1, j:j + 1] for h, j in tiles}
    cq = {(h, j): _mm_nt(jnp.broadcast_to(q[h, j], (8, ML_DK)).astype(BF16), c_ref[j, h].astype(BF16))[0:1, :]
          for h, j in tiles}
    yield
    qk = {t: jnp.sum(q[t] * k[t], axis=1, keepdims=True) for t in tiles}
    nq = {(h, j): jnp.sum(n_ref[nrow_of(h, j), :] * q[h, j], axis=1, keepdims=True) for h, j in tiles}
    yield
    hrow = {}
    for h, j in tiles:
        sw = qk[h, j] * w[h, j]
        floor = scal[2 * ML_HEADS + h:2 * ML_HEADS + h + 1, j:j + 1]
        den = jnp.maximum(jnp.abs(sw + a[h, j] * nq[h, j]), floor)
        hrow[h, j] = (sw * v_ref[j:j + 1, hs(h)] + a[h, j] * cq[h, j]) / den
        n_out_ref[nrow_of(h, j), :] = a[h, j] * n_ref[nrow_of(h, j), :] + w[h, j] * k[h, j]
    yield
    for h in range(ML_HEADS):
        vt = pltpu.roll(vt_ref[hs(h), :], bring, axis=1)
        for j in range(bb):
            c_out_ref[j, h] = a[h, j] * c_ref[j, h] + (w[h, j] * vt[:, j:j + 1]) * k[h, j]
        yield
    hc = {t: hrow[t] - jnp.mean(hrow[t], axis=1, keepdims=True) for t in tiles}
    yield
    var = {t: jnp.mean(hc[t] * hc[t], axis=1, keepdims=True) for t in tiles}
    yield
    for h, j in tiles:
        hn = hc[h, j] * lax.rsqrt(var[h, j] + HEAD_NORM_EPS) * gain_ref[:, hs(h)]
        y_ref[j:j + 1, hs(h)] = (_sigmoid(mo_ref[j:j + 1, hs(h)]) * hn).astype(BF16)


def _dswa_stream(i, q_ref, kn_ref, vn_ref, kvt_ref, ck_ref, cv_ref, sink_ref, y_ref, ko_ref, vo_ref):
    bb = q_ref.shape[0]
    lane = lax.broadcasted_iota(jnp.int32, (LANES, LANES), 1)
    row8 = lax.broadcasted_iota(jnp.int32, (SWA_HEADS, LANES), 0)
    low8 = lax.broadcasted_iota(jnp.int32, (SWA_HEADS, LANES), 1) < SWA_HD
    low1 = lax.broadcasted_iota(jnp.int32, (1, LANES), 1) < SWA_HD
    scale = SWA_HD ** -0.5
    sink = sink_ref[:, 0:1]
    qm, s = [], []
    for j in range(bb):
        rows = jnp.zeros((SWA_HEADS, LANES), F32)
        for pp in range(SWA_HEADS // 2):
            g = pp // 2
            pair = q_ref[j:j + 1, pp * LANES:(pp + 1) * LANES]
            swap = pltpu.roll(pair, SWA_HD, axis=1)
            in_g = low8 if g == 0 else jnp.logical_not(low8)
            for t in range(2):
                rows = jnp.where((row8 == 2 * pp + t) & in_g, pair if t == g else swap, rows)
        qm.append(rows)
        s.append(_mm(rows.astype(BF16), ck_ref[j].astype(BF16)) * scale)
    yield
    e, e_new, den = [], [], []
    for j in range(bb):
        s_new = jnp.sum(qm[j] * kn_ref[j:j + 1, :], axis=1, keepdims=True) * scale
        mx = jnp.maximum(jnp.maximum(jnp.max(s[j], axis=1, keepdims=True), s_new), sink)
        e.append(jnp.exp(s[j] - mx))
        e_new.append(jnp.exp(s_new - mx))
        den.append(jnp.sum(e[j], axis=1, keepdims=True) + e_new[j] + jnp.exp(sink - mx))
    yield
    for j in range(bb):
        o = (_mm_nt(e[j].astype(BF16), cv_ref[j].astype(BF16)) + e_new[j] * vn_ref[j:j + 1, :]) / den[j]
        for pp in range(SWA_HEADS // 2):
            g = pp // 2
            halves = []
            for t in range(2):
                oh = o[2 * pp + t:2 * pp + t + 1, :]
                halves.append(oh if t == g else pltpu.roll(oh, SWA_HD, axis=1))
            y_ref[j:j + 1, pp * LANES:(pp + 1) * LANES] = jnp.where(low1, halves[0], halves[1]).astype(BF16)
    yield
    knew_t = kvt_ref[0:LANES, :]
    vnew_t = kvt_ref[LANES:, :]
    for j in range(bb):
        bring = LANES - 1 - (i * bb + j)
        ko_ref[j] = jnp.where(lane == LANES - 1, pltpu.roll(knew_t, bring, axis=1),
                              pltpu.roll(ck_ref[j], LANES - 1, axis=1))
        vo_ref[j] = jnp.where(lane == LANES - 1, pltpu.roll(vnew_t, bring, axis=1),
                              pltpu.roll(cv_ref[j], LANES - 1, axis=1))
        if j % 2 == 1:
            yield


def _dcross_stream(q_ref, k_ref, v_ref, y_ref):
    bb = q_ref.shape[0]
    scale = X_HD ** -0.5
    row8 = lax.broadcasted_iota(jnp.int32, (8, LANES), 0) % X_HEADS
    ones = jnp.ones((X_HD, LANES), BF16)
    tiles, rows = k_ref.shape[1], k_ref.shape[1] * k_ref.shape[2]
    s = []
    for j in range(bb):
        qrep = jnp.zeros((8, LANES), F32)
        for h in range(X_HEADS):
            qrep = jnp.where(row8 == h, q_ref[j:j + 1, h * X_HD:(h + 1) * X_HD], qrep)
        prod = (k_ref[j] * qrep[None]).astype(BF16).reshape(rows, X_HD)
        s.append(_mm(prod, ones).reshape(tiles, 8, LANES) * scale)
        if j % 2 == 1:
            yield
    for j in range(bb):
        mx8 = jnp.max(s[j], axis=0)
        mx4 = jnp.maximum(mx8[0:X_HEADS], mx8[X_HEADS:])
        e = jnp.exp(s[j] - jnp.concatenate([mx4, mx4], axis=0)[None])
        den8 = jnp.sum(e, axis=0)
        o8 = jnp.sum(e * v_ref[j], axis=0)
        o4 = (o8[0:X_HEADS] + o8[X_HEADS:]) / (den8[0:X_HEADS] + den8[X_HEADS:])
        for h in range(X_HEADS):
            y_ref[j:j + 1, h * X_HD:(h + 1) * X_HD] = o4[h:h + 1, :].astype(BF16)
        if j % 2 == 1:
            yield


_N_DEC_IN = 20


def _decode_mixers_kernel(*refs):
    ins, outs = refs[:_N_DEC_IN], refs[-8:]
    (q_ref, v_ref, mo_ref, k_ref, vt_ref, gt_ref, c_ref, n_ref, m_ref, gain_ref,
     sq_ref, kn_ref, vn_ref, kvt_ref, ck_ref, cv_ref, sink_ref, xq_ref, mk_ref, mv_ref) = ins
    ya_ref, c_out_ref, n_out_ref, m_out_ref, yb_ref, ko_ref, vo_ref, yc_ref = outs
    i = pl.program_id(0)
    streams = [
        _dcross_stream(xq_ref, mk_ref, mv_ref, yc_ref),
        _dmlstm_stream(i, q_ref, v_ref, mo_ref, k_ref, vt_ref, gt_ref, c_ref, n_ref, m_ref, gain_ref,
                       ya_ref, c_out_ref, n_out_ref, m_out_ref),
        _dswa_stream(i, sq_ref, kn_ref, vn_ref, kvt_ref, ck_ref, cv_ref, sink_ref, yb_ref, ko_ref, vo_ref),
    ]
    while streams:
        for g in list(streams):
            if next(g, StopIteration) is StopIteration:
                streams.remove(g)


def _decode_mixers(p, p2, pt, c_all, n_all, mt_all, gain_all, ck_all, cv_all, sinks_all, mk_all, mv_all,
                   layer, stacks):
    n = p.shape[0]
    pblk = lambda width, col: pl.BlockSpec((BB, width), lambda i: (i, col // width))
    fixed = lambda rows, row0: pl.BlockSpec((rows, n), lambda i: (row0 // rows, 0), pipeline_mode=pl.Buffered(1))
    c_spec = pl.BlockSpec((None, BB, ML_HEADS, ML_DV, ML_DK), lambda i: (layer, i, 0, 0, 0))
    win = pl.BlockSpec((None, BB, LANES, WINDOW), lambda i: (layer, i, 0, 0))
    mem = pl.BlockSpec((None, BB, MEM_TOKENS * X_HEADS // 8, 8, X_HD), lambda i: (layer, i, 0, 0, 0))
    in_specs = [pblk(BRANCH, _N_MQ), pblk(BRANCH, _N_MV), pblk(BRANCH, _N_MO), pblk(BRANCH, _P2_MK),
                fixed(BRANCH, _T_MV), fixed(16, _T_G), c_spec,
                pl.BlockSpec((None, BB * ML_HEADS, ML_DK), lambda i: (layer, i, 0)),
                _layer_spec((ML_HEADS, n), layer), _layer_spec((1, BRANCH), layer),
                pblk(BRANCH, _N_SQ), pblk(LANES, _P2_SK), pblk(LANES, _N_SV), fixed(2 * LANES, _T_SK),
                win, win, _layer_spec((8, LANES), layer),
                pblk(BRANCH, _N_XQ), mem, mem]
    args = [p, p, p, p2, pt, pt, c_all, n_all, mt_all, gain_all,
            p, p2, p, pt, ck_all, cv_all, sinks_all, p, mk_all, mv_all]
    assert len(args) == _N_DEC_IN
    aliases = {}
    if stacks is not None:
        in_specs += [pl.BlockSpec(memory_space=pl.ANY)] * 3
        args += list(stacks)
        aliases = {_N_DEC_IN: 1, _N_DEC_IN + 1: 5, _N_DEC_IN + 2: 6}
    row = lambda width: pl.BlockSpec((BB, width), lambda i: (i, 0))
    return pl.pallas_call(
        _decode_mixers_kernel,
        grid=(n // BB,),
        in_specs=in_specs,
        out_specs=[row(BRANCH), c_spec,
                   pl.BlockSpec((BB * ML_HEADS, ML_DK), lambda i: (i, 0)),
                   pl.BlockSpec((ML_HEADS, n), lambda i: (0, 0)),
                   row(BRANCH), win, win, row(BRANCH)],
        out_shape=[jax.ShapeDtypeStruct((n, BRANCH), BF16),
                   jax.ShapeDtypeStruct((DEPTH, n, ML_HEADS, ML_DV, ML_DK), F32),
                   jax.ShapeDtypeStruct((n * ML_HEADS, ML_DK), F32),
                   jax.ShapeDtypeStruct((ML_HEADS, n), F32),
                   jax.ShapeDtypeStruct((n, BRANCH), BF16),
                   jax.ShapeDtypeStruct((DEPTH, n, LANES, WINDOW), F32),
                   jax.ShapeDtypeStruct((DEPTH, n, LANES, WINDOW), F32),
                   jax.ShapeDtypeStruct((n, BRANCH), BF16)],
        input_output_aliases=aliases,
        compiler_params=_params(("arbitrary",)),
        name="decode_mixers",
    )(*args)


def _rope_tables(positions):
    inv_freq = ROPE_THETA ** (-jnp.arange(ROT_HALF, dtype=F32) / ROT_HALF)
    ang = positions.astype(F32)[:, None] * inv_freq[None, :]
    cos = jnp.cos(ang)
    sin = jnp.sin(ang)
    reps = LANES // ROT_HALF
    return jnp.tile(cos, (1, reps)), jnp.tile(sin, (1, reps)), cos.T, sin.T


def kernel(x_prompt, x_sample, mem_prompt, cache_swa_k, cache_swa_v, cache_mem_k, cache_mem_v, state_mlstm_c, state_mlstm_n, state_mlstm_m, w_in, b_gates, mlstm_norm_g, swa_sinks, w_mem_kv, w_branch, w_mix_out, ln1_g, ln1_b, w_ffn_in, w_ffn_out, ln2_g, ln2_b):
    nb, seq, _ = x_prompt.shape
    ns = x_sample.shape[0]
    assert ns == LANES and PAST_LEN >= WINDOW

    cosn, sinn, cost, sint = _rope_tables(jnp.arange(seq))
    cos_s, sin_s, cost_s, sint_s = _rope_tables(jnp.full((ns,), PAST_LEN))

    ck_all = jnp.transpose(cache_swa_k, (0, 1, 3, 4, 2)).reshape(DEPTH, ns, SWA_KV * SWA_HD, WINDOW)
    cv_all = jnp.transpose(cache_swa_v, (0, 1, 3, 4, 2)).reshape(DEPTH, ns, SWA_KV * SWA_HD, WINDOW)
    mk_all = cache_mem_k.reshape(DEPTH, ns, MEM_TOKENS * X_HEADS // 8, 8, X_HD)
    mv_all = cache_mem_v.reshape(DEPTH, ns, MEM_TOKENS * X_HEADS // 8, 8, X_HD)
    n_all = state_mlstm_n.reshape(DEPTH, ns * ML_HEADS, ML_DK)
    mt_all = jnp.transpose(state_mlstm_m, (0, 2, 1))

    w_all, wt_all = _relayout_w_in(w_in)
    z4 = jnp.zeros((DEPTH, 4), F32)
    bias_all = jnp.concatenate([b_gates[:, :ML_HEADS], z4, b_gates[:, ML_HEADS:], z4], axis=1)[..., None]
    gain_all = mlstm_norm_g[:, None, :]
    sinks_all = jnp.broadcast_to(swa_sinks[:, :, None], (DEPTH, SWA_HEADS, LANES))
    wbr_all = w_branch.astype(BF16)
    wmix_all = w_mix_out.astype(BF16)
    wfi_all = w_ffn_in.astype(BF16)
    wfo_all = w_ffn_out.astype(BF16)
    ln_all = (ln1_g[:, None, :], ln1_b[:, None, :], ln2_g[:, None, :], ln2_b[:, None, :])

    yp = x_prompt.reshape(nb * seq, D_MODEL)
    ys = x_sample.reshape(ns, D_MODEL)
    mem2d = mem_prompt.reshape(nb * MEM_TOKENS, D_MODEL)

    outs = {k: [] for k in ("kp", "vp", "cp", "np", "mp", "ns", "ms")}
    c_stack = k_stack = v_stack = None
    mk32, mv32, mkt, mv16 = _memkv(mem2d, w_mem_kv, nb)
    for l in range(DEPTH):
        ya, yb, xq, k32, v32, m_fin, s_fin = _mixer(yp, w_all, wt_all, bias_all, cosn, sinn, cost, sint,
                                                    gain_all, sinks_all, l, nb, seq)
        yp = _merge_ffn(yp, ya, yb, xq, w_all, wbr_all, wmix_all, wfi_all, wfo_all, *ln_all, l, TM_MERGE,
                        memory=(mkt, mv16, seq))
        outs["kp"].append(jnp.transpose(k32.reshape(nb, SWA_KV, SWA_HD, WINDOW), (0, 3, 1, 2)))
        outs["vp"].append(jnp.transpose(v32.reshape(nb, SWA_KV, SWA_HD, WINDOW), (0, 3, 1, 2)))
        outs["cp"].append(jnp.swapaxes(s_fin[..., :ML_DV], -1, -2))
        outs["np"].append(s_fin[..., ML_DV])
        outs["mp"].append(m_fin.reshape(nb, 8, LANES)[:, :ML_HEADS, 0])

        p, pt, p2 = _dproj(ys, w_all, wt_all, bias_all, cos_s, sin_s, cost_s, sint_s, l)
        ya_s, c_stack, n_new, m_new, yb_s, k_stack, v_stack, yc_s = _decode_mixers(
            p, p2, pt, state_mlstm_c, n_all, mt_all, gain_all, ck_all, cv_all, sinks_all, mk_all, mv_all, l,
            None if l == 0 else (c_stack, k_stack, v_stack))
        ys = _merge_ffn(ys, ya_s, yb_s, yc_s, w_all, wbr_all, wmix_all, wfi_all, wfo_all, *ln_all, l, ns)
        outs["ns"].append(n_new.reshape(ns, ML_HEADS, ML_DK))
        outs["ms"].append(m_new)

    st = {k: jnp.stack(vals) for k, vals in outs.items()}
    window_out = lambda t: jnp.transpose(t.reshape(DEPTH, ns, SWA_KV, SWA_HD, WINDOW), (0, 1, 4, 2, 3))
    memory_out = lambda t: t.reshape(DEPTH, nb, MEM_TOKENS, X_HEADS, X_HD)
    return (yp.reshape(nb, seq, D_MODEL), ys.reshape(ns, 1, D_MODEL),
            st["kp"], st["vp"], window_out(k_stack), window_out(v_stack), memory_out(mk32), memory_out(mv32),
            st["cp"], st["np"], st["mp"], c_stack, st["ns"], jnp.transpose(st["ms"], (0, 2, 1)))
```

```python
import functools

import jax
import jax.numpy as jnp
from jax import lax
from jax.experimental import pallas as pl
from jax.experimental.pallas import tpu as pltpu

F32 = jnp.float32
BF16 = jnp.bfloat16

D_MODEL = 1024
DEPTH = 2
BRANCH = 512
ML_HEADS = 4
ML_DK = 128
ML_DV = 128
ML_CHUNK = 128
SWA_HD = 64
SWA_HEADS = 8
SWA_KV = 2
WINDOW = 128
ROT_DIM = 16
ROT_HALF = 8
ROPE_THETA = 500000.0
MEM_TOKENS = 256
X_HEADS = 4
X_HD = 128
D_FF = 2816
LN_EPS = 1e-5
HEAD_NORM_EPS = 1e-6
DEEPNORM_ALPHA = (2 * DEPTH) ** 0.25
NEG_INF = -1e30
PAST_LEN = 8192

LANES = 128
VMEM_LIMIT = 56 * 1024 * 1024

_C_MQ, _C_MK, _C_MV, _C_MO = 0, 512, 1024, 1536
_C_MI, _C_MF = 2048, 2052
_C_SQ, _C_SK, _C_SV, _C_XQ, _C_GL = 2056, 2568, 2696, 2824, 3336

W_HALF = 3 * D_MODEL
_N_MQ, _N_MV, _N_MO, _N_SQ, _N_XQ, _N_SV, _N_END = 0, 512, 1024, 1536, 2048, 2560, 2688
_T_MV, _T_MK, _T_SK, _T_SV, _T_G, _T_END = 0, 512, 1024, 1152, 1280, 1296

TM_PROJ = 512
TM_MERGE = 512
BB = 8
ML_GROUP = 4


def _mm(a, b):
    return jnp.dot(a, b, preferred_element_type=F32)


def _mm_nt(a, b):
    return lax.dot_general(a, b, (((1,), (1,)), ((), ())), preferred_element_type=F32)


def _sigmoid(x):
    return 1.0 / (1.0 + jnp.exp(-x))


def _log_sigmoid(x):
    return jnp.minimum(x, 0.0) - jnp.log(1.0 + jnp.exp(-jnp.abs(x)))


def _layer_norm(x, g, b):
    mu = jnp.mean(x, axis=-1, keepdims=True)
    xc = x - mu
    var = jnp.mean(xc * xc, axis=-1, keepdims=True)
    return xc * lax.rsqrt(var + LN_EPS) * g + b


def _rope_lanes(x, cos, sin):
    lane = lax.broadcasted_iota(jnp.int32, x.shape, 1) % SWA_HD
    up = pltpu.roll(x, LANES - ROT_HALF, axis=1)
    dn = pltpu.roll(x, ROT_HALF, axis=1)
    first = x * cos - up * sin
    second = x * cos + dn * sin
    return jnp.where(lane < ROT_HALF, first, jnp.where(lane < ROT_DIM, second, x))


def _rope_rows(xt, cost, sint):
    x1 = xt[0:ROT_HALF, :]
    x2 = xt[ROT_HALF:ROT_DIM, :]
    return ((0, x1 * cost - x2 * sint), (ROT_HALF, x2 * cost + x1 * sint), (ROT_DIM, xt[ROT_DIM:SWA_HD, :]))


def _const_spec(shape):
    nd = len(shape)
    return pl.BlockSpec(shape, lambda *_: (0,) * nd, pipeline_mode=pl.Buffered(1))


def _layer_spec(shape, layer, *tail):
    idx = (layer,) + (tail if tail else (0,) * len(shape))
    return pl.BlockSpec((None,) + tuple(shape), lambda *_: idx, pipeline_mode=pl.Buffered(1))


def _params(sem):
    return pltpu.CompilerParams(dimension_semantics=sem, vmem_limit_bytes=VMEM_LIMIT)


def _relayout_kernel(w_ref, wn_ref, wt_ref):
    half = pl.program_id(1)

    def put(dst, lo, hi):
        for c in range((hi - lo) // LANES):
            tile = w_ref[lo + c * LANES:lo + (c + 1) * LANES, :]
            wn_ref[:, dst + c * LANES:dst + (c + 1) * LANES] = tile.T.astype(BF16)

    @pl.when(half == 0)
    def _():
        put(0, _C_GL, _C_GL + W_HALF)

    @pl.when(half == 1)
    def _():
        for dst, lo, hi in ((_N_MQ, _C_MQ, _C_MK), (_N_MV, _C_MV, _C_MI), (_N_SQ, _C_SQ, _C_SK),
                            (_N_XQ, _C_XQ, _C_GL), (_N_SV, _C_SV, _C_XQ)):
            put(dst, lo, hi)
        wn_ref[:, _N_END:] = jnp.zeros((wn_ref.shape[0], W_HALF - _N_END), BF16)
        for dst, lo, hi in ((_T_MV, _C_MV, _C_MO), (_T_MK, _C_MK, _C_MV), (_T_SK, _C_SK, _C_XQ)):
            wt_ref[dst:dst + hi - lo, :] = w_ref[lo:hi, :].astype(BF16)
        z4 = jnp.zeros((4, w_ref.shape[1]), F32)
        gates = jnp.concatenate([w_ref[_C_MI:_C_MF, :], z4, w_ref[_C_MF:_C_SQ, :], z4], axis=0)
        wt_ref[_T_G:_T_END, :] = gates.astype(BF16)


def _relayout_w_in(w_in):
    depth, rows, cols = w_in.shape
    wt_view = jnp.swapaxes(w_in, 1, 2)
    return pl.pallas_call(
        _relayout_kernel,
        grid=(depth, 2),
        in_specs=[pl.BlockSpec((None, cols, rows), lambda l, j: (l, 0, 0), pipeline_mode=pl.Buffered(1))],
        out_specs=[pl.BlockSpec((None, rows, W_HALF), lambda l, j: (l, 0, j)),
                   pl.BlockSpec((None, _T_END, rows), lambda l, j: (l, 0, 0))],
        out_shape=[jax.ShapeDtypeStruct((depth, rows, 2 * W_HALF), BF16),
                   jax.ShapeDtypeStruct((depth, _T_END, rows), BF16)],
        compiler_params=_params(("arbitrary", "arbitrary")),
        name="relayout_w_in",
    )(wt_view)


def _memkv_kernel(mem_ref, wt_ref, k32_ref, v32_ref, kt_ref, v16_ref):
    m = mem_ref[...].astype(BF16)
    kv = _mm_nt(m, wt_ref[...].astype(BF16))
    k32_ref[...] = kv[:, :BRANCH]
    v32_ref[...] = kv[:, BRANCH:]
    v16_ref[...] = kv[:, BRANCH:].astype(BF16)
    kt_ref[...] = kv[:, :BRANCH].T.astype(BF16)


def _memkv(mem2d, w_mem_kv, nb):
    depth = w_mem_kv.shape[0]
    rows = mem2d.shape[0]
    wt_view = jnp.swapaxes(w_mem_kv, 1, 2)
    tok = lambda l, b: (l, b, 0)
    return pl.pallas_call(
        _memkv_kernel,
        grid=(depth, nb),
        in_specs=[pl.BlockSpec((MEM_TOKENS, D_MODEL), lambda l, b: (b, 0)),
                  pl.BlockSpec((None, 2 * BRANCH, D_MODEL), lambda l, b: (l, 0, 0), pipeline_mode=pl.Buffered(1))],
        out_specs=[pl.BlockSpec((None, MEM_TOKENS, BRANCH), tok),
                   pl.BlockSpec((None, MEM_TOKENS, BRANCH), tok),
                   pl.BlockSpec((None, None, BRANCH, MEM_TOKENS), lambda l, b: (l, b, 0, 0)),
                   pl.BlockSpec((None, MEM_TOKENS, BRANCH), tok)],
        out_shape=[jax.ShapeDtypeStruct((depth, rows, BRANCH), F32),
                   jax.ShapeDtypeStruct((depth, rows, BRANCH), F32),
                   jax.ShapeDtypeStruct((depth, nb, BRANCH, MEM_TOKENS), BF16),
                   jax.ShapeDtypeStruct((depth, rows, BRANCH), BF16)],
        compiler_params=_params(("arbitrary", "arbitrary")),
        name="memkv_proj",
    )(mem2d, wt_view)


def _mlstm_gate_weights(pre, gt_ref, gc_ref, m_out_ref, m_scr, valid):
    tm = pre.shape[1]
    li = pre[0:8, :]
    lane8 = lax.broadcasted_iota(jnp.int32, li.shape, 1) % ML_CHUNK
    b = _log_sigmoid(pre[8:16, :])
    shift = 1
    while shift < ML_CHUNK:
        b = b + jnp.where(lane8 >= shift, pltpu.roll(b, shift, axis=1), 0.0)
        shift *= 2
    g = li - b
    cm = g
    shift = 1
    while shift < ML_CHUNK:
        cm = jnp.maximum(cm, jnp.where(lane8 >= shift, pltpu.roll(cm, shift, axis=1), -jnp.inf))
        shift *= 2
    gt_ref[0:8, :] = g
    pad = jnp.zeros((LANES - 24, ML_CHUNK), F32)
    m_start = m_scr[...]
    m_prev = m_start
    for c in range(tm // ML_CHUNK):
        cs = slice(c * ML_CHUNK, (c + 1) * ML_CHUNK)
        b_c = b[:, cs]
        b_last = jnp.broadcast_to(b_c[:, ML_CHUNK - 1:ML_CHUNK], b_c.shape)
        cm_last = jnp.broadcast_to(cm[:, cs][:, ML_CHUNK - 1:ML_CHUNK], b_c.shape)
        m_t = b_c + jnp.maximum(m_prev, cm[:, cs])
        m_new = b_last + jnp.maximum(m_prev, cm_last)
        gt_ref[8:16, cs] = jnp.exp(b_last + g[:, cs] - m_new)
        gt_ref[16:24, cs] = jnp.exp(b_last + m_prev - m_new)
        rows = jnp.concatenate([b_c - m_t, jnp.exp(b_c + m_prev - m_t), jnp.exp(-m_t), pad], axis=0)
        gc_ref[cs, :] = rows.T
        m_prev = m_new
    m_prev = jnp.where(valid, m_prev, m_start)
    m_scr[...] = m_prev
    m_out_ref[...] = m_prev


def _proj_stream(x_ref, wn_ref, wt_ref, bias_ref, cosn_ref, sinn_ref, cost_ref, sint_ref, dst,
                 xq_ref, k32_ref, v32_ref, m_out_ref, m_scr, valid):
    tm = x_ref.shape[0]
    xb = x_ref[...].astype(BF16)
    tr_rows = lambda lo, hi: _mm_nt(wt_ref[lo:hi, :], xb)
    _mlstm_gate_weights(tr_rows(_T_G, _T_END) + bias_ref[...], dst["gt"], dst["gc"], m_out_ref, m_scr, valid)
    yield
    dst["q"][...] = _mm(xb, wn_ref[:, _N_MQ:_N_MV]).astype(BF16)
    yield
    dst["v"][...] = _mm(xb, wn_ref[:, _N_MV:_N_MO]).astype(BF16)
    yield
    dst["mo"][...] = _mm(xb, wn_ref[:, _N_MO:_N_SQ])
    yield
    xq_ref[...] = _mm(xb, wn_ref[:, _N_XQ:_N_SV]).astype(BF16)
    yield
    cosn = cosn_ref[...]
    sinn = sinn_ref[...]
    sq = _mm(xb, wn_ref[:, _N_SQ:_N_XQ])
    for c in range(BRANCH // LANES):
        blk = _rope_lanes(sq[:, c * LANES:(c + 1) * LANES], cosn, sinn)
        dst["sq"][:, c * LANES:(c + 1) * LANES] = blk.astype(BF16)
    yield
    dst["sv"][...] = _mm(xb, wn_ref[:, _N_SV:_N_END]).astype(BF16)
    dst["kt"][...] = (tr_rows(_T_MK, _T_SK) * (ML_DK ** -0.5)).astype(BF16)
    yield
    skt = tr_rows(_T_SK, _T_SV)
    svt = tr_rows(_T_SV, _T_G)
    cost = cost_ref[...]
    sint = sint_ref[...]
    tail = slice(tm - WINDOW, tm)
    v32_ref[...] = svt[:, tail]
    for g in range(SWA_KV):
        base = g * SWA_HD
        for off, val in _rope_rows(skt[base:base + SWA_HD, :], cost, sint):
            dst["skt"][base + off:base + off + val.shape[0], :] = val.astype(BF16)
            k32_ref[base + off:base + off + val.shape[0], :] = val[:, tail]


def _mlstm_stream(q_ref, v_ref, kt_ref, mo_ref, gt_ref, gc_ref, gain_ref, y_ref, s_ref):
    L = ML_CHUNK
    r_i = lax.broadcasted_iota(jnp.int32, (L, L), 0)
    c_i = lax.broadcasted_iota(jnp.int32, (L, L), 1)
    causal = c_i <= r_i
    ones = jnp.ones((L, ML_DV), BF16)

    ts = lambda c: slice(c * L, (c + 1) * L)
    hs = lambda h: slice(h * ML_DK, (h + 1) * ML_DK)
    n_chunks = q_ref.shape[0] // L

    for c0 in range(0, n_chunks, ML_GROUP):
        chunks = range(c0, min(c0 + ML_GROUP, n_chunks))
        units = [(c, h) for c in chunks for h in range(ML_HEADS)]

        qk = {(c, h): _mm(q_ref[ts(c), hs(h)], kt_ref[hs(h), ts(c)]) for c, h in units}
        yield

        sw, kts, vext = {}, {}, {}
        for c, h in units:
            g_r = gt_ref[h:h + 1, ts(c)]
            es_r = gt_ref[8 + h:9 + h, ts(c)]
            u_c = gc_ref[ts(c), h:h + 1]
            sw[c, h] = (qk[c, h] * jnp.exp(jnp.where(causal, u_c + g_r, -jnp.inf))).astype(BF16)
            kts[c, h] = (kt_ref[hs(h), ts(c)].astype(F32) * es_r).astype(BF16)
            vext[c, h] = jnp.concatenate([v_ref[ts(c), hs(h)], ones], axis=1)
        yield

        intra = {u: _mm(sw[u], vext[u]) for u in units}
        delta = {u: _mm(kts[u], vext[u]) for u in units}
        yield

        s_in = {}
        for h in range(ML_HEADS):
            state = s_ref[h]
            for c in chunks:
                s_in[c, h] = state.astype(BF16)
                state = gt_ref[16 + h:17 + h, c * L:c * L + 1] * state + delta[c, h]
            s_ref[h] = state
        qs_all = {u: _mm(q_ref[ts(u[0]), hs(u[1])], s_in[u]) for u in units}
        yield

        hh, hc = {}, {}
        for c, h in units:
            tot = intra[c, h] + gc_ref[ts(c), 8 + h:9 + h] * qs_all[c, h]
            floor = gc_ref[ts(c), 16 + h:17 + h]
            hh[c, h] = tot[:, :ML_DV] * (1.0 / jnp.maximum(jnp.abs(tot[:, ML_DV:]), floor))
        for u in units:
            hc[u] = hh[u] - jnp.mean(hh[u], axis=1, keepdims=True)
        for c, h in units:
            var = jnp.mean(hc[c, h] * hc[c, h], axis=1, keepdims=True)
            hn = hc[c, h] * lax.rsqrt(var + HEAD_NORM_EPS) * gain_ref[:, hs(h)]
            y_ref[ts(c), hs(h)] = (_sigmoid(mo_ref[ts(c), hs(h)]) * hn).astype(BF16)
        yield


def _swa_stream(q_ref, ktp_ref, ktc_ref, vp_ref, vc_ref, sink_ref, y_ref, seq_start):
    L = WINDOW
    nblk = q_ref.shape[0] // L
    r_i = lax.broadcasted_iota(jnp.int32, (L, 2 * L), 0)
    c_i = lax.broadcasted_iota(jnp.int32, (L, 2 * L), 1)
    band = (c_i >= r_i) & (c_i <= r_i + L)
    first = band & (c_i >= jnp.where(seq_start, L, 0))
    low_half = lax.broadcasted_iota(jnp.int32, (2 * L, LANES), 1) < SWA_HD
    out_low = lax.broadcasted_iota(jnp.int32, (L, LANES), 1) < SWA_HD
    zeros_k = jnp.zeros((SWA_HD, 2 * L), BF16)
    ones_lo = jnp.where(low_half, 1.0, 0.0).astype(BF16)
    ones_hi = jnp.where(low_half, 0.0, 1.0).astype(BF16)

    kt_all = jnp.concatenate([ktp_ref[...], ktc_ref[...]], axis=1)
    v_all = jnp.concatenate([vp_ref[...], vc_ref[...]], axis=0).astype(F32)
    v_swap = pltpu.roll(v_all, SWA_HD, axis=1)

    def scores(c):
        win = slice(c * L, (c + 2) * L)
        out = []
        for g in range(SWA_KV):
            kt2 = kt_all[g * SWA_HD:(g + 1) * SWA_HD, win]
            kblk = jnp.concatenate([jnp.concatenate([kt2, zeros_k], axis=0),
                                    jnp.concatenate([zeros_k, kt2], axis=0)], axis=1)
            for pp in range(2 * g, 2 * g + 2):
                out.append(_mm(q_ref[c * L:(c + 1) * L, pp * LANES:(pp + 1) * LANES], kblk))
        return out

    def weights(c, s_list):
        allowed = first if c == 0 else band
        out = []
        for head in range(SWA_HEADS):
            s = s_list[head // 2][:, (head % 2) * 2 * L:(head % 2 + 1) * 2 * L]
            sc = jnp.where(allowed, s * (SWA_HD ** -0.5), NEG_INF)
            sink = sink_ref[head:head + 1, 0:1]
            mx = jnp.broadcast_to(jnp.maximum(jnp.max(sc, axis=1, keepdims=True), sink), sc.shape)
            out.append((jnp.exp(sc - mx).astype(BF16), jnp.exp(sink - mx[:, :LANES])))
        return out

    def outputs(c, e_list):
        win = slice(c * L, (c + 2) * L)
        v2 = v_all[win, :]
        v2s = v_swap[win, :]
        for g in range(SWA_KV):
            va = jnp.where(low_half, v2 if g == 0 else v2s, 0.0).astype(BF16)
            vb = jnp.where(low_half, 0.0, v2s if g == 0 else v2).astype(BF16)
            vden = jnp.concatenate([jnp.concatenate([va, ones_lo], axis=1),
                                    jnp.concatenate([vb, ones_hi], axis=1)], axis=0)
            for pp in range(2 * g, 2 * g + 2):
                (e0, k0), (e1, k1) = e_list[2 * pp], e_list[2 * pp + 1]
                res = _mm(jnp.concatenate([e0, e1], axis=1), vden)
                den = res[:, LANES:] + jnp.where(out_low, k0, k1)
                y_ref[c * L:(c + 1) * L, pp * LANES:(pp + 1) * LANES] = (res[:, :LANES] * (1.0 / den)).astype(BF16)

    s_next = scores(0)
    yield
    for c in range(nblk):
        s_cur = s_next
        if c + 1 < nblk:
            s_next = scores(c + 1)
        outputs(c, weights(c, s_cur))
        yield


_SLOT_BUFFERS = (("q", (TM_PROJ, BRANCH), BF16), ("v", (TM_PROJ, BRANCH), BF16), ("mo", (TM_PROJ, BRANCH), F32),
                 ("sq", (TM_PROJ, BRANCH), BF16), ("sv", (TM_PROJ, LANES), BF16), ("kt", (BRANCH, TM_PROJ), BF16),
                 ("skt", (LANES, TM_PROJ), BF16), ("gt", (24, TM_PROJ), F32), ("gc", (TM_PROJ, LANES), F32))


def _mixer_kernel(x_ref, wn_ref, wt_ref, bias_ref, cosn_ref, sinn_ref, cost_ref, sint_ref, gain_ref, sink_ref,
                  ya_ref, yb_ref, xq_ref, k32_ref, v32_ref, m_out_ref, s_out_ref, *scratch, n_tiles, tiles_per_seq):
    slots = {name: ref for (name, _, _), ref in zip(_SLOT_BUFFERS, scratch)}
    s_ref, m_scr, ktp_ref, vp_ref = scratch[len(_SLOT_BUFFERS):]
    i = pl.program_id(0)
    wr = i % 2
    mix_tile = i - 1
    valid = i < n_tiles

    @pl.when(i == 0)
    def _():
        for ref in scratch:
            ref[...] = jnp.zeros_like(ref)

    @pl.when((i % tiles_per_seq == 0) & valid)
    def _():
        m_scr[...] = jnp.zeros_like(m_scr)

    seq_start = mix_tile % tiles_per_seq == 0

    @pl.when(seq_start)
    def _():
        s_ref[...] = jnp.zeros_like(s_ref)

    dst = {name: ref.at[wr] for name, ref in slots.items()}
    src = {name: ref.at[1 - wr] for name, ref in slots.items()}
    streams = [
        _proj_stream(x_ref, wn_ref, wt_ref, bias_ref, cosn_ref, sinn_ref, cost_ref, sint_ref, dst,
                     xq_ref, k32_ref, v32_ref, m_out_ref, m_scr, valid),
        _mlstm_stream(src["q"], src["v"], src["kt"], src["mo"], src["gt"], src["gc"], gain_ref, ya_ref, s_ref),
        _swa_stream(src["sq"], ktp_ref, src["skt"], vp_ref, src["sv"], sink_ref, yb_ref, seq_start),
    ]
    while streams:
        for g in list(streams):
            if next(g, StopIteration) is StopIteration:
                streams.remove(g)

    tm = x_ref.shape[0]
    ktp_ref[...] = src["skt"][:, tm - WINDOW:]
    vp_ref[...] = src["sv"][tm - WINDOW:, :]

    @pl.when(mix_tile % tiles_per_seq == tiles_per_seq - 1)
    def _():
        s_out_ref[...] = s_ref[...]


def _mixer(x2d, w_all, wt_all, bias_all, cosn, sinn, cost, sint, gain_all, sinks_all, layer, nb, seq):
    m = x2d.shape[0]
    tm = TM_PROJ
    nt = seq // tm
    n = m // tm
    proj = lambda i: jnp.minimum(i, n - 1)
    mix = lambda i: jnp.maximum(i - 1, 0)
    return pl.pallas_call(
        functools.partial(_mixer_kernel, n_tiles=n, tiles_per_seq=nt),
        grid=(n + 1,),
        in_specs=[pl.BlockSpec((tm, D_MODEL), lambda i: (proj(i), 0)),
                  _layer_spec((D_MODEL, W_HALF), layer, 0, 1),
                  _layer_spec((_T_END, D_MODEL), layer),
                  _layer_spec((16, 1), layer),
                  pl.BlockSpec((tm, LANES), lambda i: (proj(i) % nt, 0)),
                  pl.BlockSpec((tm, LANES), lambda i: (proj(i) % nt, 0)),
                  pl.BlockSpec((ROT_HALF, tm), lambda i: (0, proj(i) % nt)),
                  pl.BlockSpec((ROT_HALF, tm), lambda i: (0, proj(i) % nt)),
                  _layer_spec((1, BRANCH), layer),
                  _layer_spec((8, LANES), layer)],
        out_specs=[pl.BlockSpec((tm, BRANCH), lambda i: (mix(i), 0)),
                   pl.BlockSpec((tm, BRANCH), lambda i: (mix(i), 0)),
                   pl.BlockSpec((tm, BRANCH), lambda i: (proj(i), 0)),
                   pl.BlockSpec((LANES, WINDOW), lambda i: (proj(i) // nt, 0)),
                   pl.BlockSpec((LANES, WINDOW), lambda i: (proj(i) // nt, 0)),
                   pl.BlockSpec((8, LANES), lambda i: (proj(i) // nt, 0)),
                   pl.BlockSpec((None, ML_HEADS, ML_DK, 2 * ML_DV), lambda i: (mix(i) // nt, 0, 0, 0))],
        out_shape=[jax.ShapeDtypeStruct((m, BRANCH), BF16),
                   jax.ShapeDtypeStruct((m, BRANCH), BF16),
                   jax.ShapeDtypeStruct((m, BRANCH), BF16),
                   jax.ShapeDtypeStruct((nb * LANES, WINDOW), F32),
                   jax.ShapeDtypeStruct((nb * LANES, WINDOW), F32),
                   jax.ShapeDtypeStruct((nb * 8, LANES), F32),
                   jax.ShapeDtypeStruct((nb, ML_HEADS, ML_DK, 2 * ML_DV), F32)],
        scratch_shapes=[pltpu.VMEM((2,) + shape, dtype) for _, shape, dtype in _SLOT_BUFFERS]
        + [pltpu.VMEM((ML_HEADS, ML_DK, 2 * ML_DV), F32), pltpu.VMEM((8, LANES), F32),
           pltpu.VMEM((LANES, WINDOW), BF16), pltpu.VMEM((WINDOW, LANES), BF16)],
        compiler_params=_params(("arbitrary",)),
        name="prompt_mixer",
    )(x2d, w_all, wt_all, bias_all, cosn, sinn, cost, sint, gain_all, sinks_all)


def _cross_scores(q, kt_ref):
    return [_mm(q[:, h * X_HD:(h + 1) * X_HD], kt_ref[h * X_HD:(h + 1) * X_HD, :]) for h in range(X_HEADS)]


def _cross_outputs(scores, v_ref):
    ones = jnp.ones((MEM_TOKENS, X_HD), BF16)
    out = []
    for h, s in enumerate(scores):
        s = s * (X_HD ** -0.5)
        e = jnp.exp(s - jnp.max(s, axis=1, keepdims=True)).astype(BF16)
        res = _mm(e, jnp.concatenate([v_ref[:, h * X_HD:(h + 1) * X_HD], ones], axis=1))
        out.append((res[:, :X_HD] * (1.0 / res[:, X_HD:])).astype(BF16))
    return jnp.concatenate(out, axis=1)


def _merge_ffn_kernel(*refs, cross_attend):
    if cross_attend:
        x_ref, ya_ref, yb_ref, xq_ref, mkt_ref, mv_ref = refs[:6]
        refs = refs[6:]
    else:
        x_ref, ya_ref, yb_ref, yc_ref = refs[:4]
        refs = refs[4:]
    wgl_ref, wbr_ref, wmix_ref, wfi_ref, wfo_ref, g1_ref, b1_ref, g2_ref, b2_ref, o_ref = refs
    tm = x_ref.shape[0]
    n_sub = 2 if tm >= TM_MERGE else 1
    halves = [slice(s * (tm // n_sub), (s + 1) * (tm // n_sub)) for s in range(n_sub)]
    x = [x_ref[s, :] for s in halves]
    xb = [v.astype(BF16) for v in x]
    if cross_attend:
        scores = [_cross_scores(xq_ref[s, :], mkt_ref) for s in halves]
    gates = [[_sigmoid(_mm(xb[i], wgl_ref[:, r * D_MODEL:(r + 1) * D_MODEL])) for r in range(3)]
             for i in range(n_sub)]
    if cross_attend:
        yc = [_cross_outputs(sc, mv_ref) for sc in scores]
    else:
        yc = [yc_ref[s, :] for s in halves]
    acc = []
    for i, s in enumerate(halves):
        tot = None
        for r, y in enumerate((ya_ref[s, :], yb_ref[s, :], yc[i])):
            term = gates[i][r] * _mm(y, wbr_ref[r])
            tot = term if tot is None else tot + term
        acc.append(tot.astype(BF16))
    x1 = [_layer_norm(DEEPNORM_ALPHA * x[i] + _mm(acc[i], wmix_ref[...]), g1_ref[...], b1_ref[...])
          for i in range(n_sub)]
    x1b = [v.astype(BF16) for v in x1]
    chunk = D_FF // 2
    ffn = [None] * n_sub
    for lo in (0, chunk):
        act = []
        for i in range(n_sub):
            gpre = _mm(x1b[i], wfi_ref[:, lo:lo + chunk])
            up = _mm(x1b[i], wfi_ref[:, D_FF + lo:D_FF + lo + chunk])
            act.append((gpre * _sigmoid(gpre) * up).astype(BF16))
        for i in range(n_sub):
            part = _mm(act[i], wfo_ref[lo:lo + chunk, :])
            ffn[i] = part if ffn[i] is None else ffn[i] + part
    for i, s in enumerate(halves):
        o_ref[s, :] = _layer_norm(DEEPNORM_ALPHA * x1[i] + ffn[i], g2_ref[...], b2_ref[...])


def _merge_ffn(x2d, ya, yb, third, w_all, wbr, wmix, wfi, wfo, g1, b1, g2, b2, layer, tm, memory=None):
    m = x2d.shape[0]
    row = lambda i: (i, 0)
    vec = _layer_spec((1, D_MODEL), layer)
    mem_specs, mem_args = [], []
    if memory is not None:
        mkt, mv16, seq = memory
        per = seq // tm
        mem_specs = [pl.BlockSpec((None, None, BRANCH, MEM_TOKENS), lambda i: (layer, i // per, 0, 0)),
                     pl.BlockSpec((None, MEM_TOKENS, BRANCH), lambda i: (layer, i // per, 0))]
        mem_args = [mkt, mv16]
    return pl.pallas_call(
        functools.partial(_merge_ffn_kernel, cross_attend=memory is not None),
        grid=(m // tm,),
        in_specs=[pl.BlockSpec((tm, D_MODEL), row),
                  pl.BlockSpec((tm, BRANCH), row),
                  pl.BlockSpec((tm, BRANCH), row),
                  pl.BlockSpec((tm, BRANCH), row),
                  *mem_specs,
                  _layer_spec((D_MODEL, W_HALF), layer, 0, 0),
                  _layer_spec((3, BRANCH, D_MODEL), layer),
                  _layer_spec((D_MODEL, D_MODEL), layer),
                  _layer_spec((D_MODEL, 2 * D_FF), layer),
                  _layer_spec((D_FF, D_MODEL), layer),
                  vec, vec, vec, vec],
        out_specs=pl.BlockSpec((tm, D_MODEL), row),
        out_shape=jax.ShapeDtypeStruct((m, D_MODEL), F32),
        compiler_params=_params(("arbitrary",)),
        name="merge_ffn",
    )(x2d, ya, yb, third, *mem_args, w_all, wbr, wmix, wfi, wfo, g1, b1, g2, b2)


_P2_MK, _P2_SK, _P2_END = 0, 512, 640


def _dproj_kernel(x_ref, wn_ref, wt_ref, bias_ref, cos_ref, sin_ref, cost_ref, sint_ref, p_ref, pt_ref, p2_ref):
    xb = x_ref[...].astype(BF16)
    cos = cos_ref[...]
    sin = sin_ref[...]
    for c in range(_N_END // LANES):
        cs = slice(c * LANES, (c + 1) * LANES)
        blk = _mm(xb, wn_ref[:, cs])
        if _N_SQ <= c * LANES < _N_XQ:
            blk = _rope_lanes(blk, cos, sin)
        p_ref[:, cs] = blk
    pt_ref[_T_MV:_T_MK, :] = _mm_nt(wt_ref[_T_MV:_T_MK, :], xb)
    kt = _mm_nt(wt_ref[_T_MK:_T_SK, :], xb) * (ML_DK ** -0.5)
    pt_ref[_T_MK:_T_SK, :] = kt
    skt = _mm_nt(wt_ref[_T_SK:_T_SV, :], xb)
    cost = cost_ref[...]
    sint = sint_ref[...]
    for g in range(SWA_KV):
        base = g * SWA_HD
        for off, val in _rope_rows(skt[base:base + SWA_HD, :], cost, sint):
            pt_ref[_T_SK + base + off:_T_SK + base + off + val.shape[0], :] = val
    pt_ref[_T_SV:_T_G, :] = _mm_nt(wt_ref[_T_SV:_T_G, :], xb)
    pt_ref[_T_G:_T_END, :] = _mm_nt(wt_ref[_T_G:_T_END, :], xb) + bias_ref[...]
    for c in range(BRANCH // LANES):
        p2_ref[:, _P2_MK + c * LANES:_P2_MK + (c + 1) * LANES] = kt[c * LANES:(c + 1) * LANES, :].T
    p2_ref[:, _P2_SK:_P2_END] = pt_ref[_T_SK:_T_SV, :].T


def _dproj(xs, w_all, wt_all, bias_all, cos, sin, cost, sint, layer):
    n = xs.shape[0]
    whole = lambda shape: pl.BlockSpec(shape, lambda i: (0, 0))
    return pl.pallas_call(
        _dproj_kernel,
        grid=(1,),
        in_specs=[_const_spec((n, D_MODEL)),
                  _layer_spec((D_MODEL, W_HALF), layer, 0, 1),
                  _layer_spec((_T_END, D_MODEL), layer),
                  _layer_spec((16, 1), layer),
                  _const_spec((n, LANES)), _const_spec((n, LANES)),
                  _const_spec((ROT_HALF, n)), _const_spec((ROT_HALF, n))],
        out_specs=[whole((n, _N_END)), whole((_T_END, n)), whole((n, _P2_END))],
        out_shape=[jax.ShapeDtypeStruct((n, _N_END), F32),
                   jax.ShapeDtypeStruct((_T_END, n), F32),
                   jax.ShapeDtypeStruct((n, _P2_END), F32)],
        compiler_params=_params(("arbitrary",)),
        name="decode_proj",
    )(xs, w_all, wt_all, bias_all, cos, sin, cost, sint)


def _dmlstm_stream(i, q_ref, v_ref, mo_ref, k_ref, vt_ref, gt_ref, c_ref, n_ref, m_ref, gain_ref,
                   y_ref, c_out_ref, n_out_ref, m_out_ref):
    bb = q_ref.shape[0]
    nlanes = gt_ref.shape[1]
    li = gt_ref[0:ML_HEADS, :]
    lf = _log_sigmoid(gt_ref[8:8 + ML_HEADS, :])
    m_prev = m_ref[...]
    m_t = jnp.maximum(lf + m_prev, li)
    m_out_ref[...] = m_t
    scal = jnp.concatenate([jnp.exp(li - m_t), jnp.exp(lf + m_prev - m_t), jnp.exp(-m_t), jnp.zeros_like(m_t)], axis=0)
    bring = jnp.where(i == 0, 0, nlanes - i * bb)
    scal = pltpu.roll(scal, bring, axis=1)
    tiles = [(h, j) for h in range(ML_HEADS) for j in range(bb)]
    hs = lambda h: slice(h * ML_DK, (h + 1) * ML_DK)
    nrow_of = lambda h, j: slice(j * ML_HEADS + h, j * ML_HEADS + h + 1)
    q = {(h, j): q_ref[j:j + 1, hs(h)] for h, j in tiles}
    k = {(h, j): k_ref[j:j + 1, hs(h)] for h, j in tiles}
    w = {(h, j): scal[h:h + 1, j:j + 1] for h, j in tiles}
    a = {(h, j): scal[ML_HEADS + h:ML_HEADS + h + 1, j:j + 1] for h, j in tiles}
    cq = {(h, j): _mm_nt(jnp.broadcast_to(q[h, j], (8, ML_DK)).astype(BF16), c_ref[j, h].astype(BF16))[0:1, :]
          for h, j in tiles}
    yield
    qk = {t: jnp.sum(q[t] * k[t], axis=1, keepdims=True) for t in tiles}
    nq = {(h, j): jnp.sum(n_ref[nrow_of(h, j), :] * q[h, j], axis=1, keepdims=True) for h, j in tiles}
    yield
    hrow = {}
    for h, j in tiles:
        sw = qk[h, j] * w[h, j]
        floor = scal[2 * ML_HEADS + h:2 * ML_HEADS + h + 1, j:j + 1]
        den = jnp.maximum(jnp.abs(sw + a[h, j] * nq[h, j]), floor)
        hrow[h, j] = (sw * v_ref[j:j + 1, hs(h)] + a[h, j] * cq[h, j]) / den
        n_out_ref[nrow_of(h, j), :] = a[h, j] * n_ref[nrow_of(h, j), :] + w[h, j] * k[h, j]
    yield
    for h in range(ML_HEADS):
        vt = pltpu.roll(vt_ref[hs(h), :], bring, axis=1)
        for j in range(bb):
            c_out_ref[j, h] = a[h, j] * c_ref[j, h] + (w[h, j] * vt[:, j:j + 1]) * k[h, j]
        yield
    hc = {t: hrow[t] - jnp.mean(hrow[t], axis=1, keepdims=True) for t in tiles}
    yield
    var = {t: jnp.mean(hc[t] * hc[t], axis=1, keepdims=True) for t in tiles}
    yield
    for h, j in tiles:
        hn = hc[h, j] * lax.rsqrt(var[h, j] + HEAD_NORM_EPS) * gain_ref[:, hs(h)]
        y_ref[j:j + 1, hs(h)] = (_sigmoid(mo_ref[j:j + 1, hs(h)]) * hn).astype(BF16)


def _dswa_stream(i, q_ref, kn_ref, vn_ref, kvt_ref, ck_ref, cv_ref, sink_ref, y_ref, ko_ref, vo_ref):
    bb = q_ref.shape[0]
    lane = lax.broadcasted_iota(jnp.int32, (LANES, LANES), 1)
    row8 = lax.broadcasted_iota(jnp.int32, (SWA_HEADS, LANES), 0)
    low8 = lax.broadcasted_iota(jnp.int32, (SWA_HEADS, LANES), 1) < SWA_HD
    low1 = lax.broadcasted_iota(jnp.int32, (1, LANES), 1) < SWA_HD
    scale = SWA_HD ** -0.5
    sink = sink_ref[:, 0:1]
    qm, s = [], []
    for j in range(bb):
        rows = jnp.zeros((SWA_HEADS, LANES), F32)
        for pp in range(SWA_HEADS // 2):
            g = pp // 2
            pair = q_ref[j:j + 1, pp * LANES:(pp + 1) * LANES]
            swap = pltpu.roll(pair, SWA_HD, axis=1)
            in_g = low8 if g == 0 else jnp.logical_not(low8)
            for t in range(2):
                rows = jnp.where((row8 == 2 * pp + t) & in_g, pair if t == g else swap, rows)
        qm.append(rows)
        s.append(_mm(rows.astype(BF16), ck_ref[j].astype(BF16)) * scale)
    yield
    e, e_new, den = [], [], []
    for j in range(bb):
        s_new = jnp.sum(qm[j] * kn_ref[j:j + 1, :], axis=1, keepdims=True) * scale
        mx = jnp.maximum(jnp.maximum(jnp.max(s[j], axis=1, keepdims=True), s_new), sink)
        e.append(jnp.exp(s[j] - mx))
        e_new.append(jnp.exp(s_new - mx))
        den.append(jnp.sum(e[j], axis=1, keepdims=True) + e_new[j] + jnp.exp(sink - mx))
    yield
    for j in range(bb):
        o = (_mm_nt(e[j].astype(BF16), cv_ref[j].astype(BF16)) + e_new[j] * vn_ref[j:j + 1, :]) / den[j]
        for pp in range(SWA_HEADS // 2):
            g = pp // 2
            halves = []
            for t in range(2):
                oh = o[2 * pp + t:2 * pp + t + 1, :]
                halves.append(oh if t == g else pltpu.roll(oh, SWA_HD, axis=1))
            y_ref[j:j + 1, pp * LANES:(pp + 1) * LANES] = jnp.where(low1, halves[0], halves[1]).astype(BF16)
    yield
    knew_t = kvt_ref[0:LANES, :]
    vnew_t = kvt_ref[LANES:, :]
    for j in range(bb):
        bring = LANES - 1 - (i * bb + j)
        ko_ref[j] = jnp.where(lane == LANES - 1, pltpu.roll(knew_t, bring, axis=1),
                              pltpu.roll(ck_ref[j], LANES - 1, axis=1))
        vo_ref[j] = jnp.where(lane == LANES - 1, pltpu.roll(vnew_t, bring, axis=1),
                              pltpu.roll(cv_ref[j], LANES - 1, axis=1))
        if j % 2 == 1:
            yield


def _dcross_stream(q_ref, k_ref, v_ref, y_ref):
    bb = q_ref.shape[0]
    scale = X_HD ** -0.5
    row8 = lax.broadcasted_iota(jnp.int32, (8, LANES), 0) % X_HEADS
    ones = jnp.ones((X_HD, LANES), BF16)
    tiles, rows = k_ref.shape[1], k_ref.shape[1] * k_ref.shape[2]
    s = []
    for j in range(bb):
        qrep = jnp.zeros((8, LANES), F32)
        for h in range(X_HEADS):
            qrep = jnp.where(row8 == h, q_ref[j:j + 1, h * X_HD:(h + 1) * X_HD], qrep)
        prod = (k_ref[j] * qrep[None]).astype(BF16).reshape(rows, X_HD)
        s.append(_mm(prod, ones).reshape(tiles, 8, LANES) * scale)
        if j % 2 == 1:
            yield
    for j in range(bb):
        mx8 = jnp.max(s[j], axis=0)
        mx4 = jnp.maximum(mx8[0:X_HEADS], mx8[X_HEADS:])
        e = jnp.exp(s[j] - jnp.concatenate([mx4, mx4], axis=0)[None])
        den8 = jnp.sum(e, axis=0)
        o8 = jnp.sum(e * v_ref[j], axis=0)
        o4 = (o8[0:X_HEADS] + o8[X_HEADS:]) / (den8[0:X_HEADS] + den8[X_HEADS:])
        for h in range(X_HEADS):
            y_ref[j:j + 1, h * X_HD:(h + 1) * X_HD] = o4[h:h + 1, :].astype(BF16)
        if j % 2 == 1:
            yield


_N_DEC_IN = 20


def _decode_mixers_kernel(*refs):
    ins, outs = refs[:_N_DEC_IN], refs[-8:]
    (q_ref, v_ref, mo_ref, k_ref, vt_ref, gt_ref, c_ref, n_ref, m_ref, gain_ref,
     sq_ref, kn_ref, vn_ref, kvt_ref, ck_ref, cv_ref, sink_ref, xq_ref, mk_ref, mv_ref) = ins
    ya_ref, c_out_ref, n_out_ref, m_out_ref, yb_ref, ko_ref, vo_ref, yc_ref = outs
    i = pl.program_id(0)
    streams = [
        _dcross_stream(xq_ref, mk_ref, mv_ref, yc_ref),
        _dmlstm_stream(i, q_ref, v_ref, mo_ref, k_ref, vt_ref, gt_ref, c_ref, n_ref, m_ref, gain_ref,
                       ya_ref, c_out_ref, n_out_ref, m_out_ref),
        _dswa_stream(i, sq_ref, kn_ref, vn_ref, kvt_ref, ck_ref, cv_ref, sink_ref, yb_ref, ko_ref, vo_ref),
    ]
    while streams:
        for g in list(streams):
            if next(g, StopIteration) is StopIteration:
                streams.remove(g)


def _decode_mixers(p, p2, pt, c_all, n_all, mt_all, gain_all, ck_all, cv_all, sinks_all, mk_all, mv_all,
                   layer, stacks):
    n = p.shape[0]
    pblk = lambda width, col: pl.BlockSpec((BB, width), lambda i: (i, col // width))
    fixed = lambda rows, row0: pl.BlockSpec((rows, n), lambda i: (row0 // rows, 0), pipeline_mode=pl.Buffered(1))
    c_spec = pl.BlockSpec((None, BB, ML_HEADS, ML_DV, ML_DK), lambda i: (layer, i, 0, 0, 0))
    win = pl.BlockSpec((None, BB, LANES, WINDOW), lambda i: (layer, i, 0, 0))
    mem = pl.BlockSpec((None, BB, MEM_TOKENS * X_HEADS // 8, 8, X_HD), lambda i: (layer, i, 0, 0, 0))
    in_specs = [pblk(BRANCH, _N_MQ), pblk(BRANCH, _N_MV), pblk(BRANCH, _N_MO), pblk(BRANCH, _P2_MK),
                fixed(BRANCH, _T_MV), fixed(16, _T_G), c_spec,
                pl.BlockSpec((None, BB * ML_HEADS, ML_DK), lambda i: (layer, i, 0)),
                _layer_spec((ML_HEADS, n), layer), _layer_spec((1, BRANCH), layer),
                pblk(BRANCH, _N_SQ), pblk(LANES, _P2_SK), pblk(LANES, _N_SV), fixed(2 * LANES, _T_SK),
                win, win, _layer_spec((8, LANES), layer),
                pblk(BRANCH, _N_XQ), mem, mem]
    args = [p, p, p, p2, pt, pt, c_all, n_all, mt_all, gain_all,
            p, p2, p, pt, ck_all, cv_all, sinks_all, p, mk_all, mv_all]
    assert len(args) == _N_DEC_IN
    aliases = {}
    if stacks is not None:
        in_specs += [pl.BlockSpec(memory_space=pl.ANY)] * 3
        args += list(stacks)
        aliases = {_N_DEC_IN: 1, _N_DEC_IN + 1: 5, _N_DEC_IN + 2: 6}
    row = lambda width: pl.BlockSpec((BB, width), lambda i: (i, 0))
    return pl.pallas_call(
        _decode_mixers_kernel,
        grid=(n // BB,),
        in_specs=in_specs,
        out_specs=[row(BRANCH), c_spec,
                   pl.BlockSpec((BB * ML_HEADS, ML_DK), lambda i: (i, 0)),
                   pl.BlockSpec((ML_HEADS, n), lambda i: (0, 0)),
                   row(BRANCH), win, win, row(BRANCH)],
        out_shape=[jax.ShapeDtypeStruct((n, BRANCH), BF16),
                   jax.ShapeDtypeStruct((DEPTH, n, ML_HEADS, ML_DV, ML_DK), F32),
                   jax.ShapeDtypeStruct((n * ML_HEADS, ML_DK), F32),
                   jax.ShapeDtypeStruct((ML_HEADS, n), F32),
                   jax.ShapeDtypeStruct((n, BRANCH), BF16),
                   jax.ShapeDtypeStruct((DEPTH, n, LANES, WINDOW), F32),
                   jax.ShapeDtypeStruct((DEPTH, n, LANES, WINDOW), F32),
                   jax.ShapeDtypeStruct((n, BRANCH), BF16)],
        input_output_aliases=aliases,
        compiler_params=_params(("arbitrary",)),
        name="decode_mixers",
    )(*args)


def _rope_tables(positions):
    inv_freq = ROPE_THETA ** (-jnp.arange(ROT_HALF, dtype=F32) / ROT_HALF)
    ang = positions.astype(F32)[:, None] * inv_freq[None, :]
    cos = jnp.cos(ang)
    sin = jnp.sin(ang)
    reps = LANES // ROT_HALF
    return jnp.tile(cos, (1, reps)), jnp.tile(sin, (1, reps)), cos.T, sin.T


def kernel(x_prompt, x_sample, mem_prompt, cache_swa_k, cache_swa_v, cache_mem_k, cache_mem_v, state_mlstm_c, state_mlstm_n, state_mlstm_m, w_in, b_gates, mlstm_norm_g, swa_sinks, w_mem_kv, w_branch, w_mix_out, ln1_g, ln1_b, w_ffn_in, w_ffn_out, ln2_g, ln2_b):
    nb, seq, _ = x_prompt.shape
    ns = x_sample.shape[0]
    assert ns == LANES and PAST_LEN >= WINDOW

    cosn, sinn, cost, sint = _rope_tables(jnp.arange(seq))
    cos_s, sin_s, cost_s, sint_s = _rope_tables(jnp.full((ns,), PAST_LEN))

    ck_all = jnp.transpose(cache_swa_k, (0, 1, 3, 4, 2)).reshape(DEPTH, ns, SWA_KV * SWA_HD, WINDOW)
    cv_all = jnp.transpose(cache_swa_v, (0, 1, 3, 4, 2)).reshape(DEPTH, ns, SWA_KV * SWA_HD, WINDOW)
    mk_all = cache_mem_k.reshape(DEPTH, ns, MEM_TOKENS * X_HEADS // 8, 8, X_HD)
    mv_all = cache_mem_v.reshape(DEPTH, ns, MEM_TOKENS * X_HEADS // 8, 8, X_HD)
    n_all = state_mlstm_n.reshape(DEPTH, ns * ML_HEADS, ML_DK)
    mt_all = jnp.transpose(state_mlstm_m, (0, 2, 1))

    w_all, wt_all = _relayout_w_in(w_in)
    z4 = jnp.zeros((DEPTH, 4), F32)
    bias_all = jnp.concatenate([b_gates[:, :ML_HEADS], z4, b_gates[:, ML_HEADS:], z4], axis=1)[..., None]
    gain_all = mlstm_norm_g[:, None, :]
    sinks_all = jnp.broadcast_to(swa_sinks[:, :, None], (DEPTH, SWA_HEADS, LANES))
    wbr_all = w_branch.astype(BF16)
    wmix_all = w_mix_out.astype(BF16)
    wfi_all = w_ffn_in.astype(BF16)
    wfo_all = w_ffn_out.astype(BF16)
    ln_all = (ln1_g[:, None, :], ln1_b[:, None, :], ln2_g[:, None, :], ln2_b[:, None, :])

    yp = x_prompt.reshape(nb * seq, D_MODEL)
    ys = x_sample.reshape(ns, D_MODEL)
    mem2d = mem_prompt.reshape(nb * MEM_TOKENS, D_MODEL)

    outs = {k: [] for k in ("kp", "vp", "cp", "np", "mp", "ns", "ms")}
    c_stack = k_stack = v_stack = None
    mk32, mv32, mkt, mv16 = _memkv(mem2d, w_mem_kv, nb)
    for l in range(DEPTH):
        ya, yb, xq, k32, v32, m_fin, s_fin = _mixer(yp, w_all, wt_all, bias_all, cosn, sinn, cost, sint,
                                                    gain_all, sinks_all, l, nb, seq)
        yp = _merge_ffn(yp, ya, yb, xq, w_all, wbr_all, wmix_all, wfi_all, wfo_all, *ln_all, l, TM_MERGE,
                        memory=(mkt, mv16, seq))
        outs["kp"].append(jnp.transpose(k32.reshape(nb, SWA_KV, SWA_HD, WINDOW), (0, 3, 1, 2)))
        outs["vp"].append(jnp.transpose(v32.reshape(nb, SWA_KV, SWA_HD, WINDOW), (0, 3, 1, 2)))
        outs["cp"].append(jnp.swapaxes(s_fin[..., :ML_DV], -1, -2))
        outs["np"].append(s_fin[..., ML_DV])
        outs["mp"].append(m_fin.reshape(nb, 8, LANES)[:, :ML_HEADS, 0])

        p, pt, p2 = _dproj(ys, w_all, wt_all, bias_all, cos_s, sin_s, cost_s, sint_s, l)
        ya_s, c_stack, n_new, m_new, yb_s, k_stack, v_stack, yc_s = _decode_mixers(
            p, p2, pt, state_mlstm_c, n_all, mt_all, gain_all, ck_all, cv_all, sinks_all, mk_all, mv_all, l,
            None if l == 0 else (c_stack, k_stack, v_stack))
        ys = _merge_ffn(ys, ya_s, yb_s, yc_s, w_all, wbr_all, wmix_all, wfi_all, wfo_all, *ln_all, l, ns)
        outs["ns"].append(n_new.reshape(ns, ML_HEADS, ML_DK))
        outs["ms"].append(m_new)

    st = {k: jnp.stack(vals) for k, vals in outs.items()}
    window_out = lambda t: jnp.transpose(t.reshape(DEPTH, ns, SWA_KV, SWA_HD, WINDOW), (0, 1, 4, 2, 3))
    memory_out = lambda t: t.reshape(DEPTH, nb, MEM_TOKENS, X_HEADS, X_HD)
    return (yp.reshape(nb, seq, D_MODEL), ys.reshape(ns, 1, D_MODEL),
            st["kp"], st["vp"], window_out(k_stack), window_out(v_stack), memory_out(mk32), memory_out(mv32),
            st["cp"], st["np"], st["mp"], c_stack, st["ns"], jnp.transpose(st["ms"], (0, 2, 1)))
```
